```python
import jax, jax.numpy as jnp
from jax import lax
import numpy as np

D_MODEL = 4096
BATCH = 4
SEQ = 4096
DEPTH = 1

CHUNK = 64
D_PLE = 256
D_CONV = D_MODEL // 2
CONV_W = 3
N_HEADS_M = 8
D_MLSTM = D_MODEL // 2
DV_HEAD = D_MLSTM // N_HEADS_M
DQK_HEAD = DV_HEAD // 2
D_QK = N_HEADS_M * DQK_HEAD
N_BRANCH = 2
N_IN = 3 * D_CONV + 2 * D_QK + 2 * D_MLSTM + 2 * N_HEADS_M + N_BRANCH * D_MODEL
N_EXPERTS = 64
TOP_K = 8
N_GROUPS = 8
TOP_GROUPS = 4
D_EXPERT = 768
D_SHARED = 768
ROUTE_SCALE = 2.5
MOE_BLOCK = 256
LN_EPS = 1e-5
ALPHA = (2 * DEPTH) ** 0.25
BETA = (8 * DEPTH) ** -0.25

kernel_name = "hybrid_conv_mlstm_moe_deepnorm"


def _in_layout():
    return (("h_a", D_CONV), ("c_a", D_CONV), ("b_a", D_CONV),
            ("q", D_QK), ("k", D_QK), ("v", D_MLSTM), ("o", D_MLSTM),
            ("i", N_HEADS_M), ("f", N_HEADS_M),
            ("gate_a", D_MODEL), ("gate_b", D_MODEL))


def layer_norm(x, g, b):
    xf = x.astype(jnp.float32)
    mu = xf.mean(-1, keepdims=True)
    var = jnp.mean(jnp.square(xf - mu), -1, keepdims=True)
    return ((xf - mu) * lax.rsqrt(var + LN_EPS) * g + b).astype(x.dtype)


def causal_dwconv(u, w, b):
    S = u.shape[1]
    up = jnp.pad(u, ((0, 0), (CONV_W - 1, 0), (0, 0)))
    return sum(up[:, j:j + S] * w[j] for j in range(CONV_W)) + b


def mlstm_chunkwise(q, k, v, i_pre, f_pre):
    B, H, S, dk = q.shape
    dv = v.shape[-1]
    nc = S // CHUNK
    f32 = jnp.float32
    q = q.astype(f32).reshape(B, H, nc, CHUNK, dk)
    k = (k.astype(f32) * dk ** -0.5).reshape(B, H, nc, CHUNK, dk)
    v = v.astype(f32).reshape(B, H, nc, CHUNK, dv)
    ig = i_pre.astype(f32).reshape(B, H, nc, CHUNK)
    lf = jax.nn.log_sigmoid(f_pre.astype(f32)).reshape(B, H, nc, CHUNK)
    bcum = jnp.cumsum(lf, axis=-1)
    g = bcum[..., -1]
    a = g[..., None] - bcum + ig
    m_loc = a.max(-1)
    wa = jnp.exp(a - m_loc[..., None])
    kv_loc = jnp.einsum('bhcsk,bhcsv->bhckv', wa[..., None] * k, v)
    n_loc = jnp.einsum('bhcs,bhcsk->bhck', wa, k)

    def step(carry, inp):
        C, n, m = carry
        g_c, m_l, kv_l, n_l = inp
        m_new = jnp.maximum(g_c + m, m_l)
        s_old = jnp.exp(g_c + m - m_new)
        s_loc = jnp.exp(m_l - m_new)
        C_new = s_old[..., None, None] * C + s_loc[..., None, None] * kv_l
        n_new = s_old[..., None] * n + s_loc[..., None] * n_l
        return (C_new, n_new, m_new), (C, n, m)

    init = (jnp.zeros((B, H, dk, dv), f32), jnp.zeros((B, H, dk), f32), jnp.zeros((B, H), f32))
    xs = (jnp.moveaxis(g, 2, 0), jnp.moveaxis(m_loc, 2, 0),
          jnp.moveaxis(kv_loc, 2, 0), jnp.moveaxis(n_loc, 2, 0))
    _, (C_prev, n_prev, m_prev) = lax.scan(step, init, xs)
    C_prev = jnp.moveaxis(C_prev, 0, 2)
    n_prev = jnp.moveaxis(n_prev, 0, 2)
    m_prev = jnp.moveaxis(m_prev, 0, 2)

    causal = jnp.tril(jnp.ones((CHUNK, CHUNK), dtype=bool))
    log_d = jnp.where(causal, bcum[..., :, None] - bcum[..., None, :] + ig[..., None, :], -jnp.inf)
    log_inter = bcum + m_prev[..., None]
    m_t = jnp.maximum(log_inter, log_d.max(-1))
    w_inter = jnp.exp(log_inter - m_t)
    s_mat = jnp.einsum('bhctk,bhcsk->bhcts', q, k) * jnp.exp(log_d - m_t[..., None])
    num = (w_inter[..., None] * jnp.einsum('bhctk,bhckv->bhctv', q, C_prev)
           + jnp.einsum('bhcts,bhcsv->bhctv', s_mat, v))
    den = w_inter * jnp.einsum('bhctk,bhck->bhct', q, n_prev) + s_mat.sum(-1)
    den = jnp.maximum(jnp.abs(den), jnp.exp(-m_t))
    return (num / den[..., None]).reshape(B, H, S, dv)


def token_mixer(x, w_in, b_in, conv_w, conv_b, mh_norm_g, w_conv_out, w_mlstm_out, w_mix_out):
    B, S, _ = x.shape
    offs = np.cumsum([n for _, n in _in_layout()])[:-1].tolist()
    h_a, c_a, b_a, q, k, v, o, i_pre, f_pre, g_a, g_b = [
        x @ w + b for w, b in zip(jnp.split(w_in, offs, axis=1), jnp.split(b_in, offs))]
    y_a = (b_a * causal_dwconv(c_a * h_a, conv_w, conv_b)) @ w_conv_out
    heads = lambda t, d: t.reshape(B, S, N_HEADS_M, d).transpose(0, 2, 1, 3)
    h = mlstm_chunkwise(heads(q, DQK_HEAD), heads(k, DQK_HEAD), heads(v, DV_HEAD),
                        i_pre.transpose(0, 2, 1), f_pre.transpose(0, 2, 1))
    mu = h.mean(-1, keepdims=True)
    var = jnp.mean(jnp.square(h - mu), -1, keepdims=True)
    hn = ((h - mu) * lax.rsqrt(var + LN_EPS)).transpose(0, 2, 1, 3).reshape(B, S, D_MLSTM) * mh_norm_g
    y_b = (jax.nn.sigmoid(o.astype(jnp.float32)) * hn).astype(x.dtype) @ w_mlstm_out
    u = jax.nn.sigmoid(g_a) * y_a + jax.nn.sigmoid(g_b) * y_b
    return u @ w_mix_out


def moe(x, w_router, router_bias, w_gu_e, w_down_e, w_gu_s, w_down_s):
    B, S, D = x.shape
    xt = x.reshape(-1, D)
    T = xt.shape[0]
    f32 = jnp.float32
    scores = jax.nn.sigmoid((xt @ w_router).astype(f32))
    sel = scores + router_bias
    grp = sel.reshape(T, N_GROUPS, N_EXPERTS // N_GROUPS)
    grp_score = lax.top_k(grp, 2)[0].sum(-1)
    top_g = lax.top_k(grp_score, TOP_GROUPS)[1]
    gmask = jnp.any(top_g[..., None] == jnp.arange(N_GROUPS), axis=-2)
    sel = jnp.where(jnp.repeat(gmask, N_EXPERTS // N_GROUPS, axis=-1), sel, -jnp.inf)
    idx = lax.top_k(sel, TOP_K)[1]
    wts = jnp.take_along_axis(scores, idx, axis=-1)
    wts = wts / wts.sum(-1, keepdims=True) * ROUTE_SCALE
    A = T * TOP_K
    e_flat = idx.reshape(-1)
    tok_flat = jnp.repeat(jnp.arange(T, dtype=jnp.int32), TOP_K)
    order = jnp.argsort(e_flat, stable=True)
    e_sorted = e_flat[order]
    counts = jnp.bincount(e_flat, length=N_EXPERTS)
    padded = (counts + MOE_BLOCK - 1) // MOE_BLOCK * MOE_BLOCK
    start = jnp.cumsum(counts) - counts
    pend = jnp.cumsum(padded)
    pstart = pend - padded
    dest = pstart[e_sorted] + jnp.arange(A) - start[e_sorted]
    P = A + N_EXPERTS * MOE_BLOCK
    n_blocks = P // MOE_BLOCK
    row_tok = jnp.zeros((P,), jnp.int32).at[dest].set(tok_flat[order])
    row_w = jnp.zeros((P,), f32).at[dest].set(wts.reshape(-1)[order])
    blk_e = jnp.minimum(jnp.searchsorted(pend, jnp.arange(n_blocks) * MOE_BLOCK, side='right'),
                        N_EXPERTS - 1)

    def expert_block(acc, inp):
        e, tok, w = inp
        xb = xt[tok]
        gu = xb @ w_gu_e[e]
        hb = jax.nn.silu(gu[:, :D_EXPERT]) * gu[:, D_EXPERT:]
        yb = (hb @ w_down_e[e]) * w[:, None]
        return acc.at[tok].add(yb.astype(acc.dtype)), None

    routed, _ = lax.scan(expert_block, jnp.zeros_like(xt),
                         (blk_e, row_tok.reshape(n_blocks, MOE_BLOCK), row_w.reshape(n_blocks, MOE_BLOCK)))
    gu_s = xt @ w_gu_s
    shared = (jax.nn.silu(gu_s[:, :D_SHARED]) * gu_s[:, D_SHARED:]) @ w_down_s
    return (routed + shared).reshape(B, S, D)


def setup_inputs(seed: int = 0) -> dict:
    key = jax.random.key(seed)
    ks = jax.random.split(key, 24)
    f32 = jnp.float32
    D = D_MODEL
    nrm = lambda k, shape, scale: jax.random.normal(k, shape, f32) * scale
    layout = _in_layout()
    col_scale = jnp.concatenate([jnp.full((n,), BETA if nm == "v" else 1.0, f32) for nm, n in layout])
    b_off = jnp.concatenate([jnp.linspace(3.0, 6.0, n, dtype=f32) if nm == "f" else jnp.zeros((n,), f32)
                             for nm, n in layout])
    return {
        "x": nrm(ks[0], (BATCH, SEQ, D), 1.0),
        "p": nrm(ks[1], (DEPTH, BATCH, SEQ, D_PLE), 1.0),
        "ln_in_g": 1.0 + nrm(ks[2], (D,), 0.02),
        "ln_in_b": nrm(ks[3], (D,), 0.02),
        "w_in": nrm(ks[4], (DEPTH, D, N_IN), D ** -0.5) * col_scale,
        "b_in": nrm(ks[5], (DEPTH, N_IN), 0.02) + b_off,
        "conv_w": nrm(ks[6], (DEPTH, CONV_W, D_CONV), CONV_W ** -0.5),
        "conv_b": nrm(ks[7], (DEPTH, D_CONV), 0.02),
        "mh_norm_g": 1.0 + nrm(ks[8], (DEPTH, D_MLSTM), 0.02),
        "w_conv_out": nrm(ks[9], (DEPTH, D_CONV, D), BETA * D_CONV ** -0.5),
        "w_mlstm_out": nrm(ks[10], (DEPTH, D_MLSTM, D), BETA * D_MLSTM ** -0.5),
        "w_mix_out": nrm(ks[11], (DEPTH, D, D), BETA * D ** -0.5),
        "ln1_g": 1.0 + nrm(ks[12], (DEPTH, D), 0.02),
        "ln1_b": nrm(ks[13], (DEPTH, D), 0.02),
        "w_router": nrm(ks[14], (DEPTH, D, N_EXPERTS), D ** -0.5),
        "router_bias": nrm(ks[15], (DEPTH, N_EXPERTS), 0.01),
        "w_gu_e": nrm(ks[16], (DEPTH, N_EXPERTS, D, 2 * D_EXPERT), D ** -0.5),
        "w_down_e": nrm(ks[17], (DEPTH, N_EXPERTS, D_EXPERT, D), BETA * D_EXPERT ** -0.5),
        "w_gu_s": nrm(ks[18], (DEPTH, D, 2 * D_SHARED), D ** -0.5),
        "w_down_s": nrm(ks[19], (DEPTH, D_SHARED, D), BETA * D_SHARED ** -0.5),
        "w_ple_gate": nrm(ks[20], (DEPTH, D, D), D ** -0.5),
        "b_ple_gate": nrm(ks[21], (DEPTH, D), 0.02),
        "w_ple_proj": nrm(ks[22], (DEPTH, D_PLE, D), BETA * D_PLE ** -0.5),
        "ln2_g": 1.0 + nrm(ks[23], (DEPTH, D), 0.02),
        "ln2_b": nrm(jax.random.fold_in(key, 99), (DEPTH, D), 0.02),
    }


def reference(x, p, ln_in_g, ln_in_b, w_in, b_in, conv_w, conv_b, mh_norm_g, w_conv_out,
              w_mlstm_out, w_mix_out, ln1_g, ln1_b, w_router, router_bias, w_gu_e, w_down_e,
              w_gu_s, w_down_s, w_ple_gate, b_ple_gate, w_ple_proj, ln2_g, ln2_b):
    h = layer_norm(x, ln_in_g, ln_in_b)
    for l in range(DEPTH):
        mix = token_mixer(h, w_in[l], b_in[l], conv_w[l], conv_b[l], mh_norm_g[l],
                          w_conv_out[l], w_mlstm_out[l], w_mix_out[l])
        h = layer_norm(ALPHA * h + mix, ln1_g[l], ln1_b[l])
        ffn = moe(h, w_router[l], router_bias[l], w_gu_e[l], w_down_e[l], w_gu_s[l], w_down_s[l])
        ple = jax.nn.sigmoid(h @ w_ple_gate[l] + b_ple_gate[l]) * (p[l] @ w_ple_proj[l])
        h = layer_norm(ALPHA * h + ffn + ple, ln2_g[l], ln2_b[l])
    return h
```

```python
import functools

import jax
import jax.numpy as jnp
from jax import lax
from jax.experimental import pallas as pl
from jax.experimental.pallas import tpu as pltpu

F32 = jnp.float32
BF16 = jnp.bfloat16
U32 = jnp.uint32
I32 = jnp.int32

N_HEADS = 8
TOP_K = 8
N_GROUPS = 8
TOP_GROUPS = 4
ROUTE_SCALE = 2.5
MOE_BLOCK = 256
LN_EPS = 1e-5
CONV_W = 3
MLSTM_CHUNK = 256
NEG_BIG = -1e30
HI_MASK = 0xFFFF0000
V7X_VMEM_BYTES = 64 * 1024 * 1024


def _cparams(n_axes, vmem_mib):
    assert vmem_mib * 1024 * 1024 <= V7X_VMEM_BYTES
    return pltpu.CompilerParams(dimension_semantics=("arbitrary",) * n_axes,
                                vmem_limit_bytes=vmem_mib * 1024 * 1024)


def _ln_rows(x, g, b):
    mu = jnp.mean(x, axis=-1, keepdims=True)
    xc = x - mu
    var = jnp.mean(xc * xc, axis=-1, keepdims=True)
    return xc * lax.rsqrt(var + LN_EPS) * g + b


def _pack_halves(lo, hi):
    lo = pltpu.bitcast(lo.astype(BF16).astype(F32), U32)
    hi = pltpu.bitcast(hi.astype(BF16).astype(F32), U32)
    return (hi & jnp.uint32(HI_MASK)) | (lo >> jnp.uint32(16))


def _ln_chunked(acc_ref, g_ref, b_ref):
    nj, _, tn = acc_ref.shape
    inv_d = 1.0 / (nj * tn)
    s = acc_ref[0].sum(axis=-1, keepdims=True)
    for jj in range(1, nj):
        s = s + acc_ref[jj].sum(axis=-1, keepdims=True)
    mu = s * inv_d
    v = jnp.zeros_like(mu)
    for jj in range(nj):
        d = acc_ref[jj] - mu
        v = v + (d * d).sum(axis=-1, keepdims=True)
    rstd = lax.rsqrt(v * inv_d + LN_EPS)

    def chunk(jj):
        cols = slice(jj * tn, (jj + 1) * tn)
        return (acc_ref[jj] - mu) * rstd * g_ref[:, cols] + b_ref[:, cols]
    return chunk


def _unpack_halves(w):
    lo = pltpu.bitcast(w << jnp.uint32(16), F32)
    hi = pltpu.bitcast(w & jnp.uint32(HI_MASK), F32)
    return lo, hi


def _ln_in_kernel(x_ref, g_ref, b_ref, of_ref, ob_ref):
    y = _ln_rows(x_ref[...], g_ref[...], b_ref[...])
    of_ref[...] = y
    ob_ref[...] = y.astype(BF16)


def _ln_in(x2, g, b, tm=256):
    T, D = x2.shape
    return pl.pallas_call(
        _ln_in_kernel,
        out_shape=(jax.ShapeDtypeStruct((T, D), F32), jax.ShapeDtypeStruct((T, D), BF16)),
        grid=(T // tm,),
        in_specs=[pl.BlockSpec((tm, D), lambda i: (i, 0)),
                  pl.BlockSpec((1, D), lambda i: (0, 0)),
                  pl.BlockSpec((1, D), lambda i: (0, 0))],
        out_specs=(pl.BlockSpec((tm, D), lambda i: (i, 0)),
                   pl.BlockSpec((tm, D), lambda i: (i, 0))),
        compiler_params=_cparams(1, 40),
        name="ln_in",
    )(x2, g.reshape(1, D), b.reshape(1, D))


def _proj_kernel(x_ref, w_ref, b_ref, o_ref, wbf_ref, *, act):
    @pl.when(pl.program_id(1) == 0)
    def _():
        wbf_ref[...] = w_ref[...].astype(BF16)

    acc = jnp.dot(x_ref[...], wbf_ref[...], preferred_element_type=F32) + b_ref[...]
    if act == "sigmoid":
        acc = jax.nn.sigmoid(acc)
    o_ref[...] = acc.astype(o_ref.dtype)


def _proj(x, w, bias2d, col0, n_cols, *, act=None, tm=1024, tn=512, out_dtype=BF16):
    T, K = x.shape
    tm = min(tm, T)
    assert col0 % tn == 0 and n_cols % tn == 0 and T % tm == 0
    jb = col0 // tn
    return pl.pallas_call(
        functools.partial(_proj_kernel, act=act),
        out_shape=jax.ShapeDtypeStruct((T, n_cols), out_dtype),
        grid=(n_cols // tn, T // tm),
        in_specs=[pl.BlockSpec((tm, K), lambda j, i: (i, 0)),
                  pl.BlockSpec((K, tn), lambda j, i: (0, jb + j)),
                  pl.BlockSpec((1, tn), lambda j, i: (0, jb + j))],
        out_specs=pl.BlockSpec((tm, tn), lambda j, i: (i, j)),
        scratch_shapes=[pltpu.VMEM((K, tn), BF16)],
        compiler_params=_cparams(2, 56),
        name="proj_" + (act or "lin"),
    )(x, w, bias2d)


def _conv_kernel(x_ref, wh_ref, wc_ref, wb_ref, bh_ref, bc_ref, bb_ref, cw_ref, cb_ref,
                 o_ref, wbf_ref, zprev_ref, *, tiles_per_seq):
    i = pl.program_id(1)

    @pl.when(i == 0)
    def _():
        wbf_ref[0] = wh_ref[...].astype(BF16)
        wbf_ref[1] = wc_ref[...].astype(BF16)
        wbf_ref[2] = wb_ref[...].astype(BF16)

    @pl.when(i % tiles_per_seq == 0)
    def _():
        zprev_ref[...] = jnp.zeros_like(zprev_ref)

    x = x_ref[...]
    ha = jnp.dot(x, wbf_ref[0], preferred_element_type=F32) + bh_ref[...]
    ca = jnp.dot(x, wbf_ref[1], preferred_element_type=F32) + bc_ref[...]
    ba = jnp.dot(x, wbf_ref[2], preferred_element_type=F32) + bb_ref[...]
    z = ca * ha
    tm = z.shape[0]
    prev = zprev_ref[...]
    row8 = lax.broadcasted_iota(I32, prev.shape, 0)
    z1 = pltpu.roll(z, 1, 0)
    z2 = pltpu.roll(z, 2, 0)
    p1 = pltpu.roll(prev, 1, 0)
    p2 = pltpu.roll(prev, 2, 0)
    z1 = jnp.concatenate([jnp.where(row8 < 1, p1, z1[:8]), z1[8:]], axis=0)
    z2 = jnp.concatenate([jnp.where(row8 < 2, p2, z2[:8]), z2[8:]], axis=0)
    cw = cw_ref[...]
    y = cw[0:1] * z2 + cw[1:2] * z1 + cw[2:3] * z + cb_ref[...]
    o_ref[...] = (ba * y).astype(o_ref.dtype)
    zprev_ref[...] = z[tm - 8:]


def _conv_branch(x, w_in, b2d, conv_w, conv_b2d, seq, d_conv, *, tm=512, tn=256):
    T, K = x.shape
    tm = min(tm, seq)
    assert seq % tm == 0 and d_conv % tn == 0 and tm % 8 == 0
    nb = d_conv // tn
    wspec = lambda g: pl.BlockSpec((K, tn), lambda j, i: (0, g * nb + j))
    bspec = lambda g: pl.BlockSpec((1, tn), lambda j, i: (0, g * nb + j))
    return pl.pallas_call(
        functools.partial(_conv_kernel, tiles_per_seq=seq // tm),
        out_shape=jax.ShapeDtypeStruct((T, d_conv), BF16),
        grid=(nb, T // tm),
        in_specs=[pl.BlockSpec((tm, K), lambda j, i: (i, 0)),
                  wspec(0), wspec(1), wspec(2), bspec(0), bspec(1), bspec(2),
                  pl.BlockSpec((CONV_W, tn), lambda j, i: (0, j)),
                  pl.BlockSpec((1, tn), lambda j, i: (0, j))],
        out_specs=pl.BlockSpec((tm, tn), lambda j, i: (i, j)),
        scratch_shapes=[pltpu.VMEM((3, K, tn), BF16), pltpu.VMEM((8, tn), F32)],
        compiler_params=_cparams(2, 56),
        name="conv_branch",
    )(x, w_in, w_in, w_in, b2d, b2d, b2d, conv_w, conv_b2d)


def _if_kernel(x_ref, w_ref, wt_ref, bc_ref, br_ref, oc_ref, or_ref):
    x = x_ref[...]
    oc_ref[...] = jnp.dot(x, w_ref[...], preferred_element_type=F32) + bc_ref[...]
    or_ref[...] = lax.dot_general(wt_ref[...], x, (((1,), (1,)), ((), ())),
                                  preferred_element_type=F32) + br_ref[...]


def _if_gates(x, w_if, b_if, tm=512):
    T, K = x.shape
    tm = min(tm, T)
    n = w_if.shape[1]
    w_pad = jnp.zeros((K, 128), BF16).at[:, :n].set(w_if.astype(BF16))
    b_pad = jnp.zeros((1, 128), F32).at[0, :n].set(b_if)
    return pl.pallas_call(
        _if_kernel,
        out_shape=(jax.ShapeDtypeStruct((T, 128), F32), jax.ShapeDtypeStruct((n, T), F32)),
        grid=(T // tm,),
        in_specs=[pl.BlockSpec((tm, K), lambda i: (i, 0)),
                  pl.BlockSpec((K, 128), lambda i: (0, 0)),
                  pl.BlockSpec((n, K), lambda i: (0, 0)),
                  pl.BlockSpec((1, 128), lambda i: (0, 0)),
                  pl.BlockSpec((n, 1), lambda i: (0, 0))],
        out_specs=(pl.BlockSpec((tm, 128), lambda i: (i, 0)),
                   pl.BlockSpec((n, tm), lambda i: (0, i))),
        compiler_params=_cparams(1, 32),
        name="if_gates",
    )(x, w_pad, w_if.T.astype(BF16), b_pad, b_if.reshape(n, 1))


def _mlstm_kernel(q_ref, k_ref, v_ref, o_ref, ifc_ref, ifr_ref, g_ref, y_ref, c_ref, m_ref,
                  *, dk, dv):
    H = N_HEADS
    L = q_ref.shape[0]

    @pl.when(pl.program_id(1) == 0)
    def _():
        c_ref[...] = jnp.zeros_like(c_ref)
        m_ref[...] = jnp.zeros_like(m_ref)

    scale = dk ** -0.5
    ifc = ifc_ref[...]
    ifr = ifr_ref[...]
    ig_c = ifc[:, 0:H]
    lf_c = jax.nn.log_sigmoid(ifc[:, H:2 * H])
    ig_r = ifr[0:H, :]
    lf_r = jax.nn.log_sigmoid(ifr[H:2 * H, :])
    r = lax.broadcasted_iota(I32, (L, L), 0)
    c = lax.broadcasted_iota(I32, (L, L), 1)
    causal = r >= c
    bcum_c = jnp.dot(causal.astype(F32), lf_c, preferred_element_type=F32,
                     precision=lax.Precision.HIGHEST)
    bcum_r = jnp.dot(lf_r, (r <= c).astype(F32), preferred_element_type=F32,
                     precision=lax.Precision.HIGHEST)
    d_c = ig_c - bcum_c
    d_r = ig_r - bcum_r
    ones_col = (lax.broadcasted_iota(I32, (L, 128), 1) == 0).astype(BF16)

    for h in range(H):
        q = q_ref[:, h * dk:(h + 1) * dk]
        k = k_ref[:, h * dk:(h + 1) * dk]
        v = v_ref[:, h * dv:(h + 1) * dv]
        v_aug = jnp.concatenate([v, ones_col], axis=1)
        bc = bcum_c[:, h:h + 1]
        m_prev = m_ref[h]
        c_prev = c_ref[h]

        qk = lax.dot_general(q, k, (((1,), (1,)), ((), ())), preferred_element_type=F32) * scale
        logd = jnp.where(causal, bc + d_r[h:h + 1, :], NEG_BIG)
        log_inter = bc + m_prev
        m_t = jnp.maximum(log_inter, jnp.max(logd, axis=1, keepdims=True))
        w_inter = jnp.exp(log_inter - m_t)
        s_mat = (qk * jnp.exp(logd - m_t)).astype(BF16)
        tot = (w_inter * jnp.dot(q, c_prev.astype(BF16), preferred_element_type=F32)
               + jnp.dot(s_mat, v_aug, preferred_element_type=F32))
        den = jnp.maximum(jnp.abs(tot[:, dv:dv + 1]), jnp.exp(-m_t))
        hh = tot[:, :dv] / den
        mu = jnp.mean(hh, axis=-1, keepdims=True)
        hc = hh - mu
        var = jnp.mean(hc * hc, axis=-1, keepdims=True)
        hn = hc * lax.rsqrt(var + LN_EPS) * g_ref[:, h * dv:(h + 1) * dv]
        og = jax.nn.sigmoid(o_ref[:, h * dv:(h + 1) * dv].astype(F32))
        y_ref[:, h * dv:(h + 1) * dv] = (og * hn).astype(y_ref.dtype)

        g_tot = bc[L - 1:L, :]
        a = g_tot + d_c[:, h:h + 1]
        m_new = jnp.maximum(g_tot + m_prev, jnp.max(a, axis=0, keepdims=True))
        kw = (k.astype(F32) * (jnp.exp(a - m_new) * scale)).astype(BF16)
        kv = lax.dot_general(kw, v_aug, (((0,), (0,)), ((), ())), preferred_element_type=F32)
        c_ref[h] = jnp.exp(g_tot + m_prev - m_new) * c_prev + kv
        m_ref[h] = m_new


def _mlstm(qkvo, if_col, if_row, mh_g2d, batch, seq, d_qk, d_v, L=MLSTM_CHUNK):
    T = qkvo.shape[0]
    L = min(L, seq)
    assert seq % L == 0 and d_v == 2 * d_qk
    nc = seq // L
    dk, dv = d_qk // N_HEADS, d_v // N_HEADS
    row = lambda b, c: b * nc + c
    return pl.pallas_call(
        functools.partial(_mlstm_kernel, dk=dk, dv=dv),
        out_shape=jax.ShapeDtypeStruct((T, d_v), BF16),
        grid=(batch, nc),
        in_specs=[pl.BlockSpec((L, d_qk), lambda b, c: (row(b, c), 0)),
                  pl.BlockSpec((L, d_qk), lambda b, c: (row(b, c), 1)),
                  pl.BlockSpec((L, d_v), lambda b, c: (row(b, c), 1)),
                  pl.BlockSpec((L, d_v), lambda b, c: (row(b, c), 2)),
                  pl.BlockSpec((L, 128), lambda b, c: (row(b, c), 0)),
                  pl.BlockSpec((2 * N_HEADS, L), lambda b, c: (0, row(b, c))),
                  pl.BlockSpec((1, d_v), lambda b, c: (0, 0))],
        out_specs=pl.BlockSpec((L, d_v), lambda b, c: (row(b, c), 0)),
        scratch_shapes=[pltpu.VMEM((N_HEADS, dk, dv + 128), F32),
                        pltpu.VMEM((N_HEADS, 1, 1), F32)],
        compiler_params=_cparams(2, 40),
        name="mlstm",
    )(qkvo, qkvo, qkvo, qkvo, if_col, if_row, mh_g2d)


def _merge_kernel(a_ref, b_ref, wa_ref, wb_ref, ga_ref, gb_ref, o_ref, wabf_ref, wbbf_ref):
    @pl.when(pl.program_id(1) == 0)
    def _():
        wabf_ref[...] = wa_ref[...].astype(BF16)
        wbbf_ref[...] = wb_ref[...].astype(BF16)

    ya = jnp.dot(a_ref[...], wabf_ref[...], preferred_element_type=F32)
    yb = jnp.dot(b_ref[...], wbbf_ref[...], preferred_element_type=F32)
    u = ga_ref[...].astype(F32) * ya + gb_ref[...].astype(F32) * yb
    o_ref[...] = u.astype(o_ref.dtype)


def _merge(ya_pre, yb_pre, w_a, w_b, gates, *, tm=1024, tn=512):
    T, K = ya_pre.shape
    D = w_a.shape[1]
    tm = min(tm, T)
    nj = D // tn
    return pl.pallas_call(
        _merge_kernel,
        out_shape=jax.ShapeDtypeStruct((T, D), BF16),
        grid=(nj, T // tm),
        in_specs=[pl.BlockSpec((tm, K), lambda j, i: (i, 0)),
                  pl.BlockSpec((tm, K), lambda j, i: (i, 0)),
                  pl.BlockSpec((K, tn), lambda j, i: (0, j)),
                  pl.BlockSpec((K, tn), lambda j, i: (0, j)),
                  pl.BlockSpec((tm, tn), lambda j, i: (i, j)),
                  pl.BlockSpec((tm, tn), lambda j, i: (i, nj + j))],
        out_specs=pl.BlockSpec((tm, tn), lambda j, i: (i, j)),
        scratch_shapes=[pltpu.VMEM((K, tn), BF16), pltpu.VMEM((K, tn), BF16)],
        compiler_params=_cparams(2, 56),
        name="merge",
    )(ya_pre, yb_pre, w_a, w_b, gates, gates)


def _mix_ln_kernel(u_ref, w_ref, h_ref, g_ref, b_ref, of_ref, op_ref, acc_ref, *, alpha):
    j = pl.program_id(1)
    nj, _, tn = acc_ref.shape
    acc_ref[j] = alpha * h_ref[...] + jnp.dot(u_ref[...], w_ref[...], preferred_element_type=F32)

    @pl.when(j == nj - 1)
    def _():
        chunk = _ln_chunked(acc_ref, g_ref, b_ref)
        for jj in range(nj // 2):
            lo, hi = chunk(jj), chunk(jj + nj // 2)
            of_ref[:, jj * tn:(jj + 1) * tn] = lo
            of_ref[:, (jj + nj // 2) * tn:(jj + nj // 2 + 1) * tn] = hi
            op_ref[:, jj * tn:(jj + 1) * tn] = _pack_halves(lo, hi)


def _mix_ln(u, w_bf, h0, g, b, alpha, *, tm=256, tn=512):
    T, D = h0.shape
    tm = min(tm, T)
    nj = D // tn
    return pl.pallas_call(
        functools.partial(_mix_ln_kernel, alpha=alpha),
        out_shape=(jax.ShapeDtypeStruct((T, D), F32), jax.ShapeDtypeStruct((T, D // 2), U32)),
        grid=(T // tm, nj),
        in_specs=[pl.BlockSpec((tm, D), lambda i, j: (i, 0)),
                  pl.BlockSpec((D, tn), lambda i, j: (0, j)),
                  pl.BlockSpec((tm, tn), lambda i, j: (i, j)),
                  pl.BlockSpec((1, D), lambda i, j: (0, 0)),
                  pl.BlockSpec((1, D), lambda i, j: (0, 0))],
        out_specs=(pl.BlockSpec((tm, D), lambda i, j: (i, 0)),
                   pl.BlockSpec((tm, D // 2), lambda i, j: (i, 0))),
        scratch_shapes=[pltpu.VMEM((nj, tm, tn), F32)],
        compiler_params=_cparams(2, 56),
        name="mix_ln1",
    )(u, w_bf, h0, g.reshape(1, D), b.reshape(1, D))


def _router_kernel(h_ref, wr_ref, rb_ref, idx_ref, wt_ref, rank_ref, cnt_ref, carry_ref):
    i = pl.program_id(0)
    E = wr_ref.shape[0]
    tm = h_ref.shape[0]
    G, M = N_GROUPS, E // N_GROUPS

    @pl.when(i == 0)
    def _():
        carry_ref[...] = jnp.zeros_like(carry_ref)

    logits = lax.dot_general(wr_ref[...], h_ref[...], (((1,), (1,)), ((), ())),
                             preferred_element_type=F32, precision=lax.Precision.HIGHEST)
    scores = jax.nn.sigmoid(logits)
    scores3 = scores.reshape(G, M, tm)
    sel3 = (scores + rb_ref[...]).reshape(G, M, tm)
    midx = lax.broadcasted_iota(I32, (G, M, tm), 1)
    gidx3 = lax.broadcasted_iota(I32, (G, M, tm), 0)
    eidx = gidx3 * M + midx
    gidx = lax.broadcasted_iota(I32, (G, 1, tm), 0)
    neg_inf = -jnp.inf

    top1 = jnp.max(sel3, axis=1, keepdims=True)
    first1 = jnp.min(jnp.where(sel3 == top1, midx, M), axis=1, keepdims=True)
    top2 = jnp.max(jnp.where(midx == first1, neg_inf, sel3), axis=1, keepdims=True)
    gs = top1 + top2
    gkeep = jnp.zeros((G, 1, tm), F32)
    for _ in range(TOP_GROUPS):
        mx = jnp.max(gs, axis=0, keepdims=True)
        first = jnp.min(jnp.where(gs == mx, gidx, G), axis=0, keepdims=True)
        hit = gidx == first
        gkeep = jnp.where(hit, 1.0, gkeep)
        gs = jnp.where(hit, neg_inf, gs)
    selm = jnp.where(gkeep > 0.5, sel3, neg_inf)

    idx_rows, sc_rows = [], []
    chosen = jnp.zeros((G, M, tm), F32)
    for _ in range(TOP_K):
        mx = jnp.max(jnp.max(selm, axis=1, keepdims=True), axis=0, keepdims=True)
        first = jnp.min(jnp.min(jnp.where(selm == mx, eidx, E), axis=1, keepdims=True),
                        axis=0, keepdims=True)
        hit = eidx == first
        sc = jnp.sum(jnp.sum(jnp.where(hit, scores3, 0.0), axis=1, keepdims=True),
                     axis=0, keepdims=True)
        chosen = jnp.where(hit, 1.0, chosen)
        selm = jnp.where(hit, neg_inf, selm)
        idx_rows.append(first)
        sc_rows.append(sc)
    denom = sc_rows[0]
    for s in sc_rows[1:]:
        denom = denom + s

    tr = lax.broadcasted_iota(I32, (tm, tm), 0)
    tc = lax.broadcasted_iota(I32, (tm, tm), 1)
    before = (tr < tc).astype(BF16)
    chosen2 = chosen.reshape(E, tm)
    rank2 = jnp.dot(chosen2.astype(BF16), before, preferred_element_type=F32) + carry_ref[...]
    rank3 = rank2.reshape(G, M, tm)
    for k in range(TOP_K):
        hit = eidx == idx_rows[k]
        rk = jnp.sum(jnp.sum(jnp.where(hit, rank3, 0.0), axis=1, keepdims=True),
                     axis=0, keepdims=True)
        idx_ref[k:k + 1, :] = idx_rows[k].reshape(1, tm)
        wt_ref[k:k + 1, :] = (sc_rows[k] / denom * ROUTE_SCALE).reshape(1, tm)
        rank_ref[k:k + 1, :] = rk.reshape(1, tm).astype(I32)
    carry_ref[...] = carry_ref[...] + jnp.sum(chosen2, axis=1, keepdims=True)

    @pl.when(i == pl.num_programs(0) - 1)
    def _():
        cnt_ref[...] = carry_ref[...]


def _router(h1, w_router, router_bias, tm=256):
    T, D = h1.shape
    E = w_router.shape[1]
    tm = min(tm, T)
    return pl.pallas_call(
        _router_kernel,
        out_shape=(jax.ShapeDtypeStruct((TOP_K, T), I32), jax.ShapeDtypeStruct((TOP_K, T), F32),
                   jax.ShapeDtypeStruct((TOP_K, T), I32), jax.ShapeDtypeStruct((E, 1), F32)),
        grid=(T // tm,),
        in_specs=[pl.BlockSpec((tm, D), lambda i: (i, 0)),
                  pl.BlockSpec((E, D), lambda i: (0, 0)),
                  pl.BlockSpec((E, 1), lambda i: (0, 0))],
        out_specs=(pl.BlockSpec((TOP_K, tm), lambda i: (0, i)),
                   pl.BlockSpec((TOP_K, tm), lambda i: (0, i)),
                   pl.BlockSpec((TOP_K, tm), lambda i: (0, i)),
                   pl.BlockSpec((E, 1), lambda i: (0, 0))),
        scratch_shapes=[pltpu.VMEM((E, 1), F32)],
        compiler_params=_cparams(1, 32),
        name="router",
    )(h1, w_router.T, router_bias.reshape(E, 1))


def _gather_rows_kernel(nused_ref, tok_hbm, src_hbm, dst_hbm, tok_smem, zero_ref, isem, gsem, zsem):
    b = pl.program_id(0)
    nb = pl.num_programs(0)
    n_used = nused_ref[0]
    R = tok_smem.shape[1]

    @pl.when(b == 0)
    def _():
        zero_ref[...] = jnp.zeros_like(zero_ref)

    @pl.when(b >= n_used)
    def _():
        fill = pltpu.make_async_copy(zero_ref, dst_hbm.at[pl.ds(b * R, R)], zsem)
        fill.start()
        fill.wait()

    def idx_copy(blk, slot):
        return pltpu.make_async_copy(tok_hbm.at[blk], tok_smem.at[slot], isem.at[slot])

    def row_copy(tok, dst_row, slot):
        return pltpu.make_async_copy(src_hbm.at[pl.ds(tok, 1)], dst_hbm.at[pl.ds(dst_row, 1)],
                                     gsem.at[slot])

    def wait_rows(slot):
        def body(r, carry):
            row_copy(0, 0, slot).wait()
            return carry
        lax.fori_loop(0, R, body, 0)

    slot = b % 2

    @pl.when(b == 0)
    def _():
        idx_copy(0, 0).start()

    @pl.when(b < n_used)
    def _():
        idx_copy(b, slot).wait()

        @pl.when(b + 1 < n_used)
        def _():
            idx_copy(b + 1, 1 - slot).start()

        def issue(r, carry):
            row_copy(tok_smem[slot, r], b * R + r, slot).start()
            return carry
        lax.fori_loop(0, R, issue, 0, unroll=8)

    @pl.when(jnp.logical_and(b >= 1, b - 1 < n_used))
    def _():
        wait_rows(1 - slot)

    @pl.when(jnp.logical_and(b == nb - 1, b < n_used))
    def _():
        wait_rows(slot)


def _gather_rows(src, row_tok2d, n_used):
    nb, R = row_tok2d.shape
    W = src.shape[1]
    return pl.pallas_call(
        _gather_rows_kernel,
        out_shape=jax.ShapeDtypeStruct((nb * R, W), src.dtype),
        grid_spec=pltpu.PrefetchScalarGridSpec(
            num_scalar_prefetch=1,
            grid=(nb,),
            in_specs=[pl.BlockSpec(memory_space=pl.ANY), pl.BlockSpec(memory_space=pl.ANY)],
            out_specs=pl.BlockSpec(memory_space=pl.ANY),
            scratch_shapes=[pltpu.SMEM((2, R), I32), pltpu.VMEM((R, W), src.dtype),
                            pltpu.SemaphoreType.DMA((2,)), pltpu.SemaphoreType.DMA((2,)),
                            pltpu.SemaphoreType.DMA]),
        compiler_params=pltpu.CompilerParams(dimension_semantics=("arbitrary",),
                                             has_side_effects=True),
        name="dispatch_gather",
    )(n_used, row_tok2d, src)


def _expert_up_kernel(be_ref, nused_ref, x_ref, wg_ref, wu_ref, o_ref, wgbf_ref, wubf_ref):
    b = pl.program_id(1)
    e = be_ref[b]
    e_prev = be_ref[jnp.maximum(b - 1, 0)]
    half = x_ref.shape[1]

    @pl.when(jnp.logical_or(b == 0, e != e_prev))
    def _():
        wgbf_ref[...] = wg_ref[...].astype(BF16)
        wubf_ref[...] = wu_ref[...].astype(BF16)

    @pl.when(b < nused_ref[0])
    def _():
        lo, hi = _unpack_halves(x_ref[...])
        lo = lo.astype(BF16)
        hi = hi.astype(BF16)
        g = (jnp.dot(lo, wgbf_ref[:half], preferred_element_type=F32)
             + jnp.dot(hi, wgbf_ref[half:], preferred_element_type=F32))
        u = (jnp.dot(lo, wubf_ref[:half], preferred_element_type=F32)
             + jnp.dot(hi, wubf_ref[half:], preferred_element_type=F32))
        o_ref[...] = (jax.nn.silu(g) * u).astype(o_ref.dtype)

    @pl.when(b >= nused_ref[0])
    def _():
        o_ref[...] = jnp.zeros_like(o_ref)


def _expert_up(xs, w_gu_e, blk_e, n_used, *, tn=384):
    P, half = xs.shape
    E, D, two_f = w_gu_e.shape
    f = two_f // 2
    assert f % tn == 0 and D == 2 * half
    nc = f // tn
    nb = P // MOE_BLOCK
    xmap = lambda c, b, be, nu: (jnp.minimum(b, nu[0] - 1), 0)
    return pl.pallas_call(
        _expert_up_kernel,
        out_shape=jax.ShapeDtypeStruct((P, f), BF16),
        grid_spec=pltpu.PrefetchScalarGridSpec(
            num_scalar_prefetch=2,
            grid=(nc, nb),
            in_specs=[pl.BlockSpec((MOE_BLOCK, half), xmap),
                      pl.BlockSpec((None, D, tn), lambda c, b, be, nu: (be[b], 0, c)),
                      pl.BlockSpec((None, D, tn), lambda c, b, be, nu: (be[b], 0, nc + c))],
            out_specs=pl.BlockSpec((MOE_BLOCK, tn), lambda c, b, be, nu: (b, c)),
            scratch_shapes=[pltpu.VMEM((D, tn), BF16), pltpu.VMEM((D, tn), BF16)]),
        compiler_params=_cparams(2, 56),
        name="expert_up",
    )(blk_e, n_used, xs, w_gu_e, w_gu_e)


def _expert_down_kernel(be_ref, nused_ref, h_ref, w_ref, o_ref, wbf_ref):
    b = pl.program_id(0)
    e = be_ref[b]
    e_prev = be_ref[jnp.maximum(b - 1, 0)]

    @pl.when(jnp.logical_or(b == 0, e != e_prev))
    def _():
        wbf_ref[...] = w_ref[...].astype(BF16)

    @pl.when(b < nused_ref[0])
    def _():
        y = jnp.dot(h_ref[...], wbf_ref[...], preferred_element_type=F32)
        half = y.shape[1] // 2
        o_ref[...] = _pack_halves(y[:, :half], y[:, half:])

    @pl.when(b >= nused_ref[0])
    def _():
        o_ref[...] = jnp.zeros_like(o_ref)


def _expert_down(hmid, w_down_e, blk_e, n_used):
    P, f = hmid.shape
    E, _, D = w_down_e.shape
    nb = P // MOE_BLOCK
    return pl.pallas_call(
        _expert_down_kernel,
        out_shape=jax.ShapeDtypeStruct((P, D // 2), U32),
        grid_spec=pltpu.PrefetchScalarGridSpec(
            num_scalar_prefetch=2,
            grid=(nb,),
            in_specs=[pl.BlockSpec((MOE_BLOCK, f), lambda b, be, nu: (jnp.minimum(b, nu[0] - 1), 0)),
                      pl.BlockSpec((None, f, D), lambda b, be, nu: (be[b], 0, 0))],
            out_specs=pl.BlockSpec((MOE_BLOCK, D // 2), lambda b, be, nu: (b, 0)),
            scratch_shapes=[pltpu.VMEM((f, D), BF16)]),
        compiler_params=_cparams(1, 56),
        name="expert_down",
    )(blk_e, n_used, hmid, w_down_e)


def _combine_kernel(dest_hbm, y_hbm, w_ref, o_ref, idx_smem, buf_ref, isem, gsem):
    i = pl.program_id(0)
    n = pl.num_programs(0)
    tc = o_ref.shape[0]
    half = buf_ref.shape[3]
    n_rows = TOP_K * tc

    def idx_copy(tile, slot):
        return pltpu.make_async_copy(dest_hbm.at[tile], idx_smem.at[slot], isem.at[slot])

    def row_copy(src_row, k, r, slot):
        return pltpu.make_async_copy(y_hbm.at[pl.ds(src_row, 1)],
                                     buf_ref.at[slot, k, pl.ds(r, 1)], gsem.at[slot])

    def issue_tile(slot):
        def body(r, carry):
            for k in range(TOP_K):
                row_copy(idx_smem[slot, k * tc + r], k, r, slot).start()
            return carry
        lax.fori_loop(0, tc, body, 0)

    slot = i % 2

    @pl.when(i == 0)
    def _():
        idx_copy(0, 0).start()
        idx_copy(0, 0).wait()
        issue_tile(0)

    @pl.when(i + 1 < n)
    def _():
        idx_copy(i + 1, 1 - slot).start()
        idx_copy(i + 1, 1 - slot).wait()
        issue_tile(1 - slot)

    def wait_body(r, carry):
        row_copy(0, 0, 0, slot).wait()
        return carry
    lax.fori_loop(0, n_rows, wait_body, 0)

    w = w_ref[...]
    acc_lo = jnp.zeros((tc, half), F32)
    acc_hi = jnp.zeros((tc, half), F32)
    for k in range(TOP_K):
        lo, hi = _unpack_halves(buf_ref[slot, k])
        wk = w[:, k:k + 1]
        acc_lo = acc_lo + wk * lo
        acc_hi = acc_hi + wk * hi
    o_ref[:, :half] = acc_lo
    o_ref[:, half:] = acc_hi


def _combine(y_packed, dest_tiles, w_col, tc):
    T = w_col.shape[0]
    half = y_packed.shape[1]
    nt = T // tc
    return pl.pallas_call(
        _combine_kernel,
        out_shape=jax.ShapeDtypeStruct((T, 2 * half), F32),
        grid=(nt,),
        in_specs=[pl.BlockSpec(memory_space=pl.ANY), pl.BlockSpec(memory_space=pl.ANY),
                  pl.BlockSpec((tc, TOP_K), lambda i: (i, 0))],
        out_specs=pl.BlockSpec((tc, 2 * half), lambda i: (i, 0)),
        scratch_shapes=[pltpu.SMEM((2, TOP_K * tc), I32),
                        pltpu.VMEM((2, TOP_K, tc, half), U32),
                        pltpu.SemaphoreType.DMA((2,)), pltpu.SemaphoreType.DMA((2,))],
        compiler_params=_cparams(1, 48),
        name="combine",
    )(dest_tiles, y_packed, w_col)


def _shared_up_kernel(h_ref, wg_ref, wu_ref, o_ref):
    h = h_ref[...].astype(BF16)
    g = jnp.dot(h, wg_ref[...], preferred_element_type=F32)
    u = jnp.dot(h, wu_ref[...], preferred_element_type=F32)
    o_ref[...] = (jax.nn.silu(g) * u).astype(o_ref.dtype)


def _shared_up(h1, w_gu_bf, *, tm=512, tn=384):
    T, D = h1.shape
    f = w_gu_bf.shape[1] // 2
    tm = min(tm, T)
    nc = f // tn
    return pl.pallas_call(
        _shared_up_kernel,
        out_shape=jax.ShapeDtypeStruct((T, f), BF16),
        grid=(T // tm, nc),
        in_specs=[pl.BlockSpec((tm, D), lambda i, c: (i, 0)),
                  pl.BlockSpec((D, tn), lambda i, c: (0, c)),
                  pl.BlockSpec((D, tn), lambda i, c: (0, nc + c))],
        out_specs=pl.BlockSpec((tm, tn), lambda i, c: (i, c)),
        compiler_params=_cparams(2, 56),
        name="shared_up",
    )(h1, w_gu_bf, w_gu_bf)


def _final_kernel(h_ref, hj_ref, r_ref, s_ref, p_ref, wd_ref, wg_ref, bg_ref, wp_ref,
                  g_ref, b_ref, o_ref, acc_ref, hbf_ref, *, alpha):
    j = pl.program_id(1)
    nj, _, tn = acc_ref.shape

    @pl.when(j == 0)
    def _():
        hbf_ref[...] = h_ref[...].astype(BF16)

    shared = jnp.dot(s_ref[...], wd_ref[...], preferred_element_type=F32)
    gate = jax.nn.sigmoid(jnp.dot(hbf_ref[...], wg_ref[...], preferred_element_type=F32)
                          + bg_ref[...])
    proj = jnp.dot(p_ref[...], wp_ref[...], preferred_element_type=F32)
    acc_ref[j] = alpha * hj_ref[...] + (r_ref[...] + shared) + gate * proj

    @pl.when(j == nj - 1)
    def _():
        chunk = _ln_chunked(acc_ref, g_ref, b_ref)
        for jj in range(nj):
            o_ref[:, jj * tn:(jj + 1) * tn] = chunk(jj)


def _final(h1, routed, s_mid, p_bf, wd_bf, wg_bf, bg, wp_bf, g, b, alpha, *, tm=256, tn=512):
    T, D = h1.shape
    f = s_mid.shape[1]
    dp = p_bf.shape[1]
    tm = min(tm, T)
    nj = D // tn
    return pl.pallas_call(
        functools.partial(_final_kernel, alpha=alpha),
        out_shape=jax.ShapeDtypeStruct((T, D), F32),
        grid=(T // tm, nj),
        in_specs=[pl.BlockSpec((tm, D), lambda i, j: (i, 0)),
                  pl.BlockSpec((tm, tn), lambda i, j: (i, j)),
                  pl.BlockSpec((tm, tn), lambda i, j: (i, j)),
                  pl.BlockSpec((tm, f), lambda i, j: (i, 0)),
                  pl.BlockSpec((tm, dp), lambda i, j: (i, 0)),
                  pl.BlockSpec((f, tn), lambda i, j: (0, j)),
                  pl.BlockSpec((D, tn), lambda i, j: (0, j)),
                  pl.BlockSpec((1, tn), lambda i, j: (0, j)),
                  pl.BlockSpec((dp, tn), lambda i, j: (0, j)),
                  pl.BlockSpec((1, D), lambda i, j: (0, 0)),
                  pl.BlockSpec((1, D), lambda i, j: (0, 0))],
        out_specs=pl.BlockSpec((tm, D), lambda i, j: (i, 0)),
        scratch_shapes=[pltpu.VMEM((nj, tm, tn), F32), pltpu.VMEM((tm, D), BF16)],
        compiler_params=_cparams(2, 60),
        name="final_ln2",
    )(h1, h1, routed, s_mid, p_bf, wd_bf, wg_bf, bg.reshape(1, D), wp_bf,
      g.reshape(1, D), b.reshape(1, D))


def _dispatch_tables(idx, rank, counts, T):
    E = counts.shape[0]
    A = T * TOP_K
    P = A + E * MOE_BLOCK
    nb = P // MOE_BLOCK
    padded = (counts + MOE_BLOCK - 1) // MOE_BLOCK * MOE_BLOCK
    pend = jnp.cumsum(padded)
    pstart = pend - padded
    dest = pstart[idx] + rank
    tok = jnp.broadcast_to(jnp.arange(T, dtype=I32)[None, :], (TOP_K, T))
    row_tok = jnp.zeros((P,), I32).at[dest.reshape(-1)].set(tok.reshape(-1))
    blk_e = jnp.minimum(jnp.searchsorted(pend, jnp.arange(nb, dtype=I32) * MOE_BLOCK, side="right"),
                        E - 1).astype(I32)
    n_used = (pend[-1] // MOE_BLOCK).astype(I32).reshape(1)
    return dest.astype(I32), row_tok.reshape(nb, MOE_BLOCK), blk_e, n_used


def _layer(h0_f, h0_bf, p_l, w_in, b_in, conv_w, conv_b, mh_norm_g, w_conv_out, w_mlstm_out,
           w_mix_out, ln1_g, ln1_b, w_router, router_bias, w_gu_e, w_down_e, w_gu_s, w_down_s,
           w_ple_gate, b_ple_gate, w_ple_proj, ln2_g, ln2_b, alpha, batch, seq):
    T, D = h0_f.shape
    d_conv = conv_w.shape[1]
    d_v = mh_norm_g.shape[0]
    d_qk = d_v // 2
    n_if = 2 * N_HEADS
    c_qk = 3 * d_conv
    c_if = c_qk + 2 * d_qk + 2 * d_v
    c_gate = c_if + n_if
    b2d = b_in.reshape(1, -1)

    ya_pre = _conv_branch(h0_bf, w_in, b2d, conv_w, conv_b.reshape(1, -1), seq, d_conv)
    qkvo = _proj(h0_bf, w_in, b2d, c_qk, 2 * d_qk + 2 * d_v)
    if_col, if_row = _if_gates(h0_bf, w_in[:, c_if:c_gate], b_in[c_if:c_gate])
    gates = _proj(h0_bf, w_in[:, c_gate:].astype(BF16), b2d[:, c_gate:], 0, 2 * D, act="sigmoid")
    yb_pre = _mlstm(qkvo, if_col, if_row, mh_norm_g.reshape(1, -1), batch, seq, d_qk, d_v)
    u = _merge(ya_pre, yb_pre, w_conv_out, w_mlstm_out, gates)
    h1, h1_packed = _mix_ln(u, w_mix_out.astype(BF16), h0_f, ln1_g, ln1_b, alpha)

    idx, wts, rank, counts = _router(h1, w_router, router_bias)
    dest, row_tok, blk_e, n_used = _dispatch_tables(idx, rank, counts[:, 0].astype(I32), T)
    xs = _gather_rows(h1_packed, row_tok, n_used)
    hmid = _expert_up(xs, w_gu_e, blk_e, n_used)
    y_packed = _expert_down(hmid, w_down_e, blk_e, n_used)
    tc = min(128, T)
    dest_tiles = dest.reshape(TOP_K, T // tc, tc).transpose(1, 0, 2).reshape(T // tc, TOP_K * tc)
    routed = _combine(y_packed, dest_tiles, wts.T, tc)

    s_mid = _shared_up(h1, w_gu_s.astype(BF16))
    return _final(h1, routed, s_mid, p_l.astype(BF16), w_down_s.astype(BF16),
                  w_ple_gate.astype(BF16), b_ple_gate, w_ple_proj.astype(BF16), ln2_g, ln2_b, alpha)


def kernel(x, p, ln_in_g, ln_in_b, w_in, b_in, conv_w, conv_b, mh_norm_g, w_conv_out, w_mlstm_out,
           w_mix_out, ln1_g, ln1_b, w_router, router_bias, w_gu_e, w_down_e, w_gu_s, w_down_s,
           w_ple_gate, b_ple_gate, w_ple_proj, ln2_g, ln2_b):
    B, S, D = x.shape
    depth = w_in.shape[0]
    alpha = (2 * depth) ** 0.25
    T = B * S
    h_f, h_bf = _ln_in(x.reshape(T, D), ln_in_g, ln_in_b)
    for l in range(depth):
        h_f = _layer(h_f, h_bf, p[l].reshape(T, -1), w_in[l], b_in[l], conv_w[l], conv_b[l],
                     mh_norm_g[l], w_conv_out[l], w_mlstm_out[l], w_mix_out[l], ln1_g[l], ln1_b[l],
                     w_router[l], router_bias[l], w_gu_e[l], w_down_e[l], w_gu_s[l], w_down_s[l],
                     w_ple_gate[l], b_ple_gate[l], w_ple_proj[l], ln2_g[l], ln2_b[l], alpha, B, S)
        if l + 1 < depth:
            h_bf = h_f.astype(BF16)
    return h_f.reshape(B, S, D)
```

```python
import functools

import jax
import jax.numpy as jnp
from jax import lax
from jax.experimental import pallas as pl
from jax.experimental.pallas import tpu as pltpu

F32 = jnp.float32
BF16 = jnp.bfloat16
U32 = jnp.uint32
I32 = jnp.int32

N_HEADS = 8
TOP_K = 8
N_GROUPS = 8
TOP_GROUPS = 4
ROUTE_SCALE = 2.5
MOE_BLOCK = 256
LN_EPS = 1e-5
CONV_W = 3
MLSTM_CHUNK = 256
ROUTE_TILE = 128
NEG_BIG = -1e30
HI_MASK = 0xFFFF0000
V7X_VMEM_BYTES = 64 * 1024 * 1024


def _cparams(n_axes, vmem_mib):
    assert vmem_mib * 1024 * 1024 <= V7X_VMEM_BYTES
    return pltpu.CompilerParams(dimension_semantics=("arbitrary",) * n_axes,
                                vmem_limit_bytes=vmem_mib * 1024 * 1024)


def _ln_rows(x, g, b):
    mu = jnp.mean(x, axis=-1, keepdims=True)
    xc = x - mu
    var = jnp.mean(xc * xc, axis=-1, keepdims=True)
    return xc * lax.rsqrt(var + LN_EPS) * g + b


def _pack_halves(lo, hi):
    lo = pltpu.bitcast(lo.astype(BF16).astype(F32), U32)
    hi = pltpu.bitcast(hi.astype(BF16).astype(F32), U32)
    return (hi & jnp.uint32(HI_MASK)) | (lo >> jnp.uint32(16))


def _ln_chunked(acc_ref, g_ref, b_ref):
    nj, _, tn = acc_ref.shape
    inv_d = 1.0 / (nj * tn)
    s = acc_ref[0].sum(axis=-1, keepdims=True)
    for jj in range(1, nj):
        s = s + acc_ref[jj].sum(axis=-1, keepdims=True)
    mu = s * inv_d
    v = jnp.zeros_like(mu)
    for jj in range(nj):
        d = acc_ref[jj] - mu
        v = v + (d * d).sum(axis=-1, keepdims=True)
    rstd = lax.rsqrt(v * inv_d + LN_EPS)

    def chunk(jj):
        cols = slice(jj * tn, (jj + 1) * tn)
        return (acc_ref[jj] - mu) * rstd * g_ref[:, cols] + b_ref[:, cols]
    return chunk


def _unpack_halves(w):
    lo = pltpu.bitcast(w << jnp.uint32(16), F32)
    hi = pltpu.bitcast(w & jnp.uint32(HI_MASK), F32)
    return lo, hi


def _ln_in_kernel(x_ref, g_ref, b_ref, of_ref, ob_ref):
    y = _ln_rows(x_ref[...], g_ref[...], b_ref[...])
    of_ref[...] = y
    ob_ref[...] = y.astype(BF16)


def _ln_in(x2, g, b, tm=256):
    T, D = x2.shape
    return pl.pallas_call(
        _ln_in_kernel,
        out_shape=(jax.ShapeDtypeStruct((T, D), F32), jax.ShapeDtypeStruct((T, D), BF16)),
        grid=(T // tm,),
        in_specs=[pl.BlockSpec((tm, D), lambda i: (i, 0)),
                  pl.BlockSpec((1, D), lambda i: (0, 0)),
                  pl.BlockSpec((1, D), lambda i: (0, 0))],
        out_specs=(pl.BlockSpec((tm, D), lambda i: (i, 0)),
                   pl.BlockSpec((tm, D), lambda i: (i, 0))),
        compiler_params=_cparams(1, 40),
        name="ln_in",
    )(x2, g.reshape(1, D), b.reshape(1, D))


def _proj_kernel(x_ref, w_ref, b_ref, o_ref, wbf_ref, *, act):
    @pl.when(pl.program_id(1) == 0)
    def _():
        wbf_ref[...] = w_ref[...].astype(BF16)

    acc = jnp.dot(x_ref[...], wbf_ref[...], preferred_element_type=F32) + b_ref[...]
    if act == "sigmoid":
        acc = jax.nn.sigmoid(acc)
    o_ref[...] = acc.astype(o_ref.dtype)


def _proj(x, w, bias2d, col0, n_cols, *, act=None, tm=1024, tn=512, out_dtype=BF16):
    T, K = x.shape
    tm = min(tm, T)
    assert col0 % tn == 0 and n_cols % tn == 0 and T % tm == 0
    jb = col0 // tn
    return pl.pallas_call(
        functools.partial(_proj_kernel, act=act),
        out_shape=jax.ShapeDtypeStruct((T, n_cols), out_dtype),
        grid=(n_cols // tn, T // tm),
        in_specs=[pl.BlockSpec((tm, K), lambda j, i: (i, 0)),
                  pl.BlockSpec((K, tn), lambda j, i: (0, jb + j)),
                  pl.BlockSpec((1, tn), lambda j, i: (0, jb + j))],
        out_specs=pl.BlockSpec((tm, tn), lambda j, i: (i, j)),
        scratch_shapes=[pltpu.VMEM((K, tn), BF16)],
        compiler_params=_cparams(2, 56),
        name="proj_" + (act or "lin"),
    )(x, w, bias2d)


def _conv_kernel(x_ref, wh_ref, wc_ref, wb_ref, bh_ref, bc_ref, bb_ref, cw_ref, cb_ref,
                 o_ref, wbf_ref, zprev_ref, *, tiles_per_seq):
    i = pl.program_id(1)

    @pl.when(i == 0)
    def _():
        wbf_ref[0] = wh_ref[...].astype(BF16)
        wbf_ref[1] = wc_ref[...].astype(BF16)
        wbf_ref[2] = wb_ref[...].astype(BF16)

    @pl.when(i % tiles_per_seq == 0)
    def _():
        zprev_ref[...] = jnp.zeros_like(zprev_ref)

    x = x_ref[...]
    ha = jnp.dot(x, wbf_ref[0], preferred_element_type=F32) + bh_ref[...]
    ca = jnp.dot(x, wbf_ref[1], preferred_element_type=F32) + bc_ref[...]
    ba = jnp.dot(x, wbf_ref[2], preferred_element_type=F32) + bb_ref[...]
    z = ca * ha
    tm = z.shape[0]
    prev = zprev_ref[...]
    row8 = lax.broadcasted_iota(I32, prev.shape, 0)
    z1 = pltpu.roll(z, 1, 0)
    z2 = pltpu.roll(z, 2, 0)
    p1 = pltpu.roll(prev, 1, 0)
    p2 = pltpu.roll(prev, 2, 0)
    z1 = jnp.concatenate([jnp.where(row8 < 1, p1, z1[:8]), z1[8:]], axis=0)
    z2 = jnp.concatenate([jnp.where(row8 < 2, p2, z2[:8]), z2[8:]], axis=0)
    cw = cw_ref[...]
    y = cw[0:1] * z2 + cw[1:2] * z1 + cw[2:3] * z + cb_ref[...]
    o_ref[...] = (ba * y).astype(o_ref.dtype)
    zprev_ref[...] = z[tm - 8:]


def _conv_branch(x, w_in, b2d, conv_w, conv_b2d, seq, d_conv, *, tm=512, tn=256):
    T, K = x.shape
    tm = min(tm, seq)
    assert seq % tm == 0 and d_conv % tn == 0 and tm % 8 == 0
    nb = d_conv // tn
    wspec = lambda g: pl.BlockSpec((K, tn), lambda j, i: (0, g * nb + j))
    bspec = lambda g: pl.BlockSpec((1, tn), lambda j, i: (0, g * nb + j))
    return pl.pallas_call(
        functools.partial(_conv_kernel, tiles_per_seq=seq // tm),
        out_shape=jax.ShapeDtypeStruct((T, d_conv), BF16),
        grid=(nb, T // tm),
        in_specs=[pl.BlockSpec((tm, K), lambda j, i: (i, 0)),
                  wspec(0), wspec(1), wspec(2), bspec(0), bspec(1), bspec(2),
                  pl.BlockSpec((CONV_W, tn), lambda j, i: (0, j)),
                  pl.BlockSpec((1, tn), lambda j, i: (0, j))],
        out_specs=pl.BlockSpec((tm, tn), lambda j, i: (i, j)),
        scratch_shapes=[pltpu.VMEM((3, K, tn), BF16), pltpu.VMEM((8, tn), F32)],
        compiler_params=_cparams(2, 56),
        name="conv_branch",
    )(x, w_in, w_in, w_in, b2d, b2d, b2d, conv_w, conv_b2d)


def _if_kernel(x_ref, w_ref, wt_ref, bc_ref, br_ref, oc_ref, or_ref):
    x = x_ref[...]
    oc_ref[...] = jnp.dot(x, w_ref[...], preferred_element_type=F32) + bc_ref[...]
    or_ref[...] = lax.dot_general(wt_ref[...], x, (((1,), (1,)), ((), ())),
                                  preferred_element_type=F32) + br_ref[...]


def _if_gates(x, w_if, b_if, tm=512):
    T, K = x.shape
    tm = min(tm, T)
    n = w_if.shape[1]
    w_pad = jnp.zeros((K, 128), BF16).at[:, :n].set(w_if.astype(BF16))
    b_pad = jnp.zeros((1, 128), F32).at[0, :n].set(b_if)
    return pl.pallas_call(
        _if_kernel,
        out_shape=(jax.ShapeDtypeStruct((T, 128), F32), jax.ShapeDtypeStruct((n, T), F32)),
        grid=(T // tm,),
        in_specs=[pl.BlockSpec((tm, K), lambda i: (i, 0)),
                  pl.BlockSpec((K, 128), lambda i: (0, 0)),
                  pl.BlockSpec((n, K), lambda i: (0, 0)),
                  pl.BlockSpec((1, 128), lambda i: (0, 0)),
                  pl.BlockSpec((n, 1), lambda i: (0, 0))],
        out_specs=(pl.BlockSpec((tm, 128), lambda i: (i, 0)),
                   pl.BlockSpec((n, tm), lambda i: (0, i))),
        compiler_params=_cparams(1, 32),
        name="if_gates",
    )(x, w_pad, w_if.T.astype(BF16), b_pad, b_if.reshape(n, 1))


def _mlstm_kernel(q_ref, k_ref, v_ref, o_ref, ifc_ref, ifr_ref, g_ref, y_ref, c_ref, m_ref,
                  *, dk, dv):
    H = N_HEADS
    L = q_ref.shape[0]

    @pl.when(pl.program_id(1) == 0)
    def _():
        c_ref[...] = jnp.zeros_like(c_ref)
        m_ref[...] = jnp.zeros_like(m_ref)

    scale = dk ** -0.5
    ifc = ifc_ref[...]
    ifr = ifr_ref[...]
    ig_c = ifc[:, 0:H]
    lf_c = jax.nn.log_sigmoid(ifc[:, H:2 * H])
    ig_r = ifr[0:H, :]
    lf_r = jax.nn.log_sigmoid(ifr[H:2 * H, :])
    r = lax.broadcasted_iota(I32, (L, L), 0)
    c = lax.broadcasted_iota(I32, (L, L), 1)
    causal = r >= c
    bcum_c = jnp.dot(causal.astype(F32), lf_c, preferred_element_type=F32,
                     precision=lax.Precision.HIGHEST)
    bcum_r = jnp.dot(lf_r, (r <= c).astype(F32), preferred_element_type=F32,
                     precision=lax.Precision.HIGHEST)
    d_c = ig_c - bcum_c
    d_r = ig_r - bcum_r
    ones_col = (lax.broadcasted_iota(I32, (L, 128), 1) == 0).astype(BF16)

    for h in range(H):
        q = q_ref[:, h * dk:(h + 1) * dk]
        k = k_ref[:, h * dk:(h + 1) * dk]
        v = v_ref[:, h * dv:(h + 1) * dv]
        v_aug = jnp.concatenate([v, ones_col], axis=1)
        bc = bcum_c[:, h:h + 1]
        m_prev = m_ref[h]
        c_prev = c_ref[h]

        qk = lax.dot_general(q, k, (((1,), (1,)), ((), ())), preferred_element_type=F32) * scale
        logd = jnp.where(causal, bc + d_r[h:h + 1, :], NEG_BIG)
        log_inter = bc + m_prev
        m_t = jnp.maximum(log_inter, jnp.max(logd, axis=1, keepdims=True))
        w_inter = jnp.exp(log_inter - m_t)
        s_mat = (qk * jnp.exp(logd - m_t)).astype(BF16)
        tot = (w_inter * jnp.dot(q, c_prev.astype(BF16), preferred_element_type=F32)
               + jnp.dot(s_mat, v_aug, preferred_element_type=F32))
        den = jnp.maximum(jnp.abs(tot[:, dv:dv + 1]), jnp.exp(-m_t))
        hh = tot[:, :dv] / den
        mu = jnp.mean(hh, axis=-1, keepdims=True)
        hc = hh - mu
        var = jnp.mean(hc * hc, axis=-1, keepdims=True)
        hn = hc * lax.rsqrt(var + LN_EPS) * g_ref[:, h * dv:(h + 1) * dv]
        og = jax.nn.sigmoid(o_ref[:, h * dv:(h + 1) * dv].astype(F32))
        y_ref[:, h * dv:(h + 1) * dv] = (og * hn).astype(y_ref.dtype)

        g_tot = bc[L - 1:L, :]
        a = g_tot + d_c[:, h:h + 1]
        m_new = jnp.maximum(g_tot + m_prev, jnp.max(a, axis=0, keepdims=True))
        kw = (k.astype(F32) * (jnp.exp(a - m_new) * scale)).astype(BF16)
        kv = lax.dot_general(kw, v_aug, (((0,), (0,)), ((), ())), preferred_element_type=F32)
        c_ref[h] = jnp.exp(g_tot + m_prev - m_new) * c_prev + kv
        m_ref[h] = m_new


def _mlstm(qkvo, if_col, if_row, mh_g2d, batch, seq, d_qk, d_v, L=MLSTM_CHUNK):
    T = qkvo.shape[0]
    L = min(L, seq)
    assert seq % L == 0 and d_v == 2 * d_qk
    nc = seq // L
    dk, dv = d_qk // N_HEADS, d_v // N_HEADS
    row = lambda b, c: b * nc + c
    return pl.pallas_call(
        functools.partial(_mlstm_kernel, dk=dk, dv=dv),
        out_shape=jax.ShapeDtypeStruct((T, d_v), BF16),
        grid=(batch, nc),
        in_specs=[pl.BlockSpec((L, d_qk), lambda b, c: (row(b, c), 0)),
                  pl.BlockSpec((L, d_qk), lambda b, c: (row(b, c), 1)),
                  pl.BlockSpec((L, d_v), lambda b, c: (row(b, c), 1)),
                  pl.BlockSpec((L, d_v), lambda b, c: (row(b, c), 2)),
                  pl.BlockSpec((L, 128), lambda b, c: (row(b, c), 0)),
                  pl.BlockSpec((2 * N_HEADS, L), lambda b, c: (0, row(b, c))),
                  pl.BlockSpec((1, d_v), lambda b, c: (0, 0))],
        out_specs=pl.BlockSpec((L, d_v), lambda b, c: (row(b, c), 0)),
        scratch_shapes=[pltpu.VMEM((N_HEADS, dk, dv + 128), F32),
                        pltpu.VMEM((N_HEADS, 1, 1), F32)],
        compiler_params=_cparams(2, 40),
        name="mlstm",
    )(qkvo, qkvo, qkvo, qkvo, if_col, if_row, mh_g2d)


def _merge_kernel(a_ref, b_ref, wa_ref, wb_ref, ga_ref, gb_ref, o_ref, wabf_ref, wbbf_ref):
    @pl.when(pl.program_id(1) == 0)
    def _():
        wabf_ref[...] = wa_ref[...].astype(BF16)
        wbbf_ref[...] = wb_ref[...].astype(BF16)

    ya = jnp.dot(a_ref[...], wabf_ref[...], preferred_element_type=F32)
    yb = jnp.dot(b_ref[...], wbbf_ref[...], preferred_element_type=F32)
    u = ga_ref[...].astype(F32) * ya + gb_ref[...].astype(F32) * yb
    o_ref[...] = u.astype(o_ref.dtype)


def _merge(ya_pre, yb_pre, w_a, w_b, gates, *, tm=1024, tn=512):
    T, K = ya_pre.shape
    D = w_a.shape[1]
    tm = min(tm, T)
    nj = D // tn
    return pl.pallas_call(
        _merge_kernel,
        out_shape=jax.ShapeDtypeStruct((T, D), BF16),
        grid=(nj, T // tm),
        in_specs=[pl.BlockSpec((tm, K), lambda j, i: (i, 0)),
                  pl.BlockSpec((tm, K), lambda j, i: (i, 0)),
                  pl.BlockSpec((K, tn), lambda j, i: (0, j)),
                  pl.BlockSpec((K, tn), lambda j, i: (0, j)),
                  pl.BlockSpec((tm, tn), lambda j, i: (i, j)),
                  pl.BlockSpec((tm, tn), lambda j, i: (i, nj + j))],
        out_specs=pl.BlockSpec((tm, tn), lambda j, i: (i, j)),
        scratch_shapes=[pltpu.VMEM((K, tn), BF16), pltpu.VMEM((K, tn), BF16)],
        compiler_params=_cparams(2, 56),
        name="merge",
    )(ya_pre, yb_pre, w_a, w_b, gates, gates)


def _mix_ln_kernel(u_ref, w_ref, h_ref, g_ref, b_ref, of_ref, op_ref, acc_ref, *, alpha):
    j = pl.program_id(1)
    nj, _, tn = acc_ref.shape
    acc_ref[j] = alpha * h_ref[...] + jnp.dot(u_ref[...], w_ref[...], preferred_element_type=F32)

    @pl.when(j == nj - 1)
    def _():
        chunk = _ln_chunked(acc_ref, g_ref, b_ref)
        for jj in range(nj // 2):
            lo, hi = chunk(jj), chunk(jj + nj // 2)
            of_ref[:, jj * tn:(jj + 1) * tn] = lo
            of_ref[:, (jj + nj // 2) * tn:(jj + nj // 2 + 1) * tn] = hi
            op_ref[:, jj * tn:(jj + 1) * tn] = _pack_halves(lo, hi)


def _mix_ln(u, w_bf, h0, g, b, alpha, *, tm=256, tn=512):
    T, D = h0.shape
    tm = min(tm, T)
    nj = D // tn
    return pl.pallas_call(
        functools.partial(_mix_ln_kernel, alpha=alpha),
        out_shape=(jax.ShapeDtypeStruct((T, D), F32), jax.ShapeDtypeStruct((T, D // 2), U32)),
        grid=(T // tm, nj),
        in_specs=[pl.BlockSpec((tm, D), lambda i, j: (i, 0)),
                  pl.BlockSpec((D, tn), lambda i, j: (0, j)),
                  pl.BlockSpec((tm, tn), lambda i, j: (i, j)),
                  pl.BlockSpec((1, D), lambda i, j: (0, 0)),
                  pl.BlockSpec((1, D), lambda i, j: (0, 0))],
        out_specs=(pl.BlockSpec((tm, D), lambda i, j: (i, 0)),
                   pl.BlockSpec((tm, D // 2), lambda i, j: (i, 0))),
        scratch_shapes=[pltpu.VMEM((nj, tm, tn), F32)],
        compiler_params=_cparams(2, 56),
        name="mix_ln1",
    )(u, w_bf, h0, g.reshape(1, D), b.reshape(1, D))


def _router_kernel(h_ref, wr_ref, rb_ref, idx_ref, rank_ref, wcol_ref, cnt_ref, carry_ref):
    i = pl.program_id(0)
    E = wr_ref.shape[0]
    tm = h_ref.shape[0]
    G, M = N_GROUPS, E // N_GROUPS

    @pl.when(i == 0)
    def _():
        carry_ref[...] = jnp.zeros_like(carry_ref)

    logits = lax.dot_general(wr_ref[...], h_ref[...], (((1,), (1,)), ((), ())),
                             preferred_element_type=F32, precision=lax.Precision.HIGHEST)
    scores = jax.nn.sigmoid(logits)
    scores3 = scores.reshape(G, M, tm)
    sel3 = (scores + rb_ref[...]).reshape(G, M, tm)
    midx = lax.broadcasted_iota(I32, (G, M, tm), 1)
    gidx3 = lax.broadcasted_iota(I32, (G, M, tm), 0)
    eidx = gidx3 * M + midx
    gidx = lax.broadcasted_iota(I32, (G, 1, tm), 0)
    neg_inf = -jnp.inf

    top1 = jnp.max(sel3, axis=1, keepdims=True)
    first1 = jnp.min(jnp.where(sel3 == top1, midx, M), axis=1, keepdims=True)
    top2 = jnp.max(jnp.where(midx == first1, neg_inf, sel3), axis=1, keepdims=True)
    gs = top1 + top2
    gkeep = jnp.zeros((G, 1, tm), F32)
    for _ in range(TOP_GROUPS):
        mx = jnp.max(gs, axis=0, keepdims=True)
        first = jnp.min(jnp.where(gs == mx, gidx, G), axis=0, keepdims=True)
        hit = gidx == first
        gkeep = jnp.where(hit, 1.0, gkeep)
        gs = jnp.where(hit, neg_inf, gs)
    selm = jnp.where(gkeep > 0.5, sel3, neg_inf)

    idx_rows, sc_rows = [], []
    chosen = jnp.zeros((G, M, tm), F32)
    for _ in range(TOP_K):
        mx = jnp.max(jnp.max(selm, axis=1, keepdims=True), axis=0, keepdims=True)
        first = jnp.min(jnp.min(jnp.where(selm == mx, eidx, E), axis=1, keepdims=True),
                        axis=0, keepdims=True)
        hit = eidx == first
        sc = jnp.sum(jnp.sum(jnp.where(hit, scores3, 0.0), axis=1, keepdims=True),
                     axis=0, keepdims=True)
        chosen = jnp.where(hit, 1.0, chosen)
        selm = jnp.where(hit, neg_inf, selm)
        idx_rows.append(first)
        sc_rows.append(sc)
    denom = sc_rows[0]
    for s in sc_rows[1:]:
        denom = denom + s

    tr = lax.broadcasted_iota(I32, (tm, tm), 0)
    tc = lax.broadcasted_iota(I32, (tm, tm), 1)
    before = (tr < tc).astype(BF16)
    chosen2 = chosen.reshape(E, tm)
    rank2 = jnp.dot(chosen2.astype(BF16), before, preferred_element_type=F32) + carry_ref[...]
    rank3 = rank2.reshape(G, M, tm)
    w_rows = []
    for k in range(TOP_K):
        hit = eidx == idx_rows[k]
        rk = jnp.sum(jnp.sum(jnp.where(hit, rank3, 0.0), axis=1, keepdims=True),
                     axis=0, keepdims=True)
        idx_ref[k:k + 1, :] = idx_rows[k].reshape(1, tm)
        rank_ref[k:k + 1, :] = rk.reshape(1, tm).astype(I32)
        w_rows.append((sc_rows[k] / denom * ROUTE_SCALE).reshape(1, tm))
    wcol_ref[...] = jnp.concatenate(w_rows, axis=0).T
    carry_ref[...] = carry_ref[...] + jnp.sum(chosen2, axis=1, keepdims=True)

    @pl.when(i == pl.num_programs(0) - 1)
    def _():
        cnt_ref[...] = carry_ref[...]


def _router(h1, w_router, router_bias, tm):
    T, D = h1.shape
    E = w_router.shape[1]
    nt = T // tm
    return pl.pallas_call(
        _router_kernel,
        out_shape=(jax.ShapeDtypeStruct((nt, TOP_K, tm), I32), jax.ShapeDtypeStruct((nt, TOP_K, tm), I32),
                   jax.ShapeDtypeStruct((T, TOP_K), F32), jax.ShapeDtypeStruct((E, 1), F32)),
        grid=(nt,),
        in_specs=[pl.BlockSpec((tm, D), lambda i: (i, 0)),
                  pl.BlockSpec((E, D), lambda i: (0, 0)),
                  pl.BlockSpec((E, 1), lambda i: (0, 0))],
        out_specs=(pl.BlockSpec((None, TOP_K, tm), lambda i: (i, 0, 0)),
                   pl.BlockSpec((None, TOP_K, tm), lambda i: (i, 0, 0)),
                   pl.BlockSpec((tm, TOP_K), lambda i: (i, 0)),
                   pl.BlockSpec((E, 1), lambda i: (0, 0))),
        scratch_shapes=[pltpu.VMEM((E, 1), F32)],
        compiler_params=_cparams(1, 32),
        name="router",
    )(h1, w_router.T, router_bias.reshape(E, 1))


def _scatter_rows_kernel(fill_ref, dest_hbm, x_ref, xs_hbm, idx_smem, zero_ref, isem, ssem, zsem):
    i = pl.program_id(0)
    n = pl.num_programs(0)
    tm = x_ref.shape[0]
    n_idx = TOP_K * tm
    R = zero_ref.shape[0]

    def idx_copy(tile, slot):
        return pltpu.make_async_copy(dest_hbm.at[tile], idx_smem.at[pl.ds(slot * n_idx, n_idx)],
                                     isem.at[slot])

    @pl.when(i == 0)
    def _():
        idx_copy(0, 0).start()
        zero_ref[...] = jnp.zeros_like(zero_ref)

        def fill(f):
            return pltpu.make_async_copy(zero_ref, xs_hbm.at[pl.ds(fill_ref[f] * R, R)], zsem)

        def start(f, carry):
            @pl.when(fill_ref[f] >= 0)
            def _():
                fill(f).start()
            return carry
        lax.fori_loop(0, fill_ref.shape[0], start, 0)

        def wait(f, carry):
            @pl.when(fill_ref[f] >= 0)
            def _():
                fill(f).wait()
            return carry
        lax.fori_loop(0, fill_ref.shape[0], wait, 0)

    slot = i % 2
    idx_copy(i, slot).wait()

    @pl.when(i + 1 < n)
    def _():
        idx_copy(i + 1, 1 - slot).start()

    base = slot * n_idx

    def issue(r8, carry):
        for s in range(8):
            r = r8 * 8 + s
            for k in range(TOP_K):
                d = idx_smem[base + k * tm + r]
                pltpu.make_async_copy(x_ref.at[pl.ds(r, 1)], xs_hbm.at[pl.ds(d, 1)], ssem).start()
        return carry
    lax.fori_loop(0, tm // 8, issue, 0)

    for k in range(TOP_K):
        pltpu.make_async_copy(x_ref, x_ref, ssem).wait()


def _scatter_rows(x_packed, dest_tiles, fill_blocks, n_rows):
    T, W = x_packed.shape
    nt, n_idx = dest_tiles.shape
    tm = n_idx // TOP_K
    return pl.pallas_call(
        _scatter_rows_kernel,
        out_shape=jax.ShapeDtypeStruct((n_rows, W), x_packed.dtype),
        grid_spec=pltpu.PrefetchScalarGridSpec(
            num_scalar_prefetch=1,
            grid=(nt,),
            in_specs=[pl.BlockSpec(memory_space=pl.ANY),
                      pl.BlockSpec((tm, W), lambda i, fb: (i, 0))],
            out_specs=pl.BlockSpec(memory_space=pl.ANY),
            scratch_shapes=[pltpu.SMEM((2 * n_idx,), I32), pltpu.VMEM((MOE_BLOCK, W), x_packed.dtype),
                            pltpu.SemaphoreType.DMA((2,)), pltpu.SemaphoreType.DMA,
                            pltpu.SemaphoreType.DMA]),
        compiler_params=pltpu.CompilerParams(dimension_semantics=("arbitrary",),
                                             vmem_limit_bytes=32 * 1024 * 1024,
                                             has_side_effects=True),
        name="dispatch_scatter",
    )(fill_blocks, dest_tiles, x_packed)


def _expert_up_kernel(be_ref, nused_ref, x_ref, wg_ref, wu_ref, o_ref, wgbf_ref, wubf_ref):
    b = pl.program_id(1)
    e = be_ref[b]
    e_prev = be_ref[jnp.maximum(b - 1, 0)]
    half = x_ref.shape[1]

    @pl.when(jnp.logical_or(b == 0, e != e_prev))
    def _():
        wgbf_ref[...] = wg_ref[...].astype(BF16)
        wubf_ref[...] = wu_ref[...].astype(BF16)

    @pl.when(b < nused_ref[0])
    def _():
        lo, hi = _unpack_halves(x_ref[...])
        lo = lo.astype(BF16)
        hi = hi.astype(BF16)
        g = (jnp.dot(lo, wgbf_ref[:half], preferred_element_type=F32)
             + jnp.dot(hi, wgbf_ref[half:], preferred_element_type=F32))
        u = (jnp.dot(lo, wubf_ref[:half], preferred_element_type=F32)
             + jnp.dot(hi, wubf_ref[half:], preferred_element_type=F32))
        o_ref[...] = (jax.nn.silu(g) * u).astype(o_ref.dtype)

    @pl.when(b >= nused_ref[0])
    def _():
        o_ref[...] = jnp.zeros_like(o_ref)


def _expert_up(xs, w_gu_e, blk_e, n_used, *, tn=384):
    P, half = xs.shape
    E, D, two_f = w_gu_e.shape
    f = two_f // 2
    assert f % tn == 0 and D == 2 * half
    nc = f // tn
    nb = P // MOE_BLOCK
    xmap = lambda c, b, be, nu: (jnp.minimum(b, nu[0] - 1), 0)
    return pl.pallas_call(
        _expert_up_kernel,
        out_shape=jax.ShapeDtypeStruct((P, f), BF16),
        grid_spec=pltpu.PrefetchScalarGridSpec(
            num_scalar_prefetch=2,
            grid=(nc, nb),
            in_specs=[pl.BlockSpec((MOE_BLOCK, half), xmap),
                      pl.BlockSpec((None, D, tn), lambda c, b, be, nu: (be[b], 0, c)),
                      pl.BlockSpec((None, D, tn), lambda c, b, be, nu: (be[b], 0, nc + c))],
            out_specs=pl.BlockSpec((MOE_BLOCK, tn), lambda c, b, be, nu: (b, c)),
            scratch_shapes=[pltpu.VMEM((D, tn), BF16), pltpu.VMEM((D, tn), BF16)]),
        compiler_params=_cparams(2, 56),
        name="expert_up",
    )(blk_e, n_used, xs, w_gu_e, w_gu_e)


def _expert_down_kernel(be_ref, nused_ref, h_ref, w_ref, o_ref, wbf_ref):
    b = pl.program_id(0)
    e = be_ref[b]
    e_prev = be_ref[jnp.maximum(b - 1, 0)]

    @pl.when(jnp.logical_or(b == 0, e != e_prev))
    def _():
        wbf_ref[...] = w_ref[...].astype(BF16)

    @pl.when(b < nused_ref[0])
    def _():
        y = jnp.dot(h_ref[...], wbf_ref[...], preferred_element_type=F32)
        half = y.shape[1] // 2
        o_ref[...] = _pack_halves(y[:, :half], y[:, half:])

    @pl.when(b >= nused_ref[0])
    def _():
        o_ref[...] = jnp.zeros_like(o_ref)


def _expert_down(hmid, w_down_e, blk_e, n_used):
    P, f = hmid.shape
    E, _, D = w_down_e.shape
    nb = P // MOE_BLOCK
    return pl.pallas_call(
        _expert_down_kernel,
        out_shape=jax.ShapeDtypeStruct((P, D // 2), U32),
        grid_spec=pltpu.PrefetchScalarGridSpec(
            num_scalar_prefetch=2,
            grid=(nb,),
            in_specs=[pl.BlockSpec((MOE_BLOCK, f), lambda b, be, nu: (jnp.minimum(b, nu[0] - 1), 0)),
                      pl.BlockSpec((None, f, D), lambda b, be, nu: (be[b], 0, 0))],
            out_specs=pl.BlockSpec((MOE_BLOCK, D // 2), lambda b, be, nu: (b, 0)),
            scratch_shapes=[pltpu.VMEM((f, D), BF16)]),
        compiler_params=_cparams(1, 56),
        name="expert_down",
    )(blk_e, n_used, hmid, w_down_e)


def _combine_kernel(dest_hbm, y_hbm, w_ref, o_ref, idx_smem, buf_ref, isem, gsem):
    i = pl.program_id(0)
    n = pl.num_programs(0)
    tc = o_ref.shape[0]
    half = buf_ref.shape[3]
    n_idx = TOP_K * tc

    def idx_copy(tile, slot):
        return pltpu.make_async_copy(dest_hbm.at[tile], idx_smem.at[pl.ds(slot * n_idx, n_idx)],
                                     isem.at[slot])

    def issue_tile(slot):
        base = slot * n_idx

        def body(r8, carry):
            for s in range(8):
                r = r8 * 8 + s
                for k in range(TOP_K):
                    d = idx_smem[base + k * tc + r]
                    pltpu.make_async_copy(y_hbm.at[pl.ds(d, 1)], buf_ref.at[slot, k, pl.ds(r, 1)],
                                          gsem.at[slot]).start()
            return carry
        lax.fori_loop(0, tc // 8, body, 0)

    slot = i % 2

    @pl.when(i == 0)
    def _():
        idx_copy(0, 0).start()
        idx_copy(0, 0).wait()
        issue_tile(0)

        @pl.when(n > 1)
        def _():
            idx_copy(1, 1).start()

    @pl.when(i + 1 < n)
    def _():
        idx_copy(i + 1, 1 - slot).wait()
        for sl in range(2):
            @pl.when(slot == 1 - sl)
            def _():
                issue_tile(sl)

    @pl.when(i + 2 < n)
    def _():
        idx_copy(i + 2, slot).start()

    for k in range(TOP_K):
        pltpu.make_async_copy(buf_ref.at[slot, k], buf_ref.at[slot, k], gsem.at[slot]).wait()

    w = w_ref[...]
    acc_lo = jnp.zeros((tc, half), F32)
    acc_hi = jnp.zeros((tc, half), F32)
    for k in range(TOP_K):
        lo, hi = _unpack_halves(buf_ref[slot, k])
        wk = w[:, k:k + 1]
        acc_lo = acc_lo + wk * lo
        acc_hi = acc_hi + wk * hi
    o_ref[:, :half] = acc_lo
    o_ref[:, half:] = acc_hi


def _combine(y_packed, dest_tiles, w_col):
    T = w_col.shape[0]
    half = y_packed.shape[1]
    nt, n_idx = dest_tiles.shape
    tc = n_idx // TOP_K
    return pl.pallas_call(
        _combine_kernel,
        out_shape=jax.ShapeDtypeStruct((T, 2 * half), F32),
        grid=(nt,),
        in_specs=[pl.BlockSpec(memory_space=pl.ANY), pl.BlockSpec(memory_space=pl.ANY),
                  pl.BlockSpec((tc, TOP_K), lambda i: (i, 0))],
        out_specs=pl.BlockSpec((tc, 2 * half), lambda i: (i, 0)),
        scratch_shapes=[pltpu.SMEM((2 * n_idx,), I32),
                        pltpu.VMEM((2, TOP_K, tc, half), U32),
                        pltpu.SemaphoreType.DMA((2,)), pltpu.SemaphoreType.DMA((2,))],
        compiler_params=_cparams(1, 48),
        name="combine",
    )(dest_tiles, y_packed, w_col)


def _shared_up_kernel(h_ref, wg_ref, wu_ref, o_ref):
    h = h_ref[...].astype(BF16)
    g = jnp.dot(h, wg_ref[...], preferred_element_type=F32)
    u = jnp.dot(h, wu_ref[...], preferred_element_type=F32)
    o_ref[...] = (jax.nn.silu(g) * u).astype(o_ref.dtype)


def _shared_up(h1, w_gu_bf, *, tm=512, tn=384):
    T, D = h1.shape
    f = w_gu_bf.shape[1] // 2
    tm = min(tm, T)
    nc = f // tn
    return pl.pallas_call(
        _shared_up_kernel,
        out_shape=jax.ShapeDtypeStruct((T, f), BF16),
        grid=(T // tm, nc),
        in_specs=[pl.BlockSpec((tm, D), lambda i, c: (i, 0)),
                  pl.BlockSpec((D, tn), lambda i, c: (0, c)),
                  pl.BlockSpec((D, tn), lambda i, c: (0, nc + c))],
        out_specs=pl.BlockSpec((tm, tn), lambda i, c: (i, c)),
        compiler_params=_cparams(2, 56),
        name="shared_up",
    )(h1, w_gu_bf, w_gu_bf)


def _final_kernel(h_ref, hj_ref, r_ref, s_ref, p_ref, wd_ref, wg_ref, bg_ref, wp_ref,
                  g_ref, b_ref, o_ref, acc_ref, hbf_ref, *, alpha):
    j = pl.program_id(1)
    nj, _, tn = acc_ref.shape

    @pl.when(j == 0)
    def _():
        hbf_ref[...] = h_ref[...].astype(BF16)

    shared = jnp.dot(s_ref[...], wd_ref[...], preferred_element_type=F32)
    gate = jax.nn.sigmoid(jnp.dot(hbf_ref[...], wg_ref[...], preferred_element_type=F32)
                          + bg_ref[...])
    proj = jnp.dot(p_ref[...], wp_ref[...], preferred_element_type=F32)
    acc_ref[j] = alpha * hj_ref[...] + (r_ref[...] + shared) + gate * proj

    @pl.when(j == nj - 1)
    def _():
        chunk = _ln_chunked(acc_ref, g_ref, b_ref)
        for jj in range(nj):
            o_ref[:, jj * tn:(jj + 1) * tn] = chunk(jj)


def _final(h1, routed, s_mid, p_bf, wd_bf, wg_bf, bg, wp_bf, g, b, alpha, *, tm=256, tn=512):
    T, D = h1.shape
    f = s_mid.shape[1]
    dp = p_bf.shape[1]
    tm = min(tm, T)
    nj = D // tn
    return pl.pallas_call(
        functools.partial(_final_kernel, alpha=alpha),
        out_shape=jax.ShapeDtypeStruct((T, D), F32),
        grid=(T // tm, nj),
        in_specs=[pl.BlockSpec((tm, D), lambda i, j: (i, 0)),
                  pl.BlockSpec((tm, tn), lambda i, j: (i, j)),
                  pl.BlockSpec((tm, tn), lambda i, j: (i, j)),
                  pl.BlockSpec((tm, f), lambda i, j: (i, 0)),
                  pl.BlockSpec((tm, dp), lambda i, j: (i, 0)),
                  pl.BlockSpec((f, tn), lambda i, j: (0, j)),
                  pl.BlockSpec((D, tn), lambda i, j: (0, j)),
                  pl.BlockSpec((1, tn), lambda i, j: (0, j)),
                  pl.BlockSpec((dp, tn), lambda i, j: (0, j)),
                  pl.BlockSpec((1, D), lambda i, j: (0, 0)),
                  pl.BlockSpec((1, D), lambda i, j: (0, 0))],
        out_specs=pl.BlockSpec((tm, D), lambda i, j: (i, 0)),
        scratch_shapes=[pltpu.VMEM((nj, tm, tn), F32), pltpu.VMEM((tm, D), BF16)],
        compiler_params=_cparams(2, 60),
        name="final_ln2",
    )(h1, h1, routed, s_mid, p_bf, wd_bf, wg_bf, bg.reshape(1, D), wp_bf,
      g.reshape(1, D), b.reshape(1, D))


def _dispatch_tables(idx, rank, counts):
    E = counts.shape[0]
    P = idx.size + E * MOE_BLOCK
    nb = P // MOE_BLOCK
    padded = (counts + MOE_BLOCK - 1) // MOE_BLOCK * MOE_BLOCK
    pend = jnp.cumsum(padded)
    pstart = pend - padded
    sel = idx[None] == jnp.arange(E, dtype=I32).reshape(E, 1, 1, 1)
    dest = rank + jnp.sum(jnp.where(sel, pstart.reshape(E, 1, 1, 1), 0), axis=0)
    blk_row = jnp.arange(nb, dtype=I32) * MOE_BLOCK
    blk_e = jnp.minimum(jnp.sum(pend[None, :] <= blk_row[:, None], axis=1), E - 1).astype(I32)
    n_used = pend[-1] // MOE_BLOCK
    last_blk = jnp.where(counts > 0, pend // MOE_BLOCK - 1, -1)
    tail_blk = n_used + jnp.arange(E, dtype=I32)
    tail_blk = jnp.where(tail_blk < nb, tail_blk, -1)
    fill = jnp.concatenate([last_blk, tail_blk]).astype(I32)
    return dest.astype(I32), blk_e, n_used.astype(I32).reshape(1), fill, P


def _layer(h0_f, h0_bf, p_l, w_in, b_in, conv_w, conv_b, mh_norm_g, w_conv_out, w_mlstm_out,
           w_mix_out, ln1_g, ln1_b, w_router, router_bias, w_gu_e, w_down_e, w_gu_s, w_down_s,
           w_ple_gate, b_ple_gate, w_ple_proj, ln2_g, ln2_b, alpha, batch, seq):
    T, D = h0_f.shape
    d_conv = conv_w.shape[1]
    d_v = mh_norm_g.shape[0]
    d_qk = d_v // 2
    n_if = 2 * N_HEADS
    c_qk = 3 * d_conv
    c_if = c_qk + 2 * d_qk + 2 * d_v
    c_gate = c_if + n_if
    b2d = b_in.reshape(1, -1)

    ya_pre = _conv_branch(h0_bf, w_in, b2d, conv_w, conv_b.reshape(1, -1), seq, d_conv)
    qkvo = _proj(h0_bf, w_in, b2d, c_qk, 2 * d_qk + 2 * d_v)
    if_col, if_row = _if_gates(h0_bf, w_in[:, c_if:c_gate], b_in[c_if:c_gate])
    gates = _proj(h0_bf, w_in[:, c_gate:].astype(BF16), b2d[:, c_gate:], 0, 2 * D, act="sigmoid")
    yb_pre = _mlstm(qkvo, if_col, if_row, mh_norm_g.reshape(1, -1), batch, seq, d_qk, d_v)
    u = _merge(ya_pre, yb_pre, w_conv_out, w_mlstm_out, gates)
    h1, h1_packed = _mix_ln(u, w_mix_out.astype(BF16), h0_f, ln1_g, ln1_b, alpha)

    rt = min(ROUTE_TILE, T)
    idx, rank, w_col, counts = _router(h1, w_router, router_bias, rt)
    dest, blk_e, n_used, fill, n_rows = _dispatch_tables(idx, rank, counts[:, 0].astype(I32))
    dest_tiles = dest.reshape(T // rt, TOP_K * rt)
    xs = _scatter_rows(h1_packed, dest_tiles, fill, n_rows)
    hmid = _expert_up(xs, w_gu_e, blk_e, n_used)
    y_packed = _expert_down(hmid, w_down_e, blk_e, n_used)
    routed = _combine(y_packed, dest_tiles, w_col)

    s_mid = _shared_up(h1, w_gu_s.astype(BF16))
    return _final(h1, routed, s_mid, p_l.astype(BF16), w_down_s.astype(BF16),
                  w_ple_gate.astype(BF16), b_ple_gate, w_ple_proj.astype(BF16), ln2_g, ln2_b, alpha)


def kernel(x, p, ln_in_g, ln_in_b, w_in, b_in, conv_w, conv_b, mh_norm_g, w_conv_out, w_mlstm_out,
           w_mix_out, ln1_g, ln1_b, w_router, router_bias, w_gu_e, w_down_e, w_gu_s, w_down_s,
           w_ple_gate, b_ple_gate, w_ple_proj, ln2_g, ln2_b):
    B, S, D = x.shape
    depth = w_in.shape[0]
    alpha = (2 * depth) ** 0.25
    T = B * S
    h_f, h_bf = _ln_in(x.reshape(T, D), ln_in_g, ln_in_b)
    for l in range(depth):
        h_f = _layer(h_f, h_bf, p[l].reshape(T, -1), w_in[l], b_in[l], conv_w[l], conv_b[l],
                     mh_norm_g[l], w_conv_out[l], w_mlstm_out[l], w_mix_out[l], ln1_g[l], ln1_b[l],
                     w_router[l], router_bias[l], w_gu_e[l], w_down_e[l], w_gu_s[l], w_down_s[l],
                     w_ple_gate[l], b_ple_gate[l], w_ple_proj[l], ln2_g[l], ln2_b[l], alpha, B, S)
        if l + 1 < depth:
            h_bf = h_f.astype(BF16)
    return h_f.reshape(B, S, D)
```

```python
import functools

import jax
import jax.numpy as jnp
from jax import lax
from jax.experimental import pallas as pl
from jax.experimental.pallas import tpu as pltpu

F32 = jnp.float32
BF16 = jnp.bfloat16
U32 = jnp.uint32
I32 = jnp.int32

N_HEADS = 8
TOP_K = 8
N_GROUPS = 8
TOP_GROUPS = 4
ROUTE_SCALE = 2.5
MOE_BLOCK = 256
LN_EPS = 1e-5
CONV_W = 3
MLSTM_CHUNK = 256
ROUTE_TILE = 128
NEG_BIG = -1e30
HI_MASK = 0xFFFF0000
V7X_VMEM_BYTES = 64 * 1024 * 1024


def _cparams(n_axes, vmem_mib):
    assert vmem_mib * 1024 * 1024 <= V7X_VMEM_BYTES
    return pltpu.CompilerParams(dimension_semantics=("arbitrary",) * n_axes,
                                vmem_limit_bytes=vmem_mib * 1024 * 1024)


def _ln_rows(x, g, b):
    mu = jnp.mean(x, axis=-1, keepdims=True)
    xc = x - mu
    var = jnp.mean(xc * xc, axis=-1, keepdims=True)
    return xc * lax.rsqrt(var + LN_EPS) * g + b


def _pack_halves(lo, hi):
    lo = pltpu.bitcast(lo.astype(BF16).astype(F32), U32)
    hi = pltpu.bitcast(hi.astype(BF16).astype(F32), U32)
    return (hi & jnp.uint32(HI_MASK)) | (lo >> jnp.uint32(16))


def _ln_chunked(acc_ref, g_ref, b_ref):
    nj, _, tn = acc_ref.shape
    inv_d = 1.0 / (nj * tn)
    s = acc_ref[0].sum(axis=-1, keepdims=True)
    for jj in range(1, nj):
        s = s + acc_ref[jj].sum(axis=-1, keepdims=True)
    mu = s * inv_d
    v = jnp.zeros_like(mu)
    for jj in range(nj):
        d = acc_ref[jj] - mu
        v = v + (d * d).sum(axis=-1, keepdims=True)
    rstd = lax.rsqrt(v * inv_d + LN_EPS)

    def chunk(jj):
        cols = slice(jj * tn, (jj + 1) * tn)
        return (acc_ref[jj] - mu) * rstd * g_ref[:, cols] + b_ref[:, cols]
    return chunk


def _unpack_halves(w):
    lo = pltpu.bitcast(w << jnp.uint32(16), F32)
    hi = pltpu.bitcast(w & jnp.uint32(HI_MASK), F32)
    return lo, hi


def _ln_in_kernel(x_ref, g_ref, b_ref, of_ref, ob_ref):
    y = _ln_rows(x_ref[...], g_ref[...], b_ref[...])
    of_ref[...] = y
    ob_ref[...] = y.astype(BF16)


def _ln_in(x2, g, b, tm=256):
    T, D = x2.shape
    return pl.pallas_call(
        _ln_in_kernel,
        out_shape=(jax.ShapeDtypeStruct((T, D), F32), jax.ShapeDtypeStruct((T, D), BF16)),
        grid=(T // tm,),
        in_specs=[pl.BlockSpec((tm, D), lambda i: (i, 0)),
                  pl.BlockSpec((1, D), lambda i: (0, 0)),
                  pl.BlockSpec((1, D), lambda i: (0, 0))],
        out_specs=(pl.BlockSpec((tm, D), lambda i: (i, 0)),
                   pl.BlockSpec((tm, D), lambda i: (i, 0))),
        compiler_params=_cparams(1, 40),
        name="ln_in",
    )(x2, g.reshape(1, D), b.reshape(1, D))


def _proj_kernel(x_ref, w_ref, b_ref, o_ref, wbf_ref, *, act):
    @pl.when(pl.program_id(1) == 0)
    def _():
        wbf_ref[...] = w_ref[...].astype(BF16)

    acc = jnp.dot(x_ref[...], wbf_ref[...], preferred_element_type=F32) + b_ref[...]
    if act == "sigmoid":
        acc = jax.nn.sigmoid(acc)
    o_ref[...] = acc.astype(o_ref.dtype)


def _proj(x, w, bias2d, col0, n_cols, *, act=None, tm=1024, tn=512, out_dtype=BF16):
    T, K = x.shape
    tm = min(tm, T)
    assert col0 % tn == 0 and n_cols % tn == 0 and T % tm == 0
    jb = col0 // tn
    return pl.pallas_call(
        functools.partial(_proj_kernel, act=act),
        out_shape=jax.ShapeDtypeStruct((T, n_cols), out_dtype),
        grid=(n_cols // tn, T // tm),
        in_specs=[pl.BlockSpec((tm, K), lambda j, i: (i, 0)),
                  pl.BlockSpec((K, tn), lambda j, i: (0, jb + j)),
                  pl.BlockSpec((1, tn), lambda j, i: (0, jb + j))],
        out_specs=pl.BlockSpec((tm, tn), lambda j, i: (i, j)),
        scratch_shapes=[pltpu.VMEM((K, tn), BF16)],
        compiler_params=_cparams(2, 56),
        name="proj_" + (act or "lin"),
    )(x, w, bias2d)


def _conv_kernel(x_ref, wh_ref, wc_ref, wb_ref, bh_ref, bc_ref, bb_ref, cw_ref, cb_ref,
                 o_ref, wbf_ref, zprev_ref, *, tiles_per_seq):
    i = pl.program_id(1)

    @pl.when(i == 0)
    def _():
        wbf_ref[0] = wh_ref[...].astype(BF16)
        wbf_ref[1] = wc_ref[...].astype(BF16)
        wbf_ref[2] = wb_ref[...].astype(BF16)

    @pl.when(i % tiles_per_seq == 0)
    def _():
        zprev_ref[...] = jnp.zeros_like(zprev_ref)

    x = x_ref[...]
    ha = jnp.dot(x, wbf_ref[0], preferred_element_type=F32) + bh_ref[...]
    ca = jnp.dot(x, wbf_ref[1], preferred_element_type=F32) + bc_ref[...]
    ba = jnp.dot(x, wbf_ref[2], preferred_element_type=F32) + bb_ref[...]
    z = ca * ha
    tm = z.shape[0]
    prev = zprev_ref[...]
    row8 = lax.broadcasted_iota(I32, prev.shape, 0)
    z1 = pltpu.roll(z, 1, 0)
    z2 = pltpu.roll(z, 2, 0)
    p1 = pltpu.roll(prev, 1, 0)
    p2 = pltpu.roll(prev, 2, 0)
    z1 = jnp.concatenate([jnp.where(row8 < 1, p1, z1[:8]), z1[8:]], axis=0)
    z2 = jnp.concatenate([jnp.where(row8 < 2, p2, z2[:8]), z2[8:]], axis=0)
    cw = cw_ref[...]
    y = cw[0:1] * z2 + cw[1:2] * z1 + cw[2:3] * z + cb_ref[...]
    o_ref[...] = (ba * y).astype(o_ref.dtype)
    zprev_ref[...] = z[tm - 8:]


def _conv_branch(x, w_in, b2d, conv_w, conv_b2d, seq, d_conv, *, tm=512, tn=256):
    T, K = x.shape
    tm = min(tm, seq)
    assert seq % tm == 0 and d_conv % tn == 0 and tm % 8 == 0
    nb = d_conv // tn
    wspec = lambda g: pl.BlockSpec((K, tn), lambda j, i: (0, g * nb + j))
    bspec = lambda g: pl.BlockSpec((1, tn), lambda j, i: (0, g * nb + j))
    return pl.pallas_call(
        functools.partial(_conv_kernel, tiles_per_seq=seq // tm),
        out_shape=jax.ShapeDtypeStruct((T, d_conv), BF16),
        grid=(nb, T // tm),
        in_specs=[pl.BlockSpec((tm, K), lambda j, i: (i, 0)),
                  wspec(0), wspec(1), wspec(2), bspec(0), bspec(1), bspec(2),
                  pl.BlockSpec((CONV_W, tn), lambda j, i: (0, j)),
                  pl.BlockSpec((1, tn), lambda j, i: (0, j))],
        out_specs=pl.BlockSpec((tm, tn), lambda j, i: (i, j)),
        scratch_shapes=[pltpu.VMEM((3, K, tn), BF16), pltpu.VMEM((8, tn), F32)],
        compiler_params=_cparams(2, 56),
        name="conv_branch",
    )(x, w_in, w_in, w_in, b2d, b2d, b2d, conv_w, conv_b2d)


def _if_kernel(x_ref, w_ref, wt_ref, bc_ref, br_ref, oc_ref, or_ref):
    x = x_ref[...]
    oc_ref[...] = jnp.dot(x, w_ref[...], preferred_element_type=F32) + bc_ref[...]
    or_ref[...] = lax.dot_general(wt_ref[...], x, (((1,), (1,)), ((), ())),
                                  preferred_element_type=F32) + br_ref[...]


def _if_gates(x, w_if, b_if, tm=512):
    T, K = x.shape
    tm = min(tm, T)
    n = w_if.shape[1]
    w_pad = jnp.zeros((K, 128), BF16).at[:, :n].set(w_if.astype(BF16))
    b_pad = jnp.zeros((1, 128), F32).at[0, :n].set(b_if)
    return pl.pallas_call(
        _if_kernel,
        out_shape=(jax.ShapeDtypeStruct((T, 128), F32), jax.ShapeDtypeStruct((n, T), F32)),
        grid=(T // tm,),
        in_specs=[pl.BlockSpec((tm, K), lambda i: (i, 0)),
                  pl.BlockSpec((K, 128), lambda i: (0, 0)),
                  pl.BlockSpec((n, K), lambda i: (0, 0)),
                  pl.BlockSpec((1, 128), lambda i: (0, 0)),
                  pl.BlockSpec((n, 1), lambda i: (0, 0))],
        out_specs=(pl.BlockSpec((tm, 128), lambda i: (i, 0)),
                   pl.BlockSpec((n, tm), lambda i: (0, i))),
        compiler_params=_cparams(1, 32),
        name="if_gates",
    )(x, w_pad, w_if.T.astype(BF16), b_pad, b_if.reshape(n, 1))


def _mlstm_kernel(q_ref, k_ref, v_ref, o_ref, ifc_ref, ifr_ref, g_ref, y_ref, c_ref, m_ref,
                  *, dk, dv):
    H = N_HEADS
    L = q_ref.shape[0]

    @pl.when(pl.program_id(1) == 0)
    def _():
        c_ref[...] = jnp.zeros_like(c_ref)
        m_ref[...] = jnp.zeros_like(m_ref)

    scale = dk ** -0.5
    ifc = ifc_ref[...]
    ifr = ifr_ref[...]
    ig_c = ifc[:, 0:H]
    lf_c = jax.nn.log_sigmoid(ifc[:, H:2 * H])
    ig_r = ifr[0:H, :]
    lf_r = jax.nn.log_sigmoid(ifr[H:2 * H, :])
    r = lax.broadcasted_iota(I32, (L, L), 0)
    c = lax.broadcasted_iota(I32, (L, L), 1)
    causal = r >= c
    bcum_c = jnp.dot(causal.astype(F32), lf_c, preferred_element_type=F32,
                     precision=lax.Precision.HIGHEST)
    bcum_r = jnp.dot(lf_r, (r <= c).astype(F32), preferred_element_type=F32,
                     precision=lax.Precision.HIGHEST)
    d_c = ig_c - bcum_c
    d_r = ig_r - bcum_r
    ones_col = (lax.broadcasted_iota(I32, (L, 128), 1) == 0).astype(BF16)

    for h in range(H):
        q = q_ref[:, h * dk:(h + 1) * dk]
        k = k_ref[:, h * dk:(h + 1) * dk]
        v = v_ref[:, h * dv:(h + 1) * dv]
        v_aug = jnp.concatenate([v, ones_col], axis=1)
        bc = bcum_c[:, h:h + 1]
        m_prev = m_ref[h]
        c_prev = c_ref[h]

        qk = lax.dot_general(q, k, (((1,), (1,)), ((), ())), preferred_element_type=F32) * scale
        logd = jnp.where(causal, bc + d_r[h:h + 1, :], NEG_BIG)
        log_inter = bc + m_prev
        m_t = jnp.maximum(log_inter, jnp.max(logd, axis=1, keepdims=True))
        w_inter = jnp.exp(log_inter - m_t)
        s_mat = (qk * jnp.exp(logd - m_t)).astype(BF16)
        tot = (w_inter * jnp.dot(q, c_prev.astype(BF16), preferred_element_type=F32)
               + jnp.dot(s_mat, v_aug, preferred_element_type=F32))
        den = jnp.maximum(jnp.abs(tot[:, dv:dv + 1]), jnp.exp(-m_t))
        hh = tot[:, :dv] / den
        mu = jnp.mean(hh, axis=-1, keepdims=True)
        hc = hh - mu
        var = jnp.mean(hc * hc, axis=-1, keepdims=True)
        hn = hc * lax.rsqrt(var + LN_EPS) * g_ref[:, h * dv:(h + 1) * dv]
        og = jax.nn.sigmoid(o_ref[:, h * dv:(h + 1) * dv].astype(F32))
        y_ref[:, h * dv:(h + 1) * dv] = (og * hn).astype(y_ref.dtype)

        g_tot = bc[L - 1:L, :]
        a = g_tot + d_c[:, h:h + 1]
        m_new = jnp.maximum(g_tot + m_prev, jnp.max(a, axis=0, keepdims=True))
        kw = (k.astype(F32) * (jnp.exp(a - m_new) * scale)).astype(BF16)
        kv = lax.dot_general(kw, v_aug, (((0,), (0,)), ((), ())), preferred_element_type=F32)
        c_ref[h] = jnp.exp(g_tot + m_prev - m_new) * c_prev + kv
        m_ref[h] = m_new


def _mlstm(qkvo, if_col, if_row, mh_g2d, batch, seq, d_qk, d_v, L=MLSTM_CHUNK):
    T = qkvo.shape[0]
    L = min(L, seq)
    assert seq % L == 0 and d_v == 2 * d_qk
    nc = seq // L
    dk, dv = d_qk // N_HEADS, d_v // N_HEADS
    row = lambda b, c: b * nc + c
    return pl.pallas_call(
        functools.partial(_mlstm_kernel, dk=dk, dv=dv),
        out_shape=jax.ShapeDtypeStruct((T, d_v), BF16),
        grid=(batch, nc),
        in_specs=[pl.BlockSpec((L, d_qk), lambda b, c: (row(b, c), 0)),
                  pl.BlockSpec((L, d_qk), lambda b, c: (row(b, c), 1)),
                  pl.BlockSpec((L, d_v), lambda b, c: (row(b, c), 1)),
                  pl.BlockSpec((L, d_v), lambda b, c: (row(b, c), 2)),
                  pl.BlockSpec((L, 128), lambda b, c: (row(b, c), 0)),
                  pl.BlockSpec((2 * N_HEADS, L), lambda b, c: (0, row(b, c))),
                  pl.BlockSpec((1, d_v), lambda b, c: (0, 0))],
        out_specs=pl.BlockSpec((L, d_v), lambda b, c: (row(b, c), 0)),
        scratch_shapes=[pltpu.VMEM((N_HEADS, dk, dv + 128), F32),
                        pltpu.VMEM((N_HEADS, 1, 1), F32)],
        compiler_params=_cparams(2, 40),
        name="mlstm",
    )(qkvo, qkvo, qkvo, qkvo, if_col, if_row, mh_g2d)


def _merge_kernel(a_ref, b_ref, wa_ref, wb_ref, ga_ref, gb_ref, o_ref, wabf_ref, wbbf_ref):
    @pl.when(pl.program_id(1) == 0)
    def _():
        wabf_ref[...] = wa_ref[...].astype(BF16)
        wbbf_ref[...] = wb_ref[...].astype(BF16)

    ya = jnp.dot(a_ref[...], wabf_ref[...], preferred_element_type=F32)
    yb = jnp.dot(b_ref[...], wbbf_ref[...], preferred_element_type=F32)
    u = ga_ref[...].astype(F32) * ya + gb_ref[...].astype(F32) * yb
    o_ref[...] = u.astype(o_ref.dtype)


def _merge(ya_pre, yb_pre, w_a, w_b, gates, *, tm=1024, tn=512):
    T, K = ya_pre.shape
    D = w_a.shape[1]
    tm = min(tm, T)
    nj = D // tn
    return pl.pallas_call(
        _merge_kernel,
        out_shape=jax.ShapeDtypeStruct((T, D), BF16),
        grid=(nj, T // tm),
        in_specs=[pl.BlockSpec((tm, K), lambda j, i: (i, 0)),
                  pl.BlockSpec((tm, K), lambda j, i: (i, 0)),
                  pl.BlockSpec((K, tn), lambda j, i: (0, j)),
                  pl.BlockSpec((K, tn), lambda j, i: (0, j)),
                  pl.BlockSpec((tm, tn), lambda j, i: (i, j)),
                  pl.BlockSpec((tm, tn), lambda j, i: (i, nj + j))],
        out_specs=pl.BlockSpec((tm, tn), lambda j, i: (i, j)),
        scratch_shapes=[pltpu.VMEM((K, tn), BF16), pltpu.VMEM((K, tn), BF16)],
        compiler_params=_cparams(2, 56),
        name="merge",
    )(ya_pre, yb_pre, w_a, w_b, gates, gates)


def _mix_kernel(u_ref, w_ref, h_ref, o_ref, wbf_ref, *, alpha):
    @pl.when(pl.program_id(1) == 0)
    def _():
        wbf_ref[...] = w_ref[...].astype(BF16)

    o_ref[...] = alpha * h_ref[...] + jnp.dot(u_ref[...], wbf_ref[...], preferred_element_type=F32)


def _mix(u, w, h0, alpha, *, tm=1024, tn=512):
    T, D = h0.shape
    tm = min(tm, T)
    return pl.pallas_call(
        functools.partial(_mix_kernel, alpha=alpha),
        out_shape=jax.ShapeDtypeStruct((T, D), F32),
        grid=(D // tn, T // tm),
        in_specs=[pl.BlockSpec((tm, D), lambda j, i: (i, 0)),
                  pl.BlockSpec((D, tn), lambda j, i: (0, j)),
                  pl.BlockSpec((tm, tn), lambda j, i: (i, j))],
        out_specs=pl.BlockSpec((tm, tn), lambda j, i: (i, j)),
        scratch_shapes=[pltpu.VMEM((D, tn), BF16)],
        compiler_params=_cparams(2, 56),
        name="mix_out",
    )(u, w, h0)


def _ln1_router_kernel(pre_ref, g_ref, b_ref, wr_ref, rb_ref, hf_ref, hb_ref, hp_ref,
                       idx_ref, rank_ref, wcol_ref, cnt_ref, carry_ref):
    i = pl.program_id(0)
    E = wr_ref.shape[0]
    tm, D = pre_ref.shape
    G, M = N_GROUPS, E // N_GROUPS

    @pl.when(i == 0)
    def _():
        carry_ref[...] = jnp.zeros_like(carry_ref)

    h = _ln_rows(pre_ref[...], g_ref[...], b_ref[...])
    hf_ref[...] = h
    hb_ref[...] = h.astype(BF16)
    hp_ref[...] = _pack_halves(h[:, :D // 2], h[:, D // 2:])

    logits = lax.dot_general(wr_ref[...], h, (((1,), (1,)), ((), ())),
                             preferred_element_type=F32, precision=lax.Precision.HIGHEST)
    scores = jax.nn.sigmoid(logits)
    scores3 = scores.reshape(G, M, tm)
    sel3 = (scores + rb_ref[...]).reshape(G, M, tm)
    midx = lax.broadcasted_iota(I32, (G, M, tm), 1)
    gidx3 = lax.broadcasted_iota(I32, (G, M, tm), 0)
    eidx = gidx3 * M + midx
    gidx = lax.broadcasted_iota(I32, (G, 1, tm), 0)
    neg_inf = -jnp.inf

    top1 = jnp.max(sel3, axis=1, keepdims=True)
    first1 = jnp.min(jnp.where(sel3 == top1, midx, M), axis=1, keepdims=True)
    top2 = jnp.max(jnp.where(midx == first1, neg_inf, sel3), axis=1, keepdims=True)
    gs = top1 + top2
    gkeep = jnp.zeros((G, 1, tm), F32)
    for _ in range(TOP_GROUPS):
        mx = jnp.max(gs, axis=0, keepdims=True)
        first = jnp.min(jnp.where(gs == mx, gidx, G), axis=0, keepdims=True)
        hit = gidx == first
        gkeep = jnp.where(hit, 1.0, gkeep)
        gs = jnp.where(hit, neg_inf, gs)
    selm = jnp.where(gkeep > 0.5, sel3, neg_inf)

    idx_rows, sc_rows = [], []
    chosen = jnp.zeros((G, M, tm), F32)
    for _ in range(TOP_K):
        mx = jnp.max(jnp.max(selm, axis=1, keepdims=True), axis=0, keepdims=True)
        first = jnp.min(jnp.min(jnp.where(selm == mx, eidx, E), axis=1, keepdims=True),
                        axis=0, keepdims=True)
        hit = eidx == first
        sc = jnp.sum(jnp.sum(jnp.where(hit, scores3, 0.0), axis=1, keepdims=True),
                     axis=0, keepdims=True)
        chosen = jnp.where(hit, 1.0, chosen)
        selm = jnp.where(hit, neg_inf, selm)
        idx_rows.append(first)
        sc_rows.append(sc)
    denom = sc_rows[0]
    for s in sc_rows[1:]:
        denom = denom + s

    tr = lax.broadcasted_iota(I32, (tm, tm), 0)
    tc = lax.broadcasted_iota(I32, (tm, tm), 1)
    before = (tr < tc).astype(BF16)
    chosen2 = chosen.reshape(E, tm)
    rank2 = jnp.dot(chosen2.astype(BF16), before, preferred_element_type=F32) + carry_ref[...]
    rank3 = rank2.reshape(G, M, tm)
    w_rows = []
    for k in range(TOP_K):
        hit = eidx == idx_rows[k]
        rk = jnp.sum(jnp.sum(jnp.where(hit, rank3, 0.0), axis=1, keepdims=True),
                     axis=0, keepdims=True)
        idx_ref[k:k + 1, :] = idx_rows[k].reshape(1, tm)
        rank_ref[k:k + 1, :] = rk.reshape(1, tm).astype(I32)
        w_rows.append((sc_rows[k] / denom * ROUTE_SCALE).reshape(1, tm))
    wcol_ref[...] = jnp.concatenate(w_rows, axis=0).T
    carry_ref[...] = carry_ref[...] + jnp.sum(chosen2, axis=1, keepdims=True)

    @pl.when(i == pl.num_programs(0) - 1)
    def _():
        cnt_ref[...] = carry_ref[...]


def _ln1_router(pre, g, b, w_router, router_bias, tm):
    T, D = pre.shape
    E = w_router.shape[1]
    nt = T // tm
    row = lambda i: (i, 0)
    fixed = lambda i: (0, 0)
    return pl.pallas_call(
        _ln1_router_kernel,
        out_shape=(jax.ShapeDtypeStruct((T, D), F32), jax.ShapeDtypeStruct((T, D), BF16),
                   jax.ShapeDtypeStruct((T, D // 2), U32),
                   jax.ShapeDtypeStruct((nt, TOP_K, tm), I32), jax.ShapeDtypeStruct((nt, TOP_K, tm), I32),
                   jax.ShapeDtypeStruct((T, TOP_K), F32), jax.ShapeDtypeStruct((E, 1), F32)),
        grid=(nt,),
        in_specs=[pl.BlockSpec((tm, D), row), pl.BlockSpec((1, D), fixed), pl.BlockSpec((1, D), fixed),
                  pl.BlockSpec((E, D), fixed), pl.BlockSpec((E, 1), fixed)],
        out_specs=(pl.BlockSpec((tm, D), row), pl.BlockSpec((tm, D), row), pl.BlockSpec((tm, D // 2), row),
                   pl.BlockSpec((None, TOP_K, tm), lambda i: (i, 0, 0)),
                   pl.BlockSpec((None, TOP_K, tm), lambda i: (i, 0, 0)),
                   pl.BlockSpec((tm, TOP_K), row),
                   pl.BlockSpec((E, 1), fixed)),
        scratch_shapes=[pltpu.VMEM((E, 1), F32)],
        compiler_params=_cparams(1, 40),
        name="ln1_router",
    )(pre, g.reshape(1, D), b.reshape(1, D), w_router.T, router_bias.reshape(E, 1))


def _scatter_rows_kernel(fill_ref, dest_hbm, x_ref, xs_hbm, idx_smem, zero_ref, isem, ssem, zsem):
    i = pl.program_id(0)
    n = pl.num_programs(0)
    tm = x_ref.shape[0]
    n_idx = TOP_K * tm
    R = zero_ref.shape[0]

    def idx_copy(tile, slot):
        return pltpu.make_async_copy(dest_hbm.at[tile], idx_smem.at[pl.ds(slot * n_idx, n_idx)],
                                     isem.at[slot])

    @pl.when(i == 0)
    def _():
        idx_copy(0, 0).start()
        zero_ref[...] = jnp.zeros_like(zero_ref)

        def fill(f):
            return pltpu.make_async_copy(zero_ref, xs_hbm.at[pl.ds(fill_ref[f] * R, R)], zsem)

        def start(f, carry):
            @pl.when(fill_ref[f] >= 0)
            def _():
                fill(f).start()
            return carry
        lax.fori_loop(0, fill_ref.shape[0], start, 0)

        def wait(f, carry):
            @pl.when(fill_ref[f] >= 0)
            def _():
                fill(f).wait()
            return carry
        lax.fori_loop(0, fill_ref.shape[0], wait, 0)

    slot = i % 2
    idx_copy(i, slot).wait()

    @pl.when(i + 1 < n)
    def _():
        idx_copy(i + 1, 1 - slot).start()

    base = slot * n_idx

    def issue(r8, carry):
        for s in range(8):
            r = r8 * 8 + s
            for k in range(TOP_K):
                d = idx_smem[base + k * tm + r]
                pltpu.make_async_copy(x_ref.at[pl.ds(r, 1)], xs_hbm.at[pl.ds(d, 1)], ssem).start()
        return carry
    lax.fori_loop(0, tm // 8, issue, 0)

    for k in range(TOP_K):
        pltpu.make_async_copy(x_ref, x_ref, ssem).wait()


def _scatter_rows(x_packed, dest_tiles, fill_blocks, n_rows):
    T, W = x_packed.shape
    nt, n_idx = dest_tiles.shape
    tm = n_idx // TOP_K
    return pl.pallas_call(
        _scatter_rows_kernel,
        out_shape=jax.ShapeDtypeStruct((n_rows, W), x_packed.dtype),
        grid_spec=pltpu.PrefetchScalarGridSpec(
            num_scalar_prefetch=1,
            grid=(nt,),
            in_specs=[pl.BlockSpec(memory_space=pl.ANY),
                      pl.BlockSpec((tm, W), lambda i, fb: (i, 0))],
            out_specs=pl.BlockSpec(memory_space=pl.ANY),
            scratch_shapes=[pltpu.SMEM((2 * n_idx,), I32), pltpu.VMEM((MOE_BLOCK, W), x_packed.dtype),
                            pltpu.SemaphoreType.DMA((2,)), pltpu.SemaphoreType.DMA,
                            pltpu.SemaphoreType.DMA]),
        compiler_params=pltpu.CompilerParams(dimension_semantics=("arbitrary",),
                                             vmem_limit_bytes=32 * 1024 * 1024,
                                             has_side_effects=True),
        name="dispatch_scatter",
    )(fill_blocks, dest_tiles, x_packed)


def _expert_up_kernel(be_ref, nused_ref, x_ref, wg_ref, wu_ref, o_ref, wbf_ref):
    b = pl.program_id(1)
    e = be_ref[b]
    e_prev = be_ref[jnp.maximum(b - 1, 0)]
    half = x_ref.shape[1]
    tn = wg_ref.shape[1]

    @pl.when(jnp.logical_or(b == 0, e != e_prev))
    def _():
        wbf_ref[:, :tn] = wg_ref[...].astype(BF16)
        wbf_ref[:, tn:] = wu_ref[...].astype(BF16)

    @pl.when(b < nused_ref[0])
    def _():
        lo, hi = _unpack_halves(x_ref[...])
        gu = (jnp.dot(lo.astype(BF16), wbf_ref[:half], preferred_element_type=F32)
              + jnp.dot(hi.astype(BF16), wbf_ref[half:], preferred_element_type=F32))
        o_ref[...] = (jax.nn.silu(gu[:, :tn]) * gu[:, tn:]).astype(o_ref.dtype)

    @pl.when(b >= nused_ref[0])
    def _():
        o_ref[...] = jnp.zeros_like(o_ref)


def _expert_up(xs, w_gu_e, blk_e, n_used, *, tn=384):
    P, half = xs.shape
    E, D, two_f = w_gu_e.shape
    f = two_f // 2
    assert f % tn == 0 and D == 2 * half
    nc = f // tn
    nb = P // MOE_BLOCK
    xmap = lambda c, b, be, nu: (jnp.minimum(b, nu[0] - 1), 0)
    return pl.pallas_call(
        _expert_up_kernel,
        out_shape=jax.ShapeDtypeStruct((P, f), BF16),
        grid_spec=pltpu.PrefetchScalarGridSpec(
            num_scalar_prefetch=2,
            grid=(nc, nb),
            in_specs=[pl.BlockSpec((MOE_BLOCK, half), xmap),
                      pl.BlockSpec((None, D, tn), lambda c, b, be, nu: (be[b], 0, c)),
                      pl.BlockSpec((None, D, tn), lambda c, b, be, nu: (be[b], 0, nc + c))],
            out_specs=pl.BlockSpec((MOE_BLOCK, tn), lambda c, b, be, nu: (b, c)),
            scratch_shapes=[pltpu.VMEM((D, 2 * tn), BF16)]),
        compiler_params=_cparams(2, 56),
        name="expert_up",
    )(blk_e, n_used, xs, w_gu_e, w_gu_e)


def _expert_down_kernel(be_ref, nused_ref, h_ref, w_ref, o_ref, wbf_ref):
    b = pl.program_id(0)
    e = be_ref[b]
    e_prev = be_ref[jnp.maximum(b - 1, 0)]

    @pl.when(jnp.logical_or(b == 0, e != e_prev))
    def _():
        wbf_ref[...] = w_ref[...].astype(BF16)

    @pl.when(b < nused_ref[0])
    def _():
        y = jnp.dot(h_ref[...], wbf_ref[...], preferred_element_type=F32)
        half = y.shape[1] // 2
        o_ref[...] = _pack_halves(y[:, :half], y[:, half:])

    @pl.when(b >= nused_ref[0])
    def _():
        o_ref[...] = jnp.zeros_like(o_ref)


def _expert_down(hmid, w_down_e, blk_e, n_used):
    P, f = hmid.shape
    E, _, D = w_down_e.shape
    nb = P // MOE_BLOCK
    return pl.pallas_call(
        _expert_down_kernel,
        out_shape=jax.ShapeDtypeStruct((P, D // 2), U32),
        grid_spec=pltpu.PrefetchScalarGridSpec(
            num_scalar_prefetch=2,
            grid=(nb,),
            in_specs=[pl.BlockSpec((MOE_BLOCK, f), lambda b, be, nu: (jnp.minimum(b, nu[0] - 1), 0)),
                      pl.BlockSpec((None, f, D), lambda b, be, nu: (be[b], 0, 0))],
            out_specs=pl.BlockSpec((MOE_BLOCK, D // 2), lambda b, be, nu: (b, 0)),
            scratch_shapes=[pltpu.VMEM((f, D), BF16)]),
        compiler_params=_cparams(1, 56),
        name="expert_down",
    )(blk_e, n_used, hmid, w_down_e)


def _combine_kernel(dest_hbm, y_hbm, w_ref, pre_ref, g_ref, b_ref, o_ref, idx_smem, buf_ref, isem, gsem):
    i = pl.program_id(0)
    n = pl.num_programs(0)
    tc = o_ref.shape[0]
    half = buf_ref.shape[3]
    n_idx = TOP_K * tc

    def idx_copy(tile, slot):
        return pltpu.make_async_copy(dest_hbm.at[tile], idx_smem.at[pl.ds(slot * n_idx, n_idx)],
                                     isem.at[slot])

    def issue_tile(slot):
        base = slot * n_idx

        def body(r8, carry):
            for s in range(8):
                r = r8 * 8 + s
                for k in range(TOP_K):
                    d = idx_smem[base + k * tc + r]
                    pltpu.make_async_copy(y_hbm.at[pl.ds(d, 1)], buf_ref.at[slot, k, pl.ds(r, 1)],
                                          gsem.at[slot]).start()
            return carry
        lax.fori_loop(0, tc // 8, body, 0)

    slot = i % 2

    @pl.when(i == 0)
    def _():
        idx_copy(0, 0).start()
        idx_copy(0, 0).wait()
        issue_tile(0)

        @pl.when(n > 1)
        def _():
            idx_copy(1, 1).start()

    @pl.when(i + 1 < n)
    def _():
        idx_copy(i + 1, 1 - slot).wait()
        for sl in range(2):
            @pl.when(slot == 1 - sl)
            def _():
                issue_tile(sl)

    @pl.when(i + 2 < n)
    def _():
        idx_copy(i + 2, slot).start()

    for k in range(TOP_K):
        pltpu.make_async_copy(buf_ref.at[slot, k], buf_ref.at[slot, k], gsem.at[slot]).wait()

    w = w_ref[...]
    acc_lo = jnp.zeros((tc, half), F32)
    acc_hi = jnp.zeros((tc, half), F32)
    for k in range(TOP_K):
        lo, hi = _unpack_halves(buf_ref[slot, k])
        wk = w[:, k:k + 1]
        acc_lo = acc_lo + wk * lo
        acc_hi = acc_hi + wk * hi
    y_lo = pre_ref[:, :half] + acc_lo
    y_hi = pre_ref[:, half:] + acc_hi
    inv_d = 1.0 / (2 * half)
    mu = (jnp.sum(y_lo, axis=-1, keepdims=True) + jnp.sum(y_hi, axis=-1, keepdims=True)) * inv_d
    c_lo = y_lo - mu
    c_hi = y_hi - mu
    var = (jnp.sum(c_lo * c_lo, axis=-1, keepdims=True)
           + jnp.sum(c_hi * c_hi, axis=-1, keepdims=True)) * inv_d
    rstd = lax.rsqrt(var + LN_EPS)
    o_ref[:, :half] = c_lo * rstd * g_ref[:, :half] + b_ref[:, :half]
    o_ref[:, half:] = c_hi * rstd * g_ref[:, half:] + b_ref[:, half:]


def _combine_ln(y_packed, dest_tiles, w_col, pre, g, b):
    T, D = pre.shape
    half = y_packed.shape[1]
    nt, n_idx = dest_tiles.shape
    tc = n_idx // TOP_K
    return pl.pallas_call(
        _combine_kernel,
        out_shape=jax.ShapeDtypeStruct((T, D), F32),
        grid=(nt,),
        in_specs=[pl.BlockSpec(memory_space=pl.ANY), pl.BlockSpec(memory_space=pl.ANY),
                  pl.BlockSpec((tc, TOP_K), lambda i: (i, 0)),
                  pl.BlockSpec((tc, D), lambda i: (i, 0)),
                  pl.BlockSpec((1, D), lambda i: (0, 0)),
                  pl.BlockSpec((1, D), lambda i: (0, 0))],
        out_specs=pl.BlockSpec((tc, D), lambda i: (i, 0)),
        scratch_shapes=[pltpu.SMEM((2 * n_idx,), I32),
                        pltpu.VMEM((2, TOP_K, tc, half), U32),
                        pltpu.SemaphoreType.DMA((2,)), pltpu.SemaphoreType.DMA((2,))],
        compiler_params=_cparams(1, 48),
        name="combine_ln2",
    )(dest_tiles, y_packed, w_col, pre, g.reshape(1, D), b.reshape(1, D))


def _glu_up_kernel(h_ref, wg_ref, wu_ref, o_ref, wbf_ref):
    tn = wg_ref.shape[1]

    @pl.when(pl.program_id(1) == 0)
    def _():
        wbf_ref[:, :tn] = wg_ref[...].astype(BF16)
        wbf_ref[:, tn:] = wu_ref[...].astype(BF16)

    gu = jnp.dot(h_ref[...], wbf_ref[...], preferred_element_type=F32)
    o_ref[...] = (jax.nn.silu(gu[:, :tn]) * gu[:, tn:]).astype(o_ref.dtype)


def _glu_up(h_bf, w_gu, *, tm=512, tn=384):
    T, D = h_bf.shape
    f = w_gu.shape[1] // 2
    tm = min(tm, T)
    nc = f // tn
    return pl.pallas_call(
        _glu_up_kernel,
        out_shape=jax.ShapeDtypeStruct((T, f), BF16),
        grid=(nc, T // tm),
        in_specs=[pl.BlockSpec((tm, D), lambda c, i: (i, 0)),
                  pl.BlockSpec((D, tn), lambda c, i: (0, c)),
                  pl.BlockSpec((D, tn), lambda c, i: (0, nc + c))],
        out_specs=pl.BlockSpec((tm, tn), lambda c, i: (i, c)),
        scratch_shapes=[pltpu.VMEM((D, 2 * tn), BF16)],
        compiler_params=_cparams(2, 56),
        name="shared_up",
    )(h_bf, w_gu, w_gu)


def _dense_tail_kernel(hb_ref, hj_ref, s_ref, p_ref, wd_ref, wg_ref, bg_ref, wp_ref, o_ref,
                       wdbf_ref, wgbf_ref, wpbf_ref, *, alpha):
    @pl.when(pl.program_id(1) == 0)
    def _():
        wdbf_ref[...] = wd_ref[...].astype(BF16)
        wgbf_ref[...] = wg_ref[...].astype(BF16)
        wpbf_ref[...] = wp_ref[...].astype(BF16)

    shared = jnp.dot(s_ref[...], wdbf_ref[...], preferred_element_type=F32)
    gate = jax.nn.sigmoid(jnp.dot(hb_ref[...], wgbf_ref[...], preferred_element_type=F32)
                          + bg_ref[...])
    proj = jnp.dot(p_ref[...].astype(BF16), wpbf_ref[...], preferred_element_type=F32)
    o_ref[...] = alpha * hj_ref[...] + shared + gate * proj


def _dense_tail(h1_bf, h1, s_mid, p, w_down_s, w_gate, b_gate, w_proj, alpha, *, tm=512, tn=512):
    T, D = h1.shape
    f = s_mid.shape[1]
    dp = p.shape[1]
    tm = min(tm, T)
    return pl.pallas_call(
        functools.partial(_dense_tail_kernel, alpha=alpha),
        out_shape=jax.ShapeDtypeStruct((T, D), F32),
        grid=(D // tn, T // tm),
        in_specs=[pl.BlockSpec((tm, D), lambda j, i: (i, 0)),
                  pl.BlockSpec((tm, tn), lambda j, i: (i, j)),
                  pl.BlockSpec((tm, f), lambda j, i: (i, 0)),
                  pl.BlockSpec((tm, dp), lambda j, i: (i, 0)),
                  pl.BlockSpec((f, tn), lambda j, i: (0, j)),
                  pl.BlockSpec((D, tn), lambda j, i: (0, j)),
                  pl.BlockSpec((1, tn), lambda j, i: (0, j)),
                  pl.BlockSpec((dp, tn), lambda j, i: (0, j))],
        out_specs=pl.BlockSpec((tm, tn), lambda j, i: (i, j)),
        scratch_shapes=[pltpu.VMEM((f, tn), BF16), pltpu.VMEM((D, tn), BF16), pltpu.VMEM((dp, tn), BF16)],
        compiler_params=_cparams(2, 56),
        name="dense_tail",
    )(h1_bf, h1, s_mid, p, w_down_s, w_gate, b_gate.reshape(1, D), w_proj)


def _dispatch_tables(idx, rank, counts):
    E = counts.shape[0]
    P = idx.size + E * MOE_BLOCK
    nb = P // MOE_BLOCK
    padded = (counts + MOE_BLOCK - 1) // MOE_BLOCK * MOE_BLOCK
    pend = jnp.cumsum(padded)
    pstart = pend - padded
    sel = idx[None] == jnp.arange(E, dtype=I32).reshape(E, 1, 1, 1)
    dest = rank + jnp.sum(jnp.where(sel, pstart.reshape(E, 1, 1, 1), 0), axis=0)
    blk_row = jnp.arange(nb, dtype=I32) * MOE_BLOCK
    blk_e = jnp.minimum(jnp.sum(pend[None, :] <= blk_row[:, None], axis=1), E - 1).astype(I32)
    n_used = pend[-1] // MOE_BLOCK
    last_blk = jnp.where(counts > 0, pend // MOE_BLOCK - 1, -1)
    tail_blk = n_used + jnp.arange(E, dtype=I32)
    tail_blk = jnp.where(tail_blk < nb, tail_blk, -1)
    fill = jnp.concatenate([last_blk, tail_blk]).astype(I32)
    return dest.astype(I32), blk_e, n_used.astype(I32).reshape(1), fill, P


def _layer(h0_f, h0_bf, p_l, w_in, b_in, conv_w, conv_b, mh_norm_g, w_conv_out, w_mlstm_out,
           w_mix_out, ln1_g, ln1_b, w_router, router_bias, w_gu_e, w_down_e, w_gu_s, w_down_s,
           w_ple_gate, b_ple_gate, w_ple_proj, ln2_g, ln2_b, alpha, batch, seq):
    T, D = h0_f.shape
    d_conv = conv_w.shape[1]
    d_v = mh_norm_g.shape[0]
    d_qk = d_v // 2
    n_if = 2 * N_HEADS
    c_qk = 3 * d_conv
    c_if = c_qk + 2 * d_qk + 2 * d_v
    c_gate = c_if + n_if
    b2d = b_in.reshape(1, -1)

    ya_pre = _conv_branch(h0_bf, w_in, b2d, conv_w, conv_b.reshape(1, -1), seq, d_conv)
    qkvo = _proj(h0_bf, w_in, b2d, c_qk, 2 * d_qk + 2 * d_v)
    if_col, if_row = _if_gates(h0_bf, w_in[:, c_if:c_gate], b_in[c_if:c_gate])
    gates = _proj(h0_bf, w_in[:, c_gate:].astype(BF16), b2d[:, c_gate:], 0, 2 * D, act="sigmoid")
    yb_pre = _mlstm(qkvo, if_col, if_row, mh_norm_g.reshape(1, -1), batch, seq, d_qk, d_v)
    u = _merge(ya_pre, yb_pre, w_conv_out, w_mlstm_out, gates)
    pre1 = _mix(u, w_mix_out, h0_f, alpha)

    rt = min(ROUTE_TILE, T)
    h1, h1_bf, h1_packed, idx, rank, w_col, counts = _ln1_router(pre1, ln1_g, ln1_b, w_router,
                                                                 router_bias, rt)
    dest, blk_e, n_used, fill, n_rows = _dispatch_tables(idx, rank, counts[:, 0].astype(I32))
    dest_tiles = dest.reshape(T // rt, TOP_K * rt)

    xs = _scatter_rows(h1_packed, dest_tiles, fill, n_rows)
    hmid = _expert_up(xs, w_gu_e, blk_e, n_used)
    y_packed = _expert_down(hmid, w_down_e, blk_e, n_used)

    s_mid = _glu_up(h1_bf, w_gu_s)
    pre2 = _dense_tail(h1_bf, h1, s_mid, p_l, w_down_s, w_ple_gate, b_ple_gate, w_ple_proj, alpha)
    return _combine_ln(y_packed, dest_tiles, w_col, pre2, ln2_g, ln2_b)


def kernel(x, p, ln_in_g, ln_in_b, w_in, b_in, conv_w, conv_b, mh_norm_g, w_conv_out, w_mlstm_out,
           w_mix_out, ln1_g, ln1_b, w_router, router_bias, w_gu_e, w_down_e, w_gu_s, w_down_s,
           w_ple_gate, b_ple_gate, w_ple_proj, ln2_g, ln2_b):
    B, S, D = x.shape
    depth = w_in.shape[0]
    alpha = (2 * depth) ** 0.25
    T = B * S
    h_f, h_bf = _ln_in(x.reshape(T, D), ln_in_g, ln_in_b)
    for l in range(depth):
        h_f = _layer(h_f, h_bf, p[l].reshape(T, -1), w_in[l], b_in[l], conv_w[l], conv_b[l],
                     mh_norm_g[l], w_conv_out[l], w_mlstm_out[l], w_mix_out[l], ln1_g[l], ln1_b[l],
                     w_router[l], router_bias[l], w_gu_e[l], w_down_e[l], w_gu_s[l], w_down_s[l],
                     w_ple_gate[l], b_ple_gate[l], w_ple_proj[l], ln2_g[l], ln2_b[l], alpha, B, S)
        if l + 1 < depth:
            h_bf = h_f.astype(BF16)
    return h_f.reshape(B, S, D)
```

```python
import functools

import jax
import jax.numpy as jnp
from jax import lax
from jax.experimental import pallas as pl
from jax.experimental.pallas import tpu as pltpu

F32 = jnp.float32
BF16 = jnp.bfloat16
U32 = jnp.uint32
I32 = jnp.int32

N_HEADS = 8
TOP_K = 8
N_GROUPS = 8
TOP_GROUPS = 4
ROUTE_SCALE = 2.5
MOE_BLOCK = 256
LN_EPS = 1e-5
CONV_W = 3
MLSTM_CHUNK = 256
ROUTE_TILE = 128
NEG_BIG = -1e30
HI_MASK = 0xFFFF0000
V7X_VMEM_BYTES = 64 * 1024 * 1024


def _cparams(n_axes, vmem_mib):
    assert vmem_mib * 1024 * 1024 <= V7X_VMEM_BYTES
    return pltpu.CompilerParams(dimension_semantics=("arbitrary",) * n_axes,
                                vmem_limit_bytes=vmem_mib * 1024 * 1024)


def _ln_rows(x, g, b):
    mu = jnp.mean(x, axis=-1, keepdims=True)
    xc = x - mu
    var = jnp.mean(xc * xc, axis=-1, keepdims=True)
    return xc * lax.rsqrt(var + LN_EPS) * g + b


def _pack_halves(lo, hi):
    lo = pltpu.bitcast(lo.astype(BF16).astype(F32), U32)
    hi = pltpu.bitcast(hi.astype(BF16).astype(F32), U32)
    return (hi & jnp.uint32(HI_MASK)) | (lo >> jnp.uint32(16))


def _ln_chunked(acc_ref, g_ref, b_ref):
    nj, _, tn = acc_ref.shape
    inv_d = 1.0 / (nj * tn)
    s = acc_ref[0].sum(axis=-1, keepdims=True)
    for jj in range(1, nj):
        s = s + acc_ref[jj].sum(axis=-1, keepdims=True)
    mu = s * inv_d
    v = jnp.zeros_like(mu)
    for jj in range(nj):
        d = acc_ref[jj] - mu
        v = v + (d * d).sum(axis=-1, keepdims=True)
    rstd = lax.rsqrt(v * inv_d + LN_EPS)

    def chunk(jj):
        cols = slice(jj * tn, (jj + 1) * tn)
        return (acc_ref[jj] - mu) * rstd * g_ref[:, cols] + b_ref[:, cols]
    return chunk


def _unpack_halves(w):
    lo = pltpu.bitcast(w << jnp.uint32(16), F32)
    hi = pltpu.bitcast(w & jnp.uint32(HI_MASK), F32)
    return lo, hi


def _ln_in_kernel(x_ref, g_ref, b_ref, of_ref, ob_ref):
    y = _ln_rows(x_ref[...], g_ref[...], b_ref[...])
    of_ref[...] = y
    ob_ref[...] = y.astype(BF16)


def _ln_in(x2, g, b, tm=256):
    T, D = x2.shape
    return pl.pallas_call(
        _ln_in_kernel,
        out_shape=(jax.ShapeDtypeStruct((T, D), F32), jax.ShapeDtypeStruct((T, D), BF16)),
        grid=(T // tm,),
        in_specs=[pl.BlockSpec((tm, D), lambda i: (i, 0)),
                  pl.BlockSpec((1, D), lambda i: (0, 0)),
                  pl.BlockSpec((1, D), lambda i: (0, 0))],
        out_specs=(pl.BlockSpec((tm, D), lambda i: (i, 0)),
                   pl.BlockSpec((tm, D), lambda i: (i, 0))),
        compiler_params=_cparams(1, 40),
        name="ln_in",
    )(x2, g.reshape(1, D), b.reshape(1, D))


def _proj_kernel(x_ref, w_ref, b_ref, o_ref, wbf_ref, *, act):
    @pl.when(pl.program_id(1) == 0)
    def _():
        wbf_ref[...] = w_ref[...].astype(BF16)

    acc = jnp.dot(x_ref[...], wbf_ref[...], preferred_element_type=F32) + b_ref[...]
    if act == "sigmoid":
        acc = jax.nn.sigmoid(acc)
    o_ref[...] = acc.astype(o_ref.dtype)


def _proj(x, w, bias2d, col0, n_cols, *, act=None, tm=1024, tn=512, out_dtype=BF16):
    T, K = x.shape
    tm = min(tm, T)
    assert col0 % tn == 0 and n_cols % tn == 0 and T % tm == 0
    jb = col0 // tn
    return pl.pallas_call(
        functools.partial(_proj_kernel, act=act),
        out_shape=jax.ShapeDtypeStruct((T, n_cols), out_dtype),
        grid=(n_cols // tn, T // tm),
        in_specs=[pl.BlockSpec((tm, K), lambda j, i: (i, 0)),
                  pl.BlockSpec((K, tn), lambda j, i: (0, jb + j)),
                  pl.BlockSpec((1, tn), lambda j, i: (0, jb + j))],
        out_specs=pl.BlockSpec((tm, tn), lambda j, i: (i, j)),
        scratch_shapes=[pltpu.VMEM((K, tn), BF16)],
        compiler_params=_cparams(2, 56),
        name="proj_" + (act or "lin"),
    )(x, w, bias2d)


def _conv_kernel(x_ref, wh_ref, wc_ref, wb_ref, bh_ref, bc_ref, bb_ref, cw_ref, cb_ref,
                 o_ref, wbf_ref, zprev_ref, *, tiles_per_seq):
    i = pl.program_id(1)

    @pl.when(i == 0)
    def _():
        wbf_ref[0] = wh_ref[...].astype(BF16)
        wbf_ref[1] = wc_ref[...].astype(BF16)
        wbf_ref[2] = wb_ref[...].astype(BF16)

    @pl.when(i % tiles_per_seq == 0)
    def _():
        zprev_ref[...] = jnp.zeros_like(zprev_ref)

    x = x_ref[...]
    ha = jnp.dot(x, wbf_ref[0], preferred_element_type=F32) + bh_ref[...]
    ca = jnp.dot(x, wbf_ref[1], preferred_element_type=F32) + bc_ref[...]
    ba = jnp.dot(x, wbf_ref[2], preferred_element_type=F32) + bb_ref[...]
    z = ca * ha
    tm = z.shape[0]
    prev = zprev_ref[...]
    row8 = lax.broadcasted_iota(I32, prev.shape, 0)
    z1 = pltpu.roll(z, 1, 0)
    z2 = pltpu.roll(z, 2, 0)
    p1 = pltpu.roll(prev, 1, 0)
    p2 = pltpu.roll(prev, 2, 0)
    z1 = jnp.concatenate([jnp.where(row8 < 1, p1, z1[:8]), z1[8:]], axis=0)
    z2 = jnp.concatenate([jnp.where(row8 < 2, p2, z2[:8]), z2[8:]], axis=0)
    cw = cw_ref[...]
    y = cw[0:1] * z2 + cw[1:2] * z1 + cw[2:3] * z + cb_ref[...]
    o_ref[...] = (ba * y).astype(o_ref.dtype)
    zprev_ref[...] = z[tm - 8:]


def _conv_branch(x, w_in, b2d, conv_w, conv_b2d, seq, d_conv, *, tm=512, tn=256):
    T, K = x.shape
    tm = min(tm, seq)
    assert seq % tm == 0 and d_conv % tn == 0 and tm % 8 == 0
    nb = d_conv // tn
    wspec = lambda g: pl.BlockSpec((K, tn), lambda j, i: (0, g * nb + j))
    bspec = lambda g: pl.BlockSpec((1, tn), lambda j, i: (0, g * nb + j))
    return pl.pallas_call(
        functools.partial(_conv_kernel, tiles_per_seq=seq // tm),
        out_shape=jax.ShapeDtypeStruct((T, d_conv), BF16),
        grid=(nb, T // tm),
        in_specs=[pl.BlockSpec((tm, K), lambda j, i: (i, 0)),
                  wspec(0), wspec(1), wspec(2), bspec(0), bspec(1), bspec(2),
                  pl.BlockSpec((CONV_W, tn), lambda j, i: (0, j)),
                  pl.BlockSpec((1, tn), lambda j, i: (0, j))],
        out_specs=pl.BlockSpec((tm, tn), lambda j, i: (i, j)),
        scratch_shapes=[pltpu.VMEM((3, K, tn), BF16), pltpu.VMEM((8, tn), F32)],
        compiler_params=_cparams(2, 56),
        name="conv_branch",
    )(x, w_in, w_in, w_in, b2d, b2d, b2d, conv_w, conv_b2d)


def _if_kernel(x_ref, w_ref, wt_ref, bc_ref, br_ref, oc_ref, or_ref):
    x = x_ref[...]
    oc_ref[...] = jnp.dot(x, w_ref[...], preferred_element_type=F32) + bc_ref[...]
    or_ref[...] = lax.dot_general(wt_ref[...], x, (((1,), (1,)), ((), ())),
                                  preferred_element_type=F32) + br_ref[...]


def _if_gates(x, w_if, b_if, tm=512):
    T, K = x.shape
    tm = min(tm, T)
    n = w_if.shape[1]
    w_pad = jnp.zeros((K, 128), BF16).at[:, :n].set(w_if.astype(BF16))
    b_pad = jnp.zeros((1, 128), F32).at[0, :n].set(b_if)
    return pl.pallas_call(
        _if_kernel,
        out_shape=(jax.ShapeDtypeStruct((T, 128), F32), jax.ShapeDtypeStruct((n, T), F32)),
        grid=(T // tm,),
        in_specs=[pl.BlockSpec((tm, K), lambda i: (i, 0)),
                  pl.BlockSpec((K, 128), lambda i: (0, 0)),
                  pl.BlockSpec((n, K), lambda i: (0, 0)),
                  pl.BlockSpec((1, 128), lambda i: (0, 0)),
                  pl.BlockSpec((n, 1), lambda i: (0, 0))],
        out_specs=(pl.BlockSpec((tm, 128), lambda i: (i, 0)),
                   pl.BlockSpec((n, tm), lambda i: (0, i))),
        compiler_params=_cparams(1, 32),
        name="if_gates",
    )(x, w_pad, w_if.T.astype(BF16), b_pad, b_if.reshape(n, 1))


def _mlstm_kernel(q_ref, k_ref, v_ref, o_ref, ifc_ref, ifr_ref, g_ref, y_ref, c_ref, m_ref,
                  *, dk, dv):
    H = N_HEADS
    L = q_ref.shape[0]

    @pl.when(pl.program_id(1) == 0)
    def _():
        c_ref[...] = jnp.zeros_like(c_ref)
        m_ref[...] = jnp.zeros_like(m_ref)

    scale = dk ** -0.5
    ifc = ifc_ref[...]
    ifr = ifr_ref[...]
    ig_c = ifc[:, 0:H]
    lf_c = jax.nn.log_sigmoid(ifc[:, H:2 * H])
    ig_r = ifr[0:H, :]
    lf_r = jax.nn.log_sigmoid(ifr[H:2 * H, :])
    r = lax.broadcasted_iota(I32, (L, L), 0)
    c = lax.broadcasted_iota(I32, (L, L), 1)
    causal = r >= c
    bcum_c = jnp.dot(causal.astype(F32), lf_c, preferred_element_type=F32,
                     precision=lax.Precision.HIGHEST)
    bcum_r = jnp.dot(lf_r, (r <= c).astype(F32), preferred_element_type=F32,
                     precision=lax.Precision.HIGHEST)
    d_c = ig_c - bcum_c
    d_r = ig_r - bcum_r
    ones_col = (lax.broadcasted_iota(I32, (L, 128), 1) == 0).astype(BF16)

    for h in range(H):
        q = q_ref[:, h * dk:(h + 1) * dk]
        k = k_ref[:, h * dk:(h + 1) * dk]
        v = v_ref[:, h * dv:(h + 1) * dv]
        v_aug = jnp.concatenate([v, ones_col], axis=1)
        bc = bcum_c[:, h:h + 1]
        m_prev = m_ref[h]
        c_prev = c_ref[h]

        qk = lax.dot_general(q, k, (((1,), (1,)), ((), ())), preferred_element_type=F32) * scale
        logd = jnp.where(causal, bc + d_r[h:h + 1, :], NEG_BIG)
        log_inter = bc + m_prev
        m_t = jnp.maximum(log_inter, jnp.max(logd, axis=1, keepdims=True))
        w_inter = jnp.exp(log_inter - m_t)
        s_mat = (qk * jnp.exp(logd - m_t)).astype(BF16)
        tot = (w_inter * jnp.dot(q, c_prev.astype(BF16), preferred_element_type=F32)
               + jnp.dot(s_mat, v_aug, preferred_element_type=F32))
        den = jnp.maximum(jnp.abs(tot[:, dv:dv + 1]), jnp.exp(-m_t))
        hh = tot[:, :dv] / den
        mu = jnp.mean(hh, axis=-1, keepdims=True)
        hc = hh - mu
        var = jnp.mean(hc * hc, axis=-1, keepdims=True)
        hn = hc * lax.rsqrt(var + LN_EPS) * g_ref[:, h * dv:(h + 1) * dv]
        og = jax.nn.sigmoid(o_ref[:, h * dv:(h + 1) * dv].astype(F32))
        y_ref[:, h * dv:(h + 1) * dv] = (og * hn).astype(y_ref.dtype)

        g_tot = bc[L - 1:L, :]
        a = g_tot + d_c[:, h:h + 1]
        m_new = jnp.maximum(g_tot + m_prev, jnp.max(a, axis=0, keepdims=True))
        kw = (k.astype(F32) * (jnp.exp(a - m_new) * scale)).astype(BF16)
        kv = lax.dot_general(kw, v_aug, (((0,), (0,)), ((), ())), preferred_element_type=F32)
        c_ref[h] = jnp.exp(g_tot + m_prev - m_new) * c_prev + kv
        m_ref[h] = m_new


def _mlstm(qkvo, if_col, if_row, mh_g2d, batch, seq, d_qk, d_v, L=MLSTM_CHUNK):
    T = qkvo.shape[0]
    L = min(L, seq)
    assert seq % L == 0 and d_v == 2 * d_qk
    nc = seq // L
    dk, dv = d_qk // N_HEADS, d_v // N_HEADS
    row = lambda b, c: b * nc + c
    return pl.pallas_call(
        functools.partial(_mlstm_kernel, dk=dk, dv=dv),
        out_shape=jax.ShapeDtypeStruct((T, d_v), BF16),
        grid=(batch, nc),
        in_specs=[pl.BlockSpec((L, d_qk), lambda b, c: (row(b, c), 0)),
                  pl.BlockSpec((L, d_qk), lambda b, c: (row(b, c), 1)),
                  pl.BlockSpec((L, d_v), lambda b, c: (row(b, c), 1)),
                  pl.BlockSpec((L, d_v), lambda b, c: (row(b, c), 2)),
                  pl.BlockSpec((L, 128), lambda b, c: (row(b, c), 0)),
                  pl.BlockSpec((2 * N_HEADS, L), lambda b, c: (0, row(b, c))),
                  pl.BlockSpec((1, d_v), lambda b, c: (0, 0))],
        out_specs=pl.BlockSpec((L, d_v), lambda b, c: (row(b, c), 0)),
        scratch_shapes=[pltpu.VMEM((N_HEADS, dk, dv + 128), F32),
                        pltpu.VMEM((N_HEADS, 1, 1), F32)],
        compiler_params=_cparams(2, 40),
        name="mlstm",
    )(qkvo, qkvo, qkvo, qkvo, if_col, if_row, mh_g2d)


def _merge_kernel(a_ref, b_ref, wa_ref, wb_ref, ga_ref, gb_ref, o_ref, wabf_ref, wbbf_ref):
    @pl.when(pl.program_id(1) == 0)
    def _():
        wabf_ref[...] = wa_ref[...].astype(BF16)
        wbbf_ref[...] = wb_ref[...].astype(BF16)

    ya = jnp.dot(a_ref[...], wabf_ref[...], preferred_element_type=F32)
    yb = jnp.dot(b_ref[...], wbbf_ref[...], preferred_element_type=F32)
    u = ga_ref[...].astype(F32) * ya + gb_ref[...].astype(F32) * yb
    o_ref[...] = u.astype(o_ref.dtype)


def _merge(ya_pre, yb_pre, w_a, w_b, gates, *, tm=1024, tn=512):
    T, K = ya_pre.shape
    D = w_a.shape[1]
    tm = min(tm, T)
    nj = D // tn
    return pl.pallas_call(
        _merge_kernel,
        out_shape=jax.ShapeDtypeStruct((T, D), BF16),
        grid=(nj, T // tm),
        in_specs=[pl.BlockSpec((tm, K), lambda j, i: (i, 0)),
                  pl.BlockSpec((tm, K), lambda j, i: (i, 0)),
                  pl.BlockSpec((K, tn), lambda j, i: (0, j)),
                  pl.BlockSpec((K, tn), lambda j, i: (0, j)),
                  pl.BlockSpec((tm, tn), lambda j, i: (i, j)),
                  pl.BlockSpec((tm, tn), lambda j, i: (i, nj + j))],
        out_specs=pl.BlockSpec((tm, tn), lambda j, i: (i, j)),
        scratch_shapes=[pltpu.VMEM((K, tn), BF16), pltpu.VMEM((K, tn), BF16)],
        compiler_params=_cparams(2, 56),
        name="merge",
    )(ya_pre, yb_pre, w_a, w_b, gates, gates)


def _mix_kernel(u_ref, w_ref, h_ref, o_ref, wbf_ref, *, alpha):
    @pl.when(pl.program_id(1) == 0)
    def _():
        wbf_ref[...] = w_ref[...].astype(BF16)

    o_ref[...] = alpha * h_ref[...] + jnp.dot(u_ref[...], wbf_ref[...], preferred_element_type=F32)


def _mix(u, w, h0, alpha, *, tm=1024, tn=512):
    T, D = h0.shape
    tm = min(tm, T)
    return pl.pallas_call(
        functools.partial(_mix_kernel, alpha=alpha),
        out_shape=jax.ShapeDtypeStruct((T, D), F32),
        grid=(D // tn, T // tm),
        in_specs=[pl.BlockSpec((tm, D), lambda j, i: (i, 0)),
                  pl.BlockSpec((D, tn), lambda j, i: (0, j)),
                  pl.BlockSpec((tm, tn), lambda j, i: (i, j))],
        out_specs=pl.BlockSpec((tm, tn), lambda j, i: (i, j)),
        scratch_shapes=[pltpu.VMEM((D, tn), BF16)],
        compiler_params=_cparams(2, 56),
        name="mix_out",
    )(u, w, h0)


def _ln1_router_kernel(pre_ref, g_ref, b_ref, wr_ref, rb_ref, hf_ref, hb_ref, hp_ref,
                       idx_ref, rank_ref, wcol_ref, cnt_ref, carry_ref):
    i = pl.program_id(0)
    E = wr_ref.shape[0]
    tm, D = pre_ref.shape
    G, M = N_GROUPS, E // N_GROUPS

    @pl.when(i == 0)
    def _():
        carry_ref[...] = jnp.zeros_like(carry_ref)

    h = _ln_rows(pre_ref[...], g_ref[...], b_ref[...])
    hf_ref[...] = h
    hb_ref[...] = h.astype(BF16)
    hp_ref[...] = _pack_halves(h[:, :D // 2], h[:, D // 2:])

    logits = lax.dot_general(wr_ref[...], h, (((1,), (1,)), ((), ())),
                             preferred_element_type=F32, precision=lax.Precision.HIGHEST)
    scores = jax.nn.sigmoid(logits)
    scores3 = scores.reshape(G, M, tm)
    sel3 = (scores + rb_ref[...]).reshape(G, M, tm)
    midx = lax.broadcasted_iota(I32, (G, M, tm), 1)
    gidx3 = lax.broadcasted_iota(I32, (G, M, tm), 0)
    eidx = gidx3 * M + midx
    gidx = lax.broadcasted_iota(I32, (G, 1, tm), 0)
    neg_inf = -jnp.inf

    top1 = jnp.max(sel3, axis=1, keepdims=True)
    first1 = jnp.min(jnp.where(sel3 == top1, midx, M), axis=1, keepdims=True)
    top2 = jnp.max(jnp.where(midx == first1, neg_inf, sel3), axis=1, keepdims=True)
    gs = top1 + top2
    gkeep = jnp.zeros((G, 1, tm), F32)
    for _ in range(TOP_GROUPS):
        mx = jnp.max(gs, axis=0, keepdims=True)
        first = jnp.min(jnp.where(gs == mx, gidx, G), axis=0, keepdims=True)
        hit = gidx == first
        gkeep = jnp.where(hit, 1.0, gkeep)
        gs = jnp.where(hit, neg_inf, gs)
    selm = jnp.where(gkeep > 0.5, sel3, neg_inf)

    idx_rows, sc_rows = [], []
    chosen = jnp.zeros((G, M, tm), F32)
    for _ in range(TOP_K):
        mx = jnp.max(jnp.max(selm, axis=1, keepdims=True), axis=0, keepdims=True)
        first = jnp.min(jnp.min(jnp.where(selm == mx, eidx, E), axis=1, keepdims=True),
                        axis=0, keepdims=True)
        hit = eidx == first
        sc = jnp.sum(jnp.sum(jnp.where(hit, scores3, 0.0), axis=1, keepdims=True),
                     axis=0, keepdims=True)
        chosen = jnp.where(hit, 1.0, chosen)
        selm = jnp.where(hit, neg_inf, selm)
        idx_rows.append(first)
        sc_rows.append(sc)
    denom = sc_rows[0]
    for s in sc_rows[1:]:
        denom = denom + s

    tr = lax.broadcasted_iota(I32, (tm, tm), 0)
    tc = lax.broadcasted_iota(I32, (tm, tm), 1)
    before = (tr < tc).astype(BF16)
    chosen2 = chosen.reshape(E, tm)
    rank2 = jnp.dot(chosen2.astype(BF16), before, preferred_element_type=F32) + carry_ref[...]
    rank3 = rank2.reshape(G, M, tm)
    w_rows = []
    for k in range(TOP_K):
        hit = eidx == idx_rows[k]
        rk = jnp.sum(jnp.sum(jnp.where(hit, rank3, 0.0), axis=1, keepdims=True),
                     axis=0, keepdims=True)
        idx_ref[k:k + 1, :] = idx_rows[k].reshape(1, tm)
        rank_ref[k:k + 1, :] = rk.reshape(1, tm).astype(I32)
        w_rows.append((sc_rows[k] / denom * ROUTE_SCALE).reshape(1, tm))
    wcol_ref[...] = jnp.concatenate(w_rows, axis=0).T
    carry_ref[...] = carry_ref[...] + jnp.sum(chosen2, axis=1, keepdims=True)

    @pl.when(i == pl.num_programs(0) - 1)
    def _():
        cnt_ref[...] = carry_ref[...]


def _ln1_router(pre, g, b, w_router, router_bias, tm):
    T, D = pre.shape
    E = w_router.shape[1]
    nt = T // tm
    row = lambda i: (i, 0)
    fixed = lambda i: (0, 0)
    return pl.pallas_call(
        _ln1_router_kernel,
        out_shape=(jax.ShapeDtypeStruct((T, D), F32), jax.ShapeDtypeStruct((T, D), BF16),
                   jax.ShapeDtypeStruct((T, D // 2), U32),
                   jax.ShapeDtypeStruct((nt, TOP_K, tm), I32), jax.ShapeDtypeStruct((nt, TOP_K, tm), I32),
                   jax.ShapeDtypeStruct((T, TOP_K), F32), jax.ShapeDtypeStruct((E, 1), F32)),
        grid=(nt,),
        in_specs=[pl.BlockSpec((tm, D), row), pl.BlockSpec((1, D), fixed), pl.BlockSpec((1, D), fixed),
                  pl.BlockSpec((E, D), fixed), pl.BlockSpec((E, 1), fixed)],
        out_specs=(pl.BlockSpec((tm, D), row), pl.BlockSpec((tm, D), row), pl.BlockSpec((tm, D // 2), row),
                   pl.BlockSpec((None, TOP_K, tm), lambda i: (i, 0, 0)),
                   pl.BlockSpec((None, TOP_K, tm), lambda i: (i, 0, 0)),
                   pl.BlockSpec((tm, TOP_K), row),
                   pl.BlockSpec((E, 1), fixed)),
        scratch_shapes=[pltpu.VMEM((E, 1), F32)],
        compiler_params=_cparams(1, 40),
        name="ln1_router",
    )(pre, g.reshape(1, D), b.reshape(1, D), w_router.T, router_bias.reshape(E, 1))


def _scatter_rows_kernel(fill_ref, dest_hbm, x_ref, xs_hbm, idx_smem, zero_ref, isem, ssem, zsem):
    i = pl.program_id(0)
    n = pl.num_programs(0)
    tm = x_ref.shape[0]
    n_idx = TOP_K * tm
    R = zero_ref.shape[0]

    def idx_copy(tile, slot):
        return pltpu.make_async_copy(dest_hbm.at[tile], idx_smem.at[pl.ds(slot * n_idx, n_idx)],
                                     isem.at[slot])

    @pl.when(i == 0)
    def _():
        idx_copy(0, 0).start()
        zero_ref[...] = jnp.zeros_like(zero_ref)

        def fill(f):
            return pltpu.make_async_copy(zero_ref, xs_hbm.at[pl.ds(fill_ref[f] * R, R)], zsem)

        def start(f, carry):
            @pl.when(fill_ref[f] >= 0)
            def _():
                fill(f).start()
            return carry
        lax.fori_loop(0, fill_ref.shape[0], start, 0)

        def wait(f, carry):
            @pl.when(fill_ref[f] >= 0)
            def _():
                fill(f).wait()
            return carry
        lax.fori_loop(0, fill_ref.shape[0], wait, 0)

    slot = i % 2
    idx_copy(i, slot).wait()

    @pl.when(i + 1 < n)
    def _():
        idx_copy(i + 1, 1 - slot).start()

    base = slot * n_idx

    def issue(r8, carry):
        for s in range(8):
            r = r8 * 8 + s
            for k in range(TOP_K):
                d = idx_smem[base + k * tm + r]
                pltpu.make_async_copy(x_ref.at[pl.ds(r, 1)], xs_hbm.at[pl.ds(d, 1)], ssem).start()
        return carry
    lax.fori_loop(0, tm // 8, issue, 0)

    for k in range(TOP_K):
        pltpu.make_async_copy(x_ref, x_ref, ssem).wait()


def _scatter_rows(x_packed, dest_tiles, fill_blocks, n_rows):
    T, W = x_packed.shape
    nt, n_idx = dest_tiles.shape
    tm = n_idx // TOP_K
    return pl.pallas_call(
        _scatter_rows_kernel,
        out_shape=jax.ShapeDtypeStruct((n_rows, W), x_packed.dtype),
        grid_spec=pltpu.PrefetchScalarGridSpec(
            num_scalar_prefetch=1,
            grid=(nt,),
            in_specs=[pl.BlockSpec(memory_space=pl.ANY),
                      pl.BlockSpec((tm, W), lambda i, fb: (i, 0))],
            out_specs=pl.BlockSpec(memory_space=pl.ANY),
            scratch_shapes=[pltpu.SMEM((2 * n_idx,), I32), pltpu.VMEM((MOE_BLOCK, W), x_packed.dtype),
                            pltpu.SemaphoreType.DMA((2,)), pltpu.SemaphoreType.DMA,
                            pltpu.SemaphoreType.DMA]),
        compiler_params=pltpu.CompilerParams(dimension_semantics=("arbitrary",),
                                             vmem_limit_bytes=32 * 1024 * 1024,
                                             has_side_effects=True),
        name="dispatch_scatter",
    )(fill_blocks, dest_tiles, x_packed)


def _expert_up_kernel(be_ref, nxt_ref, nused_ref, x_ref, w_hbm, o_ref, wbf_ref, stage_ref, run_ref,
                      wsem):
    c = pl.program_id(0)
    b = pl.program_id(1)
    nc = pl.num_programs(0)
    nb = pl.num_programs(1)
    n_used = nused_ref[0]
    half = x_ref.shape[1]
    tn = o_ref.shape[1]
    f = w_hbm.shape[2] // 2
    e = be_ref[b]
    run_start = jnp.logical_or(b == 0, e != be_ref[jnp.maximum(b - 1, 0)])

    def fetch(e_, c_, slot):
        col = pl.multiple_of(c_ * tn, 128)
        return (pltpu.make_async_copy(w_hbm.at[e_, :, pl.ds(col, tn)],
                                      stage_ref.at[slot, :, pl.ds(0, tn)], wsem.at[slot]),
                pltpu.make_async_copy(w_hbm.at[e_, :, pl.ds(f + col, tn)],
                                      stage_ref.at[slot, :, pl.ds(tn, tn)], wsem.at[slot]))

    @pl.when(jnp.logical_and(c == 0, b == 0))
    def _():
        run_ref[0] = 0
        for cp in fetch(e, c, 0):
            cp.start()

    @pl.when(jnp.logical_and(run_start, b < n_used))
    def _():
        k = run_ref[0]
        slot = k % 2
        for cp in fetch(e, c, slot):
            cp.wait()
        wbf_ref[...] = stage_ref[slot].astype(BF16)
        b_next = nxt_ref[b]
        more_here = b_next < n_used

        @pl.when(jnp.logical_or(more_here, c + 1 < nc))
        def _():
            e_next = jnp.where(more_here, be_ref[jnp.minimum(b_next, nb - 1)], be_ref[0])
            c_next = jnp.where(more_here, c, c + 1)
            for cp in fetch(e_next, c_next, 1 - slot):
                cp.start()
        run_ref[0] = k + 1

    @pl.when(b < n_used)
    def _():
        lo, hi = _unpack_halves(x_ref[...])
        gu = (jnp.dot(lo.astype(BF16), wbf_ref[:half], preferred_element_type=F32)
              + jnp.dot(hi.astype(BF16), wbf_ref[half:], preferred_element_type=F32))
        o_ref[...] = (jax.nn.silu(gu[:, :tn]) * gu[:, tn:]).astype(o_ref.dtype)

    @pl.when(b >= n_used)
    def _():
        o_ref[...] = jnp.zeros_like(o_ref)


def _expert_up(xs, w_gu_e, blk_e, nxt_blk, n_used, *, tn=384):
    P, half = xs.shape
    E, D, two_f = w_gu_e.shape
    f = two_f // 2
    assert f % tn == 0 and D == 2 * half and tn % 128 == 0
    nc = f // tn
    nb = P // MOE_BLOCK
    return pl.pallas_call(
        _expert_up_kernel,
        out_shape=jax.ShapeDtypeStruct((P, f), BF16),
        grid_spec=pltpu.PrefetchScalarGridSpec(
            num_scalar_prefetch=3,
            grid=(nc, nb),
            in_specs=[pl.BlockSpec((MOE_BLOCK, half), lambda c, b, be, nx, nu: (jnp.minimum(b, nu[0] - 1), 0)),
                      pl.BlockSpec(memory_space=pl.ANY)],
            out_specs=pl.BlockSpec((MOE_BLOCK, tn), lambda c, b, be, nx, nu: (b, c)),
            scratch_shapes=[pltpu.VMEM((D, 2 * tn), BF16), pltpu.VMEM((2, D, 2 * tn), F32),
                            pltpu.SMEM((1,), I32), pltpu.SemaphoreType.DMA((2,))]),
        compiler_params=_cparams(2, 56),
        name="expert_up",
    )(blk_e, nxt_blk, n_used, xs, w_gu_e)


def _expert_down_kernel(be_ref, nxt_ref, nused_ref, h_ref, w_hbm, o_ref, wbf_ref, stage_ref, run_ref,
                        wsem):
    b = pl.program_id(0)
    nb = pl.num_programs(0)
    n_used = nused_ref[0]
    e = be_ref[b]
    run_start = jnp.logical_or(b == 0, e != be_ref[jnp.maximum(b - 1, 0)])

    def fetch(e_, slot):
        return pltpu.make_async_copy(w_hbm.at[e_], stage_ref.at[slot], wsem.at[slot])

    @pl.when(b == 0)
    def _():
        run_ref[0] = 0
        fetch(e, 0).start()

    @pl.when(jnp.logical_and(run_start, b < n_used))
    def _():
        k = run_ref[0]
        slot = k % 2
        fetch(e, slot).wait()
        wbf_ref[...] = stage_ref[slot].astype(BF16)
        b_next = nxt_ref[b]

        @pl.when(b_next < n_used)
        def _():
            fetch(be_ref[jnp.minimum(b_next, nb - 1)], 1 - slot).start()
        run_ref[0] = k + 1

    @pl.when(b < n_used)
    def _():
        y = jnp.dot(h_ref[...], wbf_ref[...], preferred_element_type=F32)
        half = y.shape[1] // 2
        o_ref[...] = _pack_halves(y[:, :half], y[:, half:])

    @pl.when(b >= n_used)
    def _():
        o_ref[...] = jnp.zeros_like(o_ref)


def _expert_down(hmid, w_down_e, blk_e, nxt_blk, n_used):
    P, f = hmid.shape
    E, _, D = w_down_e.shape
    nb = P // MOE_BLOCK
    return pl.pallas_call(
        _expert_down_kernel,
        out_shape=jax.ShapeDtypeStruct((P, D // 2), U32),
        grid_spec=pltpu.PrefetchScalarGridSpec(
            num_scalar_prefetch=3,
            grid=(nb,),
            in_specs=[pl.BlockSpec((MOE_BLOCK, f), lambda b, be, nx, nu: (jnp.minimum(b, nu[0] - 1), 0)),
                      pl.BlockSpec(memory_space=pl.ANY)],
            out_specs=pl.BlockSpec((MOE_BLOCK, D // 2), lambda b, be, nx, nu: (b, 0)),
            scratch_shapes=[pltpu.VMEM((f, D), BF16), pltpu.VMEM((2, f, D), F32),
                            pltpu.SMEM((1,), I32), pltpu.SemaphoreType.DMA((2,))]),
        compiler_params=_cparams(1, 56),
        name="expert_down",
    )(blk_e, nxt_blk, n_used, hmid, w_down_e)


def _combine_kernel(dest_hbm, y_hbm, w_ref, pre_ref, g_ref, b_ref, o_ref, idx_smem, buf_ref, isem, gsem):
    i = pl.program_id(0)
    n = pl.num_programs(0)
    tc = o_ref.shape[0]
    half = buf_ref.shape[3]
    n_idx = TOP_K * tc

    def idx_copy(tile, slot):
        return pltpu.make_async_copy(dest_hbm.at[tile], idx_smem.at[pl.ds(slot * n_idx, n_idx)],
                                     isem.at[slot])

    def issue_tile(slot):
        base = slot * n_idx

        def body(r8, carry):
            for s in range(8):
                r = r8 * 8 + s
                for k in range(TOP_K):
                    d = idx_smem[base + k * tc + r]
                    pltpu.make_async_copy(y_hbm.at[pl.ds(d, 1)], buf_ref.at[slot, k, pl.ds(r, 1)],
                                          gsem.at[slot]).start()
            return carry
        lax.fori_loop(0, tc // 8, body, 0)

    slot = i % 2

    @pl.when(i == 0)
    def _():
        idx_copy(0, 0).start()
        idx_copy(0, 0).wait()
        issue_tile(0)

        @pl.when(n > 1)
        def _():
            idx_copy(1, 1).start()

    @pl.when(i + 1 < n)
    def _():
        idx_copy(i + 1, 1 - slot).wait()
        for sl in range(2):
            @pl.when(slot == 1 - sl)
            def _():
                issue_tile(sl)

    @pl.when(i + 2 < n)
    def _():
        idx_copy(i + 2, slot).start()

    for k in range(TOP_K):
        pltpu.make_async_copy(buf_ref.at[slot, k], buf_ref.at[slot, k], gsem.at[slot]).wait()

    w = w_ref[...]
    acc_lo = jnp.zeros((tc, half), F32)
    acc_hi = jnp.zeros((tc, half), F32)
    for k in range(TOP_K):
        lo, hi = _unpack_halves(buf_ref[slot, k])
        wk = w[:, k:k + 1]
        acc_lo = acc_lo + wk * lo
        acc_hi = acc_hi + wk * hi
    y_lo = pre_ref[:, :half] + acc_lo
    y_hi = pre_ref[:, half:] + acc_hi
    inv_d = 1.0 / (2 * half)
    mu = (jnp.sum(y_lo, axis=-1, keepdims=True) + jnp.sum(y_hi, axis=-1, keepdims=True)) * inv_d
    c_lo = y_lo - mu
    c_hi = y_hi - mu
    var = (jnp.sum(c_lo * c_lo, axis=-1, keepdims=True)
           + jnp.sum(c_hi * c_hi, axis=-1, keepdims=True)) * inv_d
    rstd = lax.rsqrt(var + LN_EPS)
    o_ref[:, :half] = c_lo * rstd * g_ref[:, :half] + b_ref[:, :half]
    o_ref[:, half:] = c_hi * rstd * g_ref[:, half:] + b_ref[:, half:]


def _combine_ln(y_packed, dest_tiles, w_col, pre, g, b):
    T, D = pre.shape
    half = y_packed.shape[1]
    nt, n_idx = dest_tiles.shape
    tc = n_idx // TOP_K
    return pl.pallas_call(
        _combine_kernel,
        out_shape=jax.ShapeDtypeStruct((T, D), F32),
        grid=(nt,),
        in_specs=[pl.BlockSpec(memory_space=pl.ANY), pl.BlockSpec(memory_space=pl.ANY),
                  pl.BlockSpec((tc, TOP_K), lambda i: (i, 0)),
                  pl.BlockSpec((tc, D), lambda i: (i, 0)),
                  pl.BlockSpec((1, D), lambda i: (0, 0)),
                  pl.BlockSpec((1, D), lambda i: (0, 0))],
        out_specs=pl.BlockSpec((tc, D), lambda i: (i, 0)),
        scratch_shapes=[pltpu.SMEM((2 * n_idx,), I32),
                        pltpu.VMEM((2, TOP_K, tc, half), U32),
                        pltpu.SemaphoreType.DMA((2,)), pltpu.SemaphoreType.DMA((2,))],
        compiler_params=_cparams(1, 48),
        name="combine_ln2",
    )(dest_tiles, y_packed, w_col, pre, g.reshape(1, D), b.reshape(1, D))


def _glu_up_kernel(h_ref, wg_ref, wu_ref, o_ref, wbf_ref):
    tn = wg_ref.shape[1]

    @pl.when(pl.program_id(1) == 0)
    def _():
        wbf_ref[:, :tn] = wg_ref[...].astype(BF16)
        wbf_ref[:, tn:] = wu_ref[...].astype(BF16)

    gu = jnp.dot(h_ref[...], wbf_ref[...], preferred_element_type=F32)
    o_ref[...] = (jax.nn.silu(gu[:, :tn]) * gu[:, tn:]).astype(o_ref.dtype)


def _glu_up(h_bf, w_gu, *, tm=512, tn=384):
    T, D = h_bf.shape
    f = w_gu.shape[1] // 2
    tm = min(tm, T)
    nc = f // tn
    return pl.pallas_call(
        _glu_up_kernel,
        out_shape=jax.ShapeDtypeStruct((T, f), BF16),
        grid=(nc, T // tm),
        in_specs=[pl.BlockSpec((tm, D), lambda c, i: (i, 0)),
                  pl.BlockSpec((D, tn), lambda c, i: (0, c)),
                  pl.BlockSpec((D, tn), lambda c, i: (0, nc + c))],
        out_specs=pl.BlockSpec((tm, tn), lambda c, i: (i, c)),
        scratch_shapes=[pltpu.VMEM((D, 2 * tn), BF16)],
        compiler_params=_cparams(2, 56),
        name="shared_up",
    )(h_bf, w_gu, w_gu)


def _dense_tail_kernel(hb_ref, hj_ref, s_ref, p_ref, wd_ref, wg_ref, bg_ref, wp_ref, o_ref,
                       wdbf_ref, wgbf_ref, wpbf_ref, *, alpha):
    @pl.when(pl.program_id(1) == 0)
    def _():
        wdbf_ref[...] = wd_ref[...].astype(BF16)
        wgbf_ref[...] = wg_ref[...].astype(BF16)
        wpbf_ref[...] = wp_ref[...].astype(BF16)

    shared = jnp.dot(s_ref[...], wdbf_ref[...], preferred_element_type=F32)
    gate = jax.nn.sigmoid(jnp.dot(hb_ref[...], wgbf_ref[...], preferred_element_type=F32)
                          + bg_ref[...])
    proj = jnp.dot(p_ref[...].astype(BF16), wpbf_ref[...], preferred_element_type=F32)
    o_ref[...] = alpha * hj_ref[...] + shared + gate * proj


def _dense_tail(h1_bf, h1, s_mid, p, w_down_s, w_gate, b_gate, w_proj, alpha, *, tm=512, tn=512):
    T, D = h1.shape
    f = s_mid.shape[1]
    dp = p.shape[1]
    tm = min(tm, T)
    return pl.pallas_call(
        functools.partial(_dense_tail_kernel, alpha=alpha),
        out_shape=jax.ShapeDtypeStruct((T, D), F32),
        grid=(D // tn, T // tm),
        in_specs=[pl.BlockSpec((tm, D), lambda j, i: (i, 0)),
                  pl.BlockSpec((tm, tn), lambda j, i: (i, j)),
                  pl.BlockSpec((tm, f), lambda j, i: (i, 0)),
                  pl.BlockSpec((tm, dp), lambda j, i: (i, 0)),
                  pl.BlockSpec((f, tn), lambda j, i: (0, j)),
                  pl.BlockSpec((D, tn), lambda j, i: (0, j)),
                  pl.BlockSpec((1, tn), lambda j, i: (0, j)),
                  pl.BlockSpec((dp, tn), lambda j, i: (0, j))],
        out_specs=pl.BlockSpec((tm, tn), lambda j, i: (i, j)),
        scratch_shapes=[pltpu.VMEM((f, tn), BF16), pltpu.VMEM((D, tn), BF16), pltpu.VMEM((dp, tn), BF16)],
        compiler_params=_cparams(2, 56),
        name="dense_tail",
    )(h1_bf, h1, s_mid, p, w_down_s, w_gate, b_gate.reshape(1, D), w_proj)


def _dispatch_tables(idx, rank, counts):
    E = counts.shape[0]
    P = idx.size + E * MOE_BLOCK
    nb = P // MOE_BLOCK
    padded = (counts + MOE_BLOCK - 1) // MOE_BLOCK * MOE_BLOCK
    pend = jnp.cumsum(padded)
    pstart = pend - padded
    sel = idx[None] == jnp.arange(E, dtype=I32).reshape(E, 1, 1, 1)
    dest = rank + jnp.sum(jnp.where(sel, pstart.reshape(E, 1, 1, 1), 0), axis=0)
    blk_row = jnp.arange(nb, dtype=I32) * MOE_BLOCK
    blk_e = jnp.minimum(jnp.sum(pend[None, :] <= blk_row[:, None], axis=1), E - 1).astype(I32)
    n_used = pend[-1] // MOE_BLOCK
    last_blk = jnp.where(counts > 0, pend // MOE_BLOCK - 1, -1)
    tail_blk = n_used + jnp.arange(E, dtype=I32)
    tail_blk = jnp.where(tail_blk < nb, tail_blk, -1)
    fill = jnp.concatenate([last_blk, tail_blk]).astype(I32)
    nxt_blk = jnp.take(pend // MOE_BLOCK, blk_e).astype(I32)
    return dest.astype(I32), blk_e, nxt_blk, n_used.astype(I32).reshape(1), fill, P


def _layer(h0_f, h0_bf, p_l, w_in, b_in, conv_w, conv_b, mh_norm_g, w_conv_out, w_mlstm_out,
           w_mix_out, ln1_g, ln1_b, w_router, router_bias, w_gu_e, w_down_e, w_gu_s, w_down_s,
           w_ple_gate, b_ple_gate, w_ple_proj, ln2_g, ln2_b, alpha, batch, seq):
    T, D = h0_f.shape
    d_conv = conv_w.shape[1]
    d_v = mh_norm_g.shape[0]
    d_qk = d_v // 2
    n_if = 2 * N_HEADS
    c_qk = 3 * d_conv
    c_if = c_qk + 2 * d_qk + 2 * d_v
    c_gate = c_if + n_if
    b2d = b_in.reshape(1, -1)

    ya_pre = _conv_branch(h0_bf, w_in, b2d, conv_w, conv_b.reshape(1, -1), seq, d_conv)
    qkvo = _proj(h0_bf, w_in, b2d, c_qk, 2 * d_qk + 2 * d_v)
    if_col, if_row = _if_gates(h0_bf, w_in[:, c_if:c_gate], b_in[c_if:c_gate])
    gates = _proj(h0_bf, w_in[:, c_gate:].astype(BF16), b2d[:, c_gate:], 0, 2 * D, act="sigmoid")
    yb_pre = _mlstm(qkvo, if_col, if_row, mh_norm_g.reshape(1, -1), batch, seq, d_qk, d_v)
    u = _merge(ya_pre, yb_pre, w_conv_out, w_mlstm_out, gates)
    pre1 = _mix(u, w_mix_out, h0_f, alpha)

    rt = min(ROUTE_TILE, T)
    h1, h1_bf, h1_packed, idx, rank, w_col, counts = _ln1_router(pre1, ln1_g, ln1_b, w_router,
                                                                 router_bias, rt)
    dest, blk_e, nxt_blk, n_used, fill, n_rows = _dispatch_tables(idx, rank, counts[:, 0].astype(I32))
    dest_tiles = dest.reshape(T // rt, TOP_K * rt)

    xs = _scatter_rows(h1_packed, dest_tiles, fill, n_rows)
    hmid = _expert_up(xs, w_gu_e, blk_e, nxt_blk, n_used)
    y_packed = _expert_down(hmid, w_down_e, blk_e, nxt_blk, n_used)

    s_mid = _glu_up(h1_bf, w_gu_s)
    pre2 = _dense_tail(h1_bf, h1, s_mid, p_l, w_down_s, w_ple_gate, b_ple_gate, w_ple_proj, alpha)
    return _combine_ln(y_packed, dest_tiles, w_col, pre2, ln2_g, ln2_b)


def kernel(x, p, ln_in_g, ln_in_b, w_in, b_in, conv_w, conv_b, mh_norm_g, w_conv_out, w_mlstm_out,
           w_mix_out, ln1_g, ln1_b, w_router, router_bias, w_gu_e, w_down_e, w_gu_s, w_down_s,
           w_ple_gate, b_ple_gate, w_ple_proj, ln2_g, ln2_b):
    B, S, D = x.shape
    depth = w_in.shape[0]
    alpha = (2 * depth) ** 0.25
    T = B * S
    h_f, h_bf = _ln_in(x.reshape(T, D), ln_in_g, ln_in_b)
    for l in range(depth):
        h_f = _layer(h_f, h_bf, p[l].reshape(T, -1), w_in[l], b_in[l], conv_w[l], conv_b[l],
                     mh_norm_g[l], w_conv_out[l], w_mlstm_out[l], w_mix_out[l], ln1_g[l], ln1_b[l],
                     w_router[l], router_bias[l], w_gu_e[l], w_down_e[l], w_gu_s[l], w_down_s[l],
                     w_ple_gate[l], b_ple_gate[l], w_ple_proj[l], ln2_g[l], ln2_b[l], alpha, B, S)
        if l + 1 < depth:
            h_bf = h_f.astype(BF16)
    return h_f.reshape(B, S, D)
```

```python
import functools

import jax
import jax.numpy as jnp
from jax import lax
from jax.experimental import pallas as pl
from jax.experimental.pallas import tpu as pltpu

F32 = jnp.float32
BF16 = jnp.bfloat16
U32 = jnp.uint32
I32 = jnp.int32

N_HEADS = 8
TOP_K = 8
N_GROUPS = 8
TOP_GROUPS = 4
ROUTE_SCALE = 2.5
MOE_BLOCK = 256
LN_EPS = 1e-5
CONV_W = 3
MLSTM_CHUNK = 256
ROUTE_TILE = 128
NEG_BIG = -1e30
HI_MASK = 0xFFFF0000
V7X_VMEM_BYTES = 64 * 1024 * 1024


def _cparams(n_axes, vmem_mib):
    assert vmem_mib * 1024 * 1024 <= V7X_VMEM_BYTES
    return pltpu.CompilerParams(dimension_semantics=("arbitrary",) * n_axes,
                                vmem_limit_bytes=vmem_mib * 1024 * 1024)


def _ln_rows(x, g, b):
    mu = jnp.mean(x, axis=-1, keepdims=True)
    xc = x - mu
    var = jnp.mean(xc * xc, axis=-1, keepdims=True)
    return xc * lax.rsqrt(var + LN_EPS) * g + b


def _pack_halves(lo, hi):
    lo = pltpu.bitcast(lo.astype(BF16).astype(F32), U32)
    hi = pltpu.bitcast(hi.astype(BF16).astype(F32), U32)
    return (hi & jnp.uint32(HI_MASK)) | (lo >> jnp.uint32(16))


def _ln_chunked(acc_ref, g_ref, b_ref):
    nj, _, tn = acc_ref.shape
    inv_d = 1.0 / (nj * tn)
    s = acc_ref[0].sum(axis=-1, keepdims=True)
    for jj in range(1, nj):
        s = s + acc_ref[jj].sum(axis=-1, keepdims=True)
    mu = s * inv_d
    v = jnp.zeros_like(mu)
    for jj in range(nj):
        d = acc_ref[jj] - mu
        v = v + (d * d).sum(axis=-1, keepdims=True)
    rstd = lax.rsqrt(v * inv_d + LN_EPS)

    def chunk(jj):
        cols = slice(jj * tn, (jj + 1) * tn)
        return (acc_ref[jj] - mu) * rstd * g_ref[:, cols] + b_ref[:, cols]
    return chunk


def _unpack_halves(w):
    lo = pltpu.bitcast(w << jnp.uint32(16), F32)
    hi = pltpu.bitcast(w & jnp.uint32(HI_MASK), F32)
    return lo, hi


def _ln_in_kernel(x_ref, g_ref, b_ref, of_ref, ob_ref):
    y = _ln_rows(x_ref[...], g_ref[...], b_ref[...])
    of_ref[...] = y
    ob_ref[...] = y.astype(BF16)


def _ln_in(x2, g, b, tm=256):
    T, D = x2.shape
    return pl.pallas_call(
        _ln_in_kernel,
        out_shape=(jax.ShapeDtypeStruct((T, D), F32), jax.ShapeDtypeStruct((T, D), BF16)),
        grid=(T // tm,),
        in_specs=[pl.BlockSpec((tm, D), lambda i: (i, 0)),
                  pl.BlockSpec((1, D), lambda i: (0, 0)),
                  pl.BlockSpec((1, D), lambda i: (0, 0))],
        out_specs=(pl.BlockSpec((tm, D), lambda i: (i, 0)),
                   pl.BlockSpec((tm, D), lambda i: (i, 0))),
        compiler_params=_cparams(1, 40),
        name="ln_in",
    )(x2, g.reshape(1, D), b.reshape(1, D))


_NT = (((1,), (1,)), ((), ()))


def _proj_kernel(x_ref, w_ref, b_ref, o_ref, wbf_ref, *, act):
    @pl.when(pl.program_id(1) == 0)
    def _():
        wbf_ref[...] = w_ref[...].astype(BF16)

    acc = lax.dot_general(x_ref[...], wbf_ref[...], _NT, preferred_element_type=F32) + b_ref[...]
    if act == "sigmoid":
        acc = jax.nn.sigmoid(acc)
    o_ref[...] = acc.astype(o_ref.dtype)


def _proj(x, w_t, bias2d, row0, n_cols, *, act=None, tm=1024, tn=512, out_dtype=BF16):
    T, K = x.shape
    tm = min(tm, T)
    assert n_cols % tn == 0 and T % tm == 0 and row0 % 8 == 0
    if row0 % tn == 0:
        jb = row0 // tn
        w_spec = pl.BlockSpec((tn, K), lambda j, i: (jb + j, 0))
        b_spec = pl.BlockSpec((1, tn), lambda j, i: (0, jb + j))
    else:
        w_spec = pl.BlockSpec((pl.Element(tn), pl.Element(K)),
                              lambda j, i: (pl.multiple_of(row0 + j * tn, 8), 0))
        bias2d = bias2d[:, row0:row0 + n_cols]
        b_spec = pl.BlockSpec((1, tn), lambda j, i: (0, j))
    return pl.pallas_call(
        functools.partial(_proj_kernel, act=act),
        out_shape=jax.ShapeDtypeStruct((T, n_cols), out_dtype),
        grid=(n_cols // tn, T // tm),
        in_specs=[pl.BlockSpec((tm, K), lambda j, i: (i, 0)), w_spec, b_spec],
        out_specs=pl.BlockSpec((tm, tn), lambda j, i: (i, j)),
        scratch_shapes=[pltpu.VMEM((tn, K), BF16)],
        compiler_params=_cparams(2, 56),
        name="proj_" + (act or "lin"),
    )(x, w_t, bias2d)


def _conv_kernel(x_ref, wh_ref, wc_ref, wb_ref, bh_ref, bc_ref, bb_ref, cw_ref, cb_ref,
                 o_ref, wbf_ref, zprev_ref, *, tiles_per_seq):
    i = pl.program_id(1)

    @pl.when(i == 0)
    def _():
        wbf_ref[0] = wh_ref[...].astype(BF16)
        wbf_ref[1] = wc_ref[...].astype(BF16)
        wbf_ref[2] = wb_ref[...].astype(BF16)

    @pl.when(i % tiles_per_seq == 0)
    def _():
        zprev_ref[...] = jnp.zeros_like(zprev_ref)

    x = x_ref[...]
    ha = lax.dot_general(x, wbf_ref[0], _NT, preferred_element_type=F32) + bh_ref[...]
    ca = lax.dot_general(x, wbf_ref[1], _NT, preferred_element_type=F32) + bc_ref[...]
    ba = lax.dot_general(x, wbf_ref[2], _NT, preferred_element_type=F32) + bb_ref[...]
    z = ca * ha
    tm = z.shape[0]
    prev = zprev_ref[...]
    row8 = lax.broadcasted_iota(I32, prev.shape, 0)
    z1 = pltpu.roll(z, 1, 0)
    z2 = pltpu.roll(z, 2, 0)
    p1 = pltpu.roll(prev, 1, 0)
    p2 = pltpu.roll(prev, 2, 0)
    z1 = jnp.concatenate([jnp.where(row8 < 1, p1, z1[:8]), z1[8:]], axis=0)
    z2 = jnp.concatenate([jnp.where(row8 < 2, p2, z2[:8]), z2[8:]], axis=0)
    cw = cw_ref[...]
    y = cw[0:1] * z2 + cw[1:2] * z1 + cw[2:3] * z + cb_ref[...]
    o_ref[...] = (ba * y).astype(o_ref.dtype)
    zprev_ref[...] = z[tm - 8:]


def _conv_branch(x, w_t, b2d, conv_w, conv_b2d, seq, d_conv, *, tm=512, tn=256):
    T, K = x.shape
    tm = min(tm, seq)
    assert seq % tm == 0 and d_conv % tn == 0 and tm % 8 == 0
    nb = d_conv // tn
    wspec = lambda g: pl.BlockSpec((tn, K), lambda j, i: (g * nb + j, 0))
    bspec = lambda g: pl.BlockSpec((1, tn), lambda j, i: (0, g * nb + j))
    return pl.pallas_call(
        functools.partial(_conv_kernel, tiles_per_seq=seq // tm),
        out_shape=jax.ShapeDtypeStruct((T, d_conv), BF16),
        grid=(nb, T // tm),
        in_specs=[pl.BlockSpec((tm, K), lambda j, i: (i, 0)),
                  wspec(0), wspec(1), wspec(2), bspec(0), bspec(1), bspec(2),
                  pl.BlockSpec((CONV_W, tn), lambda j, i: (0, j)),
                  pl.BlockSpec((1, tn), lambda j, i: (0, j))],
        out_specs=pl.BlockSpec((tm, tn), lambda j, i: (i, j)),
        scratch_shapes=[pltpu.VMEM((3, tn, K), BF16), pltpu.VMEM((8, tn), F32)],
        compiler_params=_cparams(2, 56),
        name="conv_branch",
    )(x, w_t, w_t, w_t, b2d, b2d, b2d, conv_w, conv_b2d)


def _if_kernel(x_ref, wp_ref, wt_ref, bc_ref, br_ref, oc_ref, or_ref):
    x = x_ref[...]
    oc_ref[...] = lax.dot_general(x, wp_ref[...], _NT, preferred_element_type=F32) + bc_ref[...]
    or_ref[...] = lax.dot_general(wt_ref[...], x, _NT, preferred_element_type=F32) + br_ref[...]


def _if_gates(x, w_if_t, b_if, tm=512):
    T, K = x.shape
    tm = min(tm, T)
    n = w_if_t.shape[0]
    w_if_t = w_if_t.astype(BF16)
    w_pad = jnp.zeros((128, K), BF16).at[:n].set(w_if_t)
    b_pad = jnp.zeros((1, 128), F32).at[0, :n].set(b_if)
    return pl.pallas_call(
        _if_kernel,
        out_shape=(jax.ShapeDtypeStruct((T, 128), F32), jax.ShapeDtypeStruct((n, T), F32)),
        grid=(T // tm,),
        in_specs=[pl.BlockSpec((tm, K), lambda i: (i, 0)),
                  pl.BlockSpec((128, K), lambda i: (0, 0)),
                  pl.BlockSpec((n, K), lambda i: (0, 0)),
                  pl.BlockSpec((1, 128), lambda i: (0, 0)),
                  pl.BlockSpec((n, 1), lambda i: (0, 0))],
        out_specs=(pl.BlockSpec((tm, 128), lambda i: (i, 0)),
                   pl.BlockSpec((n, tm), lambda i: (0, i))),
        compiler_params=_cparams(1, 32),
        name="if_gates",
    )(x, w_pad, w_if_t, b_pad, b_if.reshape(n, 1))


def _mlstm_kernel(q_ref, k_ref, v_ref, o_ref, ifc_ref, ifr_ref, g_ref, y_ref, c_ref, m_ref,
                  *, dk, dv):
    H = N_HEADS
    L = q_ref.shape[0]

    @pl.when(pl.program_id(1) == 0)
    def _():
        c_ref[...] = jnp.zeros_like(c_ref)
        m_ref[...] = jnp.zeros_like(m_ref)

    scale = dk ** -0.5
    ifc = ifc_ref[...]
    ifr = ifr_ref[...]
    ig_c = ifc[:, 0:H]
    lf_c = jax.nn.log_sigmoid(ifc[:, H:2 * H])
    ig_r = ifr[0:H, :]
    lf_r = jax.nn.log_sigmoid(ifr[H:2 * H, :])
    r = lax.broadcasted_iota(I32, (L, L), 0)
    c = lax.broadcasted_iota(I32, (L, L), 1)
    causal = r >= c
    bcum_c = jnp.dot(causal.astype(F32), lf_c, preferred_element_type=F32,
                     precision=lax.Precision.HIGHEST)
    bcum_r = jnp.dot(lf_r, (r <= c).astype(F32), preferred_element_type=F32,
                     precision=lax.Precision.HIGHEST)
    d_c = ig_c - bcum_c
    d_r = ig_r - bcum_r
    ones_col = (lax.broadcasted_iota(I32, (L, 128), 1) == 0).astype(BF16)

    for h in range(H):
        q = q_ref[:, h * dk:(h + 1) * dk]
        k = k_ref[:, h * dk:(h + 1) * dk]
        v = v_ref[:, h * dv:(h + 1) * dv]
        v_aug = jnp.concatenate([v, ones_col], axis=1)
        bc = bcum_c[:, h:h + 1]
        m_prev = m_ref[h]
        c_prev = c_ref[h]

        qk = lax.dot_general(q, k, (((1,), (1,)), ((), ())), preferred_element_type=F32) * scale
        logd = jnp.where(causal, bc + d_r[h:h + 1, :], NEG_BIG)
        log_inter = bc + m_prev
        m_t = jnp.maximum(log_inter, jnp.max(logd, axis=1, keepdims=True))
        w_inter = jnp.exp(log_inter - m_t)
        s_mat = (qk * jnp.exp(logd - m_t)).astype(BF16)
        tot = (w_inter * jnp.dot(q, c_prev.astype(BF16), preferred_element_type=F32)
               + jnp.dot(s_mat, v_aug, preferred_element_type=F32))
        den = jnp.maximum(jnp.abs(tot[:, dv:dv + 1]), jnp.exp(-m_t))
        hh = tot[:, :dv] / den
        mu = jnp.mean(hh, axis=-1, keepdims=True)
        hc = hh - mu
        var = jnp.mean(hc * hc, axis=-1, keepdims=True)
        hn = hc * lax.rsqrt(var + LN_EPS) * g_ref[:, h * dv:(h + 1) * dv]
        og = jax.nn.sigmoid(o_ref[:, h * dv:(h + 1) * dv].astype(F32))
        y_ref[:, h * dv:(h + 1) * dv] = (og * hn).astype(y_ref.dtype)

        g_tot = bc[L - 1:L, :]
        a = g_tot + d_c[:, h:h + 1]
        m_new = jnp.maximum(g_tot + m_prev, jnp.max(a, axis=0, keepdims=True))
        kw = (k.astype(F32) * (jnp.exp(a - m_new) * scale)).astype(BF16)
        kv = lax.dot_general(kw, v_aug, (((0,), (0,)), ((), ())), preferred_element_type=F32)
        c_ref[h] = jnp.exp(g_tot + m_prev - m_new) * c_prev + kv
        m_ref[h] = m_new


def _mlstm(qkvo, if_col, if_row, mh_g2d, batch, seq, d_qk, d_v, L=MLSTM_CHUNK):
    T = qkvo.shape[0]
    L = min(L, seq)
    assert seq % L == 0 and d_v == 2 * d_qk
    nc = seq // L
    dk, dv = d_qk // N_HEADS, d_v // N_HEADS
    row = lambda b, c: b * nc + c
    return pl.pallas_call(
        functools.partial(_mlstm_kernel, dk=dk, dv=dv),
        out_shape=jax.ShapeDtypeStruct((T, d_v), BF16),
        grid=(batch, nc),
        in_specs=[pl.BlockSpec((L, d_qk), lambda b, c: (row(b, c), 0)),
                  pl.BlockSpec((L, d_qk), lambda b, c: (row(b, c), 1)),
                  pl.BlockSpec((L, d_v), lambda b, c: (row(b, c), 1)),
                  pl.BlockSpec((L, d_v), lambda b, c: (row(b, c), 2)),
                  pl.BlockSpec((L, 128), lambda b, c: (row(b, c), 0)),
                  pl.BlockSpec((2 * N_HEADS, L), lambda b, c: (0, row(b, c))),
                  pl.BlockSpec((1, d_v), lambda b, c: (0, 0))],
        out_specs=pl.BlockSpec((L, d_v), lambda b, c: (row(b, c), 0)),
        scratch_shapes=[pltpu.VMEM((N_HEADS, dk, dv + 128), F32),
                        pltpu.VMEM((N_HEADS, 1, 1), F32)],
        compiler_params=_cparams(2, 40),
        name="mlstm",
    )(qkvo, qkvo, qkvo, qkvo, if_col, if_row, mh_g2d)


def _merge_kernel(a_ref, b_ref, wa_ref, wb_ref, ga_ref, gb_ref, o_ref, wabf_ref, wbbf_ref):
    @pl.when(pl.program_id(1) == 0)
    def _():
        wabf_ref[...] = wa_ref[...].astype(BF16)
        wbbf_ref[...] = wb_ref[...].astype(BF16)

    ya = jnp.dot(a_ref[...], wabf_ref[...], preferred_element_type=F32)
    yb = jnp.dot(b_ref[...], wbbf_ref[...], preferred_element_type=F32)
    u = ga_ref[...].astype(F32) * ya + gb_ref[...].astype(F32) * yb
    o_ref[...] = u.astype(o_ref.dtype)


def _merge(ya_pre, yb_pre, w_a, w_b, gates, *, tm=1024, tn=512):
    T, K = ya_pre.shape
    D = w_a.shape[1]
    tm = min(tm, T)
    nj = D // tn
    return pl.pallas_call(
        _merge_kernel,
        out_shape=jax.ShapeDtypeStruct((T, D), BF16),
        grid=(nj, T // tm),
        in_specs=[pl.BlockSpec((tm, K), lambda j, i: (i, 0)),
                  pl.BlockSpec((tm, K), lambda j, i: (i, 0)),
                  pl.BlockSpec((K, tn), lambda j, i: (0, j)),
                  pl.BlockSpec((K, tn), lambda j, i: (0, j)),
                  pl.BlockSpec((tm, tn), lambda j, i: (i, j)),
                  pl.BlockSpec((tm, tn), lambda j, i: (i, nj + j))],
        out_specs=pl.BlockSpec((tm, tn), lambda j, i: (i, j)),
        scratch_shapes=[pltpu.VMEM((K, tn), BF16), pltpu.VMEM((K, tn), BF16)],
        compiler_params=_cparams(2, 56),
        name="merge",
    )(ya_pre, yb_pre, w_a, w_b, gates, gates)


def _mix_kernel(u_ref, w_ref, h_ref, o_ref, wbf_ref, *, alpha):
    @pl.when(pl.program_id(1) == 0)
    def _():
        wbf_ref[...] = w_ref[...].astype(BF16)

    o_ref[...] = alpha * h_ref[...] + jnp.dot(u_ref[...], wbf_ref[...], preferred_element_type=F32)


def _mix(u, w, h0, alpha, *, tm=1024, tn=512):
    T, D = h0.shape
    tm = min(tm, T)
    return pl.pallas_call(
        functools.partial(_mix_kernel, alpha=alpha),
        out_shape=jax.ShapeDtypeStruct((T, D), F32),
        grid=(D // tn, T // tm),
        in_specs=[pl.BlockSpec((tm, D), lambda j, i: (i, 0)),
                  pl.BlockSpec((D, tn), lambda j, i: (0, j)),
                  pl.BlockSpec((tm, tn), lambda j, i: (i, j))],
        out_specs=pl.BlockSpec((tm, tn), lambda j, i: (i, j)),
        scratch_shapes=[pltpu.VMEM((D, tn), BF16)],
        compiler_params=_cparams(2, 56),
        name="mix_out",
    )(u, w, h0)


def _ln1_router_kernel(pre_ref, g_ref, b_ref, wr_ref, rb_ref, hf_ref, hb_ref, hp_ref,
                       idx_ref, rank_ref, wcol_ref, cnt_ref, carry_ref):
    i = pl.program_id(0)
    E = wr_ref.shape[0]
    tm, D = pre_ref.shape
    G, M = N_GROUPS, E // N_GROUPS

    @pl.when(i == 0)
    def _():
        carry_ref[...] = jnp.zeros_like(carry_ref)

    h = _ln_rows(pre_ref[...], g_ref[...], b_ref[...])
    hf_ref[...] = h
    hb_ref[...] = h.astype(BF16)
    hp_ref[...] = _pack_halves(h[:, :D // 2], h[:, D // 2:])

    logits = lax.dot_general(wr_ref[...], h, (((1,), (1,)), ((), ())),
                             preferred_element_type=F32, precision=lax.Precision.HIGHEST)
    scores = jax.nn.sigmoid(logits)
    scores3 = scores.reshape(G, M, tm)
    sel3 = (scores + rb_ref[...]).reshape(G, M, tm)
    midx = lax.broadcasted_iota(I32, (G, M, tm), 1)
    gidx3 = lax.broadcasted_iota(I32, (G, M, tm), 0)
    eidx = gidx3 * M + midx
    gidx = lax.broadcasted_iota(I32, (G, 1, tm), 0)
    neg_inf = -jnp.inf

    top1 = jnp.max(sel3, axis=1, keepdims=True)
    first1 = jnp.min(jnp.where(sel3 == top1, midx, M), axis=1, keepdims=True)
    top2 = jnp.max(jnp.where(midx == first1, neg_inf, sel3), axis=1, keepdims=True)
    gs = top1 + top2
    gkeep = jnp.zeros((G, 1, tm), F32)
    for _ in range(TOP_GROUPS):
        mx = jnp.max(gs, axis=0, keepdims=True)
        first = jnp.min(jnp.where(gs == mx, gidx, G), axis=0, keepdims=True)
        hit = gidx == first
        gkeep = jnp.where(hit, 1.0, gkeep)
        gs = jnp.where(hit, neg_inf, gs)
    selm = jnp.where(gkeep > 0.5, sel3, neg_inf)

    idx_rows, sc_rows = [], []
    chosen = jnp.zeros((G, M, tm), F32)
    for _ in range(TOP_K):
        mx = jnp.max(jnp.max(selm, axis=1, keepdims=True), axis=0, keepdims=True)
        first = jnp.min(jnp.min(jnp.where(selm == mx, eidx, E), axis=1, keepdims=True),
                        axis=0, keepdims=True)
        hit = eidx == first
        sc = jnp.sum(jnp.sum(jnp.where(hit, scores3, 0.0), axis=1, keepdims=True),
                     axis=0, keepdims=True)
        chosen = jnp.where(hit, 1.0, chosen)
        selm = jnp.where(hit, neg_inf, selm)
        idx_rows.append(first)
        sc_rows.append(sc)
    denom = sc_rows[0]
    for s in sc_rows[1:]:
        denom = denom + s

    tr = lax.broadcasted_iota(I32, (tm, tm), 0)
    tc = lax.broadcasted_iota(I32, (tm, tm), 1)
    before = (tr < tc).astype(BF16)
    chosen2 = chosen.reshape(E, tm)
    rank2 = jnp.dot(chosen2.astype(BF16), before, preferred_element_type=F32) + carry_ref[...]
    rank3 = rank2.reshape(G, M, tm)
    w_rows = []
    for k in range(TOP_K):
        hit = eidx == idx_rows[k]
        rk = jnp.sum(jnp.sum(jnp.where(hit, rank3, 0.0), axis=1, keepdims=True),
                     axis=0, keepdims=True)
        idx_ref[k:k + 1, :] = idx_rows[k].reshape(1, tm)
        rank_ref[k:k + 1, :] = rk.reshape(1, tm).astype(I32)
        w_rows.append((sc_rows[k] / denom * ROUTE_SCALE).reshape(1, tm))
    wcol_ref[...] = jnp.concatenate(w_rows, axis=0).T
    carry_ref[...] = carry_ref[...] + jnp.sum(chosen2, axis=1, keepdims=True)

    @pl.when(i == pl.num_programs(0) - 1)
    def _():
        cnt_ref[...] = carry_ref[...]


def _ln1_router(pre, g, b, w_router, router_bias, tm):
    T, D = pre.shape
    E = w_router.shape[1]
    nt = T // tm
    row = lambda i: (i, 0)
    fixed = lambda i: (0, 0)
    return pl.pallas_call(
        _ln1_router_kernel,
        out_shape=(jax.ShapeDtypeStruct((T, D), F32), jax.ShapeDtypeStruct((T, D), BF16),
                   jax.ShapeDtypeStruct((T, D // 2), U32),
                   jax.ShapeDtypeStruct((nt, TOP_K, tm), I32), jax.ShapeDtypeStruct((nt, TOP_K, tm), I32),
                   jax.ShapeDtypeStruct((T, TOP_K), F32), jax.ShapeDtypeStruct((E, 1), F32)),
        grid=(nt,),
        in_specs=[pl.BlockSpec((tm, D), row), pl.BlockSpec((1, D), fixed), pl.BlockSpec((1, D), fixed),
                  pl.BlockSpec((E, D), fixed), pl.BlockSpec((E, 1), fixed)],
        out_specs=(pl.BlockSpec((tm, D), row), pl.BlockSpec((tm, D), row), pl.BlockSpec((tm, D // 2), row),
                   pl.BlockSpec((None, TOP_K, tm), lambda i: (i, 0, 0)),
                   pl.BlockSpec((None, TOP_K, tm), lambda i: (i, 0, 0)),
                   pl.BlockSpec((tm, TOP_K), row),
                   pl.BlockSpec((E, 1), fixed)),
        scratch_shapes=[pltpu.VMEM((E, 1), F32)],
        compiler_params=_cparams(1, 40),
        name="ln1_router",
    )(pre, g.reshape(1, D), b.reshape(1, D), w_router.T, router_bias.reshape(E, 1))


def _scatter_rows_kernel(fill_ref, dest_hbm, x_ref, xs_hbm, idx_smem, zero_ref, isem, ssem, zsem):
    i = pl.program_id(0)
    n = pl.num_programs(0)
    tm = x_ref.shape[0]
    n_idx = TOP_K * tm
    R = zero_ref.shape[0]

    def idx_copy(tile, slot):
        return pltpu.make_async_copy(dest_hbm.at[tile], idx_smem.at[pl.ds(slot * n_idx, n_idx)],
                                     isem.at[slot])

    @pl.when(i == 0)
    def _():
        idx_copy(0, 0).start()
        zero_ref[...] = jnp.zeros_like(zero_ref)

        def fill(f):
            return pltpu.make_async_copy(zero_ref, xs_hbm.at[pl.ds(fill_ref[f] * R, R)], zsem)

        def start(f, carry):
            @pl.when(fill_ref[f] >= 0)
            def _():
                fill(f).start()
            return carry
        lax.fori_loop(0, fill_ref.shape[0], start, 0)

        def wait(f, carry):
            @pl.when(fill_ref[f] >= 0)
            def _():
                fill(f).wait()
            return carry
        lax.fori_loop(0, fill_ref.shape[0], wait, 0)

    slot = i % 2
    idx_copy(i, slot).wait()

    @pl.when(i + 1 < n)
    def _():
        idx_copy(i + 1, 1 - slot).start()

    base = slot * n_idx

    def issue(r8, carry):
        for s in range(8):
            r = r8 * 8 + s
            for k in range(TOP_K):
                d = idx_smem[base + k * tm + r]
                pltpu.make_async_copy(x_ref.at[pl.ds(r, 1)], xs_hbm.at[pl.ds(d, 1)], ssem).start()
        return carry
    lax.fori_loop(0, tm // 8, issue, 0)

    for k in range(TOP_K):
        pltpu.make_async_copy(x_ref, x_ref, ssem).wait()


def _scatter_rows(x_packed, dest_tiles, fill_blocks, n_rows):
    T, W = x_packed.shape
    nt, n_idx = dest_tiles.shape
    tm = n_idx // TOP_K
    return pl.pallas_call(
        _scatter_rows_kernel,
        out_shape=jax.ShapeDtypeStruct((n_rows, W), x_packed.dtype),
        grid_spec=pltpu.PrefetchScalarGridSpec(
            num_scalar_prefetch=1,
            grid=(nt,),
            in_specs=[pl.BlockSpec(memory_space=pl.ANY),
                      pl.BlockSpec((tm, W), lambda i, fb: (i, 0))],
            out_specs=pl.BlockSpec(memory_space=pl.ANY),
            scratch_shapes=[pltpu.SMEM((2 * n_idx,), I32), pltpu.VMEM((MOE_BLOCK, W), x_packed.dtype),
                            pltpu.SemaphoreType.DMA((2,)), pltpu.SemaphoreType.DMA,
                            pltpu.SemaphoreType.DMA]),
        compiler_params=pltpu.CompilerParams(dimension_semantics=("arbitrary",),
                                             vmem_limit_bytes=32 * 1024 * 1024,
                                             has_side_effects=True),
        name="dispatch_scatter",
    )(fill_blocks, dest_tiles, x_packed)


def _expert_up_kernel(be_ref, nxt_ref, nused_ref, x_ref, w_hbm, o_ref, wbf_ref, stage_ref, run_ref,
                      wsem):
    c = pl.program_id(0)
    b = pl.program_id(1)
    nc = pl.num_programs(0)
    nb = pl.num_programs(1)
    n_used = nused_ref[0]
    half = x_ref.shape[1]
    tn = o_ref.shape[1]
    f = w_hbm.shape[2] // 2
    e = be_ref[b]
    run_start = jnp.logical_or(b == 0, e != be_ref[jnp.maximum(b - 1, 0)])

    def fetch(e_, c_, slot):
        col = pl.multiple_of(c_ * tn, 128)
        return (pltpu.make_async_copy(w_hbm.at[e_, :, pl.ds(col, tn)],
                                      stage_ref.at[slot, :, pl.ds(0, tn)], wsem.at[slot]),
                pltpu.make_async_copy(w_hbm.at[e_, :, pl.ds(f + col, tn)],
                                      stage_ref.at[slot, :, pl.ds(tn, tn)], wsem.at[slot]))

    @pl.when(jnp.logical_and(c == 0, b == 0))
    def _():
        run_ref[0] = 0
        for cp in fetch(e, c, 0):
            cp.start()

    @pl.when(jnp.logical_and(run_start, b < n_used))
    def _():
        k = run_ref[0]
        slot = k % 2
        for cp in fetch(e, c, slot):
            cp.wait()
        wbf_ref[...] = stage_ref[slot].astype(BF16)
        b_next = nxt_ref[b]
        more_here = b_next < n_used

        @pl.when(jnp.logical_or(more_here, c + 1 < nc))
        def _():
            e_next = jnp.where(more_here, be_ref[jnp.minimum(b_next, nb - 1)], be_ref[0])
            c_next = jnp.where(more_here, c, c + 1)
            for cp in fetch(e_next, c_next, 1 - slot):
                cp.start()
        run_ref[0] = k + 1

    @pl.when(b < n_used)
    def _():
        lo, hi = _unpack_halves(x_ref[...])
        gu = (jnp.dot(lo.astype(BF16), wbf_ref[:half], preferred_element_type=F32)
              + jnp.dot(hi.astype(BF16), wbf_ref[half:], preferred_element_type=F32))
        o_ref[...] = (jax.nn.silu(gu[:, :tn]) * gu[:, tn:]).astype(o_ref.dtype)

    @pl.when(b >= n_used)
    def _():
        o_ref[...] = jnp.zeros_like(o_ref)


def _expert_up(xs, w_gu_e, blk_e, nxt_blk, n_used, *, tn=384):
    P, half = xs.shape
    E, D, two_f = w_gu_e.shape
    f = two_f // 2
    assert f % tn == 0 and D == 2 * half and tn % 128 == 0
    nc = f // tn
    nb = P // MOE_BLOCK
    return pl.pallas_call(
        _expert_up_kernel,
        out_shape=jax.ShapeDtypeStruct((P, f), BF16),
        grid_spec=pltpu.PrefetchScalarGridSpec(
            num_scalar_prefetch=3,
            grid=(nc, nb),
            in_specs=[pl.BlockSpec((MOE_BLOCK, half), lambda c, b, be, nx, nu: (jnp.minimum(b, nu[0] - 1), 0)),
                      pl.BlockSpec(memory_space=pl.ANY)],
            out_specs=pl.BlockSpec((MOE_BLOCK, tn), lambda c, b, be, nx, nu: (b, c)),
            scratch_shapes=[pltpu.VMEM((D, 2 * tn), BF16), pltpu.VMEM((2, D, 2 * tn), F32),
                            pltpu.SMEM((1,), I32), pltpu.SemaphoreType.DMA((2,))]),
        compiler_params=_cparams(2, 56),
        name="expert_up",
    )(blk_e, nxt_blk, n_used, xs, w_gu_e)


def _expert_down_kernel(be_ref, nxt_ref, nused_ref, h_ref, w_hbm, o_ref, wbf_ref, stage_ref, run_ref,
                        wsem):
    b = pl.program_id(0)
    nb = pl.num_programs(0)
    n_used = nused_ref[0]
    e = be_ref[b]
    run_start = jnp.logical_or(b == 0, e != be_ref[jnp.maximum(b - 1, 0)])

    def fetch(e_, slot):
        return pltpu.make_async_copy(w_hbm.at[e_], stage_ref.at[slot], wsem.at[slot])

    @pl.when(b == 0)
    def _():
        run_ref[0] = 0
        fetch(e, 0).start()

    @pl.when(jnp.logical_and(run_start, b < n_used))
    def _():
        k = run_ref[0]
        slot = k % 2
        fetch(e, slot).wait()
        wbf_ref[...] = stage_ref[slot].astype(BF16)
        b_next = nxt_ref[b]

        @pl.when(b_next < n_used)
        def _():
            fetch(be_ref[jnp.minimum(b_next, nb - 1)], 1 - slot).start()
        run_ref[0] = k + 1

    @pl.when(b < n_used)
    def _():
        y = jnp.dot(h_ref[...], wbf_ref[...], preferred_element_type=F32)
        half = y.shape[1] // 2
        o_ref[...] = _pack_halves(y[:, :half], y[:, half:])

    @pl.when(b >= n_used)
    def _():
        o_ref[...] = jnp.zeros_like(o_ref)


def _expert_down(hmid, w_down_e, blk_e, nxt_blk, n_used):
    P, f = hmid.shape
    E, _, D = w_down_e.shape
    nb = P // MOE_BLOCK
    return pl.pallas_call(
        _expert_down_kernel,
        out_shape=jax.ShapeDtypeStruct((P, D // 2), U32),
        grid_spec=pltpu.PrefetchScalarGridSpec(
            num_scalar_prefetch=3,
            grid=(nb,),
            in_specs=[pl.BlockSpec((MOE_BLOCK, f), lambda b, be, nx, nu: (jnp.minimum(b, nu[0] - 1), 0)),
                      pl.BlockSpec(memory_space=pl.ANY)],
            out_specs=pl.BlockSpec((MOE_BLOCK, D // 2), lambda b, be, nx, nu: (b, 0)),
            scratch_shapes=[pltpu.VMEM((f, D), BF16), pltpu.VMEM((2, f, D), F32),
                            pltpu.SMEM((1,), I32), pltpu.SemaphoreType.DMA((2,))]),
        compiler_params=_cparams(1, 56),
        name="expert_down",
    )(blk_e, nxt_blk, n_used, hmid, w_down_e)


def _combine_kernel(dest_hbm, y_hbm, w_ref, pre_ref, g_ref, b_ref, o_ref, idx_smem, buf_ref, isem, gsem):
    i = pl.program_id(0)
    n = pl.num_programs(0)
    tc = o_ref.shape[0]
    half = buf_ref.shape[3]
    n_idx = TOP_K * tc

    def idx_copy(tile, slot):
        return pltpu.make_async_copy(dest_hbm.at[tile], idx_smem.at[pl.ds(slot * n_idx, n_idx)],
                                     isem.at[slot])

    def issue_tile(slot):
        base = slot * n_idx

        def body(r8, carry):
            for s in range(8):
                r = r8 * 8 + s
                for k in range(TOP_K):
                    d = idx_smem[base + k * tc + r]
                    pltpu.make_async_copy(y_hbm.at[pl.ds(d, 1)], buf_ref.at[slot, k, pl.ds(r, 1)],
                                          gsem.at[slot]).start()
            return carry
        lax.fori_loop(0, tc // 8, body, 0)

    slot = i % 2

    @pl.when(i == 0)
    def _():
        idx_copy(0, 0).start()
        idx_copy(0, 0).wait()
        issue_tile(0)

        @pl.when(n > 1)
        def _():
            idx_copy(1, 1).start()

    @pl.when(i + 1 < n)
    def _():
        idx_copy(i + 1, 1 - slot).wait()
        for sl in range(2):
            @pl.when(slot == 1 - sl)
            def _():
                issue_tile(sl)

    @pl.when(i + 2 < n)
    def _():
        idx_copy(i + 2, slot).start()

    for k in range(TOP_K):
        pltpu.make_async_copy(buf_ref.at[slot, k], buf_ref.at[slot, k], gsem.at[slot]).wait()

    w = w_ref[...]
    acc_lo = jnp.zeros((tc, half), F32)
    acc_hi = jnp.zeros((tc, half), F32)
    for k in range(TOP_K):
        lo, hi = _unpack_halves(buf_ref[slot, k])
        wk = w[:, k:k + 1]
        acc_lo = acc_lo + wk * lo
        acc_hi = acc_hi + wk * hi
    y_lo = pre_ref[:, :half] + acc_lo
    y_hi = pre_ref[:, half:] + acc_hi
    inv_d = 1.0 / (2 * half)
    mu = (jnp.sum(y_lo, axis=-1, keepdims=True) + jnp.sum(y_hi, axis=-1, keepdims=True)) * inv_d
    c_lo = y_lo - mu
    c_hi = y_hi - mu
    var = (jnp.sum(c_lo * c_lo, axis=-1, keepdims=True)
           + jnp.sum(c_hi * c_hi, axis=-1, keepdims=True)) * inv_d
    rstd = lax.rsqrt(var + LN_EPS)
    o_ref[:, :half] = c_lo * rstd * g_ref[:, :half] + b_ref[:, :half]
    o_ref[:, half:] = c_hi * rstd * g_ref[:, half:] + b_ref[:, half:]


def _combine_ln(y_packed, dest_tiles, w_col, pre, g, b):
    T, D = pre.shape
    half = y_packed.shape[1]
    nt, n_idx = dest_tiles.shape
    tc = n_idx // TOP_K
    return pl.pallas_call(
        _combine_kernel,
        out_shape=jax.ShapeDtypeStruct((T, D), F32),
        grid=(nt,),
        in_specs=[pl.BlockSpec(memory_space=pl.ANY), pl.BlockSpec(memory_space=pl.ANY),
                  pl.BlockSpec((tc, TOP_K), lambda i: (i, 0)),
                  pl.BlockSpec((tc, D), lambda i: (i, 0)),
                  pl.BlockSpec((1, D), lambda i: (0, 0)),
                  pl.BlockSpec((1, D), lambda i: (0, 0))],
        out_specs=pl.BlockSpec((tc, D), lambda i: (i, 0)),
        scratch_shapes=[pltpu.SMEM((2 * n_idx,), I32),
                        pltpu.VMEM((2, TOP_K, tc, half), U32),
                        pltpu.SemaphoreType.DMA((2,)), pltpu.SemaphoreType.DMA((2,))],
        compiler_params=_cparams(1, 48),
        name="combine_ln2",
    )(dest_tiles, y_packed, w_col, pre, g.reshape(1, D), b.reshape(1, D))


def _glu_up_kernel(h_ref, wg_ref, wu_ref, o_ref, wbf_ref):
    tn = wg_ref.shape[1]

    @pl.when(pl.program_id(1) == 0)
    def _():
        wbf_ref[:, :tn] = wg_ref[...].astype(BF16)
        wbf_ref[:, tn:] = wu_ref[...].astype(BF16)

    gu = jnp.dot(h_ref[...], wbf_ref[...], preferred_element_type=F32)
    o_ref[...] = (jax.nn.silu(gu[:, :tn]) * gu[:, tn:]).astype(o_ref.dtype)


def _glu_up(h_bf, w_gu, *, tm=512, tn=384):
    T, D = h_bf.shape
    f = w_gu.shape[1] // 2
    tm = min(tm, T)
    nc = f // tn
    return pl.pallas_call(
        _glu_up_kernel,
        out_shape=jax.ShapeDtypeStruct((T, f), BF16),
        grid=(nc, T // tm),
        in_specs=[pl.BlockSpec((tm, D), lambda c, i: (i, 0)),
                  pl.BlockSpec((D, tn), lambda c, i: (0, c)),
                  pl.BlockSpec((D, tn), lambda c, i: (0, nc + c))],
        out_specs=pl.BlockSpec((tm, tn), lambda c, i: (i, c)),
        scratch_shapes=[pltpu.VMEM((D, 2 * tn), BF16)],
        compiler_params=_cparams(2, 56),
        name="shared_up",
    )(h_bf, w_gu, w_gu)


def _dense_tail_kernel(hb_ref, hj_ref, s_ref, p_ref, wd_ref, wg_ref, bg_ref, wp_ref, o_ref,
                       wdbf_ref, wgbf_ref, wpbf_ref, *, alpha):
    @pl.when(pl.program_id(1) == 0)
    def _():
        wdbf_ref[...] = wd_ref[...].astype(BF16)
        wgbf_ref[...] = wg_ref[...].astype(BF16)
        wpbf_ref[...] = wp_ref[...].astype(BF16)

    shared = jnp.dot(s_ref[...], wdbf_ref[...], preferred_element_type=F32)
    gate = jax.nn.sigmoid(jnp.dot(hb_ref[...], wgbf_ref[...], preferred_element_type=F32)
                          + bg_ref[...])
    proj = jnp.dot(p_ref[...].astype(BF16), wpbf_ref[...], preferred_element_type=F32)
    o_ref[...] = alpha * hj_ref[...] + shared + gate * proj


def _dense_tail(h1_bf, h1, s_mid, p, w_down_s, w_gate, b_gate, w_proj, alpha, *, tm=512, tn=512):
    T, D = h1.shape
    f = s_mid.shape[1]
    dp = p.shape[1]
    tm = min(tm, T)
    return pl.pallas_call(
        functools.partial(_dense_tail_kernel, alpha=alpha),
        out_shape=jax.ShapeDtypeStruct((T, D), F32),
        grid=(D // tn, T // tm),
        in_specs=[pl.BlockSpec((tm, D), lambda j, i: (i, 0)),
                  pl.BlockSpec((tm, tn), lambda j, i: (i, j)),
                  pl.BlockSpec((tm, f), lambda j, i: (i, 0)),
                  pl.BlockSpec((tm, dp), lambda j, i: (i, 0)),
                  pl.BlockSpec((f, tn), lambda j, i: (0, j)),
                  pl.BlockSpec((D, tn), lambda j, i: (0, j)),
                  pl.BlockSpec((1, tn), lambda j, i: (0, j)),
                  pl.BlockSpec((dp, tn), lambda j, i: (0, j))],
        out_specs=pl.BlockSpec((tm, tn), lambda j, i: (i, j)),
        scratch_shapes=[pltpu.VMEM((f, tn), BF16), pltpu.VMEM((D, tn), BF16), pltpu.VMEM((dp, tn), BF16)],
        compiler_params=_cparams(2, 56),
        name="dense_tail",
    )(h1_bf, h1, s_mid, p, w_down_s, w_gate, b_gate.reshape(1, D), w_proj)


def _dispatch_tables(idx, rank, counts):
    E = counts.shape[0]
    P = idx.size + E * MOE_BLOCK
    nb = P // MOE_BLOCK
    padded = (counts + MOE_BLOCK - 1) // MOE_BLOCK * MOE_BLOCK
    pend = jnp.cumsum(padded)
    pstart = pend - padded
    sel = idx[None] == jnp.arange(E, dtype=I32).reshape(E, 1, 1, 1)
    dest = rank + jnp.sum(jnp.where(sel, pstart.reshape(E, 1, 1, 1), 0), axis=0)
    blk_row = jnp.arange(nb, dtype=I32) * MOE_BLOCK
    blk_e = jnp.minimum(jnp.sum(pend[None, :] <= blk_row[:, None], axis=1), E - 1).astype(I32)
    n_used = pend[-1] // MOE_BLOCK
    last_blk = jnp.where(counts > 0, pend // MOE_BLOCK - 1, -1)
    tail_blk = n_used + jnp.arange(E, dtype=I32)
    tail_blk = jnp.where(tail_blk < nb, tail_blk, -1)
    fill = jnp.concatenate([last_blk, tail_blk]).astype(I32)
    nxt_blk = jnp.take(pend // MOE_BLOCK, blk_e).astype(I32)
    return dest.astype(I32), blk_e, nxt_blk, n_used.astype(I32).reshape(1), fill, P


def _layer(h0_f, h0_bf, p_l, w_in, b_in, conv_w, conv_b, mh_norm_g, w_conv_out, w_mlstm_out,
           w_mix_out, ln1_g, ln1_b, w_router, router_bias, w_gu_e, w_down_e, w_gu_s, w_down_s,
           w_ple_gate, b_ple_gate, w_ple_proj, ln2_g, ln2_b, alpha, batch, seq):
    T, D = h0_f.shape
    d_conv = conv_w.shape[1]
    d_v = mh_norm_g.shape[0]
    d_qk = d_v // 2
    n_if = 2 * N_HEADS
    c_qk = 3 * d_conv
    c_if = c_qk + 2 * d_qk + 2 * d_v
    c_gate = c_if + n_if
    b2d = b_in.reshape(1, -1)

    w_t = jnp.swapaxes(w_in, 0, 1)
    ya_pre = _conv_branch(h0_bf, w_t, b2d, conv_w, conv_b.reshape(1, -1), seq, d_conv)
    qkvo = _proj(h0_bf, w_t, b2d, c_qk, 2 * d_qk + 2 * d_v)
    if_col, if_row = _if_gates(h0_bf, w_t[c_if:c_gate], b_in[c_if:c_gate])
    gates = _proj(h0_bf, w_t, b2d, c_gate, 2 * D, act="sigmoid")
    yb_pre = _mlstm(qkvo, if_col, if_row, mh_norm_g.reshape(1, -1), batch, seq, d_qk, d_v)
    u = _merge(ya_pre, yb_pre, w_conv_out, w_mlstm_out, gates)
    pre1 = _mix(u, w_mix_out, h0_f, alpha)

    rt = min(ROUTE_TILE, T)
    h1, h1_bf, h1_packed, idx, rank, w_col, counts = _ln1_router(pre1, ln1_g, ln1_b, w_router,
                                                                 router_bias, rt)
    dest, blk_e, nxt_blk, n_used, fill, n_rows = _dispatch_tables(idx, rank, counts[:, 0].astype(I32))
    dest_tiles = dest.reshape(T // rt, TOP_K * rt)

    xs = _scatter_rows(h1_packed, dest_tiles, fill, n_rows)
    hmid = _expert_up(xs, w_gu_e, blk_e, nxt_blk, n_used)
    y_packed = _expert_down(hmid, w_down_e, blk_e, nxt_blk, n_used)

    s_mid = _glu_up(h1_bf, w_gu_s)
    pre2 = _dense_tail(h1_bf, h1, s_mid, p_l, w_down_s, w_ple_gate, b_ple_gate, w_ple_proj, alpha)
    return _combine_ln(y_packed, dest_tiles, w_col, pre2, ln2_g, ln2_b)


def kernel(x, p, ln_in_g, ln_in_b, w_in, b_in, conv_w, conv_b, mh_norm_g, w_conv_out, w_mlstm_out,
           w_mix_out, ln1_g, ln1_b, w_router, router_bias, w_gu_e, w_down_e, w_gu_s, w_down_s,
           w_ple_gate, b_ple_gate, w_ple_proj, ln2_g, ln2_b):
    B, S, D = x.shape
    depth = w_in.shape[0]
    alpha = (2 * depth) ** 0.25
    T = B * S
    h_f, h_bf = _ln_in(x.reshape(T, D), ln_in_g, ln_in_b)
    for l in range(depth):
        h_f = _layer(h_f, h_bf, p[l].reshape(T, -1), w_in[l], b_in[l], conv_w[l], conv_b[l],
                     mh_norm_g[l], w_conv_out[l], w_mlstm_out[l], w_mix_out[l], ln1_g[l], ln1_b[l],
                     w_router[l], router_bias[l], w_gu_e[l], w_down_e[l], w_gu_s[l], w_down_s[l],
                     w_ple_gate[l], b_ple_gate[l], w_ple_proj[l], ln2_g[l], ln2_b[l], alpha, B, S)
        if l + 1 < depth:
            h_bf = h_f.astype(BF16)
    return h_f.reshape(B, S, D)
```

```python
import functools

import jax
import jax.numpy as jnp
from jax import lax
from jax.experimental import pallas as pl
from jax.experimental.pallas import tpu as pltpu

F32 = jnp.float32
BF16 = jnp.bfloat16
U32 = jnp.uint32
I32 = jnp.int32

N_HEADS = 8
TOP_K = 8
N_GROUPS = 8
TOP_GROUPS = 4
ROUTE_SCALE = 2.5
MOE_BLOCK = 256
LN_EPS = 1e-5
CONV_W = 3
MLSTM_CHUNK = 256
ROUTE_TILE = 128
NEG_BIG = -1e30
HI_MASK = 0xFFFF0000
V7X_VMEM_BYTES = 64 * 1024 * 1024


def _cparams(n_axes, vmem_mib):
    assert vmem_mib * 1024 * 1024 <= V7X_VMEM_BYTES
    return pltpu.CompilerParams(dimension_semantics=("arbitrary",) * n_axes,
                                vmem_limit_bytes=vmem_mib * 1024 * 1024)


def _ln_rows(x, g, b):
    mu = jnp.mean(x, axis=-1, keepdims=True)
    xc = x - mu
    var = jnp.mean(xc * xc, axis=-1, keepdims=True)
    return xc * lax.rsqrt(var + LN_EPS) * g + b


def _pack_halves(lo, hi):
    lo = pltpu.bitcast(lo.astype(BF16).astype(F32), U32)
    hi = pltpu.bitcast(hi.astype(BF16).astype(F32), U32)
    return (hi & jnp.uint32(HI_MASK)) | (lo >> jnp.uint32(16))


def _ln_chunked(acc_ref, g_ref, b_ref):
    nj, _, tn = acc_ref.shape
    inv_d = 1.0 / (nj * tn)
    s = acc_ref[0].sum(axis=-1, keepdims=True)
    for jj in range(1, nj):
        s = s + acc_ref[jj].sum(axis=-1, keepdims=True)
    mu = s * inv_d
    v = jnp.zeros_like(mu)
    for jj in range(nj):
        d = acc_ref[jj] - mu
        v = v + (d * d).sum(axis=-1, keepdims=True)
    rstd = lax.rsqrt(v * inv_d + LN_EPS)

    def chunk(jj):
        cols = slice(jj * tn, (jj + 1) * tn)
        return (acc_ref[jj] - mu) * rstd * g_ref[:, cols] + b_ref[:, cols]
    return chunk


def _unpack_halves(w):
    lo = pltpu.bitcast(w << jnp.uint32(16), F32)
    hi = pltpu.bitcast(w & jnp.uint32(HI_MASK), F32)
    return lo, hi


def _ln_in_kernel(x_ref, g_ref, b_ref, of_ref, ob_ref):
    y = _ln_rows(x_ref[...], g_ref[...], b_ref[...])
    of_ref[...] = y
    ob_ref[...] = y.astype(BF16)


def _ln_in(x2, g, b, tm=256):
    T, D = x2.shape
    return pl.pallas_call(
        _ln_in_kernel,
        out_shape=(jax.ShapeDtypeStruct((T, D), F32), jax.ShapeDtypeStruct((T, D), BF16)),
        grid=(T // tm,),
        in_specs=[pl.BlockSpec((tm, D), lambda i: (i, 0)),
                  pl.BlockSpec((1, D), lambda i: (0, 0)),
                  pl.BlockSpec((1, D), lambda i: (0, 0))],
        out_specs=(pl.BlockSpec((tm, D), lambda i: (i, 0)),
                   pl.BlockSpec((tm, D), lambda i: (i, 0))),
        compiler_params=_cparams(1, 40),
        name="ln_in",
    )(x2, g.reshape(1, D), b.reshape(1, D))


_NT = (((1,), (1,)), ((), ()))


def _proj_kernel(x_ref, w_ref, b_ref, o_ref, wbf_ref, *, act):
    @pl.when(pl.program_id(1) == 0)
    def _():
        wbf_ref[...] = w_ref[...].astype(BF16)

    acc = lax.dot_general(x_ref[...], wbf_ref[...], _NT, preferred_element_type=F32) + b_ref[...]
    if act == "sigmoid":
        acc = jax.nn.sigmoid(acc)
    o_ref[...] = acc.astype(o_ref.dtype)


def _proj(x, w_t, bias2d, row0, n_cols, *, act=None, tm=1024, tn=512, out_dtype=BF16):
    T, K = x.shape
    tm = min(tm, T)
    assert n_cols % tn == 0 and T % tm == 0 and row0 % 8 == 0
    if row0 % tn == 0:
        jb = row0 // tn
        w_spec = pl.BlockSpec((tn, K), lambda j, i: (jb + j, 0))
        b_spec = pl.BlockSpec((1, tn), lambda j, i: (0, jb + j))
    else:
        w_spec = pl.BlockSpec((pl.Element(tn), pl.Element(K)),
                              lambda j, i: (pl.multiple_of(row0 + j * tn, 8), 0))
        bias2d = bias2d[:, row0:row0 + n_cols]
        b_spec = pl.BlockSpec((1, tn), lambda j, i: (0, j))
    return pl.pallas_call(
        functools.partial(_proj_kernel, act=act),
        out_shape=jax.ShapeDtypeStruct((T, n_cols), out_dtype),
        grid=(n_cols // tn, T // tm),
        in_specs=[pl.BlockSpec((tm, K), lambda j, i: (i, 0)), w_spec, b_spec],
        out_specs=pl.BlockSpec((tm, tn), lambda j, i: (i, j)),
        scratch_shapes=[pltpu.VMEM((tn, K), BF16)],
        compiler_params=_cparams(2, 56),
        name="proj_" + (act or "lin"),
    )(x, w_t, bias2d)


def _conv_kernel(x_ref, wh_ref, wc_ref, wb_ref, bh_ref, bc_ref, bb_ref, cw_ref, cb_ref,
                 o_ref, wbf_ref, zprev_ref, *, tiles_per_seq):
    i = pl.program_id(1)

    @pl.when(i == 0)
    def _():
        wbf_ref[0] = wh_ref[...].astype(BF16)
        wbf_ref[1] = wc_ref[...].astype(BF16)
        wbf_ref[2] = wb_ref[...].astype(BF16)

    @pl.when(i % tiles_per_seq == 0)
    def _():
        zprev_ref[...] = jnp.zeros_like(zprev_ref)

    x = x_ref[...]
    ha = lax.dot_general(x, wbf_ref[0], _NT, preferred_element_type=F32) + bh_ref[...]
    ca = lax.dot_general(x, wbf_ref[1], _NT, preferred_element_type=F32) + bc_ref[...]
    ba = lax.dot_general(x, wbf_ref[2], _NT, preferred_element_type=F32) + bb_ref[...]
    z = ca * ha
    tm = z.shape[0]
    prev = zprev_ref[...]
    row8 = lax.broadcasted_iota(I32, prev.shape, 0)
    z1 = pltpu.roll(z, 1, 0)
    z2 = pltpu.roll(z, 2, 0)
    p1 = pltpu.roll(prev, 1, 0)
    p2 = pltpu.roll(prev, 2, 0)
    z1 = jnp.concatenate([jnp.where(row8 < 1, p1, z1[:8]), z1[8:]], axis=0)
    z2 = jnp.concatenate([jnp.where(row8 < 2, p2, z2[:8]), z2[8:]], axis=0)
    cw = cw_ref[...]
    y = cw[0:1] * z2 + cw[1:2] * z1 + cw[2:3] * z + cb_ref[...]
    o_ref[...] = (ba * y).astype(o_ref.dtype)
    zprev_ref[...] = z[tm - 8:]


def _conv_branch(x, w_t, b2d, conv_w, conv_b2d, seq, d_conv, *, tm=512, tn=256):
    T, K = x.shape
    tm = min(tm, seq)
    assert seq % tm == 0 and d_conv % tn == 0 and tm % 8 == 0
    nb = d_conv // tn
    wspec = lambda g: pl.BlockSpec((tn, K), lambda j, i: (g * nb + j, 0))
    bspec = lambda g: pl.BlockSpec((1, tn), lambda j, i: (0, g * nb + j))
    return pl.pallas_call(
        functools.partial(_conv_kernel, tiles_per_seq=seq // tm),
        out_shape=jax.ShapeDtypeStruct((T, d_conv), BF16),
        grid=(nb, T // tm),
        in_specs=[pl.BlockSpec((tm, K), lambda j, i: (i, 0)),
                  wspec(0), wspec(1), wspec(2), bspec(0), bspec(1), bspec(2),
                  pl.BlockSpec((CONV_W, tn), lambda j, i: (0, j)),
                  pl.BlockSpec((1, tn), lambda j, i: (0, j))],
        out_specs=pl.BlockSpec((tm, tn), lambda j, i: (i, j)),
        scratch_shapes=[pltpu.VMEM((3, tn, K), BF16), pltpu.VMEM((8, tn), F32)],
        compiler_params=_cparams(2, 56),
        name="conv_branch",
    )(x, w_t, w_t, w_t, b2d, b2d, b2d, conv_w, conv_b2d)


def _if_kernel(x_ref, wp_ref, wt_ref, bc_ref, br_ref, oc_ref, or_ref):
    x = x_ref[...]
    oc_ref[...] = lax.dot_general(x, wp_ref[...], _NT, preferred_element_type=F32) + bc_ref[...]
    or_ref[...] = lax.dot_general(wt_ref[...], x, _NT, preferred_element_type=F32) + br_ref[...]


def _if_gates(x, w_if_t, b_if, tm=512):
    T, K = x.shape
    tm = min(tm, T)
    n = w_if_t.shape[0]
    w_if_t = w_if_t.astype(BF16)
    w_pad = jnp.zeros((128, K), BF16).at[:n].set(w_if_t)
    b_pad = jnp.zeros((1, 128), F32).at[0, :n].set(b_if)
    return pl.pallas_call(
        _if_kernel,
        out_shape=(jax.ShapeDtypeStruct((T, 128), F32), jax.ShapeDtypeStruct((n, T), F32)),
        grid=(T // tm,),
        in_specs=[pl.BlockSpec((tm, K), lambda i: (i, 0)),
                  pl.BlockSpec((128, K), lambda i: (0, 0)),
                  pl.BlockSpec((n, K), lambda i: (0, 0)),
                  pl.BlockSpec((1, 128), lambda i: (0, 0)),
                  pl.BlockSpec((n, 1), lambda i: (0, 0))],
        out_specs=(pl.BlockSpec((tm, 128), lambda i: (i, 0)),
                   pl.BlockSpec((n, tm), lambda i: (0, i))),
        compiler_params=_cparams(1, 32),
        name="if_gates",
    )(x, w_pad, w_if_t, b_pad, b_if.reshape(n, 1))


def _mlstm_kernel(q_ref, k_ref, v_ref, o_ref, ifc_ref, ifr_ref, g_ref, y_ref, c_ref, m_ref,
                  *, dk, dv):
    H = N_HEADS
    L = q_ref.shape[0]

    @pl.when(pl.program_id(1) == 0)
    def _():
        c_ref[...] = jnp.zeros_like(c_ref)
        m_ref[...] = jnp.zeros_like(m_ref)

    scale = dk ** -0.5
    ifc = ifc_ref[...]
    ifr = ifr_ref[...]
    ig_c = ifc[:, 0:H]
    lf_c = jax.nn.log_sigmoid(ifc[:, H:2 * H])
    ig_r = ifr[0:H, :]
    lf_r = jax.nn.log_sigmoid(ifr[H:2 * H, :])
    r = lax.broadcasted_iota(I32, (L, L), 0)
    c = lax.broadcasted_iota(I32, (L, L), 1)
    causal = r >= c
    bcum_c = jnp.dot(causal.astype(F32), lf_c, preferred_element_type=F32,
                     precision=lax.Precision.HIGHEST)
    bcum_r = jnp.dot(lf_r, (r <= c).astype(F32), preferred_element_type=F32,
                     precision=lax.Precision.HIGHEST)
    d_c = ig_c - bcum_c
    d_r = ig_r - bcum_r
    ones_col = (lax.broadcasted_iota(I32, (L, 128), 1) == 0).astype(BF16)

    for h in range(H):
        q = q_ref[:, h * dk:(h + 1) * dk]
        k = k_ref[:, h * dk:(h + 1) * dk]
        v = v_ref[:, h * dv:(h + 1) * dv]
        v_aug = jnp.concatenate([v, ones_col], axis=1)
        bc = bcum_c[:, h:h + 1]
        m_prev = m_ref[h]
        c_prev = c_ref[h]

        qk = lax.dot_general(q, k, (((1,), (1,)), ((), ())), preferred_element_type=F32) * scale
        logd = jnp.where(causal, bc + d_r[h:h + 1, :], NEG_BIG)
        log_inter = bc + m_prev
        m_t = jnp.maximum(log_inter, jnp.max(logd, axis=1, keepdims=True))
        w_inter = jnp.exp(log_inter - m_t)
        s_mat = (qk * jnp.exp(logd - m_t)).astype(BF16)
        tot = (w_inter * jnp.dot(q, c_prev.astype(BF16), preferred_element_type=F32)
               + jnp.dot(s_mat, v_aug, preferred_element_type=F32))
        den = jnp.maximum(jnp.abs(tot[:, dv:dv + 1]), jnp.exp(-m_t))
        hh = tot[:, :dv] / den
        mu = jnp.mean(hh, axis=-1, keepdims=True)
        hc = hh - mu
        var = jnp.mean(hc * hc, axis=-1, keepdims=True)
        hn = hc * lax.rsqrt(var + LN_EPS) * g_ref[:, h * dv:(h + 1) * dv]
        og = jax.nn.sigmoid(o_ref[:, h * dv:(h + 1) * dv].astype(F32))
        y_ref[:, h * dv:(h + 1) * dv] = (og * hn).astype(y_ref.dtype)

        g_tot = bc[L - 1:L, :]
        a = g_tot + d_c[:, h:h + 1]
        m_new = jnp.maximum(g_tot + m_prev, jnp.max(a, axis=0, keepdims=True))
        kw = (k.astype(F32) * (jnp.exp(a - m_new) * scale)).astype(BF16)
        kv = lax.dot_general(kw, v_aug, (((0,), (0,)), ((), ())), preferred_element_type=F32)
        c_ref[h] = jnp.exp(g_tot + m_prev - m_new) * c_prev + kv
        m_ref[h] = m_new


def _mlstm(qkvo, if_col, if_row, mh_g2d, batch, seq, d_qk, d_v, L=MLSTM_CHUNK):
    T = qkvo.shape[0]
    L = min(L, seq)
    assert seq % L == 0 and d_v == 2 * d_qk
    nc = seq // L
    dk, dv = d_qk // N_HEADS, d_v // N_HEADS
    row = lambda b, c: b * nc + c
    return pl.pallas_call(
        functools.partial(_mlstm_kernel, dk=dk, dv=dv),
        out_shape=jax.ShapeDtypeStruct((T, d_v), BF16),
        grid=(batch, nc),
        in_specs=[pl.BlockSpec((L, d_qk), lambda b, c: (row(b, c), 0)),
                  pl.BlockSpec((L, d_qk), lambda b, c: (row(b, c), 1)),
                  pl.BlockSpec((L, d_v), lambda b, c: (row(b, c), 1)),
                  pl.BlockSpec((L, d_v), lambda b, c: (row(b, c), 2)),
                  pl.BlockSpec((L, 128), lambda b, c: (row(b, c), 0)),
                  pl.BlockSpec((2 * N_HEADS, L), lambda b, c: (0, row(b, c))),
                  pl.BlockSpec((1, d_v), lambda b, c: (0, 0))],
        out_specs=pl.BlockSpec((L, d_v), lambda b, c: (row(b, c), 0)),
        scratch_shapes=[pltpu.VMEM((N_HEADS, dk, dv + 128), F32),
                        pltpu.VMEM((N_HEADS, 1, 1), F32)],
        compiler_params=_cparams(2, 40),
        name="mlstm",
    )(qkvo, qkvo, qkvo, qkvo, if_col, if_row, mh_g2d)


def _merge_kernel(a_ref, b_ref, wa_ref, wb_ref, ga_ref, gb_ref, o_ref, wabf_ref, wbbf_ref):
    @pl.when(pl.program_id(1) == 0)
    def _():
        wabf_ref[...] = wa_ref[...].astype(BF16)
        wbbf_ref[...] = wb_ref[...].astype(BF16)

    ya = jnp.dot(a_ref[...], wabf_ref[...], preferred_element_type=F32)
    yb = jnp.dot(b_ref[...], wbbf_ref[...], preferred_element_type=F32)
    u = ga_ref[...].astype(F32) * ya + gb_ref[...].astype(F32) * yb
    o_ref[...] = u.astype(o_ref.dtype)


def _merge(ya_pre, yb_pre, w_a, w_b, gates, *, tm=1024, tn=512):
    T, K = ya_pre.shape
    D = w_a.shape[1]
    tm = min(tm, T)
    nj = D // tn
    return pl.pallas_call(
        _merge_kernel,
        out_shape=jax.ShapeDtypeStruct((T, D), BF16),
        grid=(nj, T // tm),
        in_specs=[pl.BlockSpec((tm, K), lambda j, i: (i, 0)),
                  pl.BlockSpec((tm, K), lambda j, i: (i, 0)),
                  pl.BlockSpec((K, tn), lambda j, i: (0, j)),
                  pl.BlockSpec((K, tn), lambda j, i: (0, j)),
                  pl.BlockSpec((tm, tn), lambda j, i: (i, j)),
                  pl.BlockSpec((tm, tn), lambda j, i: (i, nj + j))],
        out_specs=pl.BlockSpec((tm, tn), lambda j, i: (i, j)),
        scratch_shapes=[pltpu.VMEM((K, tn), BF16), pltpu.VMEM((K, tn), BF16)],
        compiler_params=_cparams(2, 56),
        name="merge",
    )(ya_pre, yb_pre, w_a, w_b, gates, gates)


def _mix_kernel(u_ref, w_ref, h_ref, o_ref, wbf_ref, *, alpha):
    @pl.when(pl.program_id(1) == 0)
    def _():
        wbf_ref[...] = w_ref[...].astype(BF16)

    o_ref[...] = alpha * h_ref[...] + jnp.dot(u_ref[...], wbf_ref[...], preferred_element_type=F32)


def _mix(u, w, h0, alpha, *, tm=1024, tn=512):
    T, D = h0.shape
    tm = min(tm, T)
    return pl.pallas_call(
        functools.partial(_mix_kernel, alpha=alpha),
        out_shape=jax.ShapeDtypeStruct((T, D), F32),
        grid=(D // tn, T // tm),
        in_specs=[pl.BlockSpec((tm, D), lambda j, i: (i, 0)),
                  pl.BlockSpec((D, tn), lambda j, i: (0, j)),
                  pl.BlockSpec((tm, tn), lambda j, i: (i, j))],
        out_specs=pl.BlockSpec((tm, tn), lambda j, i: (i, j)),
        scratch_shapes=[pltpu.VMEM((D, tn), BF16)],
        compiler_params=_cparams(2, 56),
        name="mix_out",
    )(u, w, h0)


def _ln1_router_kernel(pre_ref, g_ref, b_ref, wr_ref, rb_ref, hf_ref, hb_ref, hp_ref,
                       idx_ref, rank_ref, wcol_ref, cnt_ref, carry_ref):
    i = pl.program_id(0)
    E = wr_ref.shape[0]
    tm, D = pre_ref.shape
    G, M = N_GROUPS, E // N_GROUPS

    @pl.when(i == 0)
    def _():
        carry_ref[...] = jnp.zeros_like(carry_ref)

    h = _ln_rows(pre_ref[...], g_ref[...], b_ref[...])
    hf_ref[...] = h
    hb_ref[...] = h.astype(BF16)
    hp_ref[...] = _pack_halves(h[:, :D // 2], h[:, D // 2:])

    logits = lax.dot_general(wr_ref[...], h, (((1,), (1,)), ((), ())),
                             preferred_element_type=F32, precision=lax.Precision.HIGHEST)
    scores = jax.nn.sigmoid(logits)
    scores3 = scores.reshape(G, M, tm)
    sel3 = (scores + rb_ref[...]).reshape(G, M, tm)
    midx = lax.broadcasted_iota(I32, (G, M, tm), 1)
    gidx3 = lax.broadcasted_iota(I32, (G, M, tm), 0)
    eidx = gidx3 * M + midx
    gidx = lax.broadcasted_iota(I32, (G, 1, tm), 0)
    neg_inf = -jnp.inf

    top1 = jnp.max(sel3, axis=1, keepdims=True)
    first1 = jnp.min(jnp.where(sel3 == top1, midx, M), axis=1, keepdims=True)
    top2 = jnp.max(jnp.where(midx == first1, neg_inf, sel3), axis=1, keepdims=True)
    gs = top1 + top2
    gkeep = jnp.zeros((G, 1, tm), F32)
    for _ in range(TOP_GROUPS):
        mx = jnp.max(gs, axis=0, keepdims=True)
        first = jnp.min(jnp.where(gs == mx, gidx, G), axis=0, keepdims=True)
        hit = gidx == first
        gkeep = jnp.where(hit, 1.0, gkeep)
        gs = jnp.where(hit, neg_inf, gs)
    selm = jnp.where(gkeep > 0.5, sel3, neg_inf)

    idx_rows, sc_rows = [], []
    chosen = jnp.zeros((G, M, tm), F32)
    for _ in range(TOP_K):
        mx = jnp.max(jnp.max(selm, axis=1, keepdims=True), axis=0, keepdims=True)
        first = jnp.min(jnp.min(jnp.where(selm == mx, eidx, E), axis=1, keepdims=True),
                        axis=0, keepdims=True)
        hit = eidx == first
        sc = jnp.sum(jnp.sum(jnp.where(hit, scores3, 0.0), axis=1, keepdims=True),
                     axis=0, keepdims=True)
        chosen = jnp.where(hit, 1.0, chosen)
        selm = jnp.where(hit, neg_inf, selm)
        idx_rows.append(first)
        sc_rows.append(sc)
    denom = sc_rows[0]
    for s in sc_rows[1:]:
        denom = denom + s

    tr = lax.broadcasted_iota(I32, (tm, tm), 0)
    tc = lax.broadcasted_iota(I32, (tm, tm), 1)
    before = (tr < tc).astype(BF16)
    chosen2 = chosen.reshape(E, tm)
    rank2 = jnp.dot(chosen2.astype(BF16), before, preferred_element_type=F32) + carry_ref[...]
    rank3 = rank2.reshape(G, M, tm)
    w_rows = []
    for k in range(TOP_K):
        hit = eidx == idx_rows[k]
        rk = jnp.sum(jnp.sum(jnp.where(hit, rank3, 0.0), axis=1, keepdims=True),
                     axis=0, keepdims=True)
        idx_ref[k:k + 1, :] = idx_rows[k].reshape(1, tm)
        rank_ref[k:k + 1, :] = rk.reshape(1, tm).astype(I32)
        w_rows.append((sc_rows[k] / denom * ROUTE_SCALE).reshape(1, tm))
    wcol_ref[...] = jnp.concatenate(w_rows, axis=0).T
    carry_ref[...] = carry_ref[...] + jnp.sum(chosen2, axis=1, keepdims=True)

    @pl.when(i == pl.num_programs(0) - 1)
    def _():
        cnt_ref[...] = carry_ref[...]


def _ln1_router(pre, g, b, w_router, router_bias, tm):
    T, D = pre.shape
    E = w_router.shape[1]
    nt = T // tm
    row = lambda i: (i, 0)
    fixed = lambda i: (0, 0)
    return pl.pallas_call(
        _ln1_router_kernel,
        out_shape=(jax.ShapeDtypeStruct((T, D), F32), jax.ShapeDtypeStruct((T, D), BF16),
                   jax.ShapeDtypeStruct((T, D // 2), U32),
                   jax.ShapeDtypeStruct((nt, TOP_K, tm), I32), jax.ShapeDtypeStruct((nt, TOP_K, tm), I32),
                   jax.ShapeDtypeStruct((T, TOP_K), F32), jax.ShapeDtypeStruct((E, 1), F32)),
        grid=(nt,),
        in_specs=[pl.BlockSpec((tm, D), row), pl.BlockSpec((1, D), fixed), pl.BlockSpec((1, D), fixed),
                  pl.BlockSpec((E, D), fixed), pl.BlockSpec((E, 1), fixed)],
        out_specs=(pl.BlockSpec((tm, D), row), pl.BlockSpec((tm, D), row), pl.BlockSpec((tm, D // 2), row),
                   pl.BlockSpec((None, TOP_K, tm), lambda i: (i, 0, 0)),
                   pl.BlockSpec((None, TOP_K, tm), lambda i: (i, 0, 0)),
                   pl.BlockSpec((tm, TOP_K), row),
                   pl.BlockSpec((E, 1), fixed)),
        scratch_shapes=[pltpu.VMEM((E, 1), F32)],
        compiler_params=_cparams(1, 40),
        name="ln1_router",
    )(pre, g.reshape(1, D), b.reshape(1, D), w_router.T, router_bias.reshape(E, 1))


def _scatter_rows_kernel(fill_ref, dest_hbm, x_ref, xs_hbm, idx_smem, zero_ref, isem, ssem, zsem):
    i = pl.program_id(0)
    n = pl.num_programs(0)
    tm = x_ref.shape[0]
    n_idx = TOP_K * tm
    R = zero_ref.shape[0]

    def idx_copy(tile, slot):
        return pltpu.make_async_copy(dest_hbm.at[tile], idx_smem.at[pl.ds(slot * n_idx, n_idx)],
                                     isem.at[slot])

    @pl.when(i == 0)
    def _():
        idx_copy(0, 0).start()
        zero_ref[...] = jnp.zeros_like(zero_ref)

        def fill(f):
            return pltpu.make_async_copy(zero_ref, xs_hbm.at[pl.ds(fill_ref[f] * R, R)], zsem)

        def start(f, carry):
            @pl.when(fill_ref[f] >= 0)
            def _():
                fill(f).start()
            return carry
        lax.fori_loop(0, fill_ref.shape[0], start, 0)

        def wait(f, carry):
            @pl.when(fill_ref[f] >= 0)
            def _():
                fill(f).wait()
            return carry
        lax.fori_loop(0, fill_ref.shape[0], wait, 0)

    slot = i % 2
    idx_copy(i, slot).wait()

    @pl.when(i + 1 < n)
    def _():
        idx_copy(i + 1, 1 - slot).start()

    base = slot * n_idx

    def issue(r8, carry):
        for s in range(8):
            r = r8 * 8 + s
            for k in range(TOP_K):
                d = idx_smem[base + k * tm + r]
                pltpu.make_async_copy(x_ref.at[pl.ds(r, 1)], xs_hbm.at[pl.ds(d, 1)], ssem).start()
        return carry
    lax.fori_loop(0, tm // 8, issue, 0)

    for k in range(TOP_K):
        pltpu.make_async_copy(x_ref, x_ref, ssem).wait()


def _scatter_rows(x_packed, dest_tiles, fill_blocks, n_rows):
    T, W = x_packed.shape
    nt, n_idx = dest_tiles.shape
    tm = n_idx // TOP_K
    return pl.pallas_call(
        _scatter_rows_kernel,
        out_shape=jax.ShapeDtypeStruct((n_rows, W), x_packed.dtype),
        grid_spec=pltpu.PrefetchScalarGridSpec(
            num_scalar_prefetch=1,
            grid=(nt,),
            in_specs=[pl.BlockSpec(memory_space=pl.ANY),
                      pl.BlockSpec((tm, W), lambda i, fb: (i, 0))],
            out_specs=pl.BlockSpec(memory_space=pl.ANY),
            scratch_shapes=[pltpu.SMEM((2 * n_idx,), I32), pltpu.VMEM((MOE_BLOCK, W), x_packed.dtype),
                            pltpu.SemaphoreType.DMA((2,)), pltpu.SemaphoreType.DMA,
                            pltpu.SemaphoreType.DMA]),
        compiler_params=pltpu.CompilerParams(dimension_semantics=("arbitrary",),
                                             vmem_limit_bytes=32 * 1024 * 1024,
                                             has_side_effects=True),
        name="dispatch_scatter",
    )(fill_blocks, dest_tiles, x_packed)


def _expert_up_kernel(be_ref, nxt_ref, nused_ref, x_ref, w_hbm, o_ref, wbf_ref, stage_ref, wsem):
    b = pl.program_id(0)
    nb = pl.num_programs(0)
    n_used = nused_ref[0]
    half = x_ref.shape[1]
    f = o_ref.shape[1]
    e = be_ref[b]
    run_start = jnp.logical_or(b == 0, e != be_ref[jnp.maximum(b - 1, 0)])

    def fetch(e_):
        return pltpu.make_async_copy(w_hbm.at[e_], stage_ref, wsem)

    @pl.when(b == 0)
    def _():
        fetch(e).start()

    @pl.when(jnp.logical_and(run_start, b < n_used))
    def _():
        fetch(e).wait()
        wbf_ref[...] = stage_ref[...].astype(BF16)
        b_next = nxt_ref[b]

        @pl.when(b_next < n_used)
        def _():
            fetch(be_ref[jnp.minimum(b_next, nb - 1)]).start()

    @pl.when(b < n_used)
    def _():
        lo, hi = _unpack_halves(x_ref[...])
        gu = (jnp.dot(lo.astype(BF16), wbf_ref[:half], preferred_element_type=F32)
              + jnp.dot(hi.astype(BF16), wbf_ref[half:], preferred_element_type=F32))
        o_ref[...] = (jax.nn.silu(gu[:, :f]) * gu[:, f:]).astype(o_ref.dtype)

    @pl.when(b >= n_used)
    def _():
        o_ref[...] = jnp.zeros_like(o_ref)


def _expert_up(xs, w_gu_e, blk_e, nxt_blk, n_used):
    P, half = xs.shape
    E, D, two_f = w_gu_e.shape
    f = two_f // 2
    assert D == 2 * half
    nb = P // MOE_BLOCK
    return pl.pallas_call(
        _expert_up_kernel,
        out_shape=jax.ShapeDtypeStruct((P, f), BF16),
        grid_spec=pltpu.PrefetchScalarGridSpec(
            num_scalar_prefetch=3,
            grid=(nb,),
            in_specs=[pl.BlockSpec((MOE_BLOCK, half), lambda b, be, nx, nu: (jnp.minimum(b, nu[0] - 1), 0)),
                      pl.BlockSpec(memory_space=pl.ANY)],
            out_specs=pl.BlockSpec((MOE_BLOCK, f), lambda b, be, nx, nu: (b, 0)),
            scratch_shapes=[pltpu.VMEM((D, two_f), BF16), pltpu.VMEM((D, two_f), F32),
                            pltpu.SemaphoreType.DMA]),
        compiler_params=_cparams(1, 56),
        name="expert_up",
    )(blk_e, nxt_blk, n_used, xs, w_gu_e)


def _expert_down_kernel(be_ref, nxt_ref, nused_ref, h_ref, w_hbm, o_ref, wbf_ref, stage_ref, run_ref,
                        wsem):
    b = pl.program_id(0)
    nb = pl.num_programs(0)
    n_used = nused_ref[0]
    e = be_ref[b]
    run_start = jnp.logical_or(b == 0, e != be_ref[jnp.maximum(b - 1, 0)])

    def fetch(e_, slot):
        return pltpu.make_async_copy(w_hbm.at[e_], stage_ref.at[slot], wsem.at[slot])

    @pl.when(b == 0)
    def _():
        run_ref[0] = 0
        fetch(e, 0).start()

    @pl.when(jnp.logical_and(run_start, b < n_used))
    def _():
        k = run_ref[0]
        slot = k % 2
        fetch(e, slot).wait()
        wbf_ref[...] = stage_ref[slot].astype(BF16)
        b_next = nxt_ref[b]

        @pl.when(b_next < n_used)
        def _():
            fetch(be_ref[jnp.minimum(b_next, nb - 1)], 1 - slot).start()
        run_ref[0] = k + 1

    @pl.when(b < n_used)
    def _():
        y = jnp.dot(h_ref[...], wbf_ref[...], preferred_element_type=F32)
        half = y.shape[1] // 2
        o_ref[...] = _pack_halves(y[:, :half], y[:, half:])

    @pl.when(b >= n_used)
    def _():
        o_ref[...] = jnp.zeros_like(o_ref)


def _expert_down(hmid, w_down_e, blk_e, nxt_blk, n_used):
    P, f = hmid.shape
    E, _, D = w_down_e.shape
    nb = P // MOE_BLOCK
    return pl.pallas_call(
        _expert_down_kernel,
        out_shape=jax.ShapeDtypeStruct((P, D // 2), U32),
        grid_spec=pltpu.PrefetchScalarGridSpec(
            num_scalar_prefetch=3,
            grid=(nb,),
            in_specs=[pl.BlockSpec((MOE_BLOCK, f), lambda b, be, nx, nu: (jnp.minimum(b, nu[0] - 1), 0)),
                      pl.BlockSpec(memory_space=pl.ANY)],
            out_specs=pl.BlockSpec((MOE_BLOCK, D // 2), lambda b, be, nx, nu: (b, 0)),
            scratch_shapes=[pltpu.VMEM((f, D), BF16), pltpu.VMEM((2, f, D), F32),
                            pltpu.SMEM((1,), I32), pltpu.SemaphoreType.DMA((2,))]),
        compiler_params=_cparams(1, 56),
        name="expert_down",
    )(blk_e, nxt_blk, n_used, hmid, w_down_e)


def _combine_kernel(dest_hbm, y_hbm, w_ref, pre_ref, g_ref, b_ref, o_ref, idx_smem, buf_ref, isem, gsem):
    i = pl.program_id(0)
    n = pl.num_programs(0)
    tc = o_ref.shape[0]
    half = buf_ref.shape[3]
    n_idx = TOP_K * tc

    def idx_copy(tile, slot):
        return pltpu.make_async_copy(dest_hbm.at[tile], idx_smem.at[pl.ds(slot * n_idx, n_idx)],
                                     isem.at[slot])

    def issue_tile(slot):
        base = slot * n_idx

        def body(r8, carry):
            for s in range(8):
                r = r8 * 8 + s
                for k in range(TOP_K):
                    d = idx_smem[base + k * tc + r]
                    pltpu.make_async_copy(y_hbm.at[pl.ds(d, 1)], buf_ref.at[slot, k, pl.ds(r, 1)],
                                          gsem.at[slot]).start()
            return carry
        lax.fori_loop(0, tc // 8, body, 0)

    slot = i % 2

    @pl.when(i == 0)
    def _():
        idx_copy(0, 0).start()
        idx_copy(0, 0).wait()
        issue_tile(0)

        @pl.when(n > 1)
        def _():
            idx_copy(1, 1).start()

    @pl.when(i + 1 < n)
    def _():
        idx_copy(i + 1, 1 - slot).wait()
        for sl in range(2):
            @pl.when(slot == 1 - sl)
            def _():
                issue_tile(sl)

    @pl.when(i + 2 < n)
    def _():
        idx_copy(i + 2, slot).start()

    for k in range(TOP_K):
        pltpu.make_async_copy(buf_ref.at[slot, k], buf_ref.at[slot, k], gsem.at[slot]).wait()

    w = w_ref[...]
    acc_lo = jnp.zeros((tc, half), F32)
    acc_hi = jnp.zeros((tc, half), F32)
    for k in range(TOP_K):
        lo, hi = _unpack_halves(buf_ref[slot, k])
        wk = w[:, k:k + 1]
        acc_lo = acc_lo + wk * lo
        acc_hi = acc_hi + wk * hi
    y_lo = pre_ref[:, :half] + acc_lo
    y_hi = pre_ref[:, half:] + acc_hi
    inv_d = 1.0 / (2 * half)
    mu = (jnp.sum(y_lo, axis=-1, keepdims=True) + jnp.sum(y_hi, axis=-1, keepdims=True)) * inv_d
    c_lo = y_lo - mu
    c_hi = y_hi - mu
    var = (jnp.sum(c_lo * c_lo, axis=-1, keepdims=True)
           + jnp.sum(c_hi * c_hi, axis=-1, keepdims=True)) * inv_d
    rstd = lax.rsqrt(var + LN_EPS)
    o_ref[:, :half] = c_lo * rstd * g_ref[:, :half] + b_ref[:, :half]
    o_ref[:, half:] = c_hi * rstd * g_ref[:, half:] + b_ref[:, half:]


def _combine_ln(y_packed, dest_tiles, w_col, pre, g, b):
    T, D = pre.shape
    half = y_packed.shape[1]
    nt, n_idx = dest_tiles.shape
    tc = n_idx // TOP_K
    return pl.pallas_call(
        _combine_kernel,
        out_shape=jax.ShapeDtypeStruct((T, D), F32),
        grid=(nt,),
        in_specs=[pl.BlockSpec(memory_space=pl.ANY), pl.BlockSpec(memory_space=pl.ANY),
                  pl.BlockSpec((tc, TOP_K), lambda i: (i, 0)),
                  pl.BlockSpec((tc, D), lambda i: (i, 0)),
                  pl.BlockSpec((1, D), lambda i: (0, 0)),
                  pl.BlockSpec((1, D), lambda i: (0, 0))],
        out_specs=pl.BlockSpec((tc, D), lambda i: (i, 0)),
        scratch_shapes=[pltpu.SMEM((2 * n_idx,), I32),
                        pltpu.VMEM((2, TOP_K, tc, half), U32),
                        pltpu.SemaphoreType.DMA((2,)), pltpu.SemaphoreType.DMA((2,))],
        compiler_params=_cparams(1, 48),
        name="combine_ln2",
    )(dest_tiles, y_packed, w_col, pre, g.reshape(1, D), b.reshape(1, D))


def _glu_up_kernel(h_ref, wg_ref, wu_ref, o_ref, wbf_ref):
    tn = wg_ref.shape[1]

    @pl.when(pl.program_id(1) == 0)
    def _():
        wbf_ref[:, :tn] = wg_ref[...].astype(BF16)
        wbf_ref[:, tn:] = wu_ref[...].astype(BF16)

    gu = jnp.dot(h_ref[...], wbf_ref[...], preferred_element_type=F32)
    o_ref[...] = (jax.nn.silu(gu[:, :tn]) * gu[:, tn:]).astype(o_ref.dtype)


def _glu_up(h_bf, w_gu, *, tm=512, tn=384):
    T, D = h_bf.shape
    f = w_gu.shape[1] // 2
    tm = min(tm, T)
    nc = f // tn
    return pl.pallas_call(
        _glu_up_kernel,
        out_shape=jax.ShapeDtypeStruct((T, f), BF16),
        grid=(nc, T // tm),
        in_specs=[pl.BlockSpec((tm, D), lambda c, i: (i, 0)),
                  pl.BlockSpec((D, tn), lambda c, i: (0, c)),
                  pl.BlockSpec((D, tn), lambda c, i: (0, nc + c))],
        out_specs=pl.BlockSpec((tm, tn), lambda c, i: (i, c)),
        scratch_shapes=[pltpu.VMEM((D, 2 * tn), BF16)],
        compiler_params=_cparams(2, 56),
        name="shared_up",
    )(h_bf, w_gu, w_gu)


def _dense_tail_kernel(hb_ref, hj_ref, s_ref, p_ref, wd_ref, wg_ref, bg_ref, wp_ref, o_ref,
                       wdbf_ref, wgbf_ref, wpbf_ref, *, alpha):
    @pl.when(pl.program_id(1) == 0)
    def _():
        wdbf_ref[...] = wd_ref[...].astype(BF16)
        wgbf_ref[...] = wg_ref[...].astype(BF16)
        wpbf_ref[...] = wp_ref[...].astype(BF16)

    shared = jnp.dot(s_ref[...], wdbf_ref[...], preferred_element_type=F32)
    gate = jax.nn.sigmoid(jnp.dot(hb_ref[...], wgbf_ref[...], preferred_element_type=F32)
                          + bg_ref[...])
    proj = jnp.dot(p_ref[...].astype(BF16), wpbf_ref[...], preferred_element_type=F32)
    o_ref[...] = alpha * hj_ref[...] + shared + gate * proj


def _dense_tail(h1_bf, h1, s_mid, p, w_down_s, w_gate, b_gate, w_proj, alpha, *, tm=512, tn=512):
    T, D = h1.shape
    f = s_mid.shape[1]
    dp = p.shape[1]
    tm = min(tm, T)
    return pl.pallas_call(
        functools.partial(_dense_tail_kernel, alpha=alpha),
        out_shape=jax.ShapeDtypeStruct((T, D), F32),
        grid=(D // tn, T // tm),
        in_specs=[pl.BlockSpec((tm, D), lambda j, i: (i, 0)),
                  pl.BlockSpec((tm, tn), lambda j, i: (i, j)),
                  pl.BlockSpec((tm, f), lambda j, i: (i, 0)),
                  pl.BlockSpec((tm, dp), lambda j, i: (i, 0)),
                  pl.BlockSpec((f, tn), lambda j, i: (0, j)),
                  pl.BlockSpec((D, tn), lambda j, i: (0, j)),
                  pl.BlockSpec((1, tn), lambda j, i: (0, j)),
                  pl.BlockSpec((dp, tn), lambda j, i: (0, j))],
        out_specs=pl.BlockSpec((tm, tn), lambda j, i: (i, j)),
        scratch_shapes=[pltpu.VMEM((f, tn), BF16), pltpu.VMEM((D, tn), BF16), pltpu.VMEM((dp, tn), BF16)],
        compiler_params=_cparams(2, 56),
        name="dense_tail",
    )(h1_bf, h1, s_mid, p, w_down_s, w_gate, b_gate.reshape(1, D), w_proj)


def _dispatch_tables(idx, rank, counts):
    E = counts.shape[0]
    P = idx.size + E * MOE_BLOCK
    nb = P // MOE_BLOCK
    padded = (counts + MOE_BLOCK - 1) // MOE_BLOCK * MOE_BLOCK
    pend = jnp.cumsum(padded)
    pstart = pend - padded
    sel = idx[None] == jnp.arange(E, dtype=I32).reshape(E, 1, 1, 1)
    dest = rank + jnp.sum(jnp.where(sel, pstart.reshape(E, 1, 1, 1), 0), axis=0)
    blk_row = jnp.arange(nb, dtype=I32) * MOE_BLOCK
    blk_e = jnp.minimum(jnp.sum(pend[None, :] <= blk_row[:, None], axis=1), E - 1).astype(I32)
    n_used = pend[-1] // MOE_BLOCK
    last_blk = jnp.where(counts > 0, pend // MOE_BLOCK - 1, -1)
    tail_blk = n_used + jnp.arange(E, dtype=I32)
    tail_blk = jnp.where(tail_blk < nb, tail_blk, -1)
    fill = jnp.concatenate([last_blk, tail_blk]).astype(I32)
    nxt_blk = jnp.take(pend // MOE_BLOCK, blk_e).astype(I32)
    return dest.astype(I32), blk_e, nxt_blk, n_used.astype(I32).reshape(1), fill, P


def _layer(h0_f, h0_bf, p_l, w_in, b_in, conv_w, conv_b, mh_norm_g, w_conv_out, w_mlstm_out,
           w_mix_out, ln1_g, ln1_b, w_router, router_bias, w_gu_e, w_down_e, w_gu_s, w_down_s,
           w_ple_gate, b_ple_gate, w_ple_proj, ln2_g, ln2_b, alpha, batch, seq):
    T, D = h0_f.shape
    d_conv = conv_w.shape[1]
    d_v = mh_norm_g.shape[0]
    d_qk = d_v // 2
    n_if = 2 * N_HEADS
    c_qk = 3 * d_conv
    c_if = c_qk + 2 * d_qk + 2 * d_v
    c_gate = c_if + n_if
    b2d = b_in.reshape(1, -1)

    w_t = jnp.swapaxes(w_in, 0, 1)
    ya_pre = _conv_branch(h0_bf, w_t, b2d, conv_w, conv_b.reshape(1, -1), seq, d_conv)
    qkvo = _proj(h0_bf, w_t, b2d, c_qk, 2 * d_qk + 2 * d_v)
    if_col, if_row = _if_gates(h0_bf, w_t[c_if:c_gate], b_in[c_if:c_gate])
    gates = _proj(h0_bf, w_t, b2d, c_gate, 2 * D, act="sigmoid")
    yb_pre = _mlstm(qkvo, if_col, if_row, mh_norm_g.reshape(1, -1), batch, seq, d_qk, d_v)
    u = _merge(ya_pre, yb_pre, w_conv_out, w_mlstm_out, gates)
    pre1 = _mix(u, w_mix_out, h0_f, alpha)

    rt = min(ROUTE_TILE, T)
    h1, h1_bf, h1_packed, idx, rank, w_col, counts = _ln1_router(pre1, ln1_g, ln1_b, w_router,
                                                                 router_bias, rt)
    dest, blk_e, nxt_blk, n_used, fill, n_rows = _dispatch_tables(idx, rank, counts[:, 0].astype(I32))
    dest_tiles = dest.reshape(T // rt, TOP_K * rt)

    xs = _scatter_rows(h1_packed, dest_tiles, fill, n_rows)
    hmid = _expert_up(xs, w_gu_e, blk_e, nxt_blk, n_used)
    y_packed = _expert_down(hmid, w_down_e, blk_e, nxt_blk, n_used)

    s_mid = _glu_up(h1_bf, w_gu_s)
    pre2 = _dense_tail(h1_bf, h1, s_mid, p_l, w_down_s, w_ple_gate, b_ple_gate, w_ple_proj, alpha)
    return _combine_ln(y_packed, dest_tiles, w_col, pre2, ln2_g, ln2_b)


def kernel(x, p, ln_in_g, ln_in_b, w_in, b_in, conv_w, conv_b, mh_norm_g, w_conv_out, w_mlstm_out,
           w_mix_out, ln1_g, ln1_b, w_router, router_bias, w_gu_e, w_down_e, w_gu_s, w_down_s,
           w_ple_gate, b_ple_gate, w_ple_proj, ln2_g, ln2_b):
    B, S, D = x.shape
    depth = w_in.shape[0]
    alpha = (2 * depth) ** 0.25
    T = B * S
    h_f, h_bf = _ln_in(x.reshape(T, D), ln_in_g, ln_in_b)
    for l in range(depth):
        h_f = _layer(h_f, h_bf, p[l].reshape(T, -1), w_in[l], b_in[l], conv_w[l], conv_b[l],
                     mh_norm_g[l], w_conv_out[l], w_mlstm_out[l], w_mix_out[l], ln1_g[l], ln1_b[l],
                     w_router[l], router_bias[l], w_gu_e[l], w_down_e[l], w_gu_s[l], w_down_s[l],
                     w_ple_gate[l], b_ple_gate[l], w_ple_proj[l], ln2_g[l], ln2_b[l], alpha, B, S)
        if l + 1 < depth:
            h_bf = h_f.astype(BF16)
    return h_f.reshape(B, S, D)
```

```python
import functools

import jax
import jax.numpy as jnp
from jax import lax
from jax.experimental import pallas as pl
from jax.experimental.pallas import tpu as pltpu

F32 = jnp.float32
BF16 = jnp.bfloat16
U32 = jnp.uint32
I32 = jnp.int32

N_HEADS = 8
TOP_K = 8
N_GROUPS = 8
TOP_GROUPS = 4
ROUTE_SCALE = 2.5
MOE_BLOCK = 256
LN_EPS = 1e-5
CONV_W = 3
MLSTM_CHUNK = 256
ROUTE_TILE = 128
NEG_BIG = -1e30
HI_MASK = 0xFFFF0000
V7X_VMEM_BYTES = 64 * 1024 * 1024


def _cparams(n_axes, vmem_mib):
    assert vmem_mib * 1024 * 1024 <= V7X_VMEM_BYTES
    return pltpu.CompilerParams(dimension_semantics=("arbitrary",) * n_axes,
                                vmem_limit_bytes=vmem_mib * 1024 * 1024)


def _ln_rows(x, g, b):
    mu = jnp.mean(x, axis=-1, keepdims=True)
    xc = x - mu
    var = jnp.mean(xc * xc, axis=-1, keepdims=True)
    return xc * lax.rsqrt(var + LN_EPS) * g + b


def _pack_halves(lo, hi):
    lo = pltpu.bitcast(lo.astype(BF16).astype(F32), U32)
    hi = pltpu.bitcast(hi.astype(BF16).astype(F32), U32)
    return (hi & jnp.uint32(HI_MASK)) | (lo >> jnp.uint32(16))


def _ln_chunked(acc_ref, g_ref, b_ref):
    nj, _, tn = acc_ref.shape
    inv_d = 1.0 / (nj * tn)
    s = acc_ref[0].sum(axis=-1, keepdims=True)
    for jj in range(1, nj):
        s = s + acc_ref[jj].sum(axis=-1, keepdims=True)
    mu = s * inv_d
    v = jnp.zeros_like(mu)
    for jj in range(nj):
        d = acc_ref[jj] - mu
        v = v + (d * d).sum(axis=-1, keepdims=True)
    rstd = lax.rsqrt(v * inv_d + LN_EPS)

    def chunk(jj):
        cols = slice(jj * tn, (jj + 1) * tn)
        return (acc_ref[jj] - mu) * rstd * g_ref[:, cols] + b_ref[:, cols]
    return chunk


def _unpack_halves(w):
    lo = pltpu.bitcast(w << jnp.uint32(16), F32)
    hi = pltpu.bitcast(w & jnp.uint32(HI_MASK), F32)
    return lo, hi


def _ln_in_kernel(x_ref, g_ref, b_ref, of_ref, ob_ref):
    y = _ln_rows(x_ref[...], g_ref[...], b_ref[...])
    of_ref[...] = y
    ob_ref[...] = y.astype(BF16)


def _ln_in(x2, g, b, tm=256):
    T, D = x2.shape
    return pl.pallas_call(
        _ln_in_kernel,
        out_shape=(jax.ShapeDtypeStruct((T, D), F32), jax.ShapeDtypeStruct((T, D), BF16)),
        grid=(T // tm,),
        in_specs=[pl.BlockSpec((tm, D), lambda i: (i, 0)),
                  pl.BlockSpec((1, D), lambda i: (0, 0)),
                  pl.BlockSpec((1, D), lambda i: (0, 0))],
        out_specs=(pl.BlockSpec((tm, D), lambda i: (i, 0)),
                   pl.BlockSpec((tm, D), lambda i: (i, 0))),
        compiler_params=_cparams(1, 40),
        name="ln_in",
    )(x2, g.reshape(1, D), b.reshape(1, D))


_NT = (((1,), (1,)), ((), ()))


def _proj_kernel(x_ref, w_ref, b_ref, o_ref, wbf_ref, *, act):
    @pl.when(pl.program_id(1) == 0)
    def _():
        wbf_ref[...] = w_ref[...].astype(BF16)

    acc = lax.dot_general(x_ref[...], wbf_ref[...], _NT, preferred_element_type=F32) + b_ref[...]
    if act == "sigmoid":
        acc = jax.nn.sigmoid(acc)
    o_ref[...] = acc.astype(o_ref.dtype)


def _proj(x, w_t, bias2d, row0, n_cols, *, act=None, tm=1024, tn=512, out_dtype=BF16):
    T, K = x.shape
    tm = min(tm, T)
    assert n_cols % tn == 0 and T % tm == 0 and row0 % 8 == 0
    if row0 % tn == 0:
        jb = row0 // tn
        w_spec = pl.BlockSpec((tn, K), lambda j, i: (jb + j, 0))
        b_spec = pl.BlockSpec((1, tn), lambda j, i: (0, jb + j))
    else:
        w_spec = pl.BlockSpec((pl.Element(tn), pl.Element(K)),
                              lambda j, i: (pl.multiple_of(row0 + j * tn, 8), 0))
        bias2d = bias2d[:, row0:row0 + n_cols]
        b_spec = pl.BlockSpec((1, tn), lambda j, i: (0, j))
    return pl.pallas_call(
        functools.partial(_proj_kernel, act=act),
        out_shape=jax.ShapeDtypeStruct((T, n_cols), out_dtype),
        grid=(n_cols // tn, T // tm),
        in_specs=[pl.BlockSpec((tm, K), lambda j, i: (i, 0)), w_spec, b_spec],
        out_specs=pl.BlockSpec((tm, tn), lambda j, i: (i, j)),
        scratch_shapes=[pltpu.VMEM((tn, K), BF16)],
        compiler_params=_cparams(2, 56),
        name="proj_" + (act or "lin"),
    )(x, w_t, bias2d)


def _conv_kernel(x_ref, wh_ref, wc_ref, wb_ref, bh_ref, bc_ref, bb_ref, cw_ref, cb_ref,
                 o_ref, wbf_ref, zprev_ref, *, tiles_per_seq):
    i = pl.program_id(1)

    @pl.when(i == 0)
    def _():
        wbf_ref[0] = wh_ref[...].astype(BF16)
        wbf_ref[1] = wc_ref[...].astype(BF16)
        wbf_ref[2] = wb_ref[...].astype(BF16)

    @pl.when(i % tiles_per_seq == 0)
    def _():
        zprev_ref[...] = jnp.zeros_like(zprev_ref)

    x = x_ref[...]
    ha = lax.dot_general(x, wbf_ref[0], _NT, preferred_element_type=F32) + bh_ref[...]
    ca = lax.dot_general(x, wbf_ref[1], _NT, preferred_element_type=F32) + bc_ref[...]
    ba = lax.dot_general(x, wbf_ref[2], _NT, preferred_element_type=F32) + bb_ref[...]
    z = ca * ha
    tm = z.shape[0]
    prev = zprev_ref[...]
    row8 = lax.broadcasted_iota(I32, prev.shape, 0)
    z1 = pltpu.roll(z, 1, 0)
    z2 = pltpu.roll(z, 2, 0)
    p1 = pltpu.roll(prev, 1, 0)
    p2 = pltpu.roll(prev, 2, 0)
    z1 = jnp.concatenate([jnp.where(row8 < 1, p1, z1[:8]), z1[8:]], axis=0)
    z2 = jnp.concatenate([jnp.where(row8 < 2, p2, z2[:8]), z2[8:]], axis=0)
    cw = cw_ref[...]
    y = cw[0:1] * z2 + cw[1:2] * z1 + cw[2:3] * z + cb_ref[...]
    o_ref[...] = (ba * y).astype(o_ref.dtype)
    zprev_ref[...] = z[tm - 8:]


def _conv_branch(x, w_t, b2d, conv_w, conv_b2d, seq, d_conv, *, tm=512, tn=256):
    T, K = x.shape
    tm = min(tm, seq)
    assert seq % tm == 0 and d_conv % tn == 0 and tm % 8 == 0
    nb = d_conv // tn
    wspec = lambda g: pl.BlockSpec((tn, K), lambda j, i: (g * nb + j, 0))
    bspec = lambda g: pl.BlockSpec((1, tn), lambda j, i: (0, g * nb + j))
    return pl.pallas_call(
        functools.partial(_conv_kernel, tiles_per_seq=seq // tm),
        out_shape=jax.ShapeDtypeStruct((T, d_conv), BF16),
        grid=(nb, T // tm),
        in_specs=[pl.BlockSpec((tm, K), lambda j, i: (i, 0)),
                  wspec(0), wspec(1), wspec(2), bspec(0), bspec(1), bspec(2),
                  pl.BlockSpec((CONV_W, tn), lambda j, i: (0, j)),
                  pl.BlockSpec((1, tn), lambda j, i: (0, j))],
        out_specs=pl.BlockSpec((tm, tn), lambda j, i: (i, j)),
        scratch_shapes=[pltpu.VMEM((3, tn, K), BF16), pltpu.VMEM((8, tn), F32)],
        compiler_params=_cparams(2, 56),
        name="conv_branch",
    )(x, w_t, w_t, w_t, b2d, b2d, b2d, conv_w, conv_b2d)


def _if_kernel(x_ref, wp_ref, wt_ref, bc_ref, br_ref, oc_ref, or_ref):
    x = x_ref[...]
    oc_ref[...] = lax.dot_general(x, wp_ref[...], _NT, preferred_element_type=F32) + bc_ref[...]
    or_ref[...] = lax.dot_general(wt_ref[...], x, _NT, preferred_element_type=F32) + br_ref[...]


def _if_gates(x, w_if_t, b_if, tm=512):
    T, K = x.shape
    tm = min(tm, T)
    n = w_if_t.shape[0]
    w_if_t = w_if_t.astype(BF16)
    w_pad = jnp.zeros((128, K), BF16).at[:n].set(w_if_t)
    b_pad = jnp.zeros((1, 128), F32).at[0, :n].set(b_if)
    return pl.pallas_call(
        _if_kernel,
        out_shape=(jax.ShapeDtypeStruct((T, 128), F32), jax.ShapeDtypeStruct((n, T), F32)),
        grid=(T // tm,),
        in_specs=[pl.BlockSpec((tm, K), lambda i: (i, 0)),
                  pl.BlockSpec((128, K), lambda i: (0, 0)),
                  pl.BlockSpec((n, K), lambda i: (0, 0)),
                  pl.BlockSpec((1, 128), lambda i: (0, 0)),
                  pl.BlockSpec((n, 1), lambda i: (0, 0))],
        out_specs=(pl.BlockSpec((tm, 128), lambda i: (i, 0)),
                   pl.BlockSpec((n, tm), lambda i: (0, i))),
        compiler_params=_cparams(1, 32),
        name="if_gates",
    )(x, w_pad, w_if_t, b_pad, b_if.reshape(n, 1))


def _mlstm_kernel(q_ref, k_ref, v_ref, o_ref, ifc_ref, ifr_ref, g_ref, y_ref, c_ref, m_ref,
                  *, dk, dv):
    H = N_HEADS
    L = q_ref.shape[0]

    @pl.when(pl.program_id(1) == 0)
    def _():
        c_ref[...] = jnp.zeros_like(c_ref)
        m_ref[...] = jnp.zeros_like(m_ref)

    scale = dk ** -0.5
    ifc = ifc_ref[...]
    ifr = ifr_ref[...]
    ig_c = ifc[:, 0:H]
    lf_c = jax.nn.log_sigmoid(ifc[:, H:2 * H])
    ig_r = ifr[0:H, :]
    lf_r = jax.nn.log_sigmoid(ifr[H:2 * H, :])
    r = lax.broadcasted_iota(I32, (L, L), 0)
    c = lax.broadcasted_iota(I32, (L, L), 1)
    causal = r >= c
    bcum_c = jnp.dot(causal.astype(F32), lf_c, preferred_element_type=F32,
                     precision=lax.Precision.HIGHEST)
    bcum_r = jnp.dot(lf_r, (r <= c).astype(F32), preferred_element_type=F32,
                     precision=lax.Precision.HIGHEST)
    d_c = ig_c - bcum_c
    d_r = ig_r - bcum_r
    ones_col = (lax.broadcasted_iota(I32, (L, 128), 1) == 0).astype(BF16)

    for h in range(H):
        q = q_ref[:, h * dk:(h + 1) * dk]
        k = k_ref[:, h * dk:(h + 1) * dk]
        v = v_ref[:, h * dv:(h + 1) * dv]
        v_aug = jnp.concatenate([v, ones_col], axis=1)
        bc = bcum_c[:, h:h + 1]
        m_prev = m_ref[h]
        c_prev = c_ref[h]

        qk = lax.dot_general(q, k, (((1,), (1,)), ((), ())), preferred_element_type=F32) * scale
        logd = jnp.where(causal, bc + d_r[h:h + 1, :], NEG_BIG)
        log_inter = bc + m_prev
        m_t = jnp.maximum(log_inter, jnp.max(logd, axis=1, keepdims=True))
        w_inter = jnp.exp(log_inter - m_t)
        s_mat = (qk * jnp.exp(logd - m_t)).astype(BF16)
        tot = (w_inter * jnp.dot(q, c_prev.astype(BF16), preferred_element_type=F32)
               + jnp.dot(s_mat, v_aug, preferred_element_type=F32))
        den = jnp.maximum(jnp.abs(tot[:, dv:dv + 1]), jnp.exp(-m_t))
        hh = tot[:, :dv] / den
        mu = jnp.mean(hh, axis=-1, keepdims=True)
        hc = hh - mu
        var = jnp.mean(hc * hc, axis=-1, keepdims=True)
        hn = hc * lax.rsqrt(var + LN_EPS) * g_ref[:, h * dv:(h + 1) * dv]
        og = jax.nn.sigmoid(o_ref[:, h * dv:(h + 1) * dv].astype(F32))
        y_ref[:, h * dv:(h + 1) * dv] = (og * hn).astype(y_ref.dtype)

        g_tot = bc[L - 1:L, :]
        a = g_tot + d_c[:, h:h + 1]
        m_new = jnp.maximum(g_tot + m_prev, jnp.max(a, axis=0, keepdims=True))
        kw = (k.astype(F32) * (jnp.exp(a - m_new) * scale)).astype(BF16)
        kv = lax.dot_general(kw, v_aug, (((0,), (0,)), ((), ())), preferred_element_type=F32)
        c_ref[h] = jnp.exp(g_tot + m_prev - m_new) * c_prev + kv
        m_ref[h] = m_new


def _mlstm(qkvo, if_col, if_row, mh_g2d, batch, seq, d_qk, d_v, L=MLSTM_CHUNK):
    T = qkvo.shape[0]
    L = min(L, seq)
    assert seq % L == 0 and d_v == 2 * d_qk
    nc = seq // L
    dk, dv = d_qk // N_HEADS, d_v // N_HEADS
    row = lambda b, c: b * nc + c
    return pl.pallas_call(
        functools.partial(_mlstm_kernel, dk=dk, dv=dv),
        out_shape=jax.ShapeDtypeStruct((T, d_v), BF16),
        grid=(batch, nc),
        in_specs=[pl.BlockSpec((L, d_qk), lambda b, c: (row(b, c), 0)),
                  pl.BlockSpec((L, d_qk), lambda b, c: (row(b, c), 1)),
                  pl.BlockSpec((L, d_v), lambda b, c: (row(b, c), 1)),
                  pl.BlockSpec((L, d_v), lambda b, c: (row(b, c), 2)),
                  pl.BlockSpec((L, 128), lambda b, c: (row(b, c), 0)),
                  pl.BlockSpec((2 * N_HEADS, L), lambda b, c: (0, row(b, c))),
                  pl.BlockSpec((1, d_v), lambda b, c: (0, 0))],
        out_specs=pl.BlockSpec((L, d_v), lambda b, c: (row(b, c), 0)),
        scratch_shapes=[pltpu.VMEM((N_HEADS, dk, dv + 128), F32),
                        pltpu.VMEM((N_HEADS, 1, 1), F32)],
        compiler_params=_cparams(2, 40),
        name="mlstm",
    )(qkvo, qkvo, qkvo, qkvo, if_col, if_row, mh_g2d)


def _merge_kernel(a_ref, b_ref, wa_ref, wb_ref, ga_ref, gb_ref, o_ref, wabf_ref, wbbf_ref):
    @pl.when(pl.program_id(1) == 0)
    def _():
        wabf_ref[...] = wa_ref[...].astype(BF16)
        wbbf_ref[...] = wb_ref[...].astype(BF16)

    ya = jnp.dot(a_ref[...], wabf_ref[...], preferred_element_type=F32)
    yb = jnp.dot(b_ref[...], wbbf_ref[...], preferred_element_type=F32)
    u = ga_ref[...].astype(F32) * ya + gb_ref[...].astype(F32) * yb
    o_ref[...] = u.astype(o_ref.dtype)


def _merge(ya_pre, yb_pre, w_a, w_b, gates, *, tm=1024, tn=512):
    T, K = ya_pre.shape
    D = w_a.shape[1]
    tm = min(tm, T)
    nj = D // tn
    return pl.pallas_call(
        _merge_kernel,
        out_shape=jax.ShapeDtypeStruct((T, D), BF16),
        grid=(nj, T // tm),
        in_specs=[pl.BlockSpec((tm, K), lambda j, i: (i, 0)),
                  pl.BlockSpec((tm, K), lambda j, i: (i, 0)),
                  pl.BlockSpec((K, tn), lambda j, i: (0, j)),
                  pl.BlockSpec((K, tn), lambda j, i: (0, j)),
                  pl.BlockSpec((tm, tn), lambda j, i: (i, j)),
                  pl.BlockSpec((tm, tn), lambda j, i: (i, nj + j))],
        out_specs=pl.BlockSpec((tm, tn), lambda j, i: (i, j)),
        scratch_shapes=[pltpu.VMEM((K, tn), BF16), pltpu.VMEM((K, tn), BF16)],
        compiler_params=_cparams(2, 56),
        name="merge",
    )(ya_pre, yb_pre, w_a, w_b, gates, gates)


def _mix_kernel(u_ref, w_ref, h_ref, o_ref, wbf_ref, *, alpha):
    @pl.when(pl.program_id(1) == 0)
    def _():
        wbf_ref[...] = w_ref[...].astype(BF16)

    o_ref[...] = alpha * h_ref[...] + jnp.dot(u_ref[...], wbf_ref[...], preferred_element_type=F32)


def _mix(u, w, h0, alpha, *, tm=1024, tn=512):
    T, D = h0.shape
    tm = min(tm, T)
    return pl.pallas_call(
        functools.partial(_mix_kernel, alpha=alpha),
        out_shape=jax.ShapeDtypeStruct((T, D), F32),
        grid=(D // tn, T // tm),
        in_specs=[pl.BlockSpec((tm, D), lambda j, i: (i, 0)),
                  pl.BlockSpec((D, tn), lambda j, i: (0, j)),
                  pl.BlockSpec((tm, tn), lambda j, i: (i, j))],
        out_specs=pl.BlockSpec((tm, tn), lambda j, i: (i, j)),
        scratch_shapes=[pltpu.VMEM((D, tn), BF16)],
        compiler_params=_cparams(2, 56),
        name="mix_out",
    )(u, w, h0)


def _ln1_router_kernel(pre_ref, g_ref, b_ref, wr_ref, rb_ref, hf_ref, hb_ref, hp_ref,
                       idx_ref, rank_ref, wcol_ref, cnt_ref, carry_ref):
    i = pl.program_id(0)
    E = wr_ref.shape[0]
    tm, D = pre_ref.shape
    G, M = N_GROUPS, E // N_GROUPS

    @pl.when(i == 0)
    def _():
        carry_ref[...] = jnp.zeros_like(carry_ref)

    h = _ln_rows(pre_ref[...], g_ref[...], b_ref[...])
    hf_ref[...] = h
    hb_ref[...] = h.astype(BF16)
    hp_ref[...] = _pack_halves(h[:, :D // 2], h[:, D // 2:])

    logits = lax.dot_general(wr_ref[...], h, (((1,), (1,)), ((), ())),
                             preferred_element_type=F32, precision=lax.Precision.HIGHEST)
    scores = jax.nn.sigmoid(logits)
    scores3 = scores.reshape(G, M, tm)
    sel3 = (scores + rb_ref[...]).reshape(G, M, tm)
    midx = lax.broadcasted_iota(I32, (G, M, tm), 1)
    gidx3 = lax.broadcasted_iota(I32, (G, M, tm), 0)
    eidx = gidx3 * M + midx
    gidx = lax.broadcasted_iota(I32, (G, 1, tm), 0)
    neg_inf = -jnp.inf

    top1 = jnp.max(sel3, axis=1, keepdims=True)
    first1 = jnp.min(jnp.where(sel3 == top1, midx, M), axis=1, keepdims=True)
    top2 = jnp.max(jnp.where(midx == first1, neg_inf, sel3), axis=1, keepdims=True)
    gs = top1 + top2
    gkeep = jnp.zeros((G, 1, tm), F32)
    for _ in range(TOP_GROUPS):
        mx = jnp.max(gs, axis=0, keepdims=True)
        first = jnp.min(jnp.where(gs == mx, gidx, G), axis=0, keepdims=True)
        hit = gidx == first
        gkeep = jnp.where(hit, 1.0, gkeep)
        gs = jnp.where(hit, neg_inf, gs)
    selm = jnp.where(gkeep > 0.5, sel3, neg_inf)

    idx_rows, sc_rows = [], []
    chosen = jnp.zeros((G, M, tm), F32)
    for _ in range(TOP_K):
        mx = jnp.max(jnp.max(selm, axis=1, keepdims=True), axis=0, keepdims=True)
        first = jnp.min(jnp.min(jnp.where(selm == mx, eidx, E), axis=1, keepdims=True),
                        axis=0, keepdims=True)
        hit = eidx == first
        sc = jnp.sum(jnp.sum(jnp.where(hit, scores3, 0.0), axis=1, keepdims=True),
                     axis=0, keepdims=True)
        chosen = jnp.where(hit, 1.0, chosen)
        selm = jnp.where(hit, neg_inf, selm)
        idx_rows.append(first)
        sc_rows.append(sc)
    denom = sc_rows[0]
    for s in sc_rows[1:]:
        denom = denom + s

    tr = lax.broadcasted_iota(I32, (tm, tm), 0)
    tc = lax.broadcasted_iota(I32, (tm, tm), 1)
    before = (tr < tc).astype(BF16)
    chosen2 = chosen.reshape(E, tm)
    rank2 = jnp.dot(chosen2.astype(BF16), before, preferred_element_type=F32) + carry_ref[...]
    rank3 = rank2.reshape(G, M, tm)
    w_rows = []
    for k in range(TOP_K):
        hit = eidx == idx_rows[k]
        rk = jnp.sum(jnp.sum(jnp.where(hit, rank3, 0.0), axis=1, keepdims=True),
                     axis=0, keepdims=True)
        idx_ref[k:k + 1, :] = idx_rows[k].reshape(1, tm)
        rank_ref[k:k + 1, :] = rk.reshape(1, tm).astype(I32)
        w_rows.append((sc_rows[k] / denom * ROUTE_SCALE).reshape(1, tm))
    wcol_ref[...] = jnp.concatenate(w_rows, axis=0).T
    carry_ref[...] = carry_ref[...] + jnp.sum(chosen2, axis=1, keepdims=True)

    @pl.when(i == pl.num_programs(0) - 1)
    def _():
        cnt_ref[...] = carry_ref[...]


def _ln1_router(pre, g, b, w_router, router_bias, tm):
    T, D = pre.shape
    E = w_router.shape[1]
    nt = T // tm
    row = lambda i: (i, 0)
    fixed = lambda i: (0, 0)
    return pl.pallas_call(
        _ln1_router_kernel,
        out_shape=(jax.ShapeDtypeStruct((T, D), F32), jax.ShapeDtypeStruct((T, D), BF16),
                   jax.ShapeDtypeStruct((T, D // 2), U32),
                   jax.ShapeDtypeStruct((nt, TOP_K, tm), I32), jax.ShapeDtypeStruct((nt, TOP_K, tm), I32),
                   jax.ShapeDtypeStruct((T, TOP_K), F32), jax.ShapeDtypeStruct((E, 1), F32)),
        grid=(nt,),
        in_specs=[pl.BlockSpec((tm, D), row), pl.BlockSpec((1, D), fixed), pl.BlockSpec((1, D), fixed),
                  pl.BlockSpec((E, D), fixed), pl.BlockSpec((E, 1), fixed)],
        out_specs=(pl.BlockSpec((tm, D), row), pl.BlockSpec((tm, D), row), pl.BlockSpec((tm, D // 2), row),
                   pl.BlockSpec((None, TOP_K, tm), lambda i: (i, 0, 0)),
                   pl.BlockSpec((None, TOP_K, tm), lambda i: (i, 0, 0)),
                   pl.BlockSpec((tm, TOP_K), row),
                   pl.BlockSpec((E, 1), fixed)),
        scratch_shapes=[pltpu.VMEM((E, 1), F32)],
        compiler_params=_cparams(1, 40),
        name="ln1_router",
    )(pre, g.reshape(1, D), b.reshape(1, D), w_router.T, router_bias.reshape(E, 1))


def _scatter_rows_kernel(fill_ref, dest_hbm, x_ref, xs_hbm, idx_smem, zero_ref, isem, ssem, zsem):
    i = pl.program_id(0)
    n = pl.num_programs(0)
    tm = x_ref.shape[0]
    n_idx = TOP_K * tm
    R = zero_ref.shape[0]

    def idx_copy(tile, slot):
        return pltpu.make_async_copy(dest_hbm.at[tile], idx_smem.at[pl.ds(slot * n_idx, n_idx)],
                                     isem.at[slot])

    @pl.when(i == 0)
    def _():
        idx_copy(0, 0).start()
        zero_ref[...] = jnp.zeros_like(zero_ref)

        def fill(f):
            return pltpu.make_async_copy(zero_ref, xs_hbm.at[pl.ds(fill_ref[f] * R, R)], zsem)

        def start(f, carry):
            @pl.when(fill_ref[f] >= 0)
            def _():
                fill(f).start()
            return carry
        lax.fori_loop(0, fill_ref.shape[0], start, 0)

        def wait(f, carry):
            @pl.when(fill_ref[f] >= 0)
            def _():
                fill(f).wait()
            return carry
        lax.fori_loop(0, fill_ref.shape[0], wait, 0)

    slot = i % 2
    idx_copy(i, slot).wait()

    @pl.when(i + 1 < n)
    def _():
        idx_copy(i + 1, 1 - slot).start()

    base = slot * n_idx

    def issue(r8, carry):
        for s in range(8):
            r = r8 * 8 + s
            for k in range(TOP_K):
                d = idx_smem[base + k * tm + r]
                pltpu.make_async_copy(x_ref.at[pl.ds(r, 1)], xs_hbm.at[pl.ds(d, 1)], ssem).start()
        return carry
    lax.fori_loop(0, tm // 8, issue, 0)

    for k in range(TOP_K):
        pltpu.make_async_copy(x_ref, x_ref, ssem).wait()


def _scatter_rows(x_packed, dest_tiles, fill_blocks, n_rows):
    T, W = x_packed.shape
    nt, n_idx = dest_tiles.shape
    tm = n_idx // TOP_K
    return pl.pallas_call(
        _scatter_rows_kernel,
        out_shape=jax.ShapeDtypeStruct((n_rows, W), x_packed.dtype),
        grid_spec=pltpu.PrefetchScalarGridSpec(
            num_scalar_prefetch=1,
            grid=(nt,),
            in_specs=[pl.BlockSpec(memory_space=pl.ANY),
                      pl.BlockSpec((tm, W), lambda i, fb: (i, 0))],
            out_specs=pl.BlockSpec(memory_space=pl.ANY),
            scratch_shapes=[pltpu.SMEM((2 * n_idx,), I32), pltpu.VMEM((MOE_BLOCK, W), x_packed.dtype),
                            pltpu.SemaphoreType.DMA((2,)), pltpu.SemaphoreType.DMA,
                            pltpu.SemaphoreType.DMA]),
        compiler_params=pltpu.CompilerParams(dimension_semantics=("arbitrary",),
                                             vmem_limit_bytes=32 * 1024 * 1024,
                                             has_side_effects=True),
        name="dispatch_scatter",
    )(fill_blocks, dest_tiles, x_packed)


def _expert_up_kernel(be_ref, nxt_ref, nused_ref, x_ref, w_hbm, o_ref, wbf_ref, stage_ref, wsem):
    b = pl.program_id(0)
    nb = pl.num_programs(0)
    n_used = nused_ref[0]
    half = x_ref.shape[1]
    f = o_ref.shape[1]
    e = be_ref[b]
    run_start = jnp.logical_or(b == 0, e != be_ref[jnp.maximum(b - 1, 0)])

    def fetch(e_):
        return pltpu.make_async_copy(w_hbm.at[e_], stage_ref, wsem)

    @pl.when(b == 0)
    def _():
        fetch(e).start()

    @pl.when(jnp.logical_and(run_start, b < n_used))
    def _():
        fetch(e).wait()
        wbf_ref[...] = stage_ref[...].astype(BF16)
        b_next = nxt_ref[b]

        @pl.when(b_next < n_used)
        def _():
            fetch(be_ref[jnp.minimum(b_next, nb - 1)]).start()

    @pl.when(b < n_used)
    def _():
        lo, hi = _unpack_halves(x_ref[...])
        gu = (jnp.dot(lo.astype(BF16), wbf_ref[:half], preferred_element_type=F32)
              + jnp.dot(hi.astype(BF16), wbf_ref[half:], preferred_element_type=F32))
        o_ref[...] = (jax.nn.silu(gu[:, :f]) * gu[:, f:]).astype(o_ref.dtype)

    @pl.when(b >= n_used)
    def _():
        o_ref[...] = jnp.zeros_like(o_ref)


def _expert_up(xs, w_gu_e, blk_e, nxt_blk, n_used):
    P, half = xs.shape
    E, D, two_f = w_gu_e.shape
    f = two_f // 2
    assert D == 2 * half
    nb = P // MOE_BLOCK
    return pl.pallas_call(
        _expert_up_kernel,
        out_shape=jax.ShapeDtypeStruct((P, f), BF16),
        grid_spec=pltpu.PrefetchScalarGridSpec(
            num_scalar_prefetch=3,
            grid=(nb,),
            in_specs=[pl.BlockSpec((MOE_BLOCK, half), lambda b, be, nx, nu: (jnp.minimum(b, nu[0] - 1), 0)),
                      pl.BlockSpec(memory_space=pl.ANY)],
            out_specs=pl.BlockSpec((MOE_BLOCK, f), lambda b, be, nx, nu: (b, 0)),
            scratch_shapes=[pltpu.VMEM((D, two_f), BF16), pltpu.VMEM((D, two_f), F32),
                            pltpu.SemaphoreType.DMA]),
        compiler_params=_cparams(1, 56),
        name="expert_up",
    )(blk_e, nxt_blk, n_used, xs, w_gu_e)


def _expert_down_kernel(be_ref, nxt_ref, nused_ref, h_ref, w_hbm, o_ref, wbf_ref, stage_ref, run_ref,
                        wsem):
    b = pl.program_id(0)
    nb = pl.num_programs(0)
    n_used = nused_ref[0]
    e = be_ref[b]
    run_start = jnp.logical_or(b == 0, e != be_ref[jnp.maximum(b - 1, 0)])

    def fetch(e_, slot):
        return pltpu.make_async_copy(w_hbm.at[e_], stage_ref.at[slot], wsem.at[slot])

    @pl.when(b == 0)
    def _():
        run_ref[0] = 0
        fetch(e, 0).start()

    @pl.when(jnp.logical_and(run_start, b < n_used))
    def _():
        k = run_ref[0]
        slot = k % 2
        fetch(e, slot).wait()
        wbf_ref[...] = stage_ref[slot].astype(BF16)
        b_next = nxt_ref[b]

        @pl.when(b_next < n_used)
        def _():
            fetch(be_ref[jnp.minimum(b_next, nb - 1)], 1 - slot).start()
        run_ref[0] = k + 1

    @pl.when(b < n_used)
    def _():
        y = jnp.dot(h_ref[...], wbf_ref[...], preferred_element_type=F32)
        half = y.shape[1] // 2
        o_ref[...] = _pack_halves(y[:, :half], y[:, half:])

    @pl.when(b >= n_used)
    def _():
        o_ref[...] = jnp.zeros_like(o_ref)


def _expert_down(hmid, w_down_e, blk_e, nxt_blk, n_used):
    P, f = hmid.shape
    E, _, D = w_down_e.shape
    nb = P // MOE_BLOCK
    return pl.pallas_call(
        _expert_down_kernel,
        out_shape=jax.ShapeDtypeStruct((P, D // 2), U32),
        grid_spec=pltpu.PrefetchScalarGridSpec(
            num_scalar_prefetch=3,
            grid=(nb,),
            in_specs=[pl.BlockSpec((MOE_BLOCK, f), lambda b, be, nx, nu: (jnp.minimum(b, nu[0] - 1), 0)),
                      pl.BlockSpec(memory_space=pl.ANY)],
            out_specs=pl.BlockSpec((MOE_BLOCK, D // 2), lambda b, be, nx, nu: (b, 0)),
            scratch_shapes=[pltpu.VMEM((f, D), BF16), pltpu.VMEM((2, f, D), F32),
                            pltpu.SMEM((1,), I32), pltpu.SemaphoreType.DMA((2,))]),
        compiler_params=_cparams(1, 56),
        name="expert_down",
    )(blk_e, nxt_blk, n_used, hmid, w_down_e)


def _combine_kernel(dest_hbm, y_hbm, w_ref, pre_ref, g_ref, b_ref, o_ref, idx_smem, buf_ref, isem, gsem):
    i = pl.program_id(0)
    n = pl.num_programs(0)
    tc = o_ref.shape[0]
    half = buf_ref.shape[3]
    n_idx = TOP_K * tc

    def idx_copy(tile, slot):
        return pltpu.make_async_copy(dest_hbm.at[tile], idx_smem.at[pl.ds(slot * n_idx, n_idx)],
                                     isem.at[slot])

    def issue_tile(slot):
        base = slot * n_idx

        def body(r8, carry):
            for s in range(8):
                r = r8 * 8 + s
                for k in range(TOP_K):
                    d = idx_smem[base + k * tc + r]
                    pltpu.make_async_copy(y_hbm.at[pl.ds(d, 1)], buf_ref.at[slot, k, pl.ds(r, 1)],
                                          gsem.at[slot]).start()
            return carry
        lax.fori_loop(0, tc // 8, body, 0)

    slot = i % 2

    @pl.when(i == 0)
    def _():
        idx_copy(0, 0).start()
        idx_copy(0, 0).wait()
        issue_tile(0)

        @pl.when(n > 1)
        def _():
            idx_copy(1, 1).start()

    @pl.when(i + 1 < n)
    def _():
        idx_copy(i + 1, 1 - slot).wait()
        for sl in range(2):
            @pl.when(slot == 1 - sl)
            def _():
                issue_tile(sl)

    @pl.when(i + 2 < n)
    def _():
        idx_copy(i + 2, slot).start()

    for k in range(TOP_K):
        pltpu.make_async_copy(buf_ref.at[slot, k], buf_ref.at[slot, k], gsem.at[slot]).wait()

    w = w_ref[...]
    acc_lo = jnp.zeros((tc, half), F32)
    acc_hi = jnp.zeros((tc, half), F32)
    for k in range(TOP_K):
        lo, hi = _unpack_halves(buf_ref[slot, k])
        wk = w[:, k:k + 1]
        acc_lo = acc_lo + wk * lo
        acc_hi = acc_hi + wk * hi
    y_lo = pre_ref[:, :half] + acc_lo
    y_hi = pre_ref[:, half:] + acc_hi
    inv_d = 1.0 / (2 * half)
    mu = (jnp.sum(y_lo, axis=-1, keepdims=True) + jnp.sum(y_hi, axis=-1, keepdims=True)) * inv_d
    c_lo = y_lo - mu
    c_hi = y_hi - mu
    var = (jnp.sum(c_lo * c_lo, axis=-1, keepdims=True)
           + jnp.sum(c_hi * c_hi, axis=-1, keepdims=True)) * inv_d
    rstd = lax.rsqrt(var + LN_EPS)
    o_ref[:, :half] = c_lo * rstd * g_ref[:, :half] + b_ref[:, :half]
    o_ref[:, half:] = c_hi * rstd * g_ref[:, half:] + b_ref[:, half:]


def _combine_ln(y_packed, dest_tiles, w_col, pre, g, b):
    T, D = pre.shape
    half = y_packed.shape[1]
    nt, n_idx = dest_tiles.shape
    tc = n_idx // TOP_K
    return pl.pallas_call(
        _combine_kernel,
        out_shape=jax.ShapeDtypeStruct((T, D), F32),
        grid=(nt,),
        in_specs=[pl.BlockSpec(memory_space=pl.ANY), pl.BlockSpec(memory_space=pl.ANY),
                  pl.BlockSpec((tc, TOP_K), lambda i: (i, 0)),
                  pl.BlockSpec((tc, D), lambda i: (i, 0)),
                  pl.BlockSpec((1, D), lambda i: (0, 0)),
                  pl.BlockSpec((1, D), lambda i: (0, 0))],
        out_specs=pl.BlockSpec((tc, D), lambda i: (i, 0)),
        scratch_shapes=[pltpu.SMEM((2 * n_idx,), I32),
                        pltpu.VMEM((2, TOP_K, tc, half), U32),
                        pltpu.SemaphoreType.DMA((2,)), pltpu.SemaphoreType.DMA((2,))],
        compiler_params=_cparams(1, 48),
        name="combine_ln2",
    )(dest_tiles, y_packed, w_col, pre, g.reshape(1, D), b.reshape(1, D))


def _glu_up_kernel(h_ref, wg_ref, wu_ref, o_ref, wbf_ref):
    tn = wg_ref.shape[1]

    @pl.when(pl.program_id(1) == 0)
    def _():
        wbf_ref[:, :tn] = wg_ref[...].astype(BF16)
        wbf_ref[:, tn:] = wu_ref[...].astype(BF16)

    gu = jnp.dot(h_ref[...], wbf_ref[...], preferred_element_type=F32)
    o_ref[...] = (jax.nn.silu(gu[:, :tn]) * gu[:, tn:]).astype(o_ref.dtype)


def _glu_up(h_bf, w_gu, *, tm=512, tn=384):
    T, D = h_bf.shape
    f = w_gu.shape[1] // 2
    tm = min(tm, T)
    nc = f // tn
    return pl.pallas_call(
        _glu_up_kernel,
        out_shape=jax.ShapeDtypeStruct((T, f), BF16),
        grid=(nc, T // tm),
        in_specs=[pl.BlockSpec((tm, D), lambda c, i: (i, 0)),
                  pl.BlockSpec((D, tn), lambda c, i: (0, c)),
                  pl.BlockSpec((D, tn), lambda c, i: (0, nc + c))],
        out_specs=pl.BlockSpec((tm, tn), lambda c, i: (i, c)),
        scratch_shapes=[pltpu.VMEM((D, 2 * tn), BF16)],
        compiler_params=_cparams(2, 56),
        name="shared_up",
    )(h_bf, w_gu, w_gu)


def _dense_tail_kernel(fill_ref, dest_hbm, x_ref, hb_ref, hj_ref, s_ref, p_ref, wd_ref, wg_ref, bg_ref,
                       wp_ref, o_ref, xs_hbm, wdbf_ref, wgbf_ref, wpbf_ref, idx_smem, zero_ref,
                       isem, ssem, zsem, *, alpha, k_per_step):
    j = pl.program_id(0)
    i = pl.program_id(1)
    ni = pl.num_programs(1)
    step = j * ni + i
    tm = x_ref.shape[0]
    n_idx = k_per_step * tm
    R = zero_ref.shape[0]

    def idx_copies(jj, ii, slot):
        return [pltpu.make_async_copy(dest_hbm.at[jj * k_per_step + kk, pl.ds(pl.multiple_of(ii * tm, 128), tm)],
                                      idx_smem.at[pl.ds(slot * n_idx + kk * tm, tm)], isem.at[slot])
                for kk in range(k_per_step)]

    @pl.when(step == 0)
    def _():
        for cp in idx_copies(0, 0, 0):
            cp.start()
        zero_ref[...] = jnp.zeros_like(zero_ref)

        def fill(f):
            return pltpu.make_async_copy(zero_ref, xs_hbm.at[pl.ds(fill_ref[f] * R, R)], zsem)

        def start(f, carry):
            @pl.when(fill_ref[f] >= 0)
            def _():
                fill(f).start()
            return carry
        lax.fori_loop(0, fill_ref.shape[0], start, 0)

        def wait(f, carry):
            @pl.when(fill_ref[f] >= 0)
            def _():
                fill(f).wait()
            return carry
        lax.fori_loop(0, fill_ref.shape[0], wait, 0)

    @pl.when(i == 0)
    def _():
        wdbf_ref[...] = wd_ref[...].astype(BF16)
        wgbf_ref[...] = wg_ref[...].astype(BF16)
        wpbf_ref[...] = wp_ref[...].astype(BF16)

    slot = step % 2
    for cp in idx_copies(j, i, slot):
        cp.wait()

    @pl.when(step + 1 < pl.num_programs(0) * ni)
    def _():
        nxt = step + 1
        for cp in idx_copies(nxt // ni, nxt % ni, 1 - slot):
            cp.start()

    base = slot * n_idx
    for kk in range(k_per_step):
        for r in range(tm):
            d = idx_smem[base + kk * tm + r]
            pltpu.make_async_copy(x_ref.at[pl.ds(r, 1)], xs_hbm.at[pl.ds(d, 1)], ssem).start()

    shared = jnp.dot(s_ref[...], wdbf_ref[...], preferred_element_type=F32)
    gate = jax.nn.sigmoid(jnp.dot(hb_ref[...], wgbf_ref[...], preferred_element_type=F32)
                          + bg_ref[...])
    proj = jnp.dot(p_ref[...].astype(BF16), wpbf_ref[...], preferred_element_type=F32)
    o_ref[...] = alpha * hj_ref[...] + shared + gate * proj

    for kk in range(k_per_step):
        pltpu.make_async_copy(x_ref, x_ref, ssem).wait()


def _dense_tail(h1_bf, h1, h1_packed, dest_kt, fill_blocks, n_rows, s_mid, p, w_down_s, w_gate, b_gate,
                w_proj, alpha, *, tm=512, tn=512):
    T, D = h1.shape
    f = s_mid.shape[1]
    dp = p.shape[1]
    W = h1_packed.shape[1]
    tm = min(tm, T)
    nj = D // tn
    assert TOP_K % nj == 0 and tm % 128 == 0
    k_per_step = TOP_K // nj
    row = lambda j, i, fb: (i, 0)
    col = lambda j, i, fb: (0, j)
    tile = lambda j, i, fb: (i, j)
    return pl.pallas_call(
        functools.partial(_dense_tail_kernel, alpha=alpha, k_per_step=k_per_step),
        out_shape=(jax.ShapeDtypeStruct((T, D), F32), jax.ShapeDtypeStruct((n_rows, W), h1_packed.dtype)),
        grid_spec=pltpu.PrefetchScalarGridSpec(
            num_scalar_prefetch=1,
            grid=(nj, T // tm),
            in_specs=[pl.BlockSpec(memory_space=pl.ANY),
                      pl.BlockSpec((tm, W), row),
                      pl.BlockSpec((tm, D), row),
                      pl.BlockSpec((tm, tn), tile),
                      pl.BlockSpec((tm, f), row),
                      pl.BlockSpec((tm, dp), row),
                      pl.BlockSpec((f, tn), col),
                      pl.BlockSpec((D, tn), col),
                      pl.BlockSpec((1, tn), col),
                      pl.BlockSpec((dp, tn), col)],
            out_specs=(pl.BlockSpec((tm, tn), tile), pl.BlockSpec(memory_space=pl.ANY)),
            scratch_shapes=[pltpu.VMEM((f, tn), BF16), pltpu.VMEM((D, tn), BF16), pltpu.VMEM((dp, tn), BF16),
                            pltpu.SMEM((2 * k_per_step * tm,), I32), pltpu.VMEM((MOE_BLOCK, W), h1_packed.dtype),
                            pltpu.SemaphoreType.DMA((2,)), pltpu.SemaphoreType.DMA, pltpu.SemaphoreType.DMA]),
        compiler_params=pltpu.CompilerParams(dimension_semantics=("arbitrary", "arbitrary"),
                                             vmem_limit_bytes=58 * 1024 * 1024, has_side_effects=True),
        name="dense_tail_dispatch",
    )(fill_blocks, dest_kt, h1_packed, h1_bf, h1, s_mid, p, w_down_s, w_gate, b_gate.reshape(1, D), w_proj)


def _dispatch_tables(idx, rank, counts):
    E = counts.shape[0]
    P = idx.size + E * MOE_BLOCK
    nb = P // MOE_BLOCK
    padded = (counts + MOE_BLOCK - 1) // MOE_BLOCK * MOE_BLOCK
    pend = jnp.cumsum(padded)
    pstart = pend - padded
    sel = idx[None] == jnp.arange(E, dtype=I32).reshape(E, 1, 1, 1)
    dest = rank + jnp.sum(jnp.where(sel, pstart.reshape(E, 1, 1, 1), 0), axis=0)
    blk_row = jnp.arange(nb, dtype=I32) * MOE_BLOCK
    blk_e = jnp.minimum(jnp.sum(pend[None, :] <= blk_row[:, None], axis=1), E - 1).astype(I32)
    n_used = pend[-1] // MOE_BLOCK
    last_blk = jnp.where(counts > 0, pend // MOE_BLOCK - 1, -1)
    tail_blk = n_used + jnp.arange(E, dtype=I32)
    tail_blk = jnp.where(tail_blk < nb, tail_blk, -1)
    fill = jnp.concatenate([last_blk, tail_blk]).astype(I32)
    nxt_blk = jnp.take(pend // MOE_BLOCK, blk_e).astype(I32)
    return dest.astype(I32), blk_e, nxt_blk, n_used.astype(I32).reshape(1), fill, P


def _layer(h0_f, h0_bf, p_l, w_in, b_in, conv_w, conv_b, mh_norm_g, w_conv_out, w_mlstm_out,
           w_mix_out, ln1_g, ln1_b, w_router, router_bias, w_gu_e, w_down_e, w_gu_s, w_down_s,
           w_ple_gate, b_ple_gate, w_ple_proj, ln2_g, ln2_b, alpha, batch, seq):
    T, D = h0_f.shape
    d_conv = conv_w.shape[1]
    d_v = mh_norm_g.shape[0]
    d_qk = d_v // 2
    n_if = 2 * N_HEADS
    c_qk = 3 * d_conv
    c_if = c_qk + 2 * d_qk + 2 * d_v
    c_gate = c_if + n_if
    b2d = b_in.reshape(1, -1)

    w_t = jnp.swapaxes(w_in, 0, 1)
    ya_pre = _conv_branch(h0_bf, w_t, b2d, conv_w, conv_b.reshape(1, -1), seq, d_conv)
    qkvo = _proj(h0_bf, w_t, b2d, c_qk, 2 * d_qk + 2 * d_v)
    if_col, if_row = _if_gates(h0_bf, w_t[c_if:c_gate], b_in[c_if:c_gate])
    gates = _proj(h0_bf, w_t, b2d, c_gate, 2 * D, act="sigmoid")
    yb_pre = _mlstm(qkvo, if_col, if_row, mh_norm_g.reshape(1, -1), batch, seq, d_qk, d_v)
    u = _merge(ya_pre, yb_pre, w_conv_out, w_mlstm_out, gates)
    pre1 = _mix(u, w_mix_out, h0_f, alpha)

    rt = min(ROUTE_TILE, T)
    h1, h1_bf, h1_packed, idx, rank, w_col, counts = _ln1_router(pre1, ln1_g, ln1_b, w_router,
                                                                 router_bias, rt)
    dest, blk_e, nxt_blk, n_used, fill, n_rows = _dispatch_tables(idx, rank, counts[:, 0].astype(I32))
    dest_tiles = dest.reshape(T // rt, TOP_K * rt)
    dest_kt = dest.transpose(1, 0, 2).reshape(TOP_K, T)

    s_mid = _glu_up(h1_bf, w_gu_s)
    pre2, xs = _dense_tail(h1_bf, h1, h1_packed, dest_kt, fill, n_rows, s_mid, p_l, w_down_s,
                           w_ple_gate, b_ple_gate, w_ple_proj, alpha)

    hmid = _expert_up(xs, w_gu_e, blk_e, nxt_blk, n_used)
    y_packed = _expert_down(hmid, w_down_e, blk_e, nxt_blk, n_used)
    return _combine_ln(y_packed, dest_tiles, w_col, pre2, ln2_g, ln2_b)


def kernel(x, p, ln_in_g, ln_in_b, w_in, b_in, conv_w, conv_b, mh_norm_g, w_conv_out, w_mlstm_out,
           w_mix_out, ln1_g, ln1_b, w_router, router_bias, w_gu_e, w_down_e, w_gu_s, w_down_s,
           w_ple_gate, b_ple_gate, w_ple_proj, ln2_g, ln2_b):
    B, S, D = x.shape
    depth = w_in.shape[0]
    alpha = (2 * depth) ** 0.25
    T = B * S
    h_f, h_bf = _ln_in(x.reshape(T, D), ln_in_g, ln_in_b)
    for l in range(depth):
        h_f = _layer(h_f, h_bf, p[l].reshape(T, -1), w_in[l], b_in[l], conv_w[l], conv_b[l],
                     mh_norm_g[l], w_conv_out[l], w_mlstm_out[l], w_mix_out[l], ln1_g[l], ln1_b[l],
                     w_router[l], router_bias[l], w_gu_e[l], w_down_e[l], w_gu_s[l], w_down_s[l],
                     w_ple_gate[l], b_ple_gate[l], w_ple_proj[l], ln2_g[l], ln2_b[l], alpha, B, S)
        if l + 1 < depth:
            h_bf = h_f.astype(BF16)
    return h_f.reshape(B, S, D)
```

```python
import functools

import jax
import jax.numpy as jnp
from jax import lax
from jax.experimental import pallas as pl
from jax.experimental.pallas import tpu as pltpu

F32 = jnp.float32
BF16 = jnp.bfloat16
U32 = jnp.uint32
I32 = jnp.int32

N_HEADS = 8
TOP_K = 8
N_GROUPS = 8
TOP_GROUPS = 4
ROUTE_SCALE = 2.5
MOE_BLOCK = 256
LN_EPS = 1e-5
CONV_W = 3
MLSTM_CHUNK = 256
ROUTE_TILE = 128
NEG_BIG = -1e30
HI_MASK = 0xFFFF0000
V7X_VMEM_BYTES = 64 * 1024 * 1024


def _cparams(n_axes, vmem_mib):
    assert vmem_mib * 1024 * 1024 <= V7X_VMEM_BYTES
    return pltpu.CompilerParams(dimension_semantics=("arbitrary",) * n_axes,
                                vmem_limit_bytes=vmem_mib * 1024 * 1024)


def _ln_rows(x, g, b):
    mu = jnp.mean(x, axis=-1, keepdims=True)
    xc = x - mu
    var = jnp.mean(xc * xc, axis=-1, keepdims=True)
    return xc * lax.rsqrt(var + LN_EPS) * g + b


def _pack_halves(lo, hi):
    lo = pltpu.bitcast(lo.astype(BF16).astype(F32), U32)
    hi = pltpu.bitcast(hi.astype(BF16).astype(F32), U32)
    return (hi & jnp.uint32(HI_MASK)) | (lo >> jnp.uint32(16))


def _ln_chunked(acc_ref, g_ref, b_ref):
    nj, _, tn = acc_ref.shape
    inv_d = 1.0 / (nj * tn)
    s = acc_ref[0].sum(axis=-1, keepdims=True)
    for jj in range(1, nj):
        s = s + acc_ref[jj].sum(axis=-1, keepdims=True)
    mu = s * inv_d
    v = jnp.zeros_like(mu)
    for jj in range(nj):
        d = acc_ref[jj] - mu
        v = v + (d * d).sum(axis=-1, keepdims=True)
    rstd = lax.rsqrt(v * inv_d + LN_EPS)

    def chunk(jj):
        cols = slice(jj * tn, (jj + 1) * tn)
        return (acc_ref[jj] - mu) * rstd * g_ref[:, cols] + b_ref[:, cols]
    return chunk


def _unpack_halves(w):
    lo = pltpu.bitcast(w << jnp.uint32(16), F32)
    hi = pltpu.bitcast(w & jnp.uint32(HI_MASK), F32)
    return lo, hi


def _ln_in_kernel(x_ref, g_ref, b_ref, of_ref, ob_ref):
    y = _ln_rows(x_ref[...], g_ref[...], b_ref[...])
    of_ref[...] = y
    ob_ref[...] = y.astype(BF16)


def _ln_in(x2, g, b, tm=256):
    T, D = x2.shape
    return pl.pallas_call(
        _ln_in_kernel,
        out_shape=(jax.ShapeDtypeStruct((T, D), F32), jax.ShapeDtypeStruct((T, D), BF16)),
        grid=(T // tm,),
        in_specs=[pl.BlockSpec((tm, D), lambda i: (i, 0)),
                  pl.BlockSpec((1, D), lambda i: (0, 0)),
                  pl.BlockSpec((1, D), lambda i: (0, 0))],
        out_specs=(pl.BlockSpec((tm, D), lambda i: (i, 0)),
                   pl.BlockSpec((tm, D), lambda i: (i, 0))),
        compiler_params=_cparams(1, 40),
        name="ln_in",
    )(x2, g.reshape(1, D), b.reshape(1, D))


_NT = (((1,), (1,)), ((), ()))


def _proj_kernel(x_ref, w_ref, b_ref, o_ref, wbf_ref, *, act):
    @pl.when(pl.program_id(1) == 0)
    def _():
        wbf_ref[...] = w_ref[...].astype(BF16)

    acc = lax.dot_general(x_ref[...], wbf_ref[...], _NT, preferred_element_type=F32) + b_ref[...]
    if act == "sigmoid":
        acc = jax.nn.sigmoid(acc)
    o_ref[...] = acc.astype(o_ref.dtype)


def _proj(x, w_t, bias2d, row0, n_cols, *, act=None, tm=1024, tn=512, out_dtype=BF16):
    T, K = x.shape
    tm = min(tm, T)
    assert n_cols % tn == 0 and T % tm == 0 and row0 % 8 == 0
    if row0 % tn == 0:
        jb = row0 // tn
        w_spec = pl.BlockSpec((tn, K), lambda j, i: (jb + j, 0))
        b_spec = pl.BlockSpec((1, tn), lambda j, i: (0, jb + j))
    else:
        w_spec = pl.BlockSpec((pl.Element(tn), pl.Element(K)),
                              lambda j, i: (pl.multiple_of(row0 + j * tn, 8), 0))
        bias2d = bias2d[:, row0:row0 + n_cols]
        b_spec = pl.BlockSpec((1, tn), lambda j, i: (0, j))
    return pl.pallas_call(
        functools.partial(_proj_kernel, act=act),
        out_shape=jax.ShapeDtypeStruct((T, n_cols), out_dtype),
        grid=(n_cols // tn, T // tm),
        in_specs=[pl.BlockSpec((tm, K), lambda j, i: (i, 0)), w_spec, b_spec],
        out_specs=pl.BlockSpec((tm, tn), lambda j, i: (i, j)),
        scratch_shapes=[pltpu.VMEM((tn, K), BF16)],
        compiler_params=_cparams(2, 56),
        name="proj_" + (act or "lin"),
    )(x, w_t, bias2d)


def _conv_kernel(x_ref, wh_ref, wc_ref, wb_ref, bh_ref, bc_ref, bb_ref, cw_ref, cb_ref,
                 o_ref, wbf_ref, zprev_ref, *, tiles_per_seq):
    i = pl.program_id(1)

    @pl.when(i == 0)
    def _():
        wbf_ref[0] = wh_ref[...].astype(BF16)
        wbf_ref[1] = wc_ref[...].astype(BF16)
        wbf_ref[2] = wb_ref[...].astype(BF16)

    @pl.when(i % tiles_per_seq == 0)
    def _():
        zprev_ref[...] = jnp.zeros_like(zprev_ref)

    x = x_ref[...]
    ha = lax.dot_general(x, wbf_ref[0], _NT, preferred_element_type=F32) + bh_ref[...]
    ca = lax.dot_general(x, wbf_ref[1], _NT, preferred_element_type=F32) + bc_ref[...]
    ba = lax.dot_general(x, wbf_ref[2], _NT, preferred_element_type=F32) + bb_ref[...]
    z = ca * ha
    tm = z.shape[0]
    prev = zprev_ref[...]
    row8 = lax.broadcasted_iota(I32, prev.shape, 0)
    z1 = pltpu.roll(z, 1, 0)
    z2 = pltpu.roll(z, 2, 0)
    p1 = pltpu.roll(prev, 1, 0)
    p2 = pltpu.roll(prev, 2, 0)
    z1 = jnp.concatenate([jnp.where(row8 < 1, p1, z1[:8]), z1[8:]], axis=0)
    z2 = jnp.concatenate([jnp.where(row8 < 2, p2, z2[:8]), z2[8:]], axis=0)
    cw = cw_ref[...]
    y = cw[0:1] * z2 + cw[1:2] * z1 + cw[2:3] * z + cb_ref[...]
    o_ref[...] = (ba * y).astype(o_ref.dtype)
    zprev_ref[...] = z[tm - 8:]


def _conv_branch(x, w_t, b2d, conv_w, conv_b2d, seq, d_conv, *, tm=512, tn=256):
    T, K = x.shape
    tm = min(tm, seq)
    assert seq % tm == 0 and d_conv % tn == 0 and tm % 8 == 0
    nb = d_conv // tn
    wspec = lambda g: pl.BlockSpec((tn, K), lambda j, i: (g * nb + j, 0))
    bspec = lambda g: pl.BlockSpec((1, tn), lambda j, i: (0, g * nb + j))
    return pl.pallas_call(
        functools.partial(_conv_kernel, tiles_per_seq=seq // tm),
        out_shape=jax.ShapeDtypeStruct((T, d_conv), BF16),
        grid=(nb, T // tm),
        in_specs=[pl.BlockSpec((tm, K), lambda j, i: (i, 0)),
                  wspec(0), wspec(1), wspec(2), bspec(0), bspec(1), bspec(2),
                  pl.BlockSpec((CONV_W, tn), lambda j, i: (0, j)),
                  pl.BlockSpec((1, tn), lambda j, i: (0, j))],
        out_specs=pl.BlockSpec((tm, tn), lambda j, i: (i, j)),
        scratch_shapes=[pltpu.VMEM((3, tn, K), BF16), pltpu.VMEM((8, tn), F32)],
        compiler_params=_cparams(2, 56),
        name="conv_branch",
    )(x, w_t, w_t, w_t, b2d, b2d, b2d, conv_w, conv_b2d)


def _if_kernel(x_ref, wp_ref, wt_ref, bc_ref, br_ref, oc_ref, or_ref):
    x = x_ref[...]
    oc_ref[...] = lax.dot_general(x, wp_ref[...], _NT, preferred_element_type=F32) + bc_ref[...]
    or_ref[...] = lax.dot_general(wt_ref[...], x, _NT, preferred_element_type=F32) + br_ref[...]


def _if_gates(x, w_if_t, b_if, tm=512):
    T, K = x.shape
    tm = min(tm, T)
    n = w_if_t.shape[0]
    w_if_t = w_if_t.astype(BF16)
    w_pad = jnp.zeros((128, K), BF16).at[:n].set(w_if_t)
    b_pad = jnp.zeros((1, 128), F32).at[0, :n].set(b_if)
    return pl.pallas_call(
        _if_kernel,
        out_shape=(jax.ShapeDtypeStruct((T, 128), F32), jax.ShapeDtypeStruct((n, T), F32)),
        grid=(T // tm,),
        in_specs=[pl.BlockSpec((tm, K), lambda i: (i, 0)),
                  pl.BlockSpec((128, K), lambda i: (0, 0)),
                  pl.BlockSpec((n, K), lambda i: (0, 0)),
                  pl.BlockSpec((1, 128), lambda i: (0, 0)),
                  pl.BlockSpec((n, 1), lambda i: (0, 0))],
        out_specs=(pl.BlockSpec((tm, 128), lambda i: (i, 0)),
                   pl.BlockSpec((n, tm), lambda i: (0, i))),
        compiler_params=_cparams(1, 32),
        name="if_gates",
    )(x, w_pad, w_if_t, b_pad, b_if.reshape(n, 1))


def _mlstm_kernel(q_ref, k_ref, v_ref, o_ref, ifc_ref, ifr_ref, g_ref, y_ref, c_ref, m_ref,
                  *, dk, dv):
    H = N_HEADS
    L = q_ref.shape[0]

    @pl.when(pl.program_id(1) == 0)
    def _():
        c_ref[...] = jnp.zeros_like(c_ref)
        m_ref[...] = jnp.zeros_like(m_ref)

    scale = dk ** -0.5
    ifc = ifc_ref[...]
    ifr = ifr_ref[...]
    ig_c = ifc[:, 0:H]
    lf_c = jax.nn.log_sigmoid(ifc[:, H:2 * H])
    ig_r = ifr[0:H, :]
    lf_r = jax.nn.log_sigmoid(ifr[H:2 * H, :])
    r = lax.broadcasted_iota(I32, (L, L), 0)
    c = lax.broadcasted_iota(I32, (L, L), 1)
    causal = r >= c
    bcum_c = jnp.dot(causal.astype(F32), lf_c, preferred_element_type=F32,
                     precision=lax.Precision.HIGHEST)
    bcum_r = jnp.dot(lf_r, (r <= c).astype(F32), preferred_element_type=F32,
                     precision=lax.Precision.HIGHEST)
    d_c = ig_c - bcum_c
    d_r = ig_r - bcum_r
    ones_col = (lax.broadcasted_iota(I32, (L, 128), 1) == 0).astype(BF16)

    for h in range(H):
        q = q_ref[:, h * dk:(h + 1) * dk]
        k = k_ref[:, h * dk:(h + 1) * dk]
        v = v_ref[:, h * dv:(h + 1) * dv]
        v_aug = jnp.concatenate([v, ones_col], axis=1)
        bc = bcum_c[:, h:h + 1]
        m_prev = m_ref[h]
        c_prev = c_ref[h]

        qk = lax.dot_general(q, k, (((1,), (1,)), ((), ())), preferred_element_type=F32) * scale
        logd = jnp.where(causal, bc + d_r[h:h + 1, :], NEG_BIG)
        log_inter = bc + m_prev
        m_t = jnp.maximum(log_inter, jnp.max(logd, axis=1, keepdims=True))
        w_inter = jnp.exp(log_inter - m_t)
        s_mat = (qk * jnp.exp(logd - m_t)).astype(BF16)
        tot = (w_inter * jnp.dot(q, c_prev.astype(BF16), preferred_element_type=F32)
               + jnp.dot(s_mat, v_aug, preferred_element_type=F32))
        den = jnp.maximum(jnp.abs(tot[:, dv:dv + 1]), jnp.exp(-m_t))
        hh = tot[:, :dv] / den
        mu = jnp.mean(hh, axis=-1, keepdims=True)
        hc = hh - mu
        var = jnp.mean(hc * hc, axis=-1, keepdims=True)
        hn = hc * lax.rsqrt(var + LN_EPS) * g_ref[:, h * dv:(h + 1) * dv]
        og = jax.nn.sigmoid(o_ref[:, h * dv:(h + 1) * dv].astype(F32))
        y_ref[:, h * dv:(h + 1) * dv] = (og * hn).astype(y_ref.dtype)

        g_tot = bc[L - 1:L, :]
        a = g_tot + d_c[:, h:h + 1]
        m_new = jnp.maximum(g_tot + m_prev, jnp.max(a, axis=0, keepdims=True))
        kw = (k.astype(F32) * (jnp.exp(a - m_new) * scale)).astype(BF16)
        kv = lax.dot_general(kw, v_aug, (((0,), (0,)), ((), ())), preferred_element_type=F32)
        c_ref[h] = jnp.exp(g_tot + m_prev - m_new) * c_prev + kv
        m_ref[h] = m_new


def _mlstm(qkvo, if_col, if_row, mh_g2d, batch, seq, d_qk, d_v, L=MLSTM_CHUNK):
    T = qkvo.shape[0]
    L = min(L, seq)
    assert seq % L == 0 and d_v == 2 * d_qk
    nc = seq // L
    dk, dv = d_qk // N_HEADS, d_v // N_HEADS
    row = lambda b, c: b * nc + c
    return pl.pallas_call(
        functools.partial(_mlstm_kernel, dk=dk, dv=dv),
        out_shape=jax.ShapeDtypeStruct((T, d_v), BF16),
        grid=(batch, nc),
        in_specs=[pl.BlockSpec((L, d_qk), lambda b, c: (row(b, c), 0)),
                  pl.BlockSpec((L, d_qk), lambda b, c: (row(b, c), 1)),
                  pl.BlockSpec((L, d_v), lambda b, c: (row(b, c), 1)),
                  pl.BlockSpec((L, d_v), lambda b, c: (row(b, c), 2)),
                  pl.BlockSpec((L, 128), lambda b, c: (row(b, c), 0)),
                  pl.BlockSpec((2 * N_HEADS, L), lambda b, c: (0, row(b, c))),
                  pl.BlockSpec((1, d_v), lambda b, c: (0, 0))],
        out_specs=pl.BlockSpec((L, d_v), lambda b, c: (row(b, c), 0)),
        scratch_shapes=[pltpu.VMEM((N_HEADS, dk, dv + 128), F32),
                        pltpu.VMEM((N_HEADS, 1, 1), F32)],
        compiler_params=_cparams(2, 40),
        name="mlstm",
    )(qkvo, qkvo, qkvo, qkvo, if_col, if_row, mh_g2d)


def _merge_kernel(a_ref, b_ref, wa_ref, wb_ref, ga_ref, gb_ref, o_ref, wabf_ref, wbbf_ref):
    @pl.when(pl.program_id(1) == 0)
    def _():
        wabf_ref[...] = wa_ref[...].astype(BF16)
        wbbf_ref[...] = wb_ref[...].astype(BF16)

    ya = jnp.dot(a_ref[...], wabf_ref[...], preferred_element_type=F32)
    yb = jnp.dot(b_ref[...], wbbf_ref[...], preferred_element_type=F32)
    u = ga_ref[...].astype(F32) * ya + gb_ref[...].astype(F32) * yb
    o_ref[...] = u.astype(o_ref.dtype)


def _merge(ya_pre, yb_pre, w_a, w_b, gates, *, tm=1024, tn=512):
    T, K = ya_pre.shape
    D = w_a.shape[1]
    tm = min(tm, T)
    nj = D // tn
    return pl.pallas_call(
        _merge_kernel,
        out_shape=jax.ShapeDtypeStruct((T, D), BF16),
        grid=(nj, T // tm),
        in_specs=[pl.BlockSpec((tm, K), lambda j, i: (i, 0)),
                  pl.BlockSpec((tm, K), lambda j, i: (i, 0)),
                  pl.BlockSpec((K, tn), lambda j, i: (0, j)),
                  pl.BlockSpec((K, tn), lambda j, i: (0, j)),
                  pl.BlockSpec((tm, tn), lambda j, i: (i, j)),
                  pl.BlockSpec((tm, tn), lambda j, i: (i, nj + j))],
        out_specs=pl.BlockSpec((tm, tn), lambda j, i: (i, j)),
        scratch_shapes=[pltpu.VMEM((K, tn), BF16), pltpu.VMEM((K, tn), BF16)],
        compiler_params=_cparams(2, 56),
        name="merge",
    )(ya_pre, yb_pre, w_a, w_b, gates, gates)


def _mix_kernel(u_ref, w_ref, h_ref, o_ref, wbf_ref, *, alpha):
    @pl.when(pl.program_id(1) == 0)
    def _():
        wbf_ref[...] = w_ref[...].astype(BF16)

    o_ref[...] = alpha * h_ref[...] + jnp.dot(u_ref[...], wbf_ref[...], preferred_element_type=F32)


def _mix(u, w, h0, alpha, *, tm=1024, tn=512):
    T, D = h0.shape
    tm = min(tm, T)
    return pl.pallas_call(
        functools.partial(_mix_kernel, alpha=alpha),
        out_shape=jax.ShapeDtypeStruct((T, D), F32),
        grid=(D // tn, T // tm),
        in_specs=[pl.BlockSpec((tm, D), lambda j, i: (i, 0)),
                  pl.BlockSpec((D, tn), lambda j, i: (0, j)),
                  pl.BlockSpec((tm, tn), lambda j, i: (i, j))],
        out_specs=pl.BlockSpec((tm, tn), lambda j, i: (i, j)),
        scratch_shapes=[pltpu.VMEM((D, tn), BF16)],
        compiler_params=_cparams(2, 56),
        name="mix_out",
    )(u, w, h0)


def _ln1_router_kernel(pre_ref, g_ref, b_ref, wr_ref, rb_ref, hf_ref, hb_ref, hp_ref,
                       idx_ref, rank_ref, wcol_ref, cnt_ref, carry_ref):
    i = pl.program_id(0)
    E = wr_ref.shape[0]
    tm, D = pre_ref.shape
    G, M = N_GROUPS, E // N_GROUPS

    @pl.when(i == 0)
    def _():
        carry_ref[...] = jnp.zeros_like(carry_ref)

    h = _ln_rows(pre_ref[...], g_ref[...], b_ref[...])
    hf_ref[...] = h
    hb_ref[...] = h.astype(BF16)
    hp_ref[...] = _pack_halves(h[:, :D // 2], h[:, D // 2:])

    logits = lax.dot_general(wr_ref[...], h, (((1,), (1,)), ((), ())),
                             preferred_element_type=F32, precision=lax.Precision.HIGHEST)
    scores = jax.nn.sigmoid(logits)
    scores3 = scores.reshape(G, M, tm)
    sel3 = (scores + rb_ref[...]).reshape(G, M, tm)
    midx = lax.broadcasted_iota(I32, (G, M, tm), 1)
    gidx3 = lax.broadcasted_iota(I32, (G, M, tm), 0)
    eidx = gidx3 * M + midx
    gidx = lax.broadcasted_iota(I32, (G, 1, tm), 0)
    neg_inf = -jnp.inf

    top1 = jnp.max(sel3, axis=1, keepdims=True)
    first1 = jnp.min(jnp.where(sel3 == top1, midx, M), axis=1, keepdims=True)
    top2 = jnp.max(jnp.where(midx == first1, neg_inf, sel3), axis=1, keepdims=True)
    gs = top1 + top2
    gkeep = jnp.zeros((G, 1, tm), F32)
    for _ in range(TOP_GROUPS):
        mx = jnp.max(gs, axis=0, keepdims=True)
        first = jnp.min(jnp.where(gs == mx, gidx, G), axis=0, keepdims=True)
        hit = gidx == first
        gkeep = jnp.where(hit, 1.0, gkeep)
        gs = jnp.where(hit, neg_inf, gs)
    selm = jnp.where(gkeep > 0.5, sel3, neg_inf)

    idx_rows, sc_rows = [], []
    chosen = jnp.zeros((G, M, tm), F32)
    for _ in range(TOP_K):
        mx = jnp.max(jnp.max(selm, axis=1, keepdims=True), axis=0, keepdims=True)
        first = jnp.min(jnp.min(jnp.where(selm == mx, eidx, E), axis=1, keepdims=True),
                        axis=0, keepdims=True)
        hit = eidx == first
        sc = jnp.sum(jnp.sum(jnp.where(hit, scores3, 0.0), axis=1, keepdims=True),
                     axis=0, keepdims=True)
        chosen = jnp.where(hit, 1.0, chosen)
        selm = jnp.where(hit, neg_inf, selm)
        idx_rows.append(first)
        sc_rows.append(sc)
    denom = sc_rows[0]
    for s in sc_rows[1:]:
        denom = denom + s

    tr = lax.broadcasted_iota(I32, (tm, tm), 0)
    tc = lax.broadcasted_iota(I32, (tm, tm), 1)
    before = (tr < tc).astype(BF16)
    chosen2 = chosen.reshape(E, tm)
    rank2 = jnp.dot(chosen2.astype(BF16), before, preferred_element_type=F32) + carry_ref[...]
    rank3 = rank2.reshape(G, M, tm)
    w_rows = []
    for k in range(TOP_K):
        hit = eidx == idx_rows[k]
        rk = jnp.sum(jnp.sum(jnp.where(hit, rank3, 0.0), axis=1, keepdims=True),
                     axis=0, keepdims=True)
        idx_ref[k:k + 1, :] = idx_rows[k].reshape(1, tm)
        rank_ref[k:k + 1, :] = rk.reshape(1, tm).astype(I32)
        w_rows.append((sc_rows[k] / denom * ROUTE_SCALE).reshape(1, tm))
    wcol_ref[...] = jnp.concatenate(w_rows, axis=0).T
    carry_ref[...] = carry_ref[...] + jnp.sum(chosen2, axis=1, keepdims=True)

    @pl.when(i == pl.num_programs(0) - 1)
    def _():
        cnt_ref[...] = carry_ref[...]


def _ln1_router(pre, g, b, w_router, router_bias, tm):
    T, D = pre.shape
    E = w_router.shape[1]
    nt = T // tm
    row = lambda i: (i, 0)
    fixed = lambda i: (0, 0)
    return pl.pallas_call(
        _ln1_router_kernel,
        out_shape=(jax.ShapeDtypeStruct((T, D), F32), jax.ShapeDtypeStruct((T, D), BF16),
                   jax.ShapeDtypeStruct((T, D // 2), U32),
                   jax.ShapeDtypeStruct((nt, TOP_K, tm), I32), jax.ShapeDtypeStruct((nt, TOP_K, tm), I32),
                   jax.ShapeDtypeStruct((T, TOP_K), F32), jax.ShapeDtypeStruct((E, 1), F32)),
        grid=(nt,),
        in_specs=[pl.BlockSpec((tm, D), row), pl.BlockSpec((1, D), fixed), pl.BlockSpec((1, D), fixed),
                  pl.BlockSpec((E, D), fixed), pl.BlockSpec((E, 1), fixed)],
        out_specs=(pl.BlockSpec((tm, D), row), pl.BlockSpec((tm, D), row), pl.BlockSpec((tm, D // 2), row),
                   pl.BlockSpec((None, TOP_K, tm), lambda i: (i, 0, 0)),
                   pl.BlockSpec((None, TOP_K, tm), lambda i: (i, 0, 0)),
                   pl.BlockSpec((tm, TOP_K), row),
                   pl.BlockSpec((E, 1), fixed)),
        scratch_shapes=[pltpu.VMEM((E, 1), F32)],
        compiler_params=_cparams(1, 40),
        name="ln1_router",
    )(pre, g.reshape(1, D), b.reshape(1, D), w_router.T, router_bias.reshape(E, 1))


def _scatter_rows_kernel(fill_ref, dest_hbm, x_ref, xs_hbm, idx_smem, zero_ref, isem, ssem, zsem):
    i = pl.program_id(0)
    n = pl.num_programs(0)
    tm = x_ref.shape[0]
    n_idx = TOP_K * tm
    R = zero_ref.shape[0]

    def idx_copy(tile, slot):
        return pltpu.make_async_copy(dest_hbm.at[tile], idx_smem.at[pl.ds(slot * n_idx, n_idx)],
                                     isem.at[slot])

    @pl.when(i == 0)
    def _():
        idx_copy(0, 0).start()
        zero_ref[...] = jnp.zeros_like(zero_ref)

        def fill(f):
            return pltpu.make_async_copy(zero_ref, xs_hbm.at[pl.ds(fill_ref[f] * R, R)], zsem)

        def start(f, carry):
            @pl.when(fill_ref[f] >= 0)
            def _():
                fill(f).start()
            return carry
        lax.fori_loop(0, fill_ref.shape[0], start, 0)

        def wait(f, carry):
            @pl.when(fill_ref[f] >= 0)
            def _():
                fill(f).wait()
            return carry
        lax.fori_loop(0, fill_ref.shape[0], wait, 0)

    slot = i % 2
    idx_copy(i, slot).wait()

    @pl.when(i + 1 < n)
    def _():
        idx_copy(i + 1, 1 - slot).start()

    base = slot * n_idx

    def issue(r8, carry):
        for s in range(8):
            r = r8 * 8 + s
            for k in range(TOP_K):
                d = idx_smem[base + k * tm + r]
                pltpu.make_async_copy(x_ref.at[pl.ds(r, 1)], xs_hbm.at[pl.ds(d, 1)], ssem).start()
        return carry
    lax.fori_loop(0, tm // 8, issue, 0)

    for k in range(TOP_K):
        pltpu.make_async_copy(x_ref, x_ref, ssem).wait()


def _scatter_rows(x_packed, dest_tiles, fill_blocks, n_rows):
    T, W = x_packed.shape
    nt, n_idx = dest_tiles.shape
    tm = n_idx // TOP_K
    return pl.pallas_call(
        _scatter_rows_kernel,
        out_shape=jax.ShapeDtypeStruct((n_rows, W), x_packed.dtype),
        grid_spec=pltpu.PrefetchScalarGridSpec(
            num_scalar_prefetch=1,
            grid=(nt,),
            in_specs=[pl.BlockSpec(memory_space=pl.ANY),
                      pl.BlockSpec((tm, W), lambda i, fb: (i, 0))],
            out_specs=pl.BlockSpec(memory_space=pl.ANY),
            scratch_shapes=[pltpu.SMEM((2 * n_idx,), I32), pltpu.VMEM((MOE_BLOCK, W), x_packed.dtype),
                            pltpu.SemaphoreType.DMA((2,)), pltpu.SemaphoreType.DMA,
                            pltpu.SemaphoreType.DMA]),
        compiler_params=pltpu.CompilerParams(dimension_semantics=("arbitrary",),
                                             vmem_limit_bytes=32 * 1024 * 1024,
                                             has_side_effects=True),
        name="dispatch_scatter",
    )(fill_blocks, dest_tiles, x_packed)


def _expert_up_kernel(be_ref, nxt_ref, nused_ref, x_ref, w_hbm, o_ref, wbf_ref, stage_ref, wsem):
    b = pl.program_id(0)
    nb = pl.num_programs(0)
    n_used = nused_ref[0]
    half = x_ref.shape[1]
    f = o_ref.shape[1]
    e = be_ref[b]
    run_start = jnp.logical_or(b == 0, e != be_ref[jnp.maximum(b - 1, 0)])

    def fetch(e_):
        return pltpu.make_async_copy(w_hbm.at[e_], stage_ref, wsem)

    @pl.when(b == 0)
    def _():
        fetch(e).start()

    @pl.when(jnp.logical_and(run_start, b < n_used))
    def _():
        fetch(e).wait()
        wbf_ref[...] = stage_ref[...].astype(BF16)
        b_next = nxt_ref[b]

        @pl.when(b_next < n_used)
        def _():
            fetch(be_ref[jnp.minimum(b_next, nb - 1)]).start()

    @pl.when(b < n_used)
    def _():
        lo, hi = _unpack_halves(x_ref[...])
        gu = (jnp.dot(lo.astype(BF16), wbf_ref[:half], preferred_element_type=F32)
              + jnp.dot(hi.astype(BF16), wbf_ref[half:], preferred_element_type=F32))
        o_ref[...] = (jax.nn.silu(gu[:, :f]) * gu[:, f:]).astype(o_ref.dtype)

    @pl.when(b >= n_used)
    def _():
        o_ref[...] = jnp.zeros_like(o_ref)


def _expert_up(xs, w_gu_e, blk_e, nxt_blk, n_used):
    P, half = xs.shape
    E, D, two_f = w_gu_e.shape
    f = two_f // 2
    assert D == 2 * half
    nb = P // MOE_BLOCK
    return pl.pallas_call(
        _expert_up_kernel,
        out_shape=jax.ShapeDtypeStruct((P, f), BF16),
        grid_spec=pltpu.PrefetchScalarGridSpec(
            num_scalar_prefetch=3,
            grid=(nb,),
            in_specs=[pl.BlockSpec((MOE_BLOCK, half), lambda b, be, nx, nu: (jnp.minimum(b, nu[0] - 1), 0)),
                      pl.BlockSpec(memory_space=pl.ANY)],
            out_specs=pl.BlockSpec((MOE_BLOCK, f), lambda b, be, nx, nu: (b, 0)),
            scratch_shapes=[pltpu.VMEM((D, two_f), BF16), pltpu.VMEM((D, two_f), F32),
                            pltpu.SemaphoreType.DMA]),
        compiler_params=_cparams(1, 56),
        name="expert_up",
    )(blk_e, nxt_blk, n_used, xs, w_gu_e)


def _expert_down_kernel(be_ref, nxt_ref, nused_ref, h_ref, w_hbm, o_ref, wbf_ref, stage_ref, run_ref,
                        wsem):
    b = pl.program_id(0)
    nb = pl.num_programs(0)
    n_used = nused_ref[0]
    e = be_ref[b]
    run_start = jnp.logical_or(b == 0, e != be_ref[jnp.maximum(b - 1, 0)])

    def fetch(e_, slot):
        return pltpu.make_async_copy(w_hbm.at[e_], stage_ref.at[slot], wsem.at[slot])

    @pl.when(b == 0)
    def _():
        run_ref[0] = 0
        fetch(e, 0).start()

    @pl.when(jnp.logical_and(run_start, b < n_used))
    def _():
        k = run_ref[0]
        slot = k % 2
        fetch(e, slot).wait()
        wbf_ref[...] = stage_ref[slot].astype(BF16)
        b_next = nxt_ref[b]

        @pl.when(b_next < n_used)
        def _():
            fetch(be_ref[jnp.minimum(b_next, nb - 1)], 1 - slot).start()
        run_ref[0] = k + 1

    @pl.when(b < n_used)
    def _():
        y = jnp.dot(h_ref[...], wbf_ref[...], preferred_element_type=F32)
        half = y.shape[1] // 2
        o_ref[...] = _pack_halves(y[:, :half], y[:, half:])

    @pl.when(b >= n_used)
    def _():
        o_ref[...] = jnp.zeros_like(o_ref)


def _expert_down(hmid, w_down_e, blk_e, nxt_blk, n_used):
    P, f = hmid.shape
    E, _, D = w_down_e.shape
    nb = P // MOE_BLOCK
    return pl.pallas_call(
        _expert_down_kernel,
        out_shape=jax.ShapeDtypeStruct((P, D // 2), U32),
        grid_spec=pltpu.PrefetchScalarGridSpec(
            num_scalar_prefetch=3,
            grid=(nb,),
            in_specs=[pl.BlockSpec((MOE_BLOCK, f), lambda b, be, nx, nu: (jnp.minimum(b, nu[0] - 1), 0)),
                      pl.BlockSpec(memory_space=pl.ANY)],
            out_specs=pl.BlockSpec((MOE_BLOCK, D // 2), lambda b, be, nx, nu: (b, 0)),
            scratch_shapes=[pltpu.VMEM((f, D), BF16), pltpu.VMEM((2, f, D), F32),
                            pltpu.SMEM((1,), I32), pltpu.SemaphoreType.DMA((2,))]),
        compiler_params=_cparams(1, 56),
        name="expert_down",
    )(blk_e, nxt_blk, n_used, hmid, w_down_e)


def _combine_kernel(dest_hbm, y_hbm, w_ref, pre_ref, g_ref, b_ref, o_ref, idx_smem, buf_ref, isem, gsem):
    i = pl.program_id(0)
    n = pl.num_programs(0)
    tc = o_ref.shape[0]
    half = buf_ref.shape[3]
    n_idx = TOP_K * tc

    def idx_copy(tile, slot):
        return pltpu.make_async_copy(dest_hbm.at[tile], idx_smem.at[pl.ds(slot * n_idx, n_idx)],
                                     isem.at[slot])

    def issue_tile(slot):
        base = slot * n_idx

        def body(r8, carry):
            for s in range(8):
                r = r8 * 8 + s
                for k in range(TOP_K):
                    d = idx_smem[base + k * tc + r]
                    pltpu.make_async_copy(y_hbm.at[pl.ds(d, 1)], buf_ref.at[slot, k, pl.ds(r, 1)],
                                          gsem.at[slot]).start(priority=k % 2)
            return carry
        lax.fori_loop(0, tc // 8, body, 0)

    slot = i % 2

    @pl.when(i == 0)
    def _():
        idx_copy(0, 0).start()
        idx_copy(0, 0).wait()
        issue_tile(0)

        @pl.when(n > 1)
        def _():
            idx_copy(1, 1).start()

    @pl.when(i + 1 < n)
    def _():
        idx_copy(i + 1, 1 - slot).wait()
        for sl in range(2):
            @pl.when(slot == 1 - sl)
            def _():
                issue_tile(sl)

    @pl.when(i + 2 < n)
    def _():
        idx_copy(i + 2, slot).start()

    for k in range(TOP_K):
        pltpu.make_async_copy(buf_ref.at[slot, k], buf_ref.at[slot, k], gsem.at[slot]).wait()

    w = w_ref[...]
    acc_lo = jnp.zeros((tc, half), F32)
    acc_hi = jnp.zeros((tc, half), F32)
    for k in range(TOP_K):
        lo, hi = _unpack_halves(buf_ref[slot, k])
        wk = w[:, k:k + 1]
        acc_lo = acc_lo + wk * lo
        acc_hi = acc_hi + wk * hi
    y_lo = pre_ref[:, :half] + acc_lo
    y_hi = pre_ref[:, half:] + acc_hi
    inv_d = 1.0 / (2 * half)
    mu = (jnp.sum(y_lo, axis=-1, keepdims=True) + jnp.sum(y_hi, axis=-1, keepdims=True)) * inv_d
    c_lo = y_lo - mu
    c_hi = y_hi - mu
    var = (jnp.sum(c_lo * c_lo, axis=-1, keepdims=True)
           + jnp.sum(c_hi * c_hi, axis=-1, keepdims=True)) * inv_d
    rstd = lax.rsqrt(var + LN_EPS)
    o_ref[:, :half] = c_lo * rstd * g_ref[:, :half] + b_ref[:, :half]
    o_ref[:, half:] = c_hi * rstd * g_ref[:, half:] + b_ref[:, half:]


def _combine_ln(y_packed, dest_tiles, w_col, pre, g, b):
    T, D = pre.shape
    half = y_packed.shape[1]
    nt, n_idx = dest_tiles.shape
    tc = n_idx // TOP_K
    return pl.pallas_call(
        _combine_kernel,
        out_shape=jax.ShapeDtypeStruct((T, D), F32),
        grid=(nt,),
        in_specs=[pl.BlockSpec(memory_space=pl.ANY), pl.BlockSpec(memory_space=pl.ANY),
                  pl.BlockSpec((tc, TOP_K), lambda i: (i, 0)),
                  pl.BlockSpec((tc, D), lambda i: (i, 0)),
                  pl.BlockSpec((1, D), lambda i: (0, 0)),
                  pl.BlockSpec((1, D), lambda i: (0, 0))],
        out_specs=pl.BlockSpec((tc, D), lambda i: (i, 0)),
        scratch_shapes=[pltpu.SMEM((2 * n_idx,), I32),
                        pltpu.VMEM((2, TOP_K, tc, half), U32),
                        pltpu.SemaphoreType.DMA((2,)), pltpu.SemaphoreType.DMA((2,))],
        compiler_params=_cparams(1, 48),
        name="combine_ln2",
    )(dest_tiles, y_packed, w_col, pre, g.reshape(1, D), b.reshape(1, D))


def _glu_up_kernel(h_ref, wg_ref, wu_ref, o_ref, wbf_ref):
    tn = wg_ref.shape[1]

    @pl.when(pl.program_id(1) == 0)
    def _():
        wbf_ref[:, :tn] = wg_ref[...].astype(BF16)
        wbf_ref[:, tn:] = wu_ref[...].astype(BF16)

    gu = jnp.dot(h_ref[...], wbf_ref[...], preferred_element_type=F32)
    o_ref[...] = (jax.nn.silu(gu[:, :tn]) * gu[:, tn:]).astype(o_ref.dtype)


def _glu_up(h_bf, w_gu, *, tm=512, tn=384):
    T, D = h_bf.shape
    f = w_gu.shape[1] // 2
    tm = min(tm, T)
    nc = f // tn
    return pl.pallas_call(
        _glu_up_kernel,
        out_shape=jax.ShapeDtypeStruct((T, f), BF16),
        grid=(nc, T // tm),
        in_specs=[pl.BlockSpec((tm, D), lambda c, i: (i, 0)),
                  pl.BlockSpec((D, tn), lambda c, i: (0, c)),
                  pl.BlockSpec((D, tn), lambda c, i: (0, nc + c))],
        out_specs=pl.BlockSpec((tm, tn), lambda c, i: (i, c)),
        scratch_shapes=[pltpu.VMEM((D, 2 * tn), BF16)],
        compiler_params=_cparams(2, 56),
        name="shared_up",
    )(h_bf, w_gu, w_gu)


def _dense_tail_kernel(fill_ref, dest_hbm, x_ref, hb_ref, hj_ref, s_ref, p_ref, wd_ref, wg_ref, bg_ref,
                       wp_ref, o_ref, xs_hbm, wdbf_ref, wgbf_ref, wpbf_ref, idx_smem, zero_ref,
                       isem, ssem, zsem, *, alpha, k_per_step):
    j = pl.program_id(0)
    i = pl.program_id(1)
    ni = pl.num_programs(1)
    step = j * ni + i
    tm = x_ref.shape[0]
    n_idx = k_per_step * tm
    R = zero_ref.shape[0]

    def idx_copies(jj, ii, slot):
        return [pltpu.make_async_copy(dest_hbm.at[jj * k_per_step + kk, pl.ds(pl.multiple_of(ii * tm, 128), tm)],
                                      idx_smem.at[pl.ds(slot * n_idx + kk * tm, tm)], isem.at[slot])
                for kk in range(k_per_step)]

    @pl.when(step == 0)
    def _():
        for cp in idx_copies(0, 0, 0):
            cp.start()
        zero_ref[...] = jnp.zeros_like(zero_ref)

        def fill(f):
            return pltpu.make_async_copy(zero_ref, xs_hbm.at[pl.ds(fill_ref[f] * R, R)], zsem)

        def start(f, carry):
            @pl.when(fill_ref[f] >= 0)
            def _():
                fill(f).start()
            return carry
        lax.fori_loop(0, fill_ref.shape[0], start, 0)

        def wait(f, carry):
            @pl.when(fill_ref[f] >= 0)
            def _():
                fill(f).wait()
            return carry
        lax.fori_loop(0, fill_ref.shape[0], wait, 0)

    @pl.when(i == 0)
    def _():
        wdbf_ref[...] = wd_ref[...].astype(BF16)
        wgbf_ref[...] = wg_ref[...].astype(BF16)
        wpbf_ref[...] = wp_ref[...].astype(BF16)

    slot = step % 2
    for cp in idx_copies(j, i, slot):
        cp.wait()

    @pl.when(step + 1 < pl.num_programs(0) * ni)
    def _():
        nxt = step + 1
        for cp in idx_copies(nxt // ni, nxt % ni, 1 - slot):
            cp.start()

    base = slot * n_idx
    for kk in range(k_per_step):
        for r in range(tm):
            d = idx_smem[base + kk * tm + r]
            pltpu.make_async_copy(x_ref.at[pl.ds(r, 1)], xs_hbm.at[pl.ds(d, 1)], ssem).start(priority=r % 2)

    shared = jnp.dot(s_ref[...], wdbf_ref[...], preferred_element_type=F32)
    gate = jax.nn.sigmoid(jnp.dot(hb_ref[...], wgbf_ref[...], preferred_element_type=F32)
                          + bg_ref[...])
    proj = jnp.dot(p_ref[...].astype(BF16), wpbf_ref[...], preferred_element_type=F32)
    o_ref[...] = alpha * hj_ref[...] + shared + gate * proj

    for kk in range(k_per_step):
        pltpu.make_async_copy(x_ref, x_ref, ssem).wait()


def _dense_tail(h1_bf, h1, h1_packed, dest_kt, fill_blocks, n_rows, s_mid, p, w_down_s, w_gate, b_gate,
                w_proj, alpha, *, tm=512, tn=512):
    T, D = h1.shape
    f = s_mid.shape[1]
    dp = p.shape[1]
    W = h1_packed.shape[1]
    tm = min(tm, T)
    nj = D // tn
    assert TOP_K % nj == 0 and tm % 128 == 0
    k_per_step = TOP_K // nj
    row = lambda j, i, fb: (i, 0)
    col = lambda j, i, fb: (0, j)
    tile = lambda j, i, fb: (i, j)
    return pl.pallas_call(
        functools.partial(_dense_tail_kernel, alpha=alpha, k_per_step=k_per_step),
        out_shape=(jax.ShapeDtypeStruct((T, D), F32), jax.ShapeDtypeStruct((n_rows, W), h1_packed.dtype)),
        grid_spec=pltpu.PrefetchScalarGridSpec(
            num_scalar_prefetch=1,
            grid=(nj, T // tm),
            in_specs=[pl.BlockSpec(memory_space=pl.ANY),
                      pl.BlockSpec((tm, W), row),
                      pl.BlockSpec((tm, D), row),
                      pl.BlockSpec((tm, tn), tile),
                      pl.BlockSpec((tm, f), row),
                      pl.BlockSpec((tm, dp), row),
                      pl.BlockSpec((f, tn), col),
                      pl.BlockSpec((D, tn), col),
                      pl.BlockSpec((1, tn), col),
                      pl.BlockSpec((dp, tn), col)],
            out_specs=(pl.BlockSpec((tm, tn), tile), pl.BlockSpec(memory_space=pl.ANY)),
            scratch_shapes=[pltpu.VMEM((f, tn), BF16), pltpu.VMEM((D, tn), BF16), pltpu.VMEM((dp, tn), BF16),
                            pltpu.SMEM((2 * k_per_step * tm,), I32), pltpu.VMEM((MOE_BLOCK, W), h1_packed.dtype),
                            pltpu.SemaphoreType.DMA((2,)), pltpu.SemaphoreType.DMA, pltpu.SemaphoreType.DMA]),
        compiler_params=pltpu.CompilerParams(dimension_semantics=("arbitrary", "arbitrary"),
                                             vmem_limit_bytes=58 * 1024 * 1024, has_side_effects=True),
        name="dense_tail_dispatch",
    )(fill_blocks, dest_kt, h1_packed, h1_bf, h1, s_mid, p, w_down_s, w_gate, b_gate.reshape(1, D), w_proj)


def _dispatch_tables(idx, rank, counts):
    E = counts.shape[0]
    P = idx.size + E * MOE_BLOCK
    nb = P // MOE_BLOCK
    padded = (counts + MOE_BLOCK - 1) // MOE_BLOCK * MOE_BLOCK
    pend = jnp.cumsum(padded)
    pstart = pend - padded
    sel = idx[None] == jnp.arange(E, dtype=I32).reshape(E, 1, 1, 1)
    dest = rank + jnp.sum(jnp.where(sel, pstart.reshape(E, 1, 1, 1), 0), axis=0)
    blk_row = jnp.arange(nb, dtype=I32) * MOE_BLOCK
    blk_e = jnp.minimum(jnp.sum(pend[None, :] <= blk_row[:, None], axis=1), E - 1).astype(I32)
    n_used = pend[-1] // MOE_BLOCK
    last_blk = jnp.where(counts > 0, pend // MOE_BLOCK - 1, -1)
    tail_blk = n_used + jnp.arange(E, dtype=I32)
    tail_blk = jnp.where(tail_blk < nb, tail_blk, -1)
    fill = jnp.concatenate([last_blk, tail_blk]).astype(I32)
    nxt_blk = jnp.take(pend // MOE_BLOCK, blk_e).astype(I32)
    return dest.astype(I32), blk_e, nxt_blk, n_used.astype(I32).reshape(1), fill, P


def _layer(h0_f, h0_bf, p_l, w_in, b_in, conv_w, conv_b, mh_norm_g, w_conv_out, w_mlstm_out,
           w_mix_out, ln1_g, ln1_b, w_router, router_bias, w_gu_e, w_down_e, w_gu_s, w_down_s,
           w_ple_gate, b_ple_gate, w_ple_proj, ln2_g, ln2_b, alpha, batch, seq):
    T, D = h0_f.shape
    d_conv = conv_w.shape[1]
    d_v = mh_norm_g.shape[0]
    d_qk = d_v // 2
    n_if = 2 * N_HEADS
    c_qk = 3 * d_conv
    c_if = c_qk + 2 * d_qk + 2 * d_v
    c_gate = c_if + n_if
    b2d = b_in.reshape(1, -1)

    w_t = jnp.swapaxes(w_in, 0, 1)
    ya_pre = _conv_branch(h0_bf, w_t, b2d, conv_w, conv_b.reshape(1, -1), seq, d_conv)
    qkvo = _proj(h0_bf, w_t, b2d, c_qk, 2 * d_qk + 2 * d_v)
    if_col, if_row = _if_gates(h0_bf, w_t[c_if:c_gate], b_in[c_if:c_gate])
    gates = _proj(h0_bf, w_t, b2d, c_gate, 2 * D, act="sigmoid")
    yb_pre = _mlstm(qkvo, if_col, if_row, mh_norm_g.reshape(1, -1), batch, seq, d_qk, d_v)
    u = _merge(ya_pre, yb_pre, w_conv_out, w_mlstm_out, gates)
    pre1 = _mix(u, w_mix_out, h0_f, alpha)

    rt = min(ROUTE_TILE, T)
    h1, h1_bf, h1_packed, idx, rank, w_col, counts = _ln1_router(pre1, ln1_g, ln1_b, w_router,
                                                                 router_bias, rt)
    dest, blk_e, nxt_blk, n_used, fill, n_rows = _dispatch_tables(idx, rank, counts[:, 0].astype(I32))
    dest_tiles = dest.reshape(T // rt, TOP_K * rt)
    dest_kt = dest.transpose(1, 0, 2).reshape(TOP_K, T)

    s_mid = _glu_up(h1_bf, w_gu_s)
    pre2, xs = _dense_tail(h1_bf, h1, h1_packed, dest_kt, fill, n_rows, s_mid, p_l, w_down_s,
                           w_ple_gate, b_ple_gate, w_ple_proj, alpha)

    hmid = _expert_up(xs, w_gu_e, blk_e, nxt_blk, n_used)
    y_packed = _expert_down(hmid, w_down_e, blk_e, nxt_blk, n_used)
    return _combine_ln(y_packed, dest_tiles, w_col, pre2, ln2_g, ln2_b)


def kernel(x, p, ln_in_g, ln_in_b, w_in, b_in, conv_w, conv_b, mh_norm_g, w_conv_out, w_mlstm_out,
           w_mix_out, ln1_g, ln1_b, w_router, router_bias, w_gu_e, w_down_e, w_gu_s, w_down_s,
           w_ple_gate, b_ple_gate, w_ple_proj, ln2_g, ln2_b):
    B, S, D = x.shape
    depth = w_in.shape[0]
    alpha = (2 * depth) ** 0.25
    T = B * S
    h_f, h_bf = _ln_in(x.reshape(T, D), ln_in_g, ln_in_b)
    for l in range(depth):
        h_f = _layer(h_f, h_bf, p[l].reshape(T, -1), w_in[l], b_in[l], conv_w[l], conv_b[l],
                     mh_norm_g[l], w_conv_out[l], w_mlstm_out[l], w_mix_out[l], ln1_g[l], ln1_b[l],
                     w_router[l], router_bias[l], w_gu_e[l], w_down_e[l], w_gu_s[l], w_down_s[l],
                     w_ple_gate[l], b_ple_gate[l], w_ple_proj[l], ln2_g[l], ln2_b[l], alpha, B, S)
        if l + 1 < depth:
            h_bf = h_f.astype(BF16)
    return h_f.reshape(B, S, D)
```

```python
import functools

import jax
import jax.numpy as jnp
from jax import lax
from jax.experimental import pallas as pl
from jax.experimental.pallas import tpu as pltpu

F32 = jnp.float32
BF16 = jnp.bfloat16
U32 = jnp.uint32
I32 = jnp.int32

N_HEADS = 8
TOP_K = 8
N_GROUPS = 8
TOP_GROUPS = 4
ROUTE_SCALE = 2.5
MOE_BLOCK = 256
LN_EPS = 1e-5
CONV_W = 3
MLSTM_CHUNK = 128
ROUTE_TILE = 128
NEG_BIG = -1e30
HI_MASK = 0xFFFF0000
V7X_VMEM_BYTES = 64 * 1024 * 1024


def _cparams(n_axes, vmem_mib):
    assert vmem_mib * 1024 * 1024 <= V7X_VMEM_BYTES
    return pltpu.CompilerParams(dimension_semantics=("arbitrary",) * n_axes,
                                vmem_limit_bytes=vmem_mib * 1024 * 1024)


def _ln_rows(x, g, b):
    mu = jnp.mean(x, axis=-1, keepdims=True)
    xc = x - mu
    var = jnp.mean(xc * xc, axis=-1, keepdims=True)
    return xc * lax.rsqrt(var + LN_EPS) * g + b


def _pack_halves(lo, hi):
    lo = pltpu.bitcast(lo.astype(BF16).astype(F32), U32)
    hi = pltpu.bitcast(hi.astype(BF16).astype(F32), U32)
    return (hi & jnp.uint32(HI_MASK)) | (lo >> jnp.uint32(16))


def _sigmoid(x):
    return 0.5 * jnp.tanh(0.5 * x) + 0.5


def _unpack_halves(w):
    lo = pltpu.bitcast(w << jnp.uint32(16), F32)
    hi = pltpu.bitcast(w & jnp.uint32(HI_MASK), F32)
    return lo, hi


def _ln_in_kernel(x_ref, g_ref, b_ref, of_ref, ob_ref):
    y = _ln_rows(x_ref[...], g_ref[...], b_ref[...])
    of_ref[...] = y
    ob_ref[...] = y.astype(BF16)


def _ln_in(x2, g, b, tm=256):
    T, D = x2.shape
    return pl.pallas_call(
        _ln_in_kernel,
        out_shape=(jax.ShapeDtypeStruct((T, D), F32), jax.ShapeDtypeStruct((T, D), BF16)),
        grid=(T // tm,),
        in_specs=[pl.BlockSpec((tm, D), lambda i: (i, 0)),
                  pl.BlockSpec((1, D), lambda i: (0, 0)),
                  pl.BlockSpec((1, D), lambda i: (0, 0))],
        out_specs=(pl.BlockSpec((tm, D), lambda i: (i, 0)),
                   pl.BlockSpec((tm, D), lambda i: (i, 0))),
        compiler_params=_cparams(1, 40),
        name="ln_in",
    )(x2, g.reshape(1, D), b.reshape(1, D))


_NT = (((1,), (1,)), ((), ()))


def _proj_kernel(x_ref, w_ref, b_ref, o_ref, wbf_ref, *, act):
    @pl.when(pl.program_id(1) == 0)
    def _():
        wbf_ref[...] = w_ref[...].astype(BF16)

    acc = lax.dot_general(x_ref[...], wbf_ref[...], _NT, preferred_element_type=F32) + b_ref[...]
    if act == "sigmoid":
        acc = _sigmoid(acc)
    o_ref[...] = acc.astype(o_ref.dtype)


def _proj(x, w_t, bias2d, row0, n_cols, *, act=None, tm=1024, tn=512, out_dtype=BF16):
    T, K = x.shape
    tm = min(tm, T)
    assert n_cols % tn == 0 and T % tm == 0 and row0 % 8 == 0
    if row0 % tn == 0:
        jb = row0 // tn
        w_spec = pl.BlockSpec((tn, K), lambda j, i: (jb + j, 0))
        b_spec = pl.BlockSpec((1, tn), lambda j, i: (0, jb + j))
    else:
        w_spec = pl.BlockSpec((pl.Element(tn), pl.Element(K)),
                              lambda j, i: (pl.multiple_of(row0 + j * tn, 8), 0))
        bias2d = bias2d[:, row0:row0 + n_cols]
        b_spec = pl.BlockSpec((1, tn), lambda j, i: (0, j))
    return pl.pallas_call(
        functools.partial(_proj_kernel, act=act),
        out_shape=jax.ShapeDtypeStruct((T, n_cols), out_dtype),
        grid=(n_cols // tn, T // tm),
        in_specs=[pl.BlockSpec((tm, K), lambda j, i: (i, 0)), w_spec, b_spec],
        out_specs=pl.BlockSpec((tm, tn), lambda j, i: (i, j)),
        scratch_shapes=[pltpu.VMEM((tn, K), BF16)],
        compiler_params=_cparams(2, 56),
        name="proj_" + (act or "lin"),
    )(x, w_t, bias2d)


def _conv_kernel(x_ref, wh_ref, wc_ref, wb_ref, bh_ref, bc_ref, bb_ref, cw_ref, cb_ref,
                 o_ref, wbf_ref, zprev_ref, *, tiles_per_seq):
    i = pl.program_id(1)

    @pl.when(i == 0)
    def _():
        wbf_ref[0] = wh_ref[...].astype(BF16)
        wbf_ref[1] = wc_ref[...].astype(BF16)
        wbf_ref[2] = wb_ref[...].astype(BF16)

    @pl.when(i % tiles_per_seq == 0)
    def _():
        zprev_ref[...] = jnp.zeros_like(zprev_ref)

    x = x_ref[...]
    ha = lax.dot_general(x, wbf_ref[0], _NT, preferred_element_type=F32) + bh_ref[...]
    ca = lax.dot_general(x, wbf_ref[1], _NT, preferred_element_type=F32) + bc_ref[...]
    ba = lax.dot_general(x, wbf_ref[2], _NT, preferred_element_type=F32) + bb_ref[...]
    z = ca * ha
    tm = z.shape[0]
    prev = zprev_ref[...]
    row8 = lax.broadcasted_iota(I32, prev.shape, 0)
    z1 = pltpu.roll(z, 1, 0)
    z2 = pltpu.roll(z, 2, 0)
    p1 = pltpu.roll(prev, 1, 0)
    p2 = pltpu.roll(prev, 2, 0)
    z1 = jnp.concatenate([jnp.where(row8 < 1, p1, z1[:8]), z1[8:]], axis=0)
    z2 = jnp.concatenate([jnp.where(row8 < 2, p2, z2[:8]), z2[8:]], axis=0)
    cw = cw_ref[...]
    y = cw[0:1] * z2 + cw[1:2] * z1 + cw[2:3] * z + cb_ref[...]
    o_ref[...] = (ba * y).astype(o_ref.dtype)
    zprev_ref[...] = z[tm - 8:]


def _conv_branch(x, w_t, b2d, conv_w, conv_b2d, seq, d_conv, *, tm=512, tn=256):
    T, K = x.shape
    tm = min(tm, seq)
    assert seq % tm == 0 and d_conv % tn == 0 and tm % 8 == 0
    nb = d_conv // tn
    wspec = lambda g: pl.BlockSpec((tn, K), lambda j, i: (g * nb + j, 0))
    bspec = lambda g: pl.BlockSpec((1, tn), lambda j, i: (0, g * nb + j))
    return pl.pallas_call(
        functools.partial(_conv_kernel, tiles_per_seq=seq // tm),
        out_shape=jax.ShapeDtypeStruct((T, d_conv), BF16),
        grid=(nb, T // tm),
        in_specs=[pl.BlockSpec((tm, K), lambda j, i: (i, 0)),
                  wspec(0), wspec(1), wspec(2), bspec(0), bspec(1), bspec(2),
                  pl.BlockSpec((CONV_W, tn), lambda j, i: (0, j)),
                  pl.BlockSpec((1, tn), lambda j, i: (0, j))],
        out_specs=pl.BlockSpec((tm, tn), lambda j, i: (i, j)),
        scratch_shapes=[pltpu.VMEM((3, tn, K), BF16), pltpu.VMEM((8, tn), F32)],
        compiler_params=_cparams(2, 56),
        name="conv_branch",
    )(x, w_t, w_t, w_t, b2d, b2d, b2d, conv_w, conv_b2d)


def _if_kernel(x_ref, wp_ref, wt_ref, bc_ref, br_ref, oc_ref, or_ref):
    x = x_ref[...]
    oc_ref[...] = lax.dot_general(x, wp_ref[...], _NT, preferred_element_type=F32) + bc_ref[...]
    or_ref[...] = lax.dot_general(wt_ref[...], x, _NT, preferred_element_type=F32) + br_ref[...]


def _if_gates(x, w_if_t, b_if, tm=512):
    T, K = x.shape
    tm = min(tm, T)
    n = w_if_t.shape[0]
    w_if_t = w_if_t.astype(BF16)
    w_pad = jnp.zeros((128, K), BF16).at[:n].set(w_if_t)
    b_pad = jnp.zeros((1, 128), F32).at[0, :n].set(b_if)
    return pl.pallas_call(
        _if_kernel,
        out_shape=(jax.ShapeDtypeStruct((T, 128), F32), jax.ShapeDtypeStruct((n, T), F32)),
        grid=(T // tm,),
        in_specs=[pl.BlockSpec((tm, K), lambda i: (i, 0)),
                  pl.BlockSpec((128, K), lambda i: (0, 0)),
                  pl.BlockSpec((n, K), lambda i: (0, 0)),
                  pl.BlockSpec((1, 128), lambda i: (0, 0)),
                  pl.BlockSpec((n, 1), lambda i: (0, 0))],
        out_specs=(pl.BlockSpec((tm, 128), lambda i: (i, 0)),
                   pl.BlockSpec((n, tm), lambda i: (0, i))),
        compiler_params=_cparams(1, 32),
        name="if_gates",
    )(x, w_pad, w_if_t, b_pad, b_if.reshape(n, 1))


def _mlstm_kernel(q_ref, k_ref, v_ref, o_ref, ifc_ref, ifr_ref, g_ref, y_ref, c_ref, m_ref,
                  *, dk, dv):
    H = N_HEADS
    L = q_ref.shape[0]

    @pl.when(pl.program_id(1) == 0)
    def _():
        c_ref[...] = jnp.zeros_like(c_ref)
        m_ref[...] = jnp.zeros_like(m_ref)

    scale = dk ** -0.5
    ifc = ifc_ref[...]
    ifr = ifr_ref[...]
    ig_c = ifc[:, 0:H]
    lf_c = jax.nn.log_sigmoid(ifc[:, H:2 * H])
    ig_r = ifr[0:H, :]
    lf_r = jax.nn.log_sigmoid(ifr[H:2 * H, :])
    r = lax.broadcasted_iota(I32, (L, L), 0)
    c = lax.broadcasted_iota(I32, (L, L), 1)
    causal = r >= c
    bcum_c = jnp.dot(causal.astype(F32), lf_c, preferred_element_type=F32,
                     precision=lax.Precision.HIGHEST)
    bcum_r = jnp.dot(lf_r, (r <= c).astype(F32), preferred_element_type=F32,
                     precision=lax.Precision.HIGHEST)
    d_c = ig_c - bcum_c
    d_r = ig_r - bcum_r
    ones_col = (lax.broadcasted_iota(I32, (L, 128), 1) == 0).astype(BF16)

    for h in range(H):
        q = q_ref[:, h * dk:(h + 1) * dk]
        k = k_ref[:, h * dk:(h + 1) * dk]
        v = v_ref[:, h * dv:(h + 1) * dv]
        v_aug = jnp.concatenate([v, ones_col], axis=1)
        bc = bcum_c[:, h:h + 1]
        m_prev = m_ref[h]
        c_prev = c_ref[h]

        qk = lax.dot_general(q, k, (((1,), (1,)), ((), ())), preferred_element_type=F32) * scale
        logd = jnp.where(causal, bc + d_r[h:h + 1, :], NEG_BIG)
        log_inter = bc + m_prev
        m_t = jnp.maximum(log_inter, jnp.max(logd, axis=1, keepdims=True))
        w_inter = jnp.exp(log_inter - m_t)
        s_mat = (qk * jnp.exp(logd - m_t)).astype(BF16)
        tot = (w_inter * jnp.dot(q, c_prev.astype(BF16), preferred_element_type=F32)
               + jnp.dot(s_mat, v_aug, preferred_element_type=F32))
        den = jnp.maximum(jnp.abs(tot[:, dv:dv + 1]), jnp.exp(-m_t))
        hh = tot[:, :dv] / den
        mu = jnp.mean(hh, axis=-1, keepdims=True)
        hc = hh - mu
        var = jnp.mean(hc * hc, axis=-1, keepdims=True)
        hn = hc * lax.rsqrt(var + LN_EPS) * g_ref[:, h * dv:(h + 1) * dv]
        og = _sigmoid(o_ref[:, h * dv:(h + 1) * dv].astype(F32))
        y_ref[:, h * dv:(h + 1) * dv] = (og * hn).astype(y_ref.dtype)

        g_tot = bc[L - 1:L, :]
        a = g_tot + d_c[:, h:h + 1]
        m_new = jnp.maximum(g_tot + m_prev, jnp.max(a, axis=0, keepdims=True))
        kw = (k.astype(F32) * (jnp.exp(a - m_new) * scale)).astype(BF16)
        kv = lax.dot_general(kw, v_aug, (((0,), (0,)), ((), ())), preferred_element_type=F32)
        c_ref[h] = jnp.exp(g_tot + m_prev - m_new) * c_prev + kv
        m_ref[h] = m_new


def _mlstm(qkvo, if_col, if_row, mh_g2d, batch, seq, d_qk, d_v, L=MLSTM_CHUNK):
    T = qkvo.shape[0]
    L = min(L, seq)
    assert seq % L == 0 and d_v == 2 * d_qk
    nc = seq // L
    dk, dv = d_qk // N_HEADS, d_v // N_HEADS
    row = lambda b, c: b * nc + c
    return pl.pallas_call(
        functools.partial(_mlstm_kernel, dk=dk, dv=dv),
        out_shape=jax.ShapeDtypeStruct((T, d_v), BF16),
        grid=(batch, nc),
        in_specs=[pl.BlockSpec((L, d_qk), lambda b, c: (row(b, c), 0)),
                  pl.BlockSpec((L, d_qk), lambda b, c: (row(b, c), 1)),
                  pl.BlockSpec((L, d_v), lambda b, c: (row(b, c), 1)),
                  pl.BlockSpec((L, d_v), lambda b, c: (row(b, c), 2)),
                  pl.BlockSpec((L, 128), lambda b, c: (row(b, c), 0)),
                  pl.BlockSpec((2 * N_HEADS, L), lambda b, c: (0, row(b, c))),
                  pl.BlockSpec((1, d_v), lambda b, c: (0, 0))],
        out_specs=pl.BlockSpec((L, d_v), lambda b, c: (row(b, c), 0)),
        scratch_shapes=[pltpu.VMEM((N_HEADS, dk, dv + 128), F32),
                        pltpu.VMEM((N_HEADS, 1, 1), F32)],
        compiler_params=_cparams(2, 40),
        name="mlstm",
    )(qkvo, qkvo, qkvo, qkvo, if_col, if_row, mh_g2d)


def _merge_kernel(a_ref, b_ref, wa_ref, wb_ref, ga_ref, gb_ref, o_ref, wabf_ref, wbbf_ref):
    @pl.when(pl.program_id(1) == 0)
    def _():
        wabf_ref[...] = wa_ref[...].astype(BF16)
        wbbf_ref[...] = wb_ref[...].astype(BF16)

    ya = jnp.dot(a_ref[...], wabf_ref[...], preferred_element_type=F32)
    yb = jnp.dot(b_ref[...], wbbf_ref[...], preferred_element_type=F32)
    u = ga_ref[...].astype(F32) * ya + gb_ref[...].astype(F32) * yb
    o_ref[...] = u.astype(o_ref.dtype)


def _merge(ya_pre, yb_pre, w_a, w_b, gates, *, tm=1024, tn=512):
    T, K = ya_pre.shape
    D = w_a.shape[1]
    tm = min(tm, T)
    nj = D // tn
    return pl.pallas_call(
        _merge_kernel,
        out_shape=jax.ShapeDtypeStruct((T, D), BF16),
        grid=(nj, T // tm),
        in_specs=[pl.BlockSpec((tm, K), lambda j, i: (i, 0)),
                  pl.BlockSpec((tm, K), lambda j, i: (i, 0)),
                  pl.BlockSpec((K, tn), lambda j, i: (0, j)),
                  pl.BlockSpec((K, tn), lambda j, i: (0, j)),
                  pl.BlockSpec((tm, tn), lambda j, i: (i, j)),
                  pl.BlockSpec((tm, tn), lambda j, i: (i, nj + j))],
        out_specs=pl.BlockSpec((tm, tn), lambda j, i: (i, j)),
        scratch_shapes=[pltpu.VMEM((K, tn), BF16), pltpu.VMEM((K, tn), BF16)],
        compiler_params=_cparams(2, 56),
        name="merge",
    )(ya_pre, yb_pre, w_a, w_b, gates, gates)


def _mix_kernel(u_ref, w_ref, h_ref, o_ref, wbf_ref, *, alpha):
    @pl.when(pl.program_id(1) == 0)
    def _():
        wbf_ref[...] = w_ref[...].astype(BF16)

    o_ref[...] = alpha * h_ref[...] + jnp.dot(u_ref[...], wbf_ref[...], preferred_element_type=F32)


def _mix(u, w, h0, alpha, *, tm=1024, tn=512):
    T, D = h0.shape
    tm = min(tm, T)
    return pl.pallas_call(
        functools.partial(_mix_kernel, alpha=alpha),
        out_shape=jax.ShapeDtypeStruct((T, D), F32),
        grid=(D // tn, T // tm),
        in_specs=[pl.BlockSpec((tm, D), lambda j, i: (i, 0)),
                  pl.BlockSpec((D, tn), lambda j, i: (0, j)),
                  pl.BlockSpec((tm, tn), lambda j, i: (i, j))],
        out_specs=pl.BlockSpec((tm, tn), lambda j, i: (i, j)),
        scratch_shapes=[pltpu.VMEM((D, tn), BF16)],
        compiler_params=_cparams(2, 56),
        name="mix_out",
    )(u, w, h0)


def _ln1_router_kernel(pre_ref, g_ref, b_ref, wr_ref, rb_ref, hf_ref, hb_ref, hp_ref,
                       idx_ref, rank_ref, wcol_ref, cnt_ref, carry_ref):
    i = pl.program_id(0)
    E = wr_ref.shape[0]
    tm, D = pre_ref.shape
    G, M = N_GROUPS, E // N_GROUPS

    @pl.when(i == 0)
    def _():
        carry_ref[...] = jnp.zeros_like(carry_ref)

    h = _ln_rows(pre_ref[...], g_ref[...], b_ref[...])
    hf_ref[...] = h
    hb_ref[...] = h.astype(BF16)
    hp_ref[...] = _pack_halves(h[:, :D // 2], h[:, D // 2:])

    logits = lax.dot_general(wr_ref[...], h, (((1,), (1,)), ((), ())),
                             preferred_element_type=F32, precision=lax.Precision.HIGHEST)
    scores = jax.nn.sigmoid(logits)
    scores3 = scores.reshape(G, M, tm)
    sel3 = (scores + rb_ref[...]).reshape(G, M, tm)
    midx = lax.broadcasted_iota(I32, (G, M, tm), 1)
    gidx3 = lax.broadcasted_iota(I32, (G, M, tm), 0)
    eidx = gidx3 * M + midx
    gidx = lax.broadcasted_iota(I32, (G, 1, tm), 0)
    neg_inf = -jnp.inf

    top1 = jnp.max(sel3, axis=1, keepdims=True)
    first1 = jnp.min(jnp.where(sel3 == top1, midx, M), axis=1, keepdims=True)
    top2 = jnp.max(jnp.where(midx == first1, neg_inf, sel3), axis=1, keepdims=True)
    gs = top1 + top2
    gkeep = jnp.zeros((G, 1, tm), F32)
    for _ in range(TOP_GROUPS):
        mx = jnp.max(gs, axis=0, keepdims=True)
        first = jnp.min(jnp.where(gs == mx, gidx, G), axis=0, keepdims=True)
        hit = gidx == first
        gkeep = jnp.where(hit, 1.0, gkeep)
        gs = jnp.where(hit, neg_inf, gs)
    selm = jnp.where(gkeep > 0.5, sel3, neg_inf)

    idx_rows, sc_rows = [], []
    chosen = jnp.zeros((G, M, tm), F32)
    for _ in range(TOP_K):
        mx = jnp.max(jnp.max(selm, axis=1, keepdims=True), axis=0, keepdims=True)
        first = jnp.min(jnp.min(jnp.where(selm == mx, eidx, E), axis=1, keepdims=True),
                        axis=0, keepdims=True)
        hit = eidx == first
        sc = jnp.sum(jnp.sum(jnp.where(hit, scores3, 0.0), axis=1, keepdims=True),
                     axis=0, keepdims=True)
        chosen = jnp.where(hit, 1.0, chosen)
        selm = jnp.where(hit, neg_inf, selm)
        idx_rows.append(first)
        sc_rows.append(sc)
    denom = sc_rows[0]
    for s in sc_rows[1:]:
        denom = denom + s

    tr = lax.broadcasted_iota(I32, (tm, tm), 0)
    tc = lax.broadcasted_iota(I32, (tm, tm), 1)
    before = (tr < tc).astype(BF16)
    chosen2 = chosen.reshape(E, tm)
    rank2 = jnp.dot(chosen2.astype(BF16), before, preferred_element_type=F32) + carry_ref[...]
    rank3 = rank2.reshape(G, M, tm)
    w_rows = []
    for k in range(TOP_K):
        hit = eidx == idx_rows[k]
        rk = jnp.sum(jnp.sum(jnp.where(hit, rank3, 0.0), axis=1, keepdims=True),
                     axis=0, keepdims=True)
        idx_ref[k:k + 1, :] = idx_rows[k].reshape(1, tm)
        rank_ref[k:k + 1, :] = rk.reshape(1, tm).astype(I32)
        w_rows.append((sc_rows[k] / denom * ROUTE_SCALE).reshape(1, tm))
    wcol_ref[...] = jnp.concatenate(w_rows, axis=0).T
    carry_ref[...] = carry_ref[...] + jnp.sum(chosen2, axis=1, keepdims=True)

    @pl.when(i == pl.num_programs(0) - 1)
    def _():
        cnt_ref[...] = carry_ref[...]


def _ln1_router(pre, g, b, w_router, router_bias, tm):
    T, D = pre.shape
    E = w_router.shape[1]
    nt = T // tm
    row = lambda i: (i, 0)
    fixed = lambda i: (0, 0)
    return pl.pallas_call(
        _ln1_router_kernel,
        out_shape=(jax.ShapeDtypeStruct((T, D), F32), jax.ShapeDtypeStruct((T, D), BF16),
                   jax.ShapeDtypeStruct((T, D // 2), U32),
                   jax.ShapeDtypeStruct((nt, TOP_K, tm), I32), jax.ShapeDtypeStruct((nt, TOP_K, tm), I32),
                   jax.ShapeDtypeStruct((T, TOP_K), F32), jax.ShapeDtypeStruct((E, 1), F32)),
        grid=(nt,),
        in_specs=[pl.BlockSpec((tm, D), row), pl.BlockSpec((1, D), fixed), pl.BlockSpec((1, D), fixed),
                  pl.BlockSpec((E, D), fixed), pl.BlockSpec((E, 1), fixed)],
        out_specs=(pl.BlockSpec((tm, D), row), pl.BlockSpec((tm, D), row), pl.BlockSpec((tm, D // 2), row),
                   pl.BlockSpec((None, TOP_K, tm), lambda i: (i, 0, 0)),
                   pl.BlockSpec((None, TOP_K, tm), lambda i: (i, 0, 0)),
                   pl.BlockSpec((tm, TOP_K), row),
                   pl.BlockSpec((E, 1), fixed)),
        scratch_shapes=[pltpu.VMEM((E, 1), F32)],
        compiler_params=_cparams(1, 40),
        name="ln1_router",
    )(pre, g.reshape(1, D), b.reshape(1, D), w_router.T, router_bias.reshape(E, 1))


def _expert_up_kernel(be_ref, nxt_ref, nused_ref, x_ref, w_hbm, o_ref, wbf_ref, stage_ref, wsem):
    b = pl.program_id(0)
    nb = pl.num_programs(0)
    n_used = nused_ref[0]
    half = x_ref.shape[1]
    f = o_ref.shape[1]
    e = be_ref[b]
    run_start = jnp.logical_or(b == 0, e != be_ref[jnp.maximum(b - 1, 0)])

    def fetch(e_):
        return pltpu.make_async_copy(w_hbm.at[e_], stage_ref, wsem)

    @pl.when(b == 0)
    def _():
        fetch(e).start()

    @pl.when(jnp.logical_and(run_start, b < n_used))
    def _():
        fetch(e).wait()
        wbf_ref[...] = stage_ref[...].astype(BF16)
        b_next = nxt_ref[b]

        @pl.when(b_next < n_used)
        def _():
            fetch(be_ref[jnp.minimum(b_next, nb - 1)]).start()

    @pl.when(b < n_used)
    def _():
        lo, hi = _unpack_halves(x_ref[...])
        gu = (jnp.dot(lo.astype(BF16), wbf_ref[:half], preferred_element_type=F32)
              + jnp.dot(hi.astype(BF16), wbf_ref[half:], preferred_element_type=F32))
        g = gu[:, :f]
        o_ref[...] = (g * _sigmoid(g) * gu[:, f:]).astype(o_ref.dtype)

    @pl.when(b >= n_used)
    def _():
        o_ref[...] = jnp.zeros_like(o_ref)


def _expert_up(xs, w_gu_e, blk_e, nxt_blk, n_used):
    P, half = xs.shape
    E, D, two_f = w_gu_e.shape
    f = two_f // 2
    assert D == 2 * half
    nb = P // MOE_BLOCK
    return pl.pallas_call(
        _expert_up_kernel,
        out_shape=jax.ShapeDtypeStruct((P, f), BF16),
        grid_spec=pltpu.PrefetchScalarGridSpec(
            num_scalar_prefetch=3,
            grid=(nb,),
            in_specs=[pl.BlockSpec((MOE_BLOCK, half), lambda b, be, nx, nu: (jnp.minimum(b, nu[0] - 1), 0)),
                      pl.BlockSpec(memory_space=pl.ANY)],
            out_specs=pl.BlockSpec((MOE_BLOCK, f), lambda b, be, nx, nu: (b, 0)),
            scratch_shapes=[pltpu.VMEM((D, two_f), BF16), pltpu.VMEM((D, two_f), F32),
                            pltpu.SemaphoreType.DMA]),
        compiler_params=_cparams(1, 56),
        name="expert_up",
    )(blk_e, nxt_blk, n_used, xs, w_gu_e)


def _expert_down_kernel(be_ref, nxt_ref, nused_ref, h_ref, w_hbm, o_ref, wbf_ref, stage_ref, run_ref,
                        wsem):
    b = pl.program_id(0)
    nb = pl.num_programs(0)
    n_used = nused_ref[0]
    e = be_ref[b]
    run_start = jnp.logical_or(b == 0, e != be_ref[jnp.maximum(b - 1, 0)])

    def fetch(e_, slot):
        return pltpu.make_async_copy(w_hbm.at[e_], stage_ref.at[slot], wsem.at[slot])

    @pl.when(b == 0)
    def _():
        run_ref[0] = 0
        fetch(e, 0).start()

    @pl.when(jnp.logical_and(run_start, b < n_used))
    def _():
        k = run_ref[0]
        slot = k % 2
        fetch(e, slot).wait()
        wbf_ref[...] = stage_ref[slot].astype(BF16)
        b_next = nxt_ref[b]

        @pl.when(b_next < n_used)
        def _():
            fetch(be_ref[jnp.minimum(b_next, nb - 1)], 1 - slot).start()
        run_ref[0] = k + 1

    @pl.when(b < n_used)
    def _():
        y = jnp.dot(h_ref[...], wbf_ref[...], preferred_element_type=F32)
        half = y.shape[1] // 2
        o_ref[...] = _pack_halves(y[:, :half], y[:, half:])

    @pl.when(b >= n_used)
    def _():
        o_ref[...] = jnp.zeros_like(o_ref)


def _expert_down(hmid, w_down_e, blk_e, nxt_blk, n_used):
    P, f = hmid.shape
    E, _, D = w_down_e.shape
    nb = P // MOE_BLOCK
    return pl.pallas_call(
        _expert_down_kernel,
        out_shape=jax.ShapeDtypeStruct((P, D // 2), U32),
        grid_spec=pltpu.PrefetchScalarGridSpec(
            num_scalar_prefetch=3,
            grid=(nb,),
            in_specs=[pl.BlockSpec((MOE_BLOCK, f), lambda b, be, nx, nu: (jnp.minimum(b, nu[0] - 1), 0)),
                      pl.BlockSpec(memory_space=pl.ANY)],
            out_specs=pl.BlockSpec((MOE_BLOCK, D // 2), lambda b, be, nx, nu: (b, 0)),
            scratch_shapes=[pltpu.VMEM((f, D), BF16), pltpu.VMEM((2, f, D), F32),
                            pltpu.SMEM((1,), I32), pltpu.SemaphoreType.DMA((2,))]),
        compiler_params=_cparams(1, 56),
        name="expert_down",
    )(blk_e, nxt_blk, n_used, hmid, w_down_e)


def _combine_kernel(dest_hbm, y_hbm, w_ref, pre_ref, g_ref, b_ref, o_ref, idx_smem, buf_ref, isem, gsem):
    i = pl.program_id(0)
    n = pl.num_programs(0)
    tc = o_ref.shape[0]
    half = buf_ref.shape[3]
    n_idx = TOP_K * tc

    def idx_copy(tile, slot):
        return pltpu.make_async_copy(dest_hbm.at[tile], idx_smem.at[pl.ds(slot * n_idx, n_idx)],
                                     isem.at[slot])

    def issue_tile(slot):
        base = slot * n_idx

        def body(r8, carry):
            for s in range(8):
                r = r8 * 8 + s
                for k in range(TOP_K):
                    d = idx_smem[base + k * tc + r]
                    pltpu.make_async_copy(y_hbm.at[pl.ds(d, 1)], buf_ref.at[slot, k, pl.ds(r, 1)],
                                          gsem.at[slot]).start(priority=k % 2)
            return carry
        lax.fori_loop(0, tc // 8, body, 0)

    slot = i % 2

    @pl.when(i == 0)
    def _():
        idx_copy(0, 0).start()
        idx_copy(0, 0).wait()
        issue_tile(0)

        @pl.when(n > 1)
        def _():
            idx_copy(1, 1).start()

    @pl.when(i + 1 < n)
    def _():
        idx_copy(i + 1, 1 - slot).wait()
        for sl in range(2):
            @pl.when(slot == 1 - sl)
            def _():
                issue_tile(sl)

    @pl.when(i + 2 < n)
    def _():
        idx_copy(i + 2, slot).start()

    for k in range(TOP_K):
        pltpu.make_async_copy(buf_ref.at[slot, k], buf_ref.at[slot, k], gsem.at[slot]).wait()

    w = w_ref[...]
    acc_lo = jnp.zeros((tc, half), F32)
    acc_hi = jnp.zeros((tc, half), F32)
    for k in range(TOP_K):
        lo, hi = _unpack_halves(buf_ref[slot, k])
        wk = w[:, k:k + 1]
        acc_lo = acc_lo + wk * lo
        acc_hi = acc_hi + wk * hi
    y_lo = pre_ref[:, :half] + acc_lo
    y_hi = pre_ref[:, half:] + acc_hi
    inv_d = 1.0 / (2 * half)
    mu = (jnp.sum(y_lo, axis=-1, keepdims=True) + jnp.sum(y_hi, axis=-1, keepdims=True)) * inv_d
    c_lo = y_lo - mu
    c_hi = y_hi - mu
    var = (jnp.sum(c_lo * c_lo, axis=-1, keepdims=True)
           + jnp.sum(c_hi * c_hi, axis=-1, keepdims=True)) * inv_d
    rstd = lax.rsqrt(var + LN_EPS)
    o_ref[:, :half] = c_lo * rstd * g_ref[:, :half] + b_ref[:, :half]
    o_ref[:, half:] = c_hi * rstd * g_ref[:, half:] + b_ref[:, half:]


def _combine_ln(y_packed, dest_tiles, w_col, pre, g, b):
    T, D = pre.shape
    half = y_packed.shape[1]
    nt, n_idx = dest_tiles.shape
    tc = n_idx // TOP_K
    return pl.pallas_call(
        _combine_kernel,
        out_shape=jax.ShapeDtypeStruct((T, D), F32),
        grid=(nt,),
        in_specs=[pl.BlockSpec(memory_space=pl.ANY), pl.BlockSpec(memory_space=pl.ANY),
                  pl.BlockSpec((tc, TOP_K), lambda i: (i, 0)),
                  pl.BlockSpec((tc, D), lambda i: (i, 0)),
                  pl.BlockSpec((1, D), lambda i: (0, 0)),
                  pl.BlockSpec((1, D), lambda i: (0, 0))],
        out_specs=pl.BlockSpec((tc, D), lambda i: (i, 0)),
        scratch_shapes=[pltpu.SMEM((2 * n_idx,), I32),
                        pltpu.VMEM((2, TOP_K, tc, half), U32),
                        pltpu.SemaphoreType.DMA((2,)), pltpu.SemaphoreType.DMA((2,))],
        compiler_params=_cparams(1, 48),
        name="combine_ln2",
    )(dest_tiles, y_packed, w_col, pre, g.reshape(1, D), b.reshape(1, D))


def _glu_up_kernel(h_ref, wg_ref, wu_ref, o_ref, wbf_ref):
    tn = wg_ref.shape[1]

    @pl.when(pl.program_id(1) == 0)
    def _():
        wbf_ref[:, :tn] = wg_ref[...].astype(BF16)
        wbf_ref[:, tn:] = wu_ref[...].astype(BF16)

    gu = jnp.dot(h_ref[...], wbf_ref[...], preferred_element_type=F32)
    g = gu[:, :tn]
    o_ref[...] = (g * _sigmoid(g) * gu[:, tn:]).astype(o_ref.dtype)


def _glu_up(h_bf, w_gu, *, tm=512, tn=384):
    T, D = h_bf.shape
    f = w_gu.shape[1] // 2
    tm = min(tm, T)
    nc = f // tn
    return pl.pallas_call(
        _glu_up_kernel,
        out_shape=jax.ShapeDtypeStruct((T, f), BF16),
        grid=(nc, T // tm),
        in_specs=[pl.BlockSpec((tm, D), lambda c, i: (i, 0)),
                  pl.BlockSpec((D, tn), lambda c, i: (0, c)),
                  pl.BlockSpec((D, tn), lambda c, i: (0, nc + c))],
        out_specs=pl.BlockSpec((tm, tn), lambda c, i: (i, c)),
        scratch_shapes=[pltpu.VMEM((D, 2 * tn), BF16)],
        compiler_params=_cparams(2, 56),
        name="shared_up",
    )(h_bf, w_gu, w_gu)


def _dense_tail_kernel(fill_ref, dest_hbm, x_ref, hb_ref, hj_ref, s_ref, p_ref, wd_ref, wg_ref, bg_ref,
                       wp_ref, o_ref, xs_hbm, wdbf_ref, wgbf_ref, wpbf_ref, idx_smem, zero_ref,
                       isem, ssem, zsem, *, alpha, k_per_step):
    j = pl.program_id(0)
    i = pl.program_id(1)
    ni = pl.num_programs(1)
    step = j * ni + i
    tm = x_ref.shape[0]
    n_idx = k_per_step * tm
    R = zero_ref.shape[0]

    def idx_copies(jj, ii, slot):
        return [pltpu.make_async_copy(dest_hbm.at[jj * k_per_step + kk, pl.ds(pl.multiple_of(ii * tm, 128), tm)],
                                      idx_smem.at[pl.ds(slot * n_idx + kk * tm, tm)], isem.at[slot])
                for kk in range(k_per_step)]

    @pl.when(step == 0)
    def _():
        for cp in idx_copies(0, 0, 0):
            cp.start()
        zero_ref[...] = jnp.zeros_like(zero_ref)

        def fill(f):
            return pltpu.make_async_copy(zero_ref, xs_hbm.at[pl.ds(fill_ref[f] * R, R)], zsem)

        def start(f, carry):
            @pl.when(fill_ref[f] >= 0)
            def _():
                fill(f).start()
            return carry
        lax.fori_loop(0, fill_ref.shape[0], start, 0)

        def wait(f, carry):
            @pl.when(fill_ref[f] >= 0)
            def _():
                fill(f).wait()
            return carry
        lax.fori_loop(0, fill_ref.shape[0], wait, 0)

    @pl.when(i == 0)
    def _():
        wdbf_ref[...] = wd_ref[...].astype(BF16)
        wgbf_ref[...] = wg_ref[...].astype(BF16)
        wpbf_ref[...] = wp_ref[...].astype(BF16)

    slot = step % 2
    for cp in idx_copies(j, i, slot):
        cp.wait()

    @pl.when(step + 1 < pl.num_programs(0) * ni)
    def _():
        nxt = step + 1
        for cp in idx_copies(nxt // ni, nxt % ni, 1 - slot):
            cp.start()

    base = slot * n_idx
    for kk in range(k_per_step):
        for r in range(tm):
            d = idx_smem[base + kk * tm + r]
            pltpu.make_async_copy(x_ref.at[pl.ds(r, 1)], xs_hbm.at[pl.ds(d, 1)], ssem).start(priority=r % 2)

    shared = jnp.dot(s_ref[...], wdbf_ref[...], preferred_element_type=F32)
    gate = _sigmoid(jnp.dot(hb_ref[...], wgbf_ref[...], preferred_element_type=F32) + bg_ref[...])
    proj = jnp.dot(p_ref[...].astype(BF16), wpbf_ref[...], preferred_element_type=F32)
    o_ref[...] = alpha * hj_ref[...] + shared + gate * proj

    for kk in range(k_per_step):
        pltpu.make_async_copy(x_ref, x_ref, ssem).wait()


def _dense_tail(h1_bf, h1, h1_packed, dest_kt, fill_blocks, n_rows, s_mid, p, w_down_s, w_gate, b_gate,
                w_proj, alpha, *, tm=512, tn=512):
    T, D = h1.shape
    f = s_mid.shape[1]
    dp = p.shape[1]
    W = h1_packed.shape[1]
    tm = min(tm, T)
    nj = D // tn
    assert TOP_K % nj == 0 and tm % 128 == 0
    k_per_step = TOP_K // nj
    row = lambda j, i, fb: (i, 0)
    col = lambda j, i, fb: (0, j)
    tile = lambda j, i, fb: (i, j)
    return pl.pallas_call(
        functools.partial(_dense_tail_kernel, alpha=alpha, k_per_step=k_per_step),
        out_shape=(jax.ShapeDtypeStruct((T, D), F32), jax.ShapeDtypeStruct((n_rows, W), h1_packed.dtype)),
        grid_spec=pltpu.PrefetchScalarGridSpec(
            num_scalar_prefetch=1,
            grid=(nj, T // tm),
            in_specs=[pl.BlockSpec(memory_space=pl.ANY),
                      pl.BlockSpec((tm, W), row),
                      pl.BlockSpec((tm, D), row),
                      pl.BlockSpec((tm, tn), tile),
                      pl.BlockSpec((tm, f), row),
                      pl.BlockSpec((tm, dp), row),
                      pl.BlockSpec((f, tn), col),
                      pl.BlockSpec((D, tn), col),
                      pl.BlockSpec((1, tn), col),
                      pl.BlockSpec((dp, tn), col)],
            out_specs=(pl.BlockSpec((tm, tn), tile), pl.BlockSpec(memory_space=pl.ANY)),
            scratch_shapes=[pltpu.VMEM((f, tn), BF16), pltpu.VMEM((D, tn), BF16), pltpu.VMEM((dp, tn), BF16),
                            pltpu.SMEM((2 * k_per_step * tm,), I32), pltpu.VMEM((MOE_BLOCK, W), h1_packed.dtype),
                            pltpu.SemaphoreType.DMA((2,)), pltpu.SemaphoreType.DMA, pltpu.SemaphoreType.DMA]),
        compiler_params=pltpu.CompilerParams(dimension_semantics=("arbitrary", "arbitrary"),
                                             vmem_limit_bytes=58 * 1024 * 1024, has_side_effects=True),
        name="dense_tail_dispatch",
    )(fill_blocks, dest_kt, h1_packed, h1_bf, h1, s_mid, p, w_down_s, w_gate, b_gate.reshape(1, D), w_proj)


def _dispatch_tables(idx, rank, counts):
    E = counts.shape[0]
    P = idx.size + E * MOE_BLOCK
    nb = P // MOE_BLOCK
    padded = (counts + MOE_BLOCK - 1) // MOE_BLOCK * MOE_BLOCK
    pend = jnp.cumsum(padded)
    pstart = pend - padded
    sel = idx[None] == jnp.arange(E, dtype=I32).reshape(E, 1, 1, 1)
    dest = rank + jnp.sum(jnp.where(sel, pstart.reshape(E, 1, 1, 1), 0), axis=0)
    blk_row = jnp.arange(nb, dtype=I32) * MOE_BLOCK
    blk_e = jnp.minimum(jnp.sum(pend[None, :] <= blk_row[:, None], axis=1), E - 1).astype(I32)
    n_used = pend[-1] // MOE_BLOCK
    last_blk = jnp.where(counts > 0, pend // MOE_BLOCK - 1, -1)
    tail_blk = n_used + jnp.arange(E, dtype=I32)
    tail_blk = jnp.where(tail_blk < nb, tail_blk, -1)
    fill = jnp.concatenate([last_blk, tail_blk]).astype(I32)
    nxt_blk = jnp.take(pend // MOE_BLOCK, blk_e).astype(I32)
    return dest.astype(I32), blk_e, nxt_blk, n_used.astype(I32).reshape(1), fill, P


def _layer(h0_f, h0_bf, p_l, w_in, b_in, conv_w, conv_b, mh_norm_g, w_conv_out, w_mlstm_out,
           w_mix_out, ln1_g, ln1_b, w_router, router_bias, w_gu_e, w_down_e, w_gu_s, w_down_s,
           w_ple_gate, b_ple_gate, w_ple_proj, ln2_g, ln2_b, alpha, batch, seq):
    T, D = h0_f.shape
    d_conv = conv_w.shape[1]
    d_v = mh_norm_g.shape[0]
    d_qk = d_v // 2
    n_if = 2 * N_HEADS
    c_qk = 3 * d_conv
    c_if = c_qk + 2 * d_qk + 2 * d_v
    c_gate = c_if + n_if
    b2d = b_in.reshape(1, -1)

    w_t = jnp.swapaxes(w_in, 0, 1)
    ya_pre = _conv_branch(h0_bf, w_t, b2d, conv_w, conv_b.reshape(1, -1), seq, d_conv)
    qkvo = _proj(h0_bf, w_t, b2d, c_qk, 2 * d_qk + 2 * d_v)
    if_col, if_row = _if_gates(h0_bf, w_t[c_if:c_gate], b_in[c_if:c_gate])
    gates = _proj(h0_bf, w_t, b2d, c_gate, 2 * D, act="sigmoid")
    yb_pre = _mlstm(qkvo, if_col, if_row, mh_norm_g.reshape(1, -1), batch, seq, d_qk, d_v)
    u = _merge(ya_pre, yb_pre, w_conv_out, w_mlstm_out, gates)
    pre1 = _mix(u, w_mix_out, h0_f, alpha)

    rt = min(ROUTE_TILE, T)
    h1, h1_bf, h1_packed, idx, rank, w_col, counts = _ln1_router(pre1, ln1_g, ln1_b, w_router,
                                                                 router_bias, rt)
    dest, blk_e, nxt_blk, n_used, fill, n_rows = _dispatch_tables(idx, rank, counts[:, 0].astype(I32))
    dest_tiles = dest.reshape(T // rt, TOP_K * rt)
    dest_kt = dest.transpose(1, 0, 2).reshape(TOP_K, T)

    s_mid = _glu_up(h1_bf, w_gu_s)
    pre2, xs = _dense_tail(h1_bf, h1, h1_packed, dest_kt, fill, n_rows, s_mid, p_l, w_down_s,
                           w_ple_gate, b_ple_gate, w_ple_proj, alpha)

    hmid = _expert_up(xs, w_gu_e, blk_e, nxt_blk, n_used)
    y_packed = _expert_down(hmid, w_down_e, blk_e, nxt_blk, n_used)
    return _combine_ln(y_packed, dest_tiles, w_col, pre2, ln2_g, ln2_b)


def kernel(x, p, ln_in_g, ln_in_b, w_in, b_in, conv_w, conv_b, mh_norm_g, w_conv_out, w_mlstm_out,
           w_mix_out, ln1_g, ln1_b, w_router, router_bias, w_gu_e, w_down_e, w_gu_s, w_down_s,
           w_ple_gate, b_ple_gate, w_ple_proj, ln2_g, ln2_b):
    B, S, D = x.shape
    depth = w_in.shape[0]
    alpha = (2 * depth) ** 0.25
    T = B * S
    h_f, h_bf = _ln_in(x.reshape(T, D), ln_in_g, ln_in_b)
    for l in range(depth):
        h_f = _layer(h_f, h_bf, p[l].reshape(T, -1), w_in[l], b_in[l], conv_w[l], conv_b[l],
                     mh_norm_g[l], w_conv_out[l], w_mlstm_out[l], w_mix_out[l], ln1_g[l], ln1_b[l],
                     w_router[l], router_bias[l], w_gu_e[l], w_down_e[l], w_gu_s[l], w_down_s[l],
                     w_ple_gate[l], b_ple_gate[l], w_ple_proj[l], ln2_g[l], ln2_b[l], alpha, B, S)
        if l + 1 < depth:
            h_bf = h_f.astype(BF16)
    return h_f.reshape(B, S, D)
```

```python
import functools

import jax
import jax.numpy as jnp
from jax import lax
from jax.experimental import pallas as pl
from jax.experimental.pallas import tpu as pltpu

F32 = jnp.float32
BF16 = jnp.bfloat16
U32 = jnp.uint32
I32 = jnp.int32

N_HEADS = 8
TOP_K = 8
N_GROUPS = 8
TOP_GROUPS = 4
ROUTE_SCALE = 2.5
MOE_BLOCK = 256
LN_EPS = 1e-5
CONV_W = 3
MLSTM_CHUNK = 128
ROUTE_TILE = 128
EXPERT_SUB = 2
NEG_BIG = -1e30
HI_MASK = 0xFFFF0000
V7X_VMEM_BYTES = 64 * 1024 * 1024


def _cparams(n_axes, vmem_mib):
    assert vmem_mib * 1024 * 1024 <= V7X_VMEM_BYTES
    return pltpu.CompilerParams(dimension_semantics=("arbitrary",) * n_axes,
                                vmem_limit_bytes=vmem_mib * 1024 * 1024)


def _ln_rows(x, g, b):
    mu = jnp.mean(x, axis=-1, keepdims=True)
    xc = x - mu
    var = jnp.mean(xc * xc, axis=-1, keepdims=True)
    return xc * lax.rsqrt(var + LN_EPS) * g + b


def _pack_halves(lo, hi):
    lo = pltpu.bitcast(lo.astype(BF16).astype(F32), U32)
    hi = pltpu.bitcast(hi.astype(BF16).astype(F32), U32)
    return (hi & jnp.uint32(HI_MASK)) | (lo >> jnp.uint32(16))


def _sigmoid(x):
    return 0.5 * jnp.tanh(0.5 * x) + 0.5


def _unpack_halves(w):
    lo = pltpu.bitcast(w << jnp.uint32(16), F32)
    hi = pltpu.bitcast(w & jnp.uint32(HI_MASK), F32)
    return lo, hi


def _ln_in_kernel(x_ref, g_ref, b_ref, of_ref, ob_ref):
    y = _ln_rows(x_ref[...], g_ref[...], b_ref[...])
    of_ref[...] = y
    ob_ref[...] = y.astype(BF16)


def _ln_in(x2, g, b, tm=256):
    T, D = x2.shape
    return pl.pallas_call(
        _ln_in_kernel,
        out_shape=(jax.ShapeDtypeStruct((T, D), F32), jax.ShapeDtypeStruct((T, D), BF16)),
        grid=(T // tm,),
        in_specs=[pl.BlockSpec((tm, D), lambda i: (i, 0)),
                  pl.BlockSpec((1, D), lambda i: (0, 0)),
                  pl.BlockSpec((1, D), lambda i: (0, 0))],
        out_specs=(pl.BlockSpec((tm, D), lambda i: (i, 0)),
                   pl.BlockSpec((tm, D), lambda i: (i, 0))),
        compiler_params=_cparams(1, 40),
        name="ln_in",
    )(x2, g.reshape(1, D), b.reshape(1, D))


_NT = (((1,), (1,)), ((), ()))


def _proj_kernel(x_ref, w_ref, b_ref, o_ref, wbf_ref, *, act):
    @pl.when(pl.program_id(1) == 0)
    def _():
        wbf_ref[...] = w_ref[...].astype(BF16)

    acc = lax.dot_general(x_ref[...], wbf_ref[...], _NT, preferred_element_type=F32) + b_ref[...]
    if act == "sigmoid":
        acc = _sigmoid(acc)
    o_ref[...] = acc.astype(o_ref.dtype)


def _proj(x, w_t, bias2d, row0, n_cols, *, act=None, tm=1024, tn=512, out_dtype=BF16):
    T, K = x.shape
    tm = min(tm, T)
    assert n_cols % tn == 0 and T % tm == 0 and row0 % 8 == 0
    if row0 % tn == 0:
        jb = row0 // tn
        w_spec = pl.BlockSpec((tn, K), lambda j, i: (jb + j, 0))
        b_spec = pl.BlockSpec((1, tn), lambda j, i: (0, jb + j))
    else:
        w_spec = pl.BlockSpec((pl.Element(tn), pl.Element(K)),
                              lambda j, i: (pl.multiple_of(row0 + j * tn, 8), 0))
        bias2d = bias2d[:, row0:row0 + n_cols]
        b_spec = pl.BlockSpec((1, tn), lambda j, i: (0, j))
    return pl.pallas_call(
        functools.partial(_proj_kernel, act=act),
        out_shape=jax.ShapeDtypeStruct((T, n_cols), out_dtype),
        grid=(n_cols // tn, T // tm),
        in_specs=[pl.BlockSpec((tm, K), lambda j, i: (i, 0)), w_spec, b_spec],
        out_specs=pl.BlockSpec((tm, tn), lambda j, i: (i, j)),
        scratch_shapes=[pltpu.VMEM((tn, K), BF16)],
        compiler_params=_cparams(2, 56),
        name="proj_" + (act or "lin"),
    )(x, w_t, bias2d)


def _conv_kernel(x_ref, wh_ref, wc_ref, wb_ref, bh_ref, bc_ref, bb_ref, cw_ref, cb_ref,
                 o_ref, wbf_ref, zprev_ref, *, tiles_per_seq):
    i = pl.program_id(1)

    @pl.when(i == 0)
    def _():
        wbf_ref[0] = wh_ref[...].astype(BF16)
        wbf_ref[1] = wc_ref[...].astype(BF16)
        wbf_ref[2] = wb_ref[...].astype(BF16)

    @pl.when(i % tiles_per_seq == 0)
    def _():
        zprev_ref[...] = jnp.zeros_like(zprev_ref)

    x = x_ref[...]
    ha = lax.dot_general(x, wbf_ref[0], _NT, preferred_element_type=F32) + bh_ref[...]
    ca = lax.dot_general(x, wbf_ref[1], _NT, preferred_element_type=F32) + bc_ref[...]
    ba = lax.dot_general(x, wbf_ref[2], _NT, preferred_element_type=F32) + bb_ref[...]
    z = ca * ha
    tm = z.shape[0]
    prev = zprev_ref[...]
    row8 = lax.broadcasted_iota(I32, prev.shape, 0)
    z1 = pltpu.roll(z, 1, 0)
    z2 = pltpu.roll(z, 2, 0)
    p1 = pltpu.roll(prev, 1, 0)
    p2 = pltpu.roll(prev, 2, 0)
    z1 = jnp.concatenate([jnp.where(row8 < 1, p1, z1[:8]), z1[8:]], axis=0)
    z2 = jnp.concatenate([jnp.where(row8 < 2, p2, z2[:8]), z2[8:]], axis=0)
    cw = cw_ref[...]
    y = cw[0:1] * z2 + cw[1:2] * z1 + cw[2:3] * z + cb_ref[...]
    o_ref[...] = (ba * y).astype(o_ref.dtype)
    zprev_ref[...] = z[tm - 8:]


def _conv_branch(x, w_t, b2d, conv_w, conv_b2d, seq, d_conv, *, tm=1024, tn=256):
    T, K = x.shape
    tm = min(tm, seq)
    assert seq % tm == 0 and d_conv % tn == 0 and tm % 8 == 0
    nb = d_conv // tn
    wspec = lambda g: pl.BlockSpec((tn, K), lambda j, i: (g * nb + j, 0))
    bspec = lambda g: pl.BlockSpec((1, tn), lambda j, i: (0, g * nb + j))
    return pl.pallas_call(
        functools.partial(_conv_kernel, tiles_per_seq=seq // tm),
        out_shape=jax.ShapeDtypeStruct((T, d_conv), BF16),
        grid=(nb, T // tm),
        in_specs=[pl.BlockSpec((tm, K), lambda j, i: (i, 0)),
                  wspec(0), wspec(1), wspec(2), bspec(0), bspec(1), bspec(2),
                  pl.BlockSpec((CONV_W, tn), lambda j, i: (0, j)),
                  pl.BlockSpec((1, tn), lambda j, i: (0, j))],
        out_specs=pl.BlockSpec((tm, tn), lambda j, i: (i, j)),
        scratch_shapes=[pltpu.VMEM((3, tn, K), BF16), pltpu.VMEM((8, tn), F32)],
        compiler_params=_cparams(2, 56),
        name="conv_branch",
    )(x, w_t, w_t, w_t, b2d, b2d, b2d, conv_w, conv_b2d)


def _if_kernel(x_ref, wp_ref, wt_ref, bc_ref, br_ref, oc_ref, or_ref):
    x = x_ref[...]
    oc_ref[...] = lax.dot_general(x, wp_ref[...], _NT, preferred_element_type=F32) + bc_ref[...]
    or_ref[...] = lax.dot_general(wt_ref[...], x, _NT, preferred_element_type=F32) + br_ref[...]


def _if_gates(x, w_if_t, b_if, tm=512):
    T, K = x.shape
    tm = min(tm, T)
    n = w_if_t.shape[0]
    w_if_t = w_if_t.astype(BF16)
    w_pad = jnp.zeros((128, K), BF16).at[:n].set(w_if_t)
    b_pad = jnp.zeros((1, 128), F32).at[0, :n].set(b_if)
    return pl.pallas_call(
        _if_kernel,
        out_shape=(jax.ShapeDtypeStruct((T, 128), F32), jax.ShapeDtypeStruct((n, T), F32)),
        grid=(T // tm,),
        in_specs=[pl.BlockSpec((tm, K), lambda i: (i, 0)),
                  pl.BlockSpec((128, K), lambda i: (0, 0)),
                  pl.BlockSpec((n, K), lambda i: (0, 0)),
                  pl.BlockSpec((1, 128), lambda i: (0, 0)),
                  pl.BlockSpec((n, 1), lambda i: (0, 0))],
        out_specs=(pl.BlockSpec((tm, 128), lambda i: (i, 0)),
                   pl.BlockSpec((n, tm), lambda i: (0, i))),
        compiler_params=_cparams(1, 32),
        name="if_gates",
    )(x, w_pad, w_if_t, b_pad, b_if.reshape(n, 1))


def _mlstm_kernel(q_ref, k_ref, v_ref, o_ref, ifc_ref, ifr_ref, g_ref, y_ref, c_ref, m_ref,
                  *, dk, dv):
    H = N_HEADS
    L = q_ref.shape[0]

    @pl.when(pl.program_id(1) == 0)
    def _():
        c_ref[...] = jnp.zeros_like(c_ref)
        m_ref[...] = jnp.zeros_like(m_ref)

    scale = dk ** -0.5
    ifc = ifc_ref[...]
    ifr = ifr_ref[...]
    ig_c = ifc[:, 0:H]
    lf_c = jax.nn.log_sigmoid(ifc[:, H:2 * H])
    ig_r = ifr[0:H, :]
    lf_r = jax.nn.log_sigmoid(ifr[H:2 * H, :])
    r = lax.broadcasted_iota(I32, (L, L), 0)
    c = lax.broadcasted_iota(I32, (L, L), 1)
    causal = r >= c
    bcum_c = jnp.dot(causal.astype(F32), lf_c, preferred_element_type=F32,
                     precision=lax.Precision.HIGHEST)
    bcum_r = jnp.dot(lf_r, (r <= c).astype(F32), preferred_element_type=F32,
                     precision=lax.Precision.HIGHEST)
    d_c = ig_c - bcum_c
    d_r = ig_r - bcum_r
    ones_col = (lax.broadcasted_iota(I32, (L, 128), 1) == 0).astype(BF16)

    for h in range(H):
        q = q_ref[:, h * dk:(h + 1) * dk]
        k = k_ref[:, h * dk:(h + 1) * dk]
        v = v_ref[:, h * dv:(h + 1) * dv]
        v_aug = jnp.concatenate([v, ones_col], axis=1)
        bc = bcum_c[:, h:h + 1]
        m_prev = m_ref[h]
        c_prev = c_ref[h]

        qk = lax.dot_general(q, k, (((1,), (1,)), ((), ())), preferred_element_type=F32) * scale
        logd = jnp.where(causal, bc + d_r[h:h + 1, :], NEG_BIG)
        log_inter = bc + m_prev
        m_t = jnp.maximum(log_inter, jnp.max(logd, axis=1, keepdims=True))
        w_inter = jnp.exp(log_inter - m_t)
        s_mat = (qk * jnp.exp(logd - m_t)).astype(BF16)
        tot = (w_inter * jnp.dot(q, c_prev.astype(BF16), preferred_element_type=F32)
               + jnp.dot(s_mat, v_aug, preferred_element_type=F32))
        den = jnp.maximum(jnp.abs(tot[:, dv:dv + 1]), jnp.exp(-m_t))
        hh = tot[:, :dv] / den
        mu = jnp.mean(hh, axis=-1, keepdims=True)
        hc = hh - mu
        var = jnp.mean(hc * hc, axis=-1, keepdims=True)
        hn = hc * lax.rsqrt(var + LN_EPS) * g_ref[:, h * dv:(h + 1) * dv]
        og = _sigmoid(o_ref[:, h * dv:(h + 1) * dv].astype(F32))
        y_ref[:, h * dv:(h + 1) * dv] = (og * hn).astype(y_ref.dtype)

        g_tot = bc[L - 1:L, :]
        a = g_tot + d_c[:, h:h + 1]
        m_new = jnp.maximum(g_tot + m_prev, jnp.max(a, axis=0, keepdims=True))
        kw = (k.astype(F32) * (jnp.exp(a - m_new) * scale)).astype(BF16)
        kv = lax.dot_general(kw, v_aug, (((0,), (0,)), ((), ())), preferred_element_type=F32)
        c_ref[h] = jnp.exp(g_tot + m_prev - m_new) * c_prev + kv
        m_ref[h] = m_new


def _mlstm(qkvo, if_col, if_row, mh_g2d, batch, seq, d_qk, d_v, L=MLSTM_CHUNK):
    T = qkvo.shape[0]
    L = min(L, seq)
    assert seq % L == 0 and d_v == 2 * d_qk
    nc = seq // L
    dk, dv = d_qk // N_HEADS, d_v // N_HEADS
    row = lambda b, c: b * nc + c
    return pl.pallas_call(
        functools.partial(_mlstm_kernel, dk=dk, dv=dv),
        out_shape=jax.ShapeDtypeStruct((T, d_v), BF16),
        grid=(batch, nc),
        in_specs=[pl.BlockSpec((L, d_qk), lambda b, c: (row(b, c), 0)),
                  pl.BlockSpec((L, d_qk), lambda b, c: (row(b, c), 1)),
                  pl.BlockSpec((L, d_v), lambda b, c: (row(b, c), 1)),
                  pl.BlockSpec((L, d_v), lambda b, c: (row(b, c), 2)),
                  pl.BlockSpec((L, 128), lambda b, c: (row(b, c), 0)),
                  pl.BlockSpec((2 * N_HEADS, L), lambda b, c: (0, row(b, c))),
                  pl.BlockSpec((1, d_v), lambda b, c: (0, 0))],
        out_specs=pl.BlockSpec((L, d_v), lambda b, c: (row(b, c), 0)),
        scratch_shapes=[pltpu.VMEM((N_HEADS, dk, dv + 128), F32),
                        pltpu.VMEM((N_HEADS, 1, 1), F32)],
        compiler_params=_cparams(2, 40),
        name="mlstm",
    )(qkvo, qkvo, qkvo, qkvo, if_col, if_row, mh_g2d)


def _merge_kernel(a_ref, b_ref, wa_ref, wb_ref, ga_ref, gb_ref, o_ref, wabf_ref, wbbf_ref):
    @pl.when(pl.program_id(1) == 0)
    def _():
        wabf_ref[...] = wa_ref[...].astype(BF16)
        wbbf_ref[...] = wb_ref[...].astype(BF16)

    ya = jnp.dot(a_ref[...], wabf_ref[...], preferred_element_type=F32)
    yb = jnp.dot(b_ref[...], wbbf_ref[...], preferred_element_type=F32)
    u = ga_ref[...].astype(F32) * ya + gb_ref[...].astype(F32) * yb
    o_ref[...] = u.astype(o_ref.dtype)


def _merge(ya_pre, yb_pre, w_a, w_b, gates, *, tm=1024, tn=512):
    T, K = ya_pre.shape
    D = w_a.shape[1]
    tm = min(tm, T)
    nj = D // tn
    return pl.pallas_call(
        _merge_kernel,
        out_shape=jax.ShapeDtypeStruct((T, D), BF16),
        grid=(nj, T // tm),
        in_specs=[pl.BlockSpec((tm, K), lambda j, i: (i, 0)),
                  pl.BlockSpec((tm, K), lambda j, i: (i, 0)),
                  pl.BlockSpec((K, tn), lambda j, i: (0, j)),
                  pl.BlockSpec((K, tn), lambda j, i: (0, j)),
                  pl.BlockSpec((tm, tn), lambda j, i: (i, j)),
                  pl.BlockSpec((tm, tn), lambda j, i: (i, nj + j))],
        out_specs=pl.BlockSpec((tm, tn), lambda j, i: (i, j)),
        scratch_shapes=[pltpu.VMEM((K, tn), BF16), pltpu.VMEM((K, tn), BF16)],
        compiler_params=_cparams(2, 56),
        name="merge",
    )(ya_pre, yb_pre, w_a, w_b, gates, gates)


def _mix_kernel(u_ref, w_ref, h_ref, o_ref, wbf_ref, *, alpha):
    @pl.when(pl.program_id(1) == 0)
    def _():
        wbf_ref[...] = w_ref[...].astype(BF16)

    o_ref[...] = alpha * h_ref[...] + jnp.dot(u_ref[...], wbf_ref[...], preferred_element_type=F32)


def _mix(u, w, h0, alpha, *, tm=1024, tn=512):
    T, D = h0.shape
    tm = min(tm, T)
    return pl.pallas_call(
        functools.partial(_mix_kernel, alpha=alpha),
        out_shape=jax.ShapeDtypeStruct((T, D), F32),
        grid=(D // tn, T // tm),
        in_specs=[pl.BlockSpec((tm, D), lambda j, i: (i, 0)),
                  pl.BlockSpec((D, tn), lambda j, i: (0, j)),
                  pl.BlockSpec((tm, tn), lambda j, i: (i, j))],
        out_specs=pl.BlockSpec((tm, tn), lambda j, i: (i, j)),
        scratch_shapes=[pltpu.VMEM((D, tn), BF16)],
        compiler_params=_cparams(2, 56),
        name="mix_out",
    )(u, w, h0)


def _ln1_router_kernel(pre_ref, g_ref, b_ref, wr_ref, rb_ref, hf_ref, hb_ref, hp_ref,
                       idx_ref, rank_ref, wcol_ref, cnt_ref, carry_ref):
    i = pl.program_id(0)
    E = wr_ref.shape[0]
    tm, D = pre_ref.shape
    G, M = N_GROUPS, E // N_GROUPS

    @pl.when(i == 0)
    def _():
        carry_ref[...] = jnp.zeros_like(carry_ref)

    h = _ln_rows(pre_ref[...], g_ref[...], b_ref[...])
    hf_ref[...] = h
    hb_ref[...] = h.astype(BF16)
    hp_ref[...] = _pack_halves(h[:, :D // 2], h[:, D // 2:])

    logits = lax.dot_general(wr_ref[...], h, (((1,), (1,)), ((), ())),
                             preferred_element_type=F32, precision=lax.Precision.HIGHEST)
    scores = jax.nn.sigmoid(logits)
    scores3 = scores.reshape(G, M, tm)
    sel3 = (scores + rb_ref[...]).reshape(G, M, tm)
    midx = lax.broadcasted_iota(I32, (G, M, tm), 1)
    gidx3 = lax.broadcasted_iota(I32, (G, M, tm), 0)
    eidx = gidx3 * M + midx
    gidx = lax.broadcasted_iota(I32, (G, 1, tm), 0)
    neg_inf = -jnp.inf

    top1 = jnp.max(sel3, axis=1, keepdims=True)
    first1 = jnp.min(jnp.where(sel3 == top1, midx, M), axis=1, keepdims=True)
    top2 = jnp.max(jnp.where(midx == first1, neg_inf, sel3), axis=1, keepdims=True)
    gs = top1 + top2
    gkeep = jnp.zeros((G, 1, tm), F32)
    for _ in range(TOP_GROUPS):
        mx = jnp.max(gs, axis=0, keepdims=True)
        first = jnp.min(jnp.where(gs == mx, gidx, G), axis=0, keepdims=True)
        hit = gidx == first
        gkeep = jnp.where(hit, 1.0, gkeep)
        gs = jnp.where(hit, neg_inf, gs)
    selm = jnp.where(gkeep > 0.5, sel3, neg_inf)

    idx_rows, sc_rows = [], []
    chosen = jnp.zeros((G, M, tm), F32)
    for _ in range(TOP_K):
        mx = jnp.max(jnp.max(selm, axis=1, keepdims=True), axis=0, keepdims=True)
        first = jnp.min(jnp.min(jnp.where(selm == mx, eidx, E), axis=1, keepdims=True),
                        axis=0, keepdims=True)
        hit = eidx == first
        sc = jnp.sum(jnp.sum(jnp.where(hit, scores3, 0.0), axis=1, keepdims=True),
                     axis=0, keepdims=True)
        chosen = jnp.where(hit, 1.0, chosen)
        selm = jnp.where(hit, neg_inf, selm)
        idx_rows.append(first)
        sc_rows.append(sc)
    denom = sc_rows[0]
    for s in sc_rows[1:]:
        denom = denom + s

    tr = lax.broadcasted_iota(I32, (tm, tm), 0)
    tc = lax.broadcasted_iota(I32, (tm, tm), 1)
    before = (tr < tc).astype(BF16)
    chosen2 = chosen.reshape(E, tm)
    rank2 = jnp.dot(chosen2.astype(BF16), before, preferred_element_type=F32) + carry_ref[...]
    rank3 = rank2.reshape(G, M, tm)
    w_rows = []
    for k in range(TOP_K):
        hit = eidx == idx_rows[k]
        rk = jnp.sum(jnp.sum(jnp.where(hit, rank3, 0.0), axis=1, keepdims=True),
                     axis=0, keepdims=True)
        idx_ref[k:k + 1, :] = idx_rows[k].reshape(1, tm)
        rank_ref[k:k + 1, :] = rk.reshape(1, tm).astype(I32)
        w_rows.append((sc_rows[k] / denom * ROUTE_SCALE).reshape(1, tm))
    wcol_ref[...] = jnp.concatenate(w_rows, axis=0).T
    carry_ref[...] = carry_ref[...] + jnp.sum(chosen2, axis=1, keepdims=True)

    @pl.when(i == pl.num_programs(0) - 1)
    def _():
        cnt_ref[...] = carry_ref[...]


def _ln1_router(pre, g, b, w_router, router_bias, tm):
    T, D = pre.shape
    E = w_router.shape[1]
    nt = T // tm
    row = lambda i: (i, 0)
    fixed = lambda i: (0, 0)
    return pl.pallas_call(
        _ln1_router_kernel,
        out_shape=(jax.ShapeDtypeStruct((T, D), F32), jax.ShapeDtypeStruct((T, D), BF16),
                   jax.ShapeDtypeStruct((T, D // 2), U32),
                   jax.ShapeDtypeStruct((nt, TOP_K, tm), I32), jax.ShapeDtypeStruct((nt, TOP_K, tm), I32),
                   jax.ShapeDtypeStruct((T, TOP_K), F32), jax.ShapeDtypeStruct((E, 1), F32)),
        grid=(nt,),
        in_specs=[pl.BlockSpec((tm, D), row), pl.BlockSpec((1, D), fixed), pl.BlockSpec((1, D), fixed),
                  pl.BlockSpec((E, D), fixed), pl.BlockSpec((E, 1), fixed)],
        out_specs=(pl.BlockSpec((tm, D), row), pl.BlockSpec((tm, D), row), pl.BlockSpec((tm, D // 2), row),
                   pl.BlockSpec((None, TOP_K, tm), lambda i: (i, 0, 0)),
                   pl.BlockSpec((None, TOP_K, tm), lambda i: (i, 0, 0)),
                   pl.BlockSpec((tm, TOP_K), row),
                   pl.BlockSpec((E, 1), fixed)),
        scratch_shapes=[pltpu.VMEM((E, 1), F32)],
        compiler_params=_cparams(1, 40),
        name="ln1_router",
    )(pre, g.reshape(1, D), b.reshape(1, D), w_router.T, router_bias.reshape(E, 1))


def _glu_block(x, wbf_ref):
    half = x.shape[1]
    f = wbf_ref.shape[1] // 2
    lo, hi = _unpack_halves(x)
    gu = (jnp.dot(lo.astype(BF16), wbf_ref[:half], preferred_element_type=F32)
          + jnp.dot(hi.astype(BF16), wbf_ref[half:], preferred_element_type=F32))
    g = gu[:, :f]
    return (g * _sigmoid(g) * gu[:, f:]).astype(BF16)


def _down_block(h, wbf_ref):
    y = jnp.dot(h, wbf_ref[...], preferred_element_type=F32)
    half = y.shape[1] // 2
    return _pack_halves(y[:, :half], y[:, half:])


def _expert_kernel(be_ref, nxt_ref, nused_ref, x_ref, w_hbm, o_ref, wbf_ref, stage_ref, wsem, *, block_fn):
    step = pl.program_id(0)
    nb = be_ref.shape[0]
    n_used = nused_ref[0]
    sub = x_ref.shape[0] // MOE_BLOCK

    def fetch(e_):
        return pltpu.make_async_copy(w_hbm.at[e_], stage_ref, wsem)

    @pl.when(step == 0)
    def _():
        fetch(be_ref[0]).start()

    for s in range(sub):
        b = step * sub + s
        rows = slice(s * MOE_BLOCK, (s + 1) * MOE_BLOCK)
        e = be_ref[b]
        run_start = jnp.logical_or(b == 0, e != be_ref[jnp.maximum(b - 1, 0)])

        @pl.when(jnp.logical_and(run_start, b < n_used))
        def _():
            fetch(e).wait()
            wbf_ref[...] = stage_ref[...].astype(BF16)
            b_next = nxt_ref[b]

            @pl.when(b_next < n_used)
            def _():
                fetch(be_ref[jnp.minimum(b_next, nb - 1)]).start()

        @pl.when(b < n_used)
        def _():
            o_ref[rows, :] = block_fn(x_ref[rows, :], wbf_ref)

        @pl.when(b >= n_used)
        def _():
            o_ref[rows, :] = jnp.zeros((MOE_BLOCK, o_ref.shape[1]), o_ref.dtype)


def _expert_matmul(x, w_e, blk_e, nxt_blk, n_used, block_fn, n_out, out_dtype, name):
    P, kx = x.shape
    E, kw, nw = w_e.shape
    nb = P // MOE_BLOCK
    assert nb % EXPERT_SUB == 0
    rows = EXPERT_SUB * MOE_BLOCK
    xmap = lambda s, be, nx, nu: (jnp.minimum(s, (nu[0] - 1) // EXPERT_SUB), 0)
    return pl.pallas_call(
        functools.partial(_expert_kernel, block_fn=block_fn),
        out_shape=jax.ShapeDtypeStruct((P, n_out), out_dtype),
        grid_spec=pltpu.PrefetchScalarGridSpec(
            num_scalar_prefetch=3,
            grid=(nb // EXPERT_SUB,),
            in_specs=[pl.BlockSpec((rows, kx), xmap), pl.BlockSpec(memory_space=pl.ANY)],
            out_specs=pl.BlockSpec((rows, n_out), lambda s, be, nx, nu: (s, 0)),
            scratch_shapes=[pltpu.VMEM((kw, nw), BF16), pltpu.VMEM((kw, nw), F32),
                            pltpu.SemaphoreType.DMA]),
        compiler_params=_cparams(1, 56),
        name=name,
    )(blk_e, nxt_blk, n_used, x, w_e)


def _combine_kernel(dest_hbm, y_hbm, w_ref, pre_ref, g_ref, b_ref, o_ref, idx_smem, buf_ref, isem, gsem):
    i = pl.program_id(0)
    n = pl.num_programs(0)
    tc = o_ref.shape[0]
    half = buf_ref.shape[3]
    n_idx = TOP_K * tc

    def idx_copy(tile, slot):
        return pltpu.make_async_copy(dest_hbm.at[tile], idx_smem.at[pl.ds(slot * n_idx, n_idx)],
                                     isem.at[slot])

    def issue_tile(slot):
        base = slot * n_idx

        def body(r8, carry):
            for s in range(8):
                r = r8 * 8 + s
                for k in range(TOP_K):
                    d = idx_smem[base + k * tc + r]
                    pltpu.make_async_copy(y_hbm.at[pl.ds(d, 1)], buf_ref.at[slot, k, pl.ds(r, 1)],
                                          gsem.at[slot]).start(priority=k % 2)
            return carry
        lax.fori_loop(0, tc // 8, body, 0)

    slot = i % 2

    @pl.when(i == 0)
    def _():
        idx_copy(0, 0).start()
        idx_copy(0, 0).wait()
        issue_tile(0)

        @pl.when(n > 1)
        def _():
            idx_copy(1, 1).start()

    @pl.when(i + 1 < n)
    def _():
        idx_copy(i + 1, 1 - slot).wait()
        for sl in range(2):
            @pl.when(slot == 1 - sl)
            def _():
                issue_tile(sl)

    @pl.when(i + 2 < n)
    def _():
        idx_copy(i + 2, slot).start()

    for k in range(TOP_K):
        pltpu.make_async_copy(buf_ref.at[slot, k], buf_ref.at[slot, k], gsem.at[slot]).wait()

    w = w_ref[...]
    acc_lo = jnp.zeros((tc, half), F32)
    acc_hi = jnp.zeros((tc, half), F32)
    for k in range(TOP_K):
        lo, hi = _unpack_halves(buf_ref[slot, k])
        wk = w[:, k:k + 1]
        acc_lo = acc_lo + wk * lo
        acc_hi = acc_hi + wk * hi
    y_lo = pre_ref[:, :half] + acc_lo
    y_hi = pre_ref[:, half:] + acc_hi
    inv_d = 1.0 / (2 * half)
    mu = (jnp.sum(y_lo, axis=-1, keepdims=True) + jnp.sum(y_hi, axis=-1, keepdims=True)) * inv_d
    c_lo = y_lo - mu
    c_hi = y_hi - mu
    var = (jnp.sum(c_lo * c_lo, axis=-1, keepdims=True)
           + jnp.sum(c_hi * c_hi, axis=-1, keepdims=True)) * inv_d
    rstd = lax.rsqrt(var + LN_EPS)
    o_ref[:, :half] = c_lo * rstd * g_ref[:, :half] + b_ref[:, :half]
    o_ref[:, half:] = c_hi * rstd * g_ref[:, half:] + b_ref[:, half:]


def _combine_ln(y_packed, dest_tiles, w_col, pre, g, b):
    T, D = pre.shape
    half = y_packed.shape[1]
    nt, n_idx = dest_tiles.shape
    tc = n_idx // TOP_K
    return pl.pallas_call(
        _combine_kernel,
        out_shape=jax.ShapeDtypeStruct((T, D), F32),
        grid=(nt,),
        in_specs=[pl.BlockSpec(memory_space=pl.ANY), pl.BlockSpec(memory_space=pl.ANY),
                  pl.BlockSpec((tc, TOP_K), lambda i: (i, 0)),
                  pl.BlockSpec((tc, D), lambda i: (i, 0)),
                  pl.BlockSpec((1, D), lambda i: (0, 0)),
                  pl.BlockSpec((1, D), lambda i: (0, 0))],
        out_specs=pl.BlockSpec((tc, D), lambda i: (i, 0)),
        scratch_shapes=[pltpu.SMEM((2 * n_idx,), I32),
                        pltpu.VMEM((2, TOP_K, tc, half), U32),
                        pltpu.SemaphoreType.DMA((2,)), pltpu.SemaphoreType.DMA((2,))],
        compiler_params=_cparams(1, 48),
        name="combine_ln2",
    )(dest_tiles, y_packed, w_col, pre, g.reshape(1, D), b.reshape(1, D))


def _glu_up_kernel(h_ref, wg_ref, wu_ref, o_ref, wbf_ref):
    tn = wg_ref.shape[1]

    @pl.when(pl.program_id(1) == 0)
    def _():
        wbf_ref[:, :tn] = wg_ref[...].astype(BF16)
        wbf_ref[:, tn:] = wu_ref[...].astype(BF16)

    gu = jnp.dot(h_ref[...], wbf_ref[...], preferred_element_type=F32)
    g = gu[:, :tn]
    o_ref[...] = (g * _sigmoid(g) * gu[:, tn:]).astype(o_ref.dtype)


def _glu_up(h_bf, w_gu, *, tm=512, tn=384):
    T, D = h_bf.shape
    f = w_gu.shape[1] // 2
    tm = min(tm, T)
    nc = f // tn
    return pl.pallas_call(
        _glu_up_kernel,
        out_shape=jax.ShapeDtypeStruct((T, f), BF16),
        grid=(nc, T // tm),
        in_specs=[pl.BlockSpec((tm, D), lambda c, i: (i, 0)),
                  pl.BlockSpec((D, tn), lambda c, i: (0, c)),
                  pl.BlockSpec((D, tn), lambda c, i: (0, nc + c))],
        out_specs=pl.BlockSpec((tm, tn), lambda c, i: (i, c)),
        scratch_shapes=[pltpu.VMEM((D, 2 * tn), BF16)],
        compiler_params=_cparams(2, 56),
        name="shared_up",
    )(h_bf, w_gu, w_gu)


def _dense_tail_kernel(fill_ref, dest_hbm, x_ref, hb_ref, hj_ref, s_ref, p_ref, wd_ref, wg_ref, bg_ref,
                       wp_ref, o_ref, xs_hbm, wdbf_ref, wgbf_ref, wpbf_ref, idx_smem, zero_ref,
                       isem, ssem, zsem, *, alpha, k_per_step):
    j = pl.program_id(0)
    i = pl.program_id(1)
    ni = pl.num_programs(1)
    step = j * ni + i
    tm = x_ref.shape[0]
    n_idx = k_per_step * tm
    R = zero_ref.shape[0]

    def idx_copies(jj, ii, slot):
        return [pltpu.make_async_copy(dest_hbm.at[jj * k_per_step + kk, pl.ds(pl.multiple_of(ii * tm, 128), tm)],
                                      idx_smem.at[pl.ds(slot * n_idx + kk * tm, tm)], isem.at[slot])
                for kk in range(k_per_step)]

    @pl.when(step == 0)
    def _():
        for cp in idx_copies(0, 0, 0):
            cp.start()
        zero_ref[...] = jnp.zeros_like(zero_ref)

        def fill(f):
            return pltpu.make_async_copy(zero_ref, xs_hbm.at[pl.ds(fill_ref[f] * R, R)], zsem)

        def start(f, carry):
            @pl.when(fill_ref[f] >= 0)
            def _():
                fill(f).start()
            return carry
        lax.fori_loop(0, fill_ref.shape[0], start, 0)

        def wait(f, carry):
            @pl.when(fill_ref[f] >= 0)
            def _():
                fill(f).wait()
            return carry
        lax.fori_loop(0, fill_ref.shape[0], wait, 0)

    @pl.when(i == 0)
    def _():
        wdbf_ref[...] = wd_ref[...].astype(BF16)
        wgbf_ref[...] = wg_ref[...].astype(BF16)
        wpbf_ref[...] = wp_ref[...].astype(BF16)

    slot = step % 2
    for cp in idx_copies(j, i, slot):
        cp.wait()

    @pl.when(step + 1 < pl.num_programs(0) * ni)
    def _():
        nxt = step + 1
        for cp in idx_copies(nxt // ni, nxt % ni, 1 - slot):
            cp.start()

    base = slot * n_idx
    for kk in range(k_per_step):
        for r in range(tm):
            d = idx_smem[base + kk * tm + r]
            pltpu.make_async_copy(x_ref.at[pl.ds(r, 1)], xs_hbm.at[pl.ds(d, 1)], ssem).start(priority=r % 2)

    shared = jnp.dot(s_ref[...], wdbf_ref[...], preferred_element_type=F32)
    gate = _sigmoid(jnp.dot(hb_ref[...], wgbf_ref[...], preferred_element_type=F32) + bg_ref[...])
    proj = jnp.dot(p_ref[...].astype(BF16), wpbf_ref[...], preferred_element_type=F32)
    o_ref[...] = alpha * hj_ref[...] + shared + gate * proj

    for kk in range(k_per_step):
        pltpu.make_async_copy(x_ref, x_ref, ssem).wait()


def _dense_tail(h1_bf, h1, h1_packed, dest_kt, fill_blocks, n_rows, s_mid, p, w_down_s, w_gate, b_gate,
                w_proj, alpha, *, tm=512, tn=512):
    T, D = h1.shape
    f = s_mid.shape[1]
    dp = p.shape[1]
    W = h1_packed.shape[1]
    tm = min(tm, T)
    nj = D // tn
    assert TOP_K % nj == 0 and tm % 128 == 0
    k_per_step = TOP_K // nj
    row = lambda j, i, fb: (i, 0)
    col = lambda j, i, fb: (0, j)
    tile = lambda j, i, fb: (i, j)
    return pl.pallas_call(
        functools.partial(_dense_tail_kernel, alpha=alpha, k_per_step=k_per_step),
        out_shape=(jax.ShapeDtypeStruct((T, D), F32), jax.ShapeDtypeStruct((n_rows, W), h1_packed.dtype)),
        grid_spec=pltpu.PrefetchScalarGridSpec(
            num_scalar_prefetch=1,
            grid=(nj, T // tm),
            in_specs=[pl.BlockSpec(memory_space=pl.ANY),
                      pl.BlockSpec((tm, W), row),
                      pl.BlockSpec((tm, D), row),
                      pl.BlockSpec((tm, tn), tile),
                      pl.BlockSpec((tm, f), row),
                      pl.BlockSpec((tm, dp), row),
                      pl.BlockSpec((f, tn), col),
                      pl.BlockSpec((D, tn), col),
                      pl.BlockSpec((1, tn), col),
                      pl.BlockSpec((dp, tn), col)],
            out_specs=(pl.BlockSpec((tm, tn), tile), pl.BlockSpec(memory_space=pl.ANY)),
            scratch_shapes=[pltpu.VMEM((f, tn), BF16), pltpu.VMEM((D, tn), BF16), pltpu.VMEM((dp, tn), BF16),
                            pltpu.SMEM((2 * k_per_step * tm,), I32), pltpu.VMEM((MOE_BLOCK, W), h1_packed.dtype),
                            pltpu.SemaphoreType.DMA((2,)), pltpu.SemaphoreType.DMA, pltpu.SemaphoreType.DMA]),
        compiler_params=pltpu.CompilerParams(dimension_semantics=("arbitrary", "arbitrary"),
                                             vmem_limit_bytes=58 * 1024 * 1024, has_side_effects=True),
        name="dense_tail_dispatch",
    )(fill_blocks, dest_kt, h1_packed, h1_bf, h1, s_mid, p, w_down_s, w_gate, b_gate.reshape(1, D), w_proj)


def _dispatch_tables(idx, rank, counts):
    E = counts.shape[0]
    P = idx.size + E * MOE_BLOCK
    nb = P // MOE_BLOCK
    padded = (counts + MOE_BLOCK - 1) // MOE_BLOCK * MOE_BLOCK
    pend = jnp.cumsum(padded)
    pstart = pend - padded
    sel = idx[None] == jnp.arange(E, dtype=I32).reshape(E, 1, 1, 1)
    dest = rank + jnp.sum(jnp.where(sel, pstart.reshape(E, 1, 1, 1), 0), axis=0)
    blk_row = jnp.arange(nb, dtype=I32) * MOE_BLOCK
    blk_e = jnp.minimum(jnp.sum(pend[None, :] <= blk_row[:, None], axis=1), E - 1).astype(I32)
    n_used = pend[-1] // MOE_BLOCK
    last_blk = jnp.where(counts > 0, pend // MOE_BLOCK - 1, -1)
    tail_blk = n_used + jnp.arange(E, dtype=I32)
    tail_blk = jnp.where(tail_blk < nb, tail_blk, -1)
    fill = jnp.concatenate([last_blk, tail_blk]).astype(I32)
    nxt_blk = jnp.take(pend // MOE_BLOCK, blk_e).astype(I32)
    return dest.astype(I32), blk_e, nxt_blk, n_used.astype(I32).reshape(1), fill, P


def _layer(h0_f, h0_bf, p_l, w_in, b_in, conv_w, conv_b, mh_norm_g, w_conv_out, w_mlstm_out,
           w_mix_out, ln1_g, ln1_b, w_router, router_bias, w_gu_e, w_down_e, w_gu_s, w_down_s,
           w_ple_gate, b_ple_gate, w_ple_proj, ln2_g, ln2_b, alpha, batch, seq):
    T, D = h0_f.shape
    d_conv = conv_w.shape[1]
    d_v = mh_norm_g.shape[0]
    d_qk = d_v // 2
    n_if = 2 * N_HEADS
    c_qk = 3 * d_conv
    c_if = c_qk + 2 * d_qk + 2 * d_v
    c_gate = c_if + n_if
    b2d = b_in.reshape(1, -1)

    w_t = jnp.swapaxes(w_in, 0, 1)
    ya_pre = _conv_branch(h0_bf, w_t, b2d, conv_w, conv_b.reshape(1, -1), seq, d_conv)
    qkvo = _proj(h0_bf, w_t, b2d, c_qk, 2 * d_qk + 2 * d_v)
    if_col, if_row = _if_gates(h0_bf, w_t[c_if:c_gate], b_in[c_if:c_gate])
    gates = _proj(h0_bf, w_t, b2d, c_gate, 2 * D, act="sigmoid")
    yb_pre = _mlstm(qkvo, if_col, if_row, mh_norm_g.reshape(1, -1), batch, seq, d_qk, d_v)
    u = _merge(ya_pre, yb_pre, w_conv_out, w_mlstm_out, gates)
    pre1 = _mix(u, w_mix_out, h0_f, alpha)

    rt = min(ROUTE_TILE, T)
    h1, h1_bf, h1_packed, idx, rank, w_col, counts = _ln1_router(pre1, ln1_g, ln1_b, w_router,
                                                                 router_bias, rt)
    dest, blk_e, nxt_blk, n_used, fill, n_rows = _dispatch_tables(idx, rank, counts[:, 0].astype(I32))
    dest_tiles = dest.reshape(T // rt, TOP_K * rt)
    dest_kt = dest.transpose(1, 0, 2).reshape(TOP_K, T)

    s_mid = _glu_up(h1_bf, w_gu_s)
    pre2, xs = _dense_tail(h1_bf, h1, h1_packed, dest_kt, fill, n_rows, s_mid, p_l, w_down_s,
                           w_ple_gate, b_ple_gate, w_ple_proj, alpha)

    hmid = _expert_matmul(xs, w_gu_e, blk_e, nxt_blk, n_used, _glu_block, w_gu_e.shape[2] // 2, BF16,
                          "expert_up")
    y_packed = _expert_matmul(hmid, w_down_e, blk_e, nxt_blk, n_used, _down_block, D // 2, U32,
                              "expert_down")
    return _combine_ln(y_packed, dest_tiles, w_col, pre2, ln2_g, ln2_b)


def kernel(x, p, ln_in_g, ln_in_b, w_in, b_in, conv_w, conv_b, mh_norm_g, w_conv_out, w_mlstm_out,
           w_mix_out, ln1_g, ln1_b, w_router, router_bias, w_gu_e, w_down_e, w_gu_s, w_down_s,
           w_ple_gate, b_ple_gate, w_ple_proj, ln2_g, ln2_b):
    B, S, D = x.shape
    depth = w_in.shape[0]
    alpha = (2 * depth) ** 0.25
    T = B * S
    h_f, h_bf = _ln_in(x.reshape(T, D), ln_in_g, ln_in_b)
    for l in range(depth):
        h_f = _layer(h_f, h_bf, p[l].reshape(T, -1), w_in[l], b_in[l], conv_w[l], conv_b[l],
                     mh_norm_g[l], w_conv_out[l], w_mlstm_out[l], w_mix_out[l], ln1_g[l], ln1_b[l],
                     w_router[l], router_bias[l], w_gu_e[l], w_down_e[l], w_gu_s[l], w_down_s[l],
                     w_ple_gate[l], b_ple_gate[l], w_ple_proj[l], ln2_g[l], ln2_b[l], alpha, B, S)
        if l + 1 < depth:
            h_bf = h_f.astype(BF16)
    return h_f.reshape(B, S, D)
```

```python
import functools

import jax
import jax.numpy as jnp
from jax import lax
from jax.experimental import pallas as pl
from jax.experimental.pallas import tpu as pltpu

F32 = jnp.float32
BF16 = jnp.bfloat16
U32 = jnp.uint32
I32 = jnp.int32

N_HEADS = 8
TOP_K = 8
N_GROUPS = 8
TOP_GROUPS = 4
ROUTE_SCALE = 2.5
MOE_BLOCK = 256
LN_EPS = 1e-5
CONV_W = 3
MLSTM_CHUNK = 128
MLSTM_STEP = 128
ROUTE_TILE = 256
EXPERT_UP_SUB = 2
EXPERT_DOWN_SUB = 4
NEG_BIG = -1e30
HI_MASK = 0xFFFF0000
V7X_VMEM_BYTES = 64 * 1024 * 1024


def _cparams(n_axes, vmem_mib):
    assert vmem_mib * 1024 * 1024 <= V7X_VMEM_BYTES
    return pltpu.CompilerParams(dimension_semantics=("arbitrary",) * n_axes,
                                vmem_limit_bytes=vmem_mib * 1024 * 1024)


def _ln_rows(x, g, b):
    mu = jnp.mean(x, axis=-1, keepdims=True)
    xc = x - mu
    var = jnp.mean(xc * xc, axis=-1, keepdims=True)
    return xc * lax.rsqrt(var + LN_EPS) * g + b


def _pack_halves(lo, hi):
    lo = pltpu.bitcast(lo.astype(BF16).astype(F32), U32)
    hi = pltpu.bitcast(hi.astype(BF16).astype(F32), U32)
    return (hi & jnp.uint32(HI_MASK)) | (lo >> jnp.uint32(16))


def _sigmoid(x):
    return 0.5 * jnp.tanh(0.5 * x) + 0.5


def _unpack_halves(w):
    lo = pltpu.bitcast(w << jnp.uint32(16), F32)
    hi = pltpu.bitcast(w & jnp.uint32(HI_MASK), F32)
    return lo, hi


def _ln_in_kernel(x_ref, g_ref, b_ref, of_ref, ob_ref):
    y = _ln_rows(x_ref[...], g_ref[...], b_ref[...])
    of_ref[...] = y
    ob_ref[...] = y.astype(BF16)


def _ln_in(x2, g, b, tm=256):
    T, D = x2.shape
    return pl.pallas_call(
        _ln_in_kernel,
        out_shape=(jax.ShapeDtypeStruct((T, D), F32), jax.ShapeDtypeStruct((T, D), BF16)),
        grid=(T // tm,),
        in_specs=[pl.BlockSpec((tm, D), lambda i: (i, 0)),
                  pl.BlockSpec((1, D), lambda i: (0, 0)),
                  pl.BlockSpec((1, D), lambda i: (0, 0))],
        out_specs=(pl.BlockSpec((tm, D), lambda i: (i, 0)),
                   pl.BlockSpec((tm, D), lambda i: (i, 0))),
        compiler_params=_cparams(1, 40),
        name="ln_in",
    )(x2, g.reshape(1, D), b.reshape(1, D))


_NT = (((1,), (1,)), ((), ()))


def _proj_kernel(x_ref, w_ref, b_ref, o_ref, wbf_ref, *, act):
    @pl.when(pl.program_id(1) == 0)
    def _():
        wbf_ref[...] = w_ref[...].astype(BF16)

    acc = lax.dot_general(x_ref[...], wbf_ref[...], _NT, preferred_element_type=F32) + b_ref[...]
    if act == "sigmoid":
        acc = _sigmoid(acc)
    o_ref[...] = acc.astype(o_ref.dtype)


def _proj(x, w_t, bias2d, row0, n_cols, *, act=None, tm=1024, tn=512, out_dtype=BF16):
    T, K = x.shape
    tm = min(tm, T)
    assert n_cols % tn == 0 and T % tm == 0 and row0 % 8 == 0
    if row0 % tn == 0:
        jb = row0 // tn
        w_spec = pl.BlockSpec((tn, K), lambda j, i: (jb + j, 0))
        b_spec = pl.BlockSpec((1, tn), lambda j, i: (0, jb + j))
    else:
        w_spec = pl.BlockSpec((pl.Element(tn), pl.Element(K)),
                              lambda j, i: (pl.multiple_of(row0 + j * tn, 8), 0))
        bias2d = bias2d[:, row0:row0 + n_cols]
        b_spec = pl.BlockSpec((1, tn), lambda j, i: (0, j))
    return pl.pallas_call(
        functools.partial(_proj_kernel, act=act),
        out_shape=jax.ShapeDtypeStruct((T, n_cols), out_dtype),
        grid=(n_cols // tn, T // tm),
        in_specs=[pl.BlockSpec((tm, K), lambda j, i: (i, 0)), w_spec, b_spec],
        out_specs=pl.BlockSpec((tm, tn), lambda j, i: (i, j)),
        scratch_shapes=[pltpu.VMEM((tn, K), BF16)],
        compiler_params=_cparams(2, 56),
        name="proj_" + (act or "lin"),
    )(x, w_t, bias2d)


def _conv_kernel(x_ref, wh_ref, wc_ref, wb_ref, bh_ref, bc_ref, bb_ref, cw_ref, cb_ref,
                 o_ref, wbf_ref, zprev_ref, *, tiles_per_seq):
    i = pl.program_id(1)

    @pl.when(i == 0)
    def _():
        wbf_ref[0] = wh_ref[...].astype(BF16)
        wbf_ref[1] = wc_ref[...].astype(BF16)
        wbf_ref[2] = wb_ref[...].astype(BF16)

    @pl.when(i % tiles_per_seq == 0)
    def _():
        zprev_ref[...] = jnp.zeros_like(zprev_ref)

    x = x_ref[...]
    ha = lax.dot_general(x, wbf_ref[0], _NT, preferred_element_type=F32) + bh_ref[...]
    ca = lax.dot_general(x, wbf_ref[1], _NT, preferred_element_type=F32) + bc_ref[...]
    ba = lax.dot_general(x, wbf_ref[2], _NT, preferred_element_type=F32) + bb_ref[...]
    z = ca * ha
    tm = z.shape[0]
    prev = zprev_ref[...]
    row8 = lax.broadcasted_iota(I32, prev.shape, 0)
    z1 = pltpu.roll(z, 1, 0)
    z2 = pltpu.roll(z, 2, 0)
    p1 = pltpu.roll(prev, 1, 0)
    p2 = pltpu.roll(prev, 2, 0)
    z1 = jnp.concatenate([jnp.where(row8 < 1, p1, z1[:8]), z1[8:]], axis=0)
    z2 = jnp.concatenate([jnp.where(row8 < 2, p2, z2[:8]), z2[8:]], axis=0)
    cw = cw_ref[...]
    y = cw[0:1] * z2 + cw[1:2] * z1 + cw[2:3] * z + cb_ref[...]
    o_ref[...] = (ba * y).astype(o_ref.dtype)
    zprev_ref[...] = z[tm - 8:]


def _conv_branch(x, w_t, b2d, conv_w, conv_b2d, seq, d_conv, *, tm=1024, tn=256):
    T, K = x.shape
    tm = min(tm, seq)
    assert seq % tm == 0 and d_conv % tn == 0 and tm % 8 == 0
    nb = d_conv // tn
    wspec = lambda g: pl.BlockSpec((tn, K), lambda j, i: (g * nb + j, 0))
    bspec = lambda g: pl.BlockSpec((1, tn), lambda j, i: (0, g * nb + j))
    return pl.pallas_call(
        functools.partial(_conv_kernel, tiles_per_seq=seq // tm),
        out_shape=jax.ShapeDtypeStruct((T, d_conv), BF16),
        grid=(nb, T // tm),
        in_specs=[pl.BlockSpec((tm, K), lambda j, i: (i, 0)),
                  wspec(0), wspec(1), wspec(2), bspec(0), bspec(1), bspec(2),
                  pl.BlockSpec((CONV_W, tn), lambda j, i: (0, j)),
                  pl.BlockSpec((1, tn), lambda j, i: (0, j))],
        out_specs=pl.BlockSpec((tm, tn), lambda j, i: (i, j)),
        scratch_shapes=[pltpu.VMEM((3, tn, K), BF16), pltpu.VMEM((8, tn), F32)],
        compiler_params=_cparams(2, 56),
        name="conv_branch",
    )(x, w_t, w_t, w_t, b2d, b2d, b2d, conv_w, conv_b2d)


def _if_kernel(x_ref, wp_ref, wt_ref, bc_ref, br_ref, oc_ref, or_ref):
    x = x_ref[...]
    oc_ref[...] = lax.dot_general(x, wp_ref[...], _NT, preferred_element_type=F32) + bc_ref[...]
    or_ref[...] = lax.dot_general(wt_ref[...], x, _NT, preferred_element_type=F32) + br_ref[...]


def _if_gates(x, w_if_t, b_if, tm=512):
    T, K = x.shape
    tm = min(tm, T)
    n = w_if_t.shape[0]
    w_if_t = w_if_t.astype(BF16)
    w_pad = jnp.zeros((128, K), BF16).at[:n].set(w_if_t)
    b_pad = jnp.zeros((1, 128), F32).at[0, :n].set(b_if)
    return pl.pallas_call(
        _if_kernel,
        out_shape=(jax.ShapeDtypeStruct((T, 128), F32), jax.ShapeDtypeStruct((n, T), F32)),
        grid=(T // tm,),
        in_specs=[pl.BlockSpec((tm, K), lambda i: (i, 0)),
                  pl.BlockSpec((128, K), lambda i: (0, 0)),
                  pl.BlockSpec((n, K), lambda i: (0, 0)),
                  pl.BlockSpec((1, 128), lambda i: (0, 0)),
                  pl.BlockSpec((n, 1), lambda i: (0, 0))],
        out_specs=(pl.BlockSpec((tm, 128), lambda i: (i, 0)),
                   pl.BlockSpec((n, tm), lambda i: (0, i))),
        compiler_params=_cparams(1, 32),
        name="if_gates",
    )(x, w_pad, w_if_t, b_pad, b_if.reshape(n, 1))


def _mlstm_kernel(q_ref, k_ref, v_ref, o_ref, ifc_ref, ifr_ref, g_ref, y_ref, c_ref, m_ref,
                  *, dk, dv, L):
    H = N_HEADS

    @pl.when(pl.program_id(1) == 0)
    def _():
        c_ref[...] = jnp.zeros_like(c_ref)
        m_ref[...] = jnp.zeros_like(m_ref)

    scale = dk ** -0.5
    r = lax.broadcasted_iota(I32, (L, L), 0)
    c = lax.broadcasted_iota(I32, (L, L), 1)
    causal = r >= c
    tri_lower = causal.astype(F32)
    tri_upper = (r <= c).astype(F32)
    ones_col = (lax.broadcasted_iota(I32, (L, 128), 1) == 0).astype(BF16)

    for cc in range(q_ref.shape[0] // L):
        rows = slice(cc * L, (cc + 1) * L)
        ifc = ifc_ref[rows, :]
        ifr = ifr_ref[:, rows]
        ig_c = ifc[:, 0:H]
        lf_c = jax.nn.log_sigmoid(ifc[:, H:2 * H])
        ig_r = ifr[0:H, :]
        lf_r = jax.nn.log_sigmoid(ifr[H:2 * H, :])
        bcum_c = jnp.dot(tri_lower, lf_c, preferred_element_type=F32,
                         precision=lax.Precision.HIGHEST)
        bcum_r = jnp.dot(lf_r, tri_upper, preferred_element_type=F32,
                         precision=lax.Precision.HIGHEST)
        d_c = ig_c - bcum_c
        d_r = ig_r - bcum_r

        for h in range(H):
            q = q_ref[rows, h * dk:(h + 1) * dk]
            k = k_ref[rows, h * dk:(h + 1) * dk]
            v = v_ref[rows, h * dv:(h + 1) * dv]
            v_aug = jnp.concatenate([v, ones_col], axis=1)
            bc = bcum_c[:, h:h + 1]
            m_prev = m_ref[h]
            c_prev = c_ref[h]

            qk = lax.dot_general(q, k, _NT, preferred_element_type=F32) * scale
            logd = jnp.where(causal, bc + d_r[h:h + 1, :], NEG_BIG)
            log_inter = bc + m_prev
            m_t = jnp.maximum(log_inter, jnp.max(logd, axis=1, keepdims=True))
            w_inter = jnp.exp(log_inter - m_t)
            s_mat = (qk * jnp.exp(logd - m_t)).astype(BF16)
            tot = (w_inter * jnp.dot(q, c_prev.astype(BF16), preferred_element_type=F32)
                   + jnp.dot(s_mat, v_aug, preferred_element_type=F32))
            den = jnp.maximum(jnp.abs(tot[:, dv:dv + 1]), jnp.exp(-m_t))
            hh = tot[:, :dv] / den
            mu = jnp.mean(hh, axis=-1, keepdims=True)
            hc = hh - mu
            var = jnp.mean(hc * hc, axis=-1, keepdims=True)
            hn = hc * lax.rsqrt(var + LN_EPS) * g_ref[:, h * dv:(h + 1) * dv]
            og = _sigmoid(o_ref[rows, h * dv:(h + 1) * dv].astype(F32))
            y_ref[rows, h * dv:(h + 1) * dv] = (og * hn).astype(y_ref.dtype)

            g_tot = bc[L - 1:L, :]
            a = g_tot + d_c[:, h:h + 1]
            m_new = jnp.maximum(g_tot + m_prev, jnp.max(a, axis=0, keepdims=True))
            kw = (k.astype(F32) * (jnp.exp(a - m_new) * scale)).astype(BF16)
            kv = lax.dot_general(kw, v_aug, (((0,), (0,)), ((), ())), preferred_element_type=F32)
            c_ref[h] = jnp.exp(g_tot + m_prev - m_new) * c_prev + kv
            m_ref[h] = m_new


def _mlstm(qkvo, if_col, if_row, mh_g2d, batch, seq, d_qk, d_v, L=MLSTM_CHUNK, step_rows=MLSTM_STEP):
    T = qkvo.shape[0]
    L = min(L, seq)
    R = min(max(step_rows, L), seq)
    assert seq % R == 0 and R % L == 0 and d_v == 2 * d_qk
    nc = seq // R
    dk, dv = d_qk // N_HEADS, d_v // N_HEADS
    row = lambda b, c: b * nc + c
    return pl.pallas_call(
        functools.partial(_mlstm_kernel, dk=dk, dv=dv, L=L),
        out_shape=jax.ShapeDtypeStruct((T, d_v), BF16),
        grid=(batch, nc),
        in_specs=[pl.BlockSpec((R, d_qk), lambda b, c: (row(b, c), 0)),
                  pl.BlockSpec((R, d_qk), lambda b, c: (row(b, c), 1)),
                  pl.BlockSpec((R, d_v), lambda b, c: (row(b, c), 1)),
                  pl.BlockSpec((R, d_v), lambda b, c: (row(b, c), 2)),
                  pl.BlockSpec((R, 128), lambda b, c: (row(b, c), 0)),
                  pl.BlockSpec((2 * N_HEADS, R), lambda b, c: (0, row(b, c))),
                  pl.BlockSpec((1, d_v), lambda b, c: (0, 0))],
        out_specs=pl.BlockSpec((R, d_v), lambda b, c: (row(b, c), 0)),
        scratch_shapes=[pltpu.VMEM((N_HEADS, dk, dv + 128), F32),
                        pltpu.VMEM((N_HEADS, 1, 1), F32)],
        compiler_params=_cparams(2, 40),
        name="mlstm",
    )(qkvo, qkvo, qkvo, qkvo, if_col, if_row, mh_g2d)


def _merge_kernel(a_ref, b_ref, wa_ref, wb_ref, ga_ref, gb_ref, o_ref, wabf_ref, wbbf_ref):
    @pl.when(pl.program_id(1) == 0)
    def _():
        wabf_ref[...] = wa_ref[...].astype(BF16)
        wbbf_ref[...] = wb_ref[...].astype(BF16)

    ya = jnp.dot(a_ref[...], wabf_ref[...], preferred_element_type=F32)
    yb = jnp.dot(b_ref[...], wbbf_ref[...], preferred_element_type=F32)
    u = ga_ref[...].astype(F32) * ya + gb_ref[...].astype(F32) * yb
    o_ref[...] = u.astype(o_ref.dtype)


def _merge(ya_pre, yb_pre, w_a, w_b, gates, *, tm=1024, tn=512):
    T, K = ya_pre.shape
    D = w_a.shape[1]
    tm = min(tm, T)
    nj = D // tn
    return pl.pallas_call(
        _merge_kernel,
        out_shape=jax.ShapeDtypeStruct((T, D), BF16),
        grid=(nj, T // tm),
        in_specs=[pl.BlockSpec((tm, K), lambda j, i: (i, 0)),
                  pl.BlockSpec((tm, K), lambda j, i: (i, 0)),
                  pl.BlockSpec((K, tn), lambda j, i: (0, j)),
                  pl.BlockSpec((K, tn), lambda j, i: (0, j)),
                  pl.BlockSpec((tm, tn), lambda j, i: (i, j)),
                  pl.BlockSpec((tm, tn), lambda j, i: (i, nj + j))],
        out_specs=pl.BlockSpec((tm, tn), lambda j, i: (i, j)),
        scratch_shapes=[pltpu.VMEM((K, tn), BF16), pltpu.VMEM((K, tn), BF16)],
        compiler_params=_cparams(2, 56),
        name="merge",
    )(ya_pre, yb_pre, w_a, w_b, gates, gates)


def _mix_kernel(u_ref, w_ref, h_ref, o_ref, wbf_ref, *, alpha):
    @pl.when(pl.program_id(1) == 0)
    def _():
        wbf_ref[...] = w_ref[...].astype(BF16)

    o_ref[...] = alpha * h_ref[...] + jnp.dot(u_ref[...], wbf_ref[...], preferred_element_type=F32)


def _mix(u, w, h0, alpha, *, tm=1024, tn=512):
    T, D = h0.shape
    tm = min(tm, T)
    return pl.pallas_call(
        functools.partial(_mix_kernel, alpha=alpha),
        out_shape=jax.ShapeDtypeStruct((T, D), F32),
        grid=(D // tn, T // tm),
        in_specs=[pl.BlockSpec((tm, D), lambda j, i: (i, 0)),
                  pl.BlockSpec((D, tn), lambda j, i: (0, j)),
                  pl.BlockSpec((tm, tn), lambda j, i: (i, j))],
        out_specs=pl.BlockSpec((tm, tn), lambda j, i: (i, j)),
        scratch_shapes=[pltpu.VMEM((D, tn), BF16)],
        compiler_params=_cparams(2, 56),
        name="mix_out",
    )(u, w, h0)


def _ln1_router_kernel(pre_ref, g_ref, b_ref, wr_ref, rb_ref, hf_ref, hb_ref, hp_ref,
                       idx_ref, rank_ref, wcol_ref, cnt_ref, carry_ref):
    i = pl.program_id(0)
    E = wr_ref.shape[0]
    tm, D = pre_ref.shape
    G, M = N_GROUPS, E // N_GROUPS

    @pl.when(i == 0)
    def _():
        carry_ref[...] = jnp.zeros_like(carry_ref)

    h = _ln_rows(pre_ref[...], g_ref[...], b_ref[...])
    hf_ref[...] = h
    hb_ref[...] = h.astype(BF16)
    hp_ref[...] = _pack_halves(h[:, :D // 2], h[:, D // 2:])

    logits = lax.dot_general(wr_ref[...], h, (((1,), (1,)), ((), ())),
                             preferred_element_type=F32, precision=lax.Precision.HIGHEST)
    scores = jax.nn.sigmoid(logits)
    scores3 = scores.reshape(G, M, tm)
    sel3 = (scores + rb_ref[...]).reshape(G, M, tm)
    midx = lax.broadcasted_iota(I32, (G, M, tm), 1)
    gidx3 = lax.broadcasted_iota(I32, (G, M, tm), 0)
    eidx = gidx3 * M + midx
    gidx = lax.broadcasted_iota(I32, (G, 1, tm), 0)
    neg_inf = -jnp.inf

    top1 = jnp.max(sel3, axis=1, keepdims=True)
    first1 = jnp.min(jnp.where(sel3 == top1, midx, M), axis=1, keepdims=True)
    top2 = jnp.max(jnp.where(midx == first1, neg_inf, sel3), axis=1, keepdims=True)
    gs = top1 + top2
    gkeep = jnp.zeros((G, 1, tm), F32)
    for _ in range(TOP_GROUPS):
        mx = jnp.max(gs, axis=0, keepdims=True)
        first = jnp.min(jnp.where(gs == mx, gidx, G), axis=0, keepdims=True)
        hit = gidx == first
        gkeep = jnp.where(hit, 1.0, gkeep)
        gs = jnp.where(hit, neg_inf, gs)
    selm = jnp.where(gkeep > 0.5, sel3, neg_inf)

    idx_rows, sc_rows = [], []
    chosen = jnp.zeros((G, M, tm), F32)
    for _ in range(TOP_K):
        mx = jnp.max(jnp.max(selm, axis=1, keepdims=True), axis=0, keepdims=True)
        first = jnp.min(jnp.min(jnp.where(selm == mx, eidx, E), axis=1, keepdims=True),
                        axis=0, keepdims=True)
        hit = eidx == first
        sc = jnp.sum(jnp.sum(jnp.where(hit, scores3, 0.0), axis=1, keepdims=True),
                     axis=0, keepdims=True)
        chosen = jnp.where(hit, 1.0, chosen)
        selm = jnp.where(hit, neg_inf, selm)
        idx_rows.append(first)
        sc_rows.append(sc)
    denom = sc_rows[0]
    for s in sc_rows[1:]:
        denom = denom + s

    tr = lax.broadcasted_iota(I32, (tm, tm), 0)
    tc = lax.broadcasted_iota(I32, (tm, tm), 1)
    before = (tr < tc).astype(BF16)
    chosen2 = chosen.reshape(E, tm)
    rank2 = jnp.dot(chosen2.astype(BF16), before, preferred_element_type=F32) + carry_ref[...]
    rank3 = rank2.reshape(G, M, tm)
    w_rows = []
    for k in range(TOP_K):
        hit = eidx == idx_rows[k]
        rk = jnp.sum(jnp.sum(jnp.where(hit, rank3, 0.0), axis=1, keepdims=True),
                     axis=0, keepdims=True)
        idx_ref[k:k + 1, :] = idx_rows[k].reshape(1, tm)
        rank_ref[k:k + 1, :] = rk.reshape(1, tm).astype(I32)
        w_rows.append((sc_rows[k] / denom * ROUTE_SCALE).reshape(1, tm))
    wcol_ref[...] = jnp.concatenate(w_rows, axis=0).T
    carry_ref[...] = carry_ref[...] + jnp.sum(chosen2, axis=1, keepdims=True)

    @pl.when(i == pl.num_programs(0) - 1)
    def _():
        cnt_ref[...] = carry_ref[...]


def _ln1_router(pre, g, b, w_router, router_bias, tm):
    T, D = pre.shape
    E = w_router.shape[1]
    nt = T // tm
    row = lambda i: (i, 0)
    fixed = lambda i: (0, 0)
    return pl.pallas_call(
        _ln1_router_kernel,
        out_shape=(jax.ShapeDtypeStruct((T, D), F32), jax.ShapeDtypeStruct((T, D), BF16),
                   jax.ShapeDtypeStruct((T, D // 2), U32),
                   jax.ShapeDtypeStruct((nt, TOP_K, tm), I32), jax.ShapeDtypeStruct((nt, TOP_K, tm), I32),
                   jax.ShapeDtypeStruct((T, TOP_K), F32), jax.ShapeDtypeStruct((E, 1), F32)),
        grid=(nt,),
        in_specs=[pl.BlockSpec((tm, D), row), pl.BlockSpec((1, D), fixed), pl.BlockSpec((1, D), fixed),
                  pl.BlockSpec((E, D), fixed), pl.BlockSpec((E, 1), fixed)],
        out_specs=(pl.BlockSpec((tm, D), row), pl.BlockSpec((tm, D), row), pl.BlockSpec((tm, D // 2), row),
                   pl.BlockSpec((None, TOP_K, tm), lambda i: (i, 0, 0)),
                   pl.BlockSpec((None, TOP_K, tm), lambda i: (i, 0, 0)),
                   pl.BlockSpec((tm, TOP_K), row),
                   pl.BlockSpec((E, 1), fixed)),
        scratch_shapes=[pltpu.VMEM((E, 1), F32)],
        compiler_params=_cparams(1, 40),
        name="ln1_router",
    )(pre, g.reshape(1, D), b.reshape(1, D), w_router.T, router_bias.reshape(E, 1))


def _glu_block(x, wbf_ref):
    half = x.shape[1]
    f = wbf_ref.shape[1] // 2
    lo, hi = _unpack_halves(x)
    gu = (jnp.dot(lo.astype(BF16), wbf_ref[:half], preferred_element_type=F32)
          + jnp.dot(hi.astype(BF16), wbf_ref[half:], preferred_element_type=F32))
    g = gu[:, :f]
    return (g * _sigmoid(g) * gu[:, f:]).astype(BF16)


def _down_block(h, wbf_ref):
    y = jnp.dot(h, wbf_ref[...], preferred_element_type=F32)
    half = y.shape[1] // 2
    return _pack_halves(y[:, :half], y[:, half:])


def _expert_kernel(be_ref, nxt_ref, nused_ref, x_ref, w_hbm, o_ref, wbf_ref, stage_ref, wsem, *, block_fn):
    step = pl.program_id(0)
    nb = be_ref.shape[0]
    n_used = nused_ref[0]
    sub = x_ref.shape[0] // MOE_BLOCK

    def fetch(e_):
        return pltpu.make_async_copy(w_hbm.at[e_], stage_ref, wsem)

    @pl.when(step == 0)
    def _():
        fetch(be_ref[0]).start()

    for s in range(sub):
        b = step * sub + s
        rows = slice(s * MOE_BLOCK, (s + 1) * MOE_BLOCK)
        e = be_ref[b]
        run_start = jnp.logical_or(b == 0, e != be_ref[jnp.maximum(b - 1, 0)])

        @pl.when(jnp.logical_and(run_start, b < n_used))
        def _():
            fetch(e).wait()
            wbf_ref[...] = stage_ref[...].astype(BF16)
            b_next = nxt_ref[b]

            @pl.when(b_next < n_used)
            def _():
                fetch(be_ref[jnp.minimum(b_next, nb - 1)]).start()

        @pl.when(b < n_used)
        def _():
            o_ref[rows, :] = block_fn(x_ref[rows, :], wbf_ref)

        @pl.when(b >= n_used)
        def _():
            o_ref[rows, :] = jnp.zeros((MOE_BLOCK, o_ref.shape[1]), o_ref.dtype)


def _expert_matmul(x, w_e, blk_e, nxt_blk, n_used, block_fn, n_out, out_dtype, name, sub):
    P, kx = x.shape
    E, kw, nw = w_e.shape
    nb = P // MOE_BLOCK
    assert nb % sub == 0
    rows = sub * MOE_BLOCK
    xmap = lambda s, be, nx, nu: (jnp.minimum(s, (nu[0] - 1) // sub), 0)
    return pl.pallas_call(
        functools.partial(_expert_kernel, block_fn=block_fn),
        out_shape=jax.ShapeDtypeStruct((P, n_out), out_dtype),
        grid_spec=pltpu.PrefetchScalarGridSpec(
            num_scalar_prefetch=3,
            grid=(nb // sub,),
            in_specs=[pl.BlockSpec((rows, kx), xmap), pl.BlockSpec(memory_space=pl.ANY)],
            out_specs=pl.BlockSpec((rows, n_out), lambda s, be, nx, nu: (s, 0)),
            scratch_shapes=[pltpu.VMEM((kw, nw), BF16), pltpu.VMEM((kw, nw), F32),
                            pltpu.SemaphoreType.DMA]),
        compiler_params=_cparams(1, 56),
        name=name,
    )(blk_e, nxt_blk, n_used, x, w_e)


def _combine_kernel(dest_hbm, y_hbm, w_ref, pre_ref, g_ref, b_ref, o_ref, idx_smem, buf_ref, isem, gsem):
    i = pl.program_id(0)
    n = pl.num_programs(0)
    tc = o_ref.shape[0]
    half = buf_ref.shape[3]
    n_idx = TOP_K * tc

    def idx_copy(tile, slot):
        return pltpu.make_async_copy(dest_hbm.at[tile], idx_smem.at[pl.ds(slot * n_idx, n_idx)],
                                     isem.at[slot])

    def issue_tile(slot):
        base = slot * n_idx

        def body(r8, carry):
            for s in range(8):
                r = r8 * 8 + s
                for k in range(TOP_K):
                    d = idx_smem[base + k * tc + r]
                    pltpu.make_async_copy(y_hbm.at[pl.ds(d, 1)], buf_ref.at[slot, k, pl.ds(r, 1)],
                                          gsem.at[slot]).start(priority=k % 2)
            return carry
        lax.fori_loop(0, tc // 8, body, 0)

    slot = i % 2

    @pl.when(i == 0)
    def _():
        idx_copy(0, 0).start()
        idx_copy(0, 0).wait()
        issue_tile(0)

        @pl.when(n > 1)
        def _():
            idx_copy(1, 1).start()

    @pl.when(i + 1 < n)
    def _():
        idx_copy(i + 1, 1 - slot).wait()
        for sl in range(2):
            @pl.when(slot == 1 - sl)
            def _():
                issue_tile(sl)

    @pl.when(i + 2 < n)
    def _():
        idx_copy(i + 2, slot).start()

    for k in range(TOP_K):
        pltpu.make_async_copy(buf_ref.at[slot, k], buf_ref.at[slot, k], gsem.at[slot]).wait()

    w = w_ref[...]
    acc_lo = jnp.zeros((tc, half), F32)
    acc_hi = jnp.zeros((tc, half), F32)
    for k in range(TOP_K):
        lo, hi = _unpack_halves(buf_ref[slot, k])
        wk = w[:, k:k + 1]
        acc_lo = acc_lo + wk * lo
        acc_hi = acc_hi + wk * hi
    y_lo = pre_ref[:, :half] + acc_lo
    y_hi = pre_ref[:, half:] + acc_hi
    inv_d = 1.0 / (2 * half)
    mu = (jnp.sum(y_lo, axis=-1, keepdims=True) + jnp.sum(y_hi, axis=-1, keepdims=True)) * inv_d
    c_lo = y_lo - mu
    c_hi = y_hi - mu
    var = (jnp.sum(c_lo * c_lo, axis=-1, keepdims=True)
           + jnp.sum(c_hi * c_hi, axis=-1, keepdims=True)) * inv_d
    rstd = lax.rsqrt(var + LN_EPS)
    o_ref[:, :half] = c_lo * rstd * g_ref[:, :half] + b_ref[:, :half]
    o_ref[:, half:] = c_hi * rstd * g_ref[:, half:] + b_ref[:, half:]


def _combine_ln(y_packed, dest_tiles, w_col, pre, g, b):
    T, D = pre.shape
    half = y_packed.shape[1]
    nt, n_idx = dest_tiles.shape
    tc = n_idx // TOP_K
    return pl.pallas_call(
        _combine_kernel,
        out_shape=jax.ShapeDtypeStruct((T, D), F32),
        grid=(nt,),
        in_specs=[pl.BlockSpec(memory_space=pl.ANY), pl.BlockSpec(memory_space=pl.ANY),
                  pl.BlockSpec((tc, TOP_K), lambda i: (i, 0)),
                  pl.BlockSpec((tc, D), lambda i: (i, 0)),
                  pl.BlockSpec((1, D), lambda i: (0, 0)),
                  pl.BlockSpec((1, D), lambda i: (0, 0))],
        out_specs=pl.BlockSpec((tc, D), lambda i: (i, 0)),
        scratch_shapes=[pltpu.SMEM((2 * n_idx,), I32),
                        pltpu.VMEM((2, TOP_K, tc, half), U32),
                        pltpu.SemaphoreType.DMA((2,)), pltpu.SemaphoreType.DMA((2,))],
        compiler_params=_cparams(1, 58),
        name="combine_ln2",
    )(dest_tiles, y_packed, w_col, pre, g.reshape(1, D), b.reshape(1, D))


def _glu_up_kernel(h_ref, wg_ref, wu_ref, o_ref, wbf_ref):
    tn = wg_ref.shape[1]

    @pl.when(pl.program_id(1) == 0)
    def _():
        wbf_ref[:, :tn] = wg_ref[...].astype(BF16)
        wbf_ref[:, tn:] = wu_ref[...].astype(BF16)

    gu = jnp.dot(h_ref[...], wbf_ref[...], preferred_element_type=F32)
    g = gu[:, :tn]
    o_ref[...] = (g * _sigmoid(g) * gu[:, tn:]).astype(o_ref.dtype)


def _glu_up(h_bf, w_gu, *, tm=512, tn=384):
    T, D = h_bf.shape
    f = w_gu.shape[1] // 2
    tm = min(tm, T)
    nc = f // tn
    return pl.pallas_call(
        _glu_up_kernel,
        out_shape=jax.ShapeDtypeStruct((T, f), BF16),
        grid=(nc, T // tm),
        in_specs=[pl.BlockSpec((tm, D), lambda c, i: (i, 0)),
                  pl.BlockSpec((D, tn), lambda c, i: (0, c)),
                  pl.BlockSpec((D, tn), lambda c, i: (0, nc + c))],
        out_specs=pl.BlockSpec((tm, tn), lambda c, i: (i, c)),
        scratch_shapes=[pltpu.VMEM((D, 2 * tn), BF16)],
        compiler_params=_cparams(2, 56),
        name="shared_up",
    )(h_bf, w_gu, w_gu)


def _dense_tail_kernel(fill_ref, dest_hbm, x_ref, hb_ref, hj_ref, s_ref, p_ref, wd_ref, wg_ref, bg_ref,
                       wp_ref, o_ref, xs_hbm, wdbf_ref, wgbf_ref, wpbf_ref, idx_smem, zero_ref,
                       isem, ssem, zsem, *, alpha, k_per_step):
    j = pl.program_id(0)
    i = pl.program_id(1)
    ni = pl.num_programs(1)
    step = j * ni + i
    tm = x_ref.shape[0]
    n_idx = k_per_step * tm
    R = zero_ref.shape[0]

    def idx_copies(jj, ii, slot):
        return [pltpu.make_async_copy(dest_hbm.at[jj * k_per_step + kk, pl.ds(pl.multiple_of(ii * tm, 128), tm)],
                                      idx_smem.at[pl.ds(slot * n_idx + kk * tm, tm)], isem.at[slot])
                for kk in range(k_per_step)]

    @pl.when(step == 0)
    def _():
        for cp in idx_copies(0, 0, 0):
            cp.start()
        zero_ref[...] = jnp.zeros_like(zero_ref)

        def fill(f):
            return pltpu.make_async_copy(zero_ref, xs_hbm.at[pl.ds(fill_ref[f] * R, R)], zsem)

        def start(f, carry):
            @pl.when(fill_ref[f] >= 0)
            def _():
                fill(f).start()
            return carry
        lax.fori_loop(0, fill_ref.shape[0], start, 0)

        def wait(f, carry):
            @pl.when(fill_ref[f] >= 0)
            def _():
                fill(f).wait()
            return carry
        lax.fori_loop(0, fill_ref.shape[0], wait, 0)

    @pl.when(i == 0)
    def _():
        wdbf_ref[...] = wd_ref[...].astype(BF16)
        wgbf_ref[...] = wg_ref[...].astype(BF16)
        wpbf_ref[...] = wp_ref[...].astype(BF16)

    slot = step % 2
    for cp in idx_copies(j, i, slot):
        cp.wait()

    @pl.when(step + 1 < pl.num_programs(0) * ni)
    def _():
        nxt = step + 1
        for cp in idx_copies(nxt // ni, nxt % ni, 1 - slot):
            cp.start()

    base = slot * n_idx
    for kk in range(k_per_step):
        for r in range(tm):
            d = idx_smem[base + kk * tm + r]
            pltpu.make_async_copy(x_ref.at[pl.ds(r, 1)], xs_hbm.at[pl.ds(d, 1)], ssem).start(priority=r % 2)

    shared = jnp.dot(s_ref[...], wdbf_ref[...], preferred_element_type=F32)
    gate = _sigmoid(jnp.dot(hb_ref[...], wgbf_ref[...], preferred_element_type=F32) + bg_ref[...])
    proj = jnp.dot(p_ref[...].astype(BF16), wpbf_ref[...], preferred_element_type=F32)
    o_ref[...] = alpha * hj_ref[...] + shared + gate * proj

    for kk in range(k_per_step):
        pltpu.make_async_copy(x_ref, x_ref, ssem).wait()


def _dense_tail(h1_bf, h1, h1_packed, dest_kt, fill_blocks, n_rows, s_mid, p, w_down_s, w_gate, b_gate,
                w_proj, alpha, *, tm=512, tn=512):
    T, D = h1.shape
    f = s_mid.shape[1]
    dp = p.shape[1]
    W = h1_packed.shape[1]
    tm = min(tm, T)
    nj = D // tn
    assert TOP_K % nj == 0 and tm % 128 == 0
    k_per_step = TOP_K // nj
    row = lambda j, i, fb: (i, 0)
    col = lambda j, i, fb: (0, j)
    tile = lambda j, i, fb: (i, j)
    return pl.pallas_call(
        functools.partial(_dense_tail_kernel, alpha=alpha, k_per_step=k_per_step),
        out_shape=(jax.ShapeDtypeStruct((T, D), F32), jax.ShapeDtypeStruct((n_rows, W), h1_packed.dtype)),
        grid_spec=pltpu.PrefetchScalarGridSpec(
            num_scalar_prefetch=1,
            grid=(nj, T // tm),
            in_specs=[pl.BlockSpec(memory_space=pl.ANY),
                      pl.BlockSpec((tm, W), row),
                      pl.BlockSpec((tm, D), row),
                      pl.BlockSpec((tm, tn), tile),
                      pl.BlockSpec((tm, f), row),
                      pl.BlockSpec((tm, dp), row),
                      pl.BlockSpec((f, tn), col),
                      pl.BlockSpec((D, tn), col),
                      pl.BlockSpec((1, tn), col),
                      pl.BlockSpec((dp, tn), col)],
            out_specs=(pl.BlockSpec((tm, tn), tile), pl.BlockSpec(memory_space=pl.ANY)),
            scratch_shapes=[pltpu.VMEM((f, tn), BF16), pltpu.VMEM((D, tn), BF16), pltpu.VMEM((dp, tn), BF16),
                            pltpu.SMEM((2 * k_per_step * tm,), I32), pltpu.VMEM((MOE_BLOCK, W), h1_packed.dtype),
                            pltpu.SemaphoreType.DMA((2,)), pltpu.SemaphoreType.DMA, pltpu.SemaphoreType.DMA]),
        compiler_params=pltpu.CompilerParams(dimension_semantics=("arbitrary", "arbitrary"),
                                             vmem_limit_bytes=58 * 1024 * 1024, has_side_effects=True),
        name="dense_tail_dispatch",
    )(fill_blocks, dest_kt, h1_packed, h1_bf, h1, s_mid, p, w_down_s, w_gate, b_gate.reshape(1, D), w_proj)


def _dispatch_tables(idx, rank, counts):
    E = counts.shape[0]
    P = idx.size + E * MOE_BLOCK
    nb = P // MOE_BLOCK
    padded = (counts + MOE_BLOCK - 1) // MOE_BLOCK * MOE_BLOCK
    pend = jnp.cumsum(padded)
    pstart = pend - padded
    sel = idx[None] == jnp.arange(E, dtype=I32).reshape(E, 1, 1, 1)
    dest = rank + jnp.sum(jnp.where(sel, pstart.reshape(E, 1, 1, 1), 0), axis=0)
    blk_row = jnp.arange(nb, dtype=I32) * MOE_BLOCK
    blk_e = jnp.minimum(jnp.sum(pend[None, :] <= blk_row[:, None], axis=1), E - 1).astype(I32)
    n_used = pend[-1] // MOE_BLOCK
    last_blk = jnp.where(counts > 0, pend // MOE_BLOCK - 1, -1)
    tail_blk = n_used + jnp.arange(E, dtype=I32)
    tail_blk = jnp.where(tail_blk < nb, tail_blk, -1)
    fill = jnp.concatenate([last_blk, tail_blk]).astype(I32)
    nxt_blk = jnp.take(pend // MOE_BLOCK, blk_e).astype(I32)
    return dest.astype(I32), blk_e, nxt_blk, n_used.astype(I32).reshape(1), fill, P


def _layer(h0_f, h0_bf, p_l, w_in, b_in, conv_w, conv_b, mh_norm_g, w_conv_out, w_mlstm_out,
           w_mix_out, ln1_g, ln1_b, w_router, router_bias, w_gu_e, w_down_e, w_gu_s, w_down_s,
           w_ple_gate, b_ple_gate, w_ple_proj, ln2_g, ln2_b, alpha, batch, seq):
    T, D = h0_f.shape
    d_conv = conv_w.shape[1]
    d_v = mh_norm_g.shape[0]
    d_qk = d_v // 2
    n_if = 2 * N_HEADS
    c_qk = 3 * d_conv
    c_if = c_qk + 2 * d_qk + 2 * d_v
    c_gate = c_if + n_if
    b2d = b_in.reshape(1, -1)

    w_t = jnp.swapaxes(w_in, 0, 1)
    ya_pre = _conv_branch(h0_bf, w_t, b2d, conv_w, conv_b.reshape(1, -1), seq, d_conv)
    qkvo = _proj(h0_bf, w_t, b2d, c_qk, 2 * d_qk + 2 * d_v)
    if_col, if_row = _if_gates(h0_bf, w_t[c_if:c_gate], b_in[c_if:c_gate])
    gates = _proj(h0_bf, w_t, b2d, c_gate, 2 * D, act="sigmoid")
    yb_pre = _mlstm(qkvo, if_col, if_row, mh_norm_g.reshape(1, -1), batch, seq, d_qk, d_v)
    u = _merge(ya_pre, yb_pre, w_conv_out, w_mlstm_out, gates)
    pre1 = _mix(u, w_mix_out, h0_f, alpha)

    rt = min(ROUTE_TILE, T)
    h1, h1_bf, h1_packed, idx, rank, w_col, counts = _ln1_router(pre1, ln1_g, ln1_b, w_router,
                                                                 router_bias, rt)
    dest, blk_e, nxt_blk, n_used, fill, n_rows = _dispatch_tables(idx, rank, counts[:, 0].astype(I32))
    dest_tiles = dest.reshape(T // rt, TOP_K * rt)
    dest_kt = dest.transpose(1, 0, 2).reshape(TOP_K, T)

    s_mid = _glu_up(h1_bf, w_gu_s)
    pre2, xs = _dense_tail(h1_bf, h1, h1_packed, dest_kt, fill, n_rows, s_mid, p_l, w_down_s,
                           w_ple_gate, b_ple_gate, w_ple_proj, alpha)

    hmid = _expert_matmul(xs, w_gu_e, blk_e, nxt_blk, n_used, _glu_block, w_gu_e.shape[2] // 2, BF16,
                          "expert_up", EXPERT_UP_SUB)
    y_packed = _expert_matmul(hmid, w_down_e, blk_e, nxt_blk, n_used, _down_block, D // 2, U32,
                              "expert_down", EXPERT_DOWN_SUB)
    return _combine_ln(y_packed, dest_tiles, w_col, pre2, ln2_g, ln2_b)


def kernel(x, p, ln_in_g, ln_in_b, w_in, b_in, conv_w, conv_b, mh_norm_g, w_conv_out, w_mlstm_out,
           w_mix_out, ln1_g, ln1_b, w_router, router_bias, w_gu_e, w_down_e, w_gu_s, w_down_s,
           w_ple_gate, b_ple_gate, w_ple_proj, ln2_g, ln2_b):
    B, S, D = x.shape
    depth = w_in.shape[0]
    alpha = (2 * depth) ** 0.25
    T = B * S
    h_f, h_bf = _ln_in(x.reshape(T, D), ln_in_g, ln_in_b)
    for l in range(depth):
        h_f = _layer(h_f, h_bf, p[l].reshape(T, -1), w_in[l], b_in[l], conv_w[l], conv_b[l],
                     mh_norm_g[l], w_conv_out[l], w_mlstm_out[l], w_mix_out[l], ln1_g[l], ln1_b[l],
                     w_router[l], router_bias[l], w_gu_e[l], w_down_e[l], w_gu_s[l], w_down_s[l],
                     w_ple_gate[l], b_ple_gate[l], w_ple_proj[l], ln2_g[l], ln2_b[l], alpha, B, S)
        if l + 1 < depth:
            h_bf = h_f.astype(BF16)
    return h_f.reshape(B, S, D)
```

```python
import functools

import jax
import jax.numpy as jnp
from jax import lax
from jax.experimental import pallas as pl
from jax.experimental.pallas import tpu as pltpu

F32 = jnp.float32
BF16 = jnp.bfloat16
U32 = jnp.uint32
I32 = jnp.int32

N_HEADS = 8
TOP_K = 8
N_GROUPS = 8
TOP_GROUPS = 4
ROUTE_SCALE = 2.5
MOE_BLOCK = 256
LN_EPS = 1e-5
CONV_W = 3
MLSTM_CHUNK = 128
MLSTM_STEP = 128
ROUTE_TILE = 256
COMBINE_TILE = 128
EXPERT_UP_SUB = 2
EXPERT_DOWN_SUB = 4
NEG_BIG = -1e30
HI_MASK = 0xFFFF0000
V7X_VMEM_BYTES = 64 * 1024 * 1024


def _cparams(n_axes, vmem_mib):
    assert vmem_mib * 1024 * 1024 <= V7X_VMEM_BYTES
    return pltpu.CompilerParams(dimension_semantics=("arbitrary",) * n_axes,
                                vmem_limit_bytes=vmem_mib * 1024 * 1024)


def _ln_rows(x, g, b):
    mu = jnp.mean(x, axis=-1, keepdims=True)
    xc = x - mu
    var = jnp.mean(xc * xc, axis=-1, keepdims=True)
    return xc * lax.rsqrt(var + LN_EPS) * g + b


def _pack_halves(lo, hi):
    lo = pltpu.bitcast(lo.astype(BF16).astype(F32), U32)
    hi = pltpu.bitcast(hi.astype(BF16).astype(F32), U32)
    return (hi & jnp.uint32(HI_MASK)) | (lo >> jnp.uint32(16))


def _sigmoid(x):
    return 0.5 * jnp.tanh(0.5 * x) + 0.5


def _unpack_halves(w):
    lo = pltpu.bitcast(w << jnp.uint32(16), F32)
    hi = pltpu.bitcast(w & jnp.uint32(HI_MASK), F32)
    return lo, hi


def _ln_in_kernel(x_ref, g_ref, b_ref, of_ref, ob_ref):
    y = _ln_rows(x_ref[...], g_ref[...], b_ref[...])
    of_ref[...] = y
    ob_ref[...] = y.astype(BF16)


def _ln_in(x2, g, b, tm=256):
    T, D = x2.shape
    return pl.pallas_call(
        _ln_in_kernel,
        out_shape=(jax.ShapeDtypeStruct((T, D), F32), jax.ShapeDtypeStruct((T, D), BF16)),
        grid=(T // tm,),
        in_specs=[pl.BlockSpec((tm, D), lambda i: (i, 0)),
                  pl.BlockSpec((1, D), lambda i: (0, 0)),
                  pl.BlockSpec((1, D), lambda i: (0, 0))],
        out_specs=(pl.BlockSpec((tm, D), lambda i: (i, 0)),
                   pl.BlockSpec((tm, D), lambda i: (i, 0))),
        compiler_params=_cparams(1, 40),
        name="ln_in",
    )(x2, g.reshape(1, D), b.reshape(1, D))


_NT = (((1,), (1,)), ((), ()))


def _proj_kernel(x_ref, w_ref, b_ref, o_ref, wbf_ref, *, act):
    @pl.when(pl.program_id(1) == 0)
    def _():
        wbf_ref[...] = w_ref[...].astype(BF16)

    acc = lax.dot_general(x_ref[...], wbf_ref[...], _NT, preferred_element_type=F32) + b_ref[...]
    if act == "sigmoid":
        acc = _sigmoid(acc)
    o_ref[...] = acc.astype(o_ref.dtype)


def _proj(x, w_t, bias2d, row0, n_cols, *, act=None, tm=1024, tn=512, out_dtype=BF16):
    T, K = x.shape
    tm = min(tm, T)
    assert n_cols % tn == 0 and T % tm == 0 and row0 % 8 == 0
    if row0 % tn == 0:
        jb = row0 // tn
        w_spec = pl.BlockSpec((tn, K), lambda j, i: (jb + j, 0))
        b_spec = pl.BlockSpec((1, tn), lambda j, i: (0, jb + j))
    else:
        w_spec = pl.BlockSpec((pl.Element(tn), pl.Element(K)),
                              lambda j, i: (pl.multiple_of(row0 + j * tn, 8), 0))
        bias2d = bias2d[:, row0:row0 + n_cols]
        b_spec = pl.BlockSpec((1, tn), lambda j, i: (0, j))
    return pl.pallas_call(
        functools.partial(_proj_kernel, act=act),
        out_shape=jax.ShapeDtypeStruct((T, n_cols), out_dtype),
        grid=(n_cols // tn, T // tm),
        in_specs=[pl.BlockSpec((tm, K), lambda j, i: (i, 0)), w_spec, b_spec],
        out_specs=pl.BlockSpec((tm, tn), lambda j, i: (i, j)),
        scratch_shapes=[pltpu.VMEM((tn, K), BF16)],
        compiler_params=_cparams(2, 56),
        name="proj_" + (act or "lin"),
    )(x, w_t, bias2d)


def _conv_kernel(x_ref, wh_ref, wc_ref, wb_ref, bh_ref, bc_ref, bb_ref, cw_ref, cb_ref,
                 o_ref, wbf_ref, zprev_ref, *, tiles_per_seq):
    i = pl.program_id(1)

    @pl.when(i == 0)
    def _():
        wbf_ref[0] = wh_ref[...].astype(BF16)
        wbf_ref[1] = wc_ref[...].astype(BF16)
        wbf_ref[2] = wb_ref[...].astype(BF16)

    @pl.when(i % tiles_per_seq == 0)
    def _():
        zprev_ref[...] = jnp.zeros_like(zprev_ref)

    x = x_ref[...]
    ha = lax.dot_general(x, wbf_ref[0], _NT, preferred_element_type=F32) + bh_ref[...]
    ca = lax.dot_general(x, wbf_ref[1], _NT, preferred_element_type=F32) + bc_ref[...]
    ba = lax.dot_general(x, wbf_ref[2], _NT, preferred_element_type=F32) + bb_ref[...]
    z = ca * ha
    tm = z.shape[0]
    prev = zprev_ref[...]
    row8 = lax.broadcasted_iota(I32, prev.shape, 0)
    z1 = pltpu.roll(z, 1, 0)
    z2 = pltpu.roll(z, 2, 0)
    p1 = pltpu.roll(prev, 1, 0)
    p2 = pltpu.roll(prev, 2, 0)
    z1 = jnp.concatenate([jnp.where(row8 < 1, p1, z1[:8]), z1[8:]], axis=0)
    z2 = jnp.concatenate([jnp.where(row8 < 2, p2, z2[:8]), z2[8:]], axis=0)
    cw = cw_ref[...]
    y = cw[0:1] * z2 + cw[1:2] * z1 + cw[2:3] * z + cb_ref[...]
    o_ref[...] = (ba * y).astype(o_ref.dtype)
    zprev_ref[...] = z[tm - 8:]


def _conv_branch(x, w_t, b2d, conv_w, conv_b2d, seq, d_conv, *, tm=1024, tn=256):
    T, K = x.shape
    tm = min(tm, seq)
    assert seq % tm == 0 and d_conv % tn == 0 and tm % 8 == 0
    nb = d_conv // tn
    wspec = lambda g: pl.BlockSpec((tn, K), lambda j, i: (g * nb + j, 0))
    bspec = lambda g: pl.BlockSpec((1, tn), lambda j, i: (0, g * nb + j))
    return pl.pallas_call(
        functools.partial(_conv_kernel, tiles_per_seq=seq // tm),
        out_shape=jax.ShapeDtypeStruct((T, d_conv), BF16),
        grid=(nb, T // tm),
        in_specs=[pl.BlockSpec((tm, K), lambda j, i: (i, 0)),
                  wspec(0), wspec(1), wspec(2), bspec(0), bspec(1), bspec(2),
                  pl.BlockSpec((CONV_W, tn), lambda j, i: (0, j)),
                  pl.BlockSpec((1, tn), lambda j, i: (0, j))],
        out_specs=pl.BlockSpec((tm, tn), lambda j, i: (i, j)),
        scratch_shapes=[pltpu.VMEM((3, tn, K), BF16), pltpu.VMEM((8, tn), F32)],
        compiler_params=_cparams(2, 56),
        name="conv_branch",
    )(x, w_t, w_t, w_t, b2d, b2d, b2d, conv_w, conv_b2d)


def _if_kernel(x_ref, wp_ref, wt_ref, bc_ref, br_ref, oc_ref, or_ref):
    x = x_ref[...]
    oc_ref[...] = lax.dot_general(x, wp_ref[...], _NT, preferred_element_type=F32) + bc_ref[...]
    or_ref[...] = lax.dot_general(wt_ref[...], x, _NT, preferred_element_type=F32) + br_ref[...]


def _if_gates(x, w_if_t, b_if, tm=512):
    T, K = x.shape
    tm = min(tm, T)
    n = w_if_t.shape[0]
    w_if_t = w_if_t.astype(BF16)
    w_pad = jnp.zeros((128, K), BF16).at[:n].set(w_if_t)
    b_pad = jnp.zeros((1, 128), F32).at[0, :n].set(b_if)
    return pl.pallas_call(
        _if_kernel,
        out_shape=(jax.ShapeDtypeStruct((T, 128), F32), jax.ShapeDtypeStruct((n, T), F32)),
        grid=(T // tm,),
        in_specs=[pl.BlockSpec((tm, K), lambda i: (i, 0)),
                  pl.BlockSpec((128, K), lambda i: (0, 0)),
                  pl.BlockSpec((n, K), lambda i: (0, 0)),
                  pl.BlockSpec((1, 128), lambda i: (0, 0)),
                  pl.BlockSpec((n, 1), lambda i: (0, 0))],
        out_specs=(pl.BlockSpec((tm, 128), lambda i: (i, 0)),
                   pl.BlockSpec((n, tm), lambda i: (0, i))),
        compiler_params=_cparams(1, 32),
        name="if_gates",
    )(x, w_pad, w_if_t, b_pad, b_if.reshape(n, 1))


def _mlstm_kernel(q_ref, k_ref, v_ref, o_ref, ifc_ref, ifr_ref, g_ref, y_ref, c_ref, m_ref,
                  *, dk, dv, L):
    H = N_HEADS

    @pl.when(pl.program_id(1) == 0)
    def _():
        c_ref[...] = jnp.zeros_like(c_ref)
        m_ref[...] = jnp.zeros_like(m_ref)

    scale = dk ** -0.5
    r = lax.broadcasted_iota(I32, (L, L), 0)
    c = lax.broadcasted_iota(I32, (L, L), 1)
    causal = r >= c
    tri_lower = causal.astype(F32)
    tri_upper = (r <= c).astype(F32)
    ones_col = (lax.broadcasted_iota(I32, (L, 128), 1) == 0).astype(BF16)

    for cc in range(q_ref.shape[0] // L):
        rows = slice(cc * L, (cc + 1) * L)
        ifc = ifc_ref[rows, :]
        ifr = ifr_ref[:, rows]
        ig_c = ifc[:, 0:H]
        lf_c = jax.nn.log_sigmoid(ifc[:, H:2 * H])
        ig_r = ifr[0:H, :]
        lf_r = jax.nn.log_sigmoid(ifr[H:2 * H, :])
        bcum_c = jnp.dot(tri_lower, lf_c, preferred_element_type=F32,
                         precision=lax.Precision.HIGHEST)
        bcum_r = jnp.dot(lf_r, tri_upper, preferred_element_type=F32,
                         precision=lax.Precision.HIGHEST)
        d_c = ig_c - bcum_c
        d_r = ig_r - bcum_r

        for h in range(H):
            q = q_ref[rows, h * dk:(h + 1) * dk]
            k = k_ref[rows, h * dk:(h + 1) * dk]
            v = v_ref[rows, h * dv:(h + 1) * dv]
            v_aug = jnp.concatenate([v, ones_col], axis=1)
            bc = bcum_c[:, h:h + 1]
            m_prev = m_ref[h]
            c_prev = c_ref[h]

            qk = lax.dot_general(q, k, _NT, preferred_element_type=F32) * scale
            logd = jnp.where(causal, bc + d_r[h:h + 1, :], NEG_BIG)
            log_inter = bc + m_prev
            m_t = jnp.maximum(log_inter, jnp.max(logd, axis=1, keepdims=True))
            w_inter = jnp.exp(log_inter - m_t)
            s_mat = (qk * jnp.exp(logd - m_t)).astype(BF16)
            tot = (w_inter * jnp.dot(q, c_prev.astype(BF16), preferred_element_type=F32)
                   + jnp.dot(s_mat, v_aug, preferred_element_type=F32))
            den = jnp.maximum(jnp.abs(tot[:, dv:dv + 1]), jnp.exp(-m_t))
            hh = tot[:, :dv] / den
            mu = jnp.mean(hh, axis=-1, keepdims=True)
            hc = hh - mu
            var = jnp.mean(hc * hc, axis=-1, keepdims=True)
            hn = hc * lax.rsqrt(var + LN_EPS) * g_ref[:, h * dv:(h + 1) * dv]
            og = _sigmoid(o_ref[rows, h * dv:(h + 1) * dv].astype(F32))
            y_ref[rows, h * dv:(h + 1) * dv] = (og * hn).astype(y_ref.dtype)

            g_tot = bc[L - 1:L, :]
            a = g_tot + d_c[:, h:h + 1]
            m_new = jnp.maximum(g_tot + m_prev, jnp.max(a, axis=0, keepdims=True))
            kw = (k.astype(F32) * (jnp.exp(a - m_new) * scale)).astype(BF16)
            kv = lax.dot_general(kw, v_aug, (((0,), (0,)), ((), ())), preferred_element_type=F32)
            c_ref[h] = jnp.exp(g_tot + m_prev - m_new) * c_prev + kv
            m_ref[h] = m_new


def _mlstm(qkvo, if_col, if_row, mh_g2d, batch, seq, d_qk, d_v, L=MLSTM_CHUNK, step_rows=MLSTM_STEP):
    T = qkvo.shape[0]
    L = min(L, seq)
    R = min(max(step_rows, L), seq)
    assert seq % R == 0 and R % L == 0 and d_v == 2 * d_qk
    nc = seq // R
    dk, dv = d_qk // N_HEADS, d_v // N_HEADS
    row = lambda b, c: b * nc + c
    return pl.pallas_call(
        functools.partial(_mlstm_kernel, dk=dk, dv=dv, L=L),
        out_shape=jax.ShapeDtypeStruct((T, d_v), BF16),
        grid=(batch, nc),
        in_specs=[pl.BlockSpec((R, d_qk), lambda b, c: (row(b, c), 0)),
                  pl.BlockSpec((R, d_qk), lambda b, c: (row(b, c), 1)),
                  pl.BlockSpec((R, d_v), lambda b, c: (row(b, c), 1)),
                  pl.BlockSpec((R, d_v), lambda b, c: (row(b, c), 2)),
                  pl.BlockSpec((R, 128), lambda b, c: (row(b, c), 0)),
                  pl.BlockSpec((2 * N_HEADS, R), lambda b, c: (0, row(b, c))),
                  pl.BlockSpec((1, d_v), lambda b, c: (0, 0))],
        out_specs=pl.BlockSpec((R, d_v), lambda b, c: (row(b, c), 0)),
        scratch_shapes=[pltpu.VMEM((N_HEADS, dk, dv + 128), F32),
                        pltpu.VMEM((N_HEADS, 1, 1), F32)],
        compiler_params=_cparams(2, 40),
        name="mlstm",
    )(qkvo, qkvo, qkvo, qkvo, if_col, if_row, mh_g2d)


def _merge_kernel(a_ref, b_ref, wa_ref, wb_ref, ga_ref, gb_ref, o_ref, wabf_ref, wbbf_ref):
    @pl.when(pl.program_id(1) == 0)
    def _():
        wabf_ref[...] = wa_ref[...].astype(BF16)
        wbbf_ref[...] = wb_ref[...].astype(BF16)

    ya = jnp.dot(a_ref[...], wabf_ref[...], preferred_element_type=F32)
    yb = jnp.dot(b_ref[...], wbbf_ref[...], preferred_element_type=F32)
    u = ga_ref[...].astype(F32) * ya + gb_ref[...].astype(F32) * yb
    o_ref[...] = u.astype(o_ref.dtype)


def _merge(ya_pre, yb_pre, w_a, w_b, gates, *, tm=1024, tn=512):
    T, K = ya_pre.shape
    D = w_a.shape[1]
    tm = min(tm, T)
    nj = D // tn
    return pl.pallas_call(
        _merge_kernel,
        out_shape=jax.ShapeDtypeStruct((T, D), BF16),
        grid=(nj, T // tm),
        in_specs=[pl.BlockSpec((tm, K), lambda j, i: (i, 0)),
                  pl.BlockSpec((tm, K), lambda j, i: (i, 0)),
                  pl.BlockSpec((K, tn), lambda j, i: (0, j)),
                  pl.BlockSpec((K, tn), lambda j, i: (0, j)),
                  pl.BlockSpec((tm, tn), lambda j, i: (i, j)),
                  pl.BlockSpec((tm, tn), lambda j, i: (i, nj + j))],
        out_specs=pl.BlockSpec((tm, tn), lambda j, i: (i, j)),
        scratch_shapes=[pltpu.VMEM((K, tn), BF16), pltpu.VMEM((K, tn), BF16)],
        compiler_params=_cparams(2, 56),
        name="merge",
    )(ya_pre, yb_pre, w_a, w_b, gates, gates)


def _mix_kernel(u_ref, w_ref, h_ref, o_ref, wbf_ref, *, alpha):
    @pl.when(pl.program_id(1) == 0)
    def _():
        wbf_ref[...] = w_ref[...].astype(BF16)

    o_ref[...] = alpha * h_ref[...] + jnp.dot(u_ref[...], wbf_ref[...], preferred_element_type=F32)


def _mix(u, w, h0, alpha, *, tm=1024, tn=512):
    T, D = h0.shape
    tm = min(tm, T)
    return pl.pallas_call(
        functools.partial(_mix_kernel, alpha=alpha),
        out_shape=jax.ShapeDtypeStruct((T, D), F32),
        grid=(D // tn, T // tm),
        in_specs=[pl.BlockSpec((tm, D), lambda j, i: (i, 0)),
                  pl.BlockSpec((D, tn), lambda j, i: (0, j)),
                  pl.BlockSpec((tm, tn), lambda j, i: (i, j))],
        out_specs=pl.BlockSpec((tm, tn), lambda j, i: (i, j)),
        scratch_shapes=[pltpu.VMEM((D, tn), BF16)],
        compiler_params=_cparams(2, 56),
        name="mix_out",
    )(u, w, h0)


def _ln1_router_kernel(pre_ref, g_ref, b_ref, wr_ref, rb_ref, hf_ref, hb_ref, hp_ref,
                       idx_ref, rank_ref, wcol_ref, cnt_ref, carry_ref):
    i = pl.program_id(0)
    E = wr_ref.shape[0]
    tm, D = pre_ref.shape
    G, M = N_GROUPS, E // N_GROUPS

    @pl.when(i == 0)
    def _():
        carry_ref[...] = jnp.zeros_like(carry_ref)

    h = _ln_rows(pre_ref[...], g_ref[...], b_ref[...])
    hf_ref[...] = h
    hb_ref[...] = h.astype(BF16)
    hp_ref[...] = _pack_halves(h[:, :D // 2], h[:, D // 2:])

    logits = lax.dot_general(wr_ref[...], h, (((1,), (1,)), ((), ())),
                             preferred_element_type=F32, precision=lax.Precision.HIGHEST)
    scores = jax.nn.sigmoid(logits)
    scores3 = scores.reshape(G, M, tm)
    sel3 = (scores + rb_ref[...]).reshape(G, M, tm)
    midx = lax.broadcasted_iota(I32, (G, M, tm), 1)
    gidx3 = lax.broadcasted_iota(I32, (G, M, tm), 0)
    eidx = gidx3 * M + midx
    gidx = lax.broadcasted_iota(I32, (G, 1, tm), 0)
    neg_inf = -jnp.inf

    top1 = jnp.max(sel3, axis=1, keepdims=True)
    first1 = jnp.min(jnp.where(sel3 == top1, midx, M), axis=1, keepdims=True)
    top2 = jnp.max(jnp.where(midx == first1, neg_inf, sel3), axis=1, keepdims=True)
    gs = top1 + top2
    gkeep = jnp.zeros((G, 1, tm), F32)
    for _ in range(TOP_GROUPS):
        mx = jnp.max(gs, axis=0, keepdims=True)
        first = jnp.min(jnp.where(gs == mx, gidx, G), axis=0, keepdims=True)
        hit = gidx == first
        gkeep = jnp.where(hit, 1.0, gkeep)
        gs = jnp.where(hit, neg_inf, gs)
    selm = jnp.where(gkeep > 0.5, sel3, neg_inf)

    idx_rows, sc_rows = [], []
    chosen = jnp.zeros((G, M, tm), F32)
    for _ in range(TOP_K):
        mx = jnp.max(jnp.max(selm, axis=1, keepdims=True), axis=0, keepdims=True)
        first = jnp.min(jnp.min(jnp.where(selm == mx, eidx, E), axis=1, keepdims=True),
                        axis=0, keepdims=True)
        hit = eidx == first
        sc = jnp.sum(jnp.sum(jnp.where(hit, scores3, 0.0), axis=1, keepdims=True),
                     axis=0, keepdims=True)
        chosen = jnp.where(hit, 1.0, chosen)
        selm = jnp.where(hit, neg_inf, selm)
        idx_rows.append(first)
        sc_rows.append(sc)
    denom = sc_rows[0]
    for s in sc_rows[1:]:
        denom = denom + s

    tr = lax.broadcasted_iota(I32, (tm, tm), 0)
    tc = lax.broadcasted_iota(I32, (tm, tm), 1)
    before = (tr < tc).astype(BF16)
    chosen2 = chosen.reshape(E, tm)
    rank2 = jnp.dot(chosen2.astype(BF16), before, preferred_element_type=F32) + carry_ref[...]
    rank3 = rank2.reshape(G, M, tm)
    w_rows = []
    for k in range(TOP_K):
        hit = eidx == idx_rows[k]
        rk = jnp.sum(jnp.sum(jnp.where(hit, rank3, 0.0), axis=1, keepdims=True),
                     axis=0, keepdims=True)
        idx_ref[k:k + 1, :] = idx_rows[k].reshape(1, tm)
        rank_ref[k:k + 1, :] = rk.reshape(1, tm).astype(I32)
        w_rows.append((sc_rows[k] / denom * ROUTE_SCALE).reshape(1, tm))
    wcol_ref[...] = jnp.concatenate(w_rows, axis=0).T
    carry_ref[...] = carry_ref[...] + jnp.sum(chosen2, axis=1, keepdims=True)

    @pl.when(i == pl.num_programs(0) - 1)
    def _():
        cnt_ref[...] = carry_ref[...]


def _ln1_router(pre, g, b, w_router, router_bias, tm):
    T, D = pre.shape
    E = w_router.shape[1]
    nt = T // tm
    row = lambda i: (i, 0)
    fixed = lambda i: (0, 0)
    return pl.pallas_call(
        _ln1_router_kernel,
        out_shape=(jax.ShapeDtypeStruct((T, D), F32), jax.ShapeDtypeStruct((T, D), BF16),
                   jax.ShapeDtypeStruct((T, D // 2), U32),
                   jax.ShapeDtypeStruct((nt, TOP_K, tm), I32), jax.ShapeDtypeStruct((nt, TOP_K, tm), I32),
                   jax.ShapeDtypeStruct((T, TOP_K), F32), jax.ShapeDtypeStruct((E, 1), F32)),
        grid=(nt,),
        in_specs=[pl.BlockSpec((tm, D), row), pl.BlockSpec((1, D), fixed), pl.BlockSpec((1, D), fixed),
                  pl.BlockSpec((E, D), fixed), pl.BlockSpec((E, 1), fixed)],
        out_specs=(pl.BlockSpec((tm, D), row), pl.BlockSpec((tm, D), row), pl.BlockSpec((tm, D // 2), row),
                   pl.BlockSpec((None, TOP_K, tm), lambda i: (i, 0, 0)),
                   pl.BlockSpec((None, TOP_K, tm), lambda i: (i, 0, 0)),
                   pl.BlockSpec((tm, TOP_K), row),
                   pl.BlockSpec((E, 1), fixed)),
        scratch_shapes=[pltpu.VMEM((E, 1), F32)],
        compiler_params=_cparams(1, 40),
        name="ln1_router",
    )(pre, g.reshape(1, D), b.reshape(1, D), w_router.T, router_bias.reshape(E, 1))


def _glu_block(x, wbf_ref):
    half = x.shape[1]
    f = wbf_ref.shape[1] // 2
    lo, hi = _unpack_halves(x)
    gu = (jnp.dot(lo.astype(BF16), wbf_ref[:half], preferred_element_type=F32)
          + jnp.dot(hi.astype(BF16), wbf_ref[half:], preferred_element_type=F32))
    g = gu[:, :f]
    return (g * _sigmoid(g) * gu[:, f:]).astype(BF16)


def _down_block(h, wbf_ref):
    y = jnp.dot(h, wbf_ref[...], preferred_element_type=F32)
    half = y.shape[1] // 2
    return _pack_halves(y[:, :half], y[:, half:])


def _expert_kernel(be_ref, nxt_ref, nused_ref, x_ref, w_hbm, o_ref, wbf_ref, stage_ref, wsem, *, block_fn):
    step = pl.program_id(0)
    nb = be_ref.shape[0]
    n_used = nused_ref[0]
    sub = x_ref.shape[0] // MOE_BLOCK

    def fetch(e_):
        return pltpu.make_async_copy(w_hbm.at[e_], stage_ref, wsem)

    @pl.when(step == 0)
    def _():
        fetch(be_ref[0]).start()

    for s in range(sub):
        b = step * sub + s
        rows = slice(s * MOE_BLOCK, (s + 1) * MOE_BLOCK)
        e = be_ref[b]
        run_start = jnp.logical_or(b == 0, e != be_ref[jnp.maximum(b - 1, 0)])

        @pl.when(jnp.logical_and(run_start, b < n_used))
        def _():
            fetch(e).wait()
            wbf_ref[...] = stage_ref[...].astype(BF16)
            b_next = nxt_ref[b]

            @pl.when(b_next < n_used)
            def _():
                fetch(be_ref[jnp.minimum(b_next, nb - 1)]).start()

        @pl.when(b < n_used)
        def _():
            o_ref[rows, :] = block_fn(x_ref[rows, :], wbf_ref)

        @pl.when(b >= n_used)
        def _():
            o_ref[rows, :] = jnp.zeros((MOE_BLOCK, o_ref.shape[1]), o_ref.dtype)


def _expert_matmul(x, w_e, blk_e, nxt_blk, n_used, block_fn, n_out, out_dtype, name, sub):
    P, kx = x.shape
    E, kw, nw = w_e.shape
    nb = P // MOE_BLOCK
    assert nb % sub == 0
    rows = sub * MOE_BLOCK
    xmap = lambda s, be, nx, nu: (jnp.minimum(s, (nu[0] - 1) // sub), 0)
    return pl.pallas_call(
        functools.partial(_expert_kernel, block_fn=block_fn),
        out_shape=jax.ShapeDtypeStruct((P, n_out), out_dtype),
        grid_spec=pltpu.PrefetchScalarGridSpec(
            num_scalar_prefetch=3,
            grid=(nb // sub,),
            in_specs=[pl.BlockSpec((rows, kx), xmap), pl.BlockSpec(memory_space=pl.ANY)],
            out_specs=pl.BlockSpec((rows, n_out), lambda s, be, nx, nu: (s, 0)),
            scratch_shapes=[pltpu.VMEM((kw, nw), BF16), pltpu.VMEM((kw, nw), F32),
                            pltpu.SemaphoreType.DMA]),
        compiler_params=_cparams(1, 56),
        name=name,
    )(blk_e, nxt_blk, n_used, x, w_e)


def _combine_kernel(dest_hbm, y_hbm, w_ref, pre_ref, g_ref, b_ref, o_ref, idx_smem, buf_ref, isem, gsem):
    i = pl.program_id(0)
    n = pl.num_programs(0)
    tc = o_ref.shape[0]
    half = buf_ref.shape[3]
    n_idx = TOP_K * tc

    def idx_copy(tile, slot):
        return pltpu.make_async_copy(dest_hbm.at[tile], idx_smem.at[pl.ds(slot * n_idx, n_idx)],
                                     isem.at[slot])

    def issue_tile(slot):
        base = slot * n_idx

        def body(r8, carry):
            for s in range(8):
                r = r8 * 8 + s
                for k in range(TOP_K):
                    d = idx_smem[base + k * tc + r]
                    pltpu.make_async_copy(y_hbm.at[pl.ds(d, 1)], buf_ref.at[slot, k, pl.ds(r, 1)],
                                          gsem.at[slot]).start(priority=k % 2)
            return carry
        lax.fori_loop(0, tc // 8, body, 0)

    slot = i % 2

    @pl.when(i == 0)
    def _():
        idx_copy(0, 0).start()
        idx_copy(0, 0).wait()
        issue_tile(0)

        @pl.when(n > 1)
        def _():
            idx_copy(1, 1).start()

    @pl.when(i + 1 < n)
    def _():
        idx_copy(i + 1, 1 - slot).wait()
        for sl in range(2):
            @pl.when(slot == 1 - sl)
            def _():
                issue_tile(sl)

    @pl.when(i + 2 < n)
    def _():
        idx_copy(i + 2, slot).start()

    for k in range(TOP_K):
        pltpu.make_async_copy(buf_ref.at[slot, k], buf_ref.at[slot, k], gsem.at[slot]).wait()

    w = w_ref[...]
    acc_lo = jnp.zeros((tc, half), F32)
    acc_hi = jnp.zeros((tc, half), F32)
    for k in range(TOP_K):
        lo, hi = _unpack_halves(buf_ref[slot, k])
        wk = w[:, k:k + 1]
        acc_lo = acc_lo + wk * lo
        acc_hi = acc_hi + wk * hi
    y_lo = pre_ref[:, :half] + acc_lo
    y_hi = pre_ref[:, half:] + acc_hi
    inv_d = 1.0 / (2 * half)
    mu = (jnp.sum(y_lo, axis=-1, keepdims=True) + jnp.sum(y_hi, axis=-1, keepdims=True)) * inv_d
    c_lo = y_lo - mu
    c_hi = y_hi - mu
    var = (jnp.sum(c_lo * c_lo, axis=-1, keepdims=True)
           + jnp.sum(c_hi * c_hi, axis=-1, keepdims=True)) * inv_d
    rstd = lax.rsqrt(var + LN_EPS)
    o_ref[:, :half] = c_lo * rstd * g_ref[:, :half] + b_ref[:, :half]
    o_ref[:, half:] = c_hi * rstd * g_ref[:, half:] + b_ref[:, half:]


def _combine_ln(y_packed, dest_tiles, w_col, pre, g, b):
    T, D = pre.shape
    half = y_packed.shape[1]
    nt, n_idx = dest_tiles.shape
    tc = n_idx // TOP_K
    return pl.pallas_call(
        _combine_kernel,
        out_shape=jax.ShapeDtypeStruct((T, D), F32),
        grid=(nt,),
        in_specs=[pl.BlockSpec(memory_space=pl.ANY), pl.BlockSpec(memory_space=pl.ANY),
                  pl.BlockSpec((tc, TOP_K), lambda i: (i, 0)),
                  pl.BlockSpec((tc, D), lambda i: (i, 0)),
                  pl.BlockSpec((1, D), lambda i: (0, 0)),
                  pl.BlockSpec((1, D), lambda i: (0, 0))],
        out_specs=pl.BlockSpec((tc, D), lambda i: (i, 0)),
        scratch_shapes=[pltpu.SMEM((2 * n_idx,), I32),
                        pltpu.VMEM((2, TOP_K, tc, half), U32),
                        pltpu.SemaphoreType.DMA((2,)), pltpu.SemaphoreType.DMA((2,))],
        compiler_params=_cparams(1, 58),
        name="combine_ln2",
    )(dest_tiles, y_packed, w_col, pre, g.reshape(1, D), b.reshape(1, D))


def _glu_up_kernel(h_ref, wg_ref, wu_ref, o_ref, wbf_ref):
    tn = wg_ref.shape[1]

    @pl.when(pl.program_id(1) == 0)
    def _():
        wbf_ref[:, :tn] = wg_ref[...].astype(BF16)
        wbf_ref[:, tn:] = wu_ref[...].astype(BF16)

    gu = jnp.dot(h_ref[...], wbf_ref[...], preferred_element_type=F32)
    g = gu[:, :tn]
    o_ref[...] = (g * _sigmoid(g) * gu[:, tn:]).astype(o_ref.dtype)


def _glu_up(h_bf, w_gu, *, tm=512, tn=384):
    T, D = h_bf.shape
    f = w_gu.shape[1] // 2
    tm = min(tm, T)
    nc = f // tn
    return pl.pallas_call(
        _glu_up_kernel,
        out_shape=jax.ShapeDtypeStruct((T, f), BF16),
        grid=(nc, T // tm),
        in_specs=[pl.BlockSpec((tm, D), lambda c, i: (i, 0)),
                  pl.BlockSpec((D, tn), lambda c, i: (0, c)),
                  pl.BlockSpec((D, tn), lambda c, i: (0, nc + c))],
        out_specs=pl.BlockSpec((tm, tn), lambda c, i: (i, c)),
        scratch_shapes=[pltpu.VMEM((D, 2 * tn), BF16)],
        compiler_params=_cparams(2, 56),
        name="shared_up",
    )(h_bf, w_gu, w_gu)


def _dense_tail_kernel(fill_ref, dest_hbm, x_ref, hb_ref, hj_ref, s_ref, p_ref, wd_ref, wg_ref, bg_ref,
                       wp_ref, o_ref, xs_hbm, wdbf_ref, wgbf_ref, wpbf_ref, idx_smem, zero_ref,
                       isem, ssem, zsem, *, alpha, k_per_step):
    j = pl.program_id(0)
    i = pl.program_id(1)
    ni = pl.num_programs(1)
    step = j * ni + i
    tm = x_ref.shape[0]
    n_idx = k_per_step * tm
    R = zero_ref.shape[0]

    def idx_copies(jj, ii, slot):
        return [pltpu.make_async_copy(dest_hbm.at[jj * k_per_step + kk, pl.ds(pl.multiple_of(ii * tm, 128), tm)],
                                      idx_smem.at[pl.ds(slot * n_idx + kk * tm, tm)], isem.at[slot])
                for kk in range(k_per_step)]

    @pl.when(step == 0)
    def _():
        for cp in idx_copies(0, 0, 0):
            cp.start()
        zero_ref[...] = jnp.zeros_like(zero_ref)

        def fill(f):
            return pltpu.make_async_copy(zero_ref, xs_hbm.at[pl.ds(fill_ref[f] * R, R)], zsem)

        def start(f, carry):
            @pl.when(fill_ref[f] >= 0)
            def _():
                fill(f).start()
            return carry
        lax.fori_loop(0, fill_ref.shape[0], start, 0)

        def wait(f, carry):
            @pl.when(fill_ref[f] >= 0)
            def _():
                fill(f).wait()
            return carry
        lax.fori_loop(0, fill_ref.shape[0], wait, 0)

    @pl.when(i == 0)
    def _():
        wdbf_ref[...] = wd_ref[...].astype(BF16)
        wgbf_ref[...] = wg_ref[...].astype(BF16)
        wpbf_ref[...] = wp_ref[...].astype(BF16)

    slot = step % 2
    for cp in idx_copies(j, i, slot):
        cp.wait()

    @pl.when(step + 1 < pl.num_programs(0) * ni)
    def _():
        nxt = step + 1
        for cp in idx_copies(nxt // ni, nxt % ni, 1 - slot):
            cp.start()

    base = slot * n_idx
    for kk in range(k_per_step):
        for r in range(tm):
            d = idx_smem[base + kk * tm + r]
            pltpu.make_async_copy(x_ref.at[pl.ds(r, 1)], xs_hbm.at[pl.ds(d, 1)], ssem).start(priority=r % 2)

    shared = jnp.dot(s_ref[...], wdbf_ref[...], preferred_element_type=F32)
    gate = _sigmoid(jnp.dot(hb_ref[...], wgbf_ref[...], preferred_element_type=F32) + bg_ref[...])
    proj = jnp.dot(p_ref[...].astype(BF16), wpbf_ref[...], preferred_element_type=F32)
    o_ref[...] = alpha * hj_ref[...] + shared + gate * proj

    for kk in range(k_per_step):
        pltpu.make_async_copy(x_ref, x_ref, ssem).wait()


def _dense_tail(h1_bf, h1, h1_packed, dest_kt, fill_blocks, n_rows, s_mid, p, w_down_s, w_gate, b_gate,
                w_proj, alpha, *, tm=512, tn=512):
    T, D = h1.shape
    f = s_mid.shape[1]
    dp = p.shape[1]
    W = h1_packed.shape[1]
    tm = min(tm, T)
    nj = D // tn
    assert TOP_K % nj == 0 and tm % 128 == 0
    k_per_step = TOP_K // nj
    row = lambda j, i, fb: (i, 0)
    col = lambda j, i, fb: (0, j)
    tile = lambda j, i, fb: (i, j)
    return pl.pallas_call(
        functools.partial(_dense_tail_kernel, alpha=alpha, k_per_step=k_per_step),
        out_shape=(jax.ShapeDtypeStruct((T, D), F32), jax.ShapeDtypeStruct((n_rows, W), h1_packed.dtype)),
        grid_spec=pltpu.PrefetchScalarGridSpec(
            num_scalar_prefetch=1,
            grid=(nj, T // tm),
            in_specs=[pl.BlockSpec(memory_space=pl.ANY),
                      pl.BlockSpec((tm, W), row),
                      pl.BlockSpec((tm, D), row),
                      pl.BlockSpec((tm, tn), tile),
                      pl.BlockSpec((tm, f), row),
                      pl.BlockSpec((tm, dp), row),
                      pl.BlockSpec((f, tn), col),
                      pl.BlockSpec((D, tn), col),
                      pl.BlockSpec((1, tn), col),
                      pl.BlockSpec((dp, tn), col)],
            out_specs=(pl.BlockSpec((tm, tn), tile), pl.BlockSpec(memory_space=pl.ANY)),
            scratch_shapes=[pltpu.VMEM((f, tn), BF16), pltpu.VMEM((D, tn), BF16), pltpu.VMEM((dp, tn), BF16),
                            pltpu.SMEM((2 * k_per_step * tm,), I32), pltpu.VMEM((MOE_BLOCK, W), h1_packed.dtype),
                            pltpu.SemaphoreType.DMA((2,)), pltpu.SemaphoreType.DMA, pltpu.SemaphoreType.DMA]),
        compiler_params=pltpu.CompilerParams(dimension_semantics=("arbitrary", "arbitrary"),
                                             vmem_limit_bytes=58 * 1024 * 1024, has_side_effects=True),
        name="dense_tail_dispatch",
    )(fill_blocks, dest_kt, h1_packed, h1_bf, h1, s_mid, p, w_down_s, w_gate, b_gate.reshape(1, D), w_proj)


def _dispatch_tables(idx, rank, counts):
    E = counts.shape[0]
    P = idx.size + E * MOE_BLOCK
    nb = P // MOE_BLOCK
    padded = (counts + MOE_BLOCK - 1) // MOE_BLOCK * MOE_BLOCK
    pend = jnp.cumsum(padded)
    pstart = pend - padded
    sel = idx[None] == jnp.arange(E, dtype=I32).reshape(E, 1, 1, 1)
    dest = rank + jnp.sum(jnp.where(sel, pstart.reshape(E, 1, 1, 1), 0), axis=0)
    blk_row = jnp.arange(nb, dtype=I32) * MOE_BLOCK
    blk_e = jnp.minimum(jnp.sum(pend[None, :] <= blk_row[:, None], axis=1), E - 1).astype(I32)
    n_used = pend[-1] // MOE_BLOCK
    last_blk = jnp.where(counts > 0, pend // MOE_BLOCK - 1, -1)
    tail_blk = n_used + jnp.arange(E, dtype=I32)
    tail_blk = jnp.where(tail_blk < nb, tail_blk, -1)
    fill = jnp.concatenate([last_blk, tail_blk]).astype(I32)
    nxt_blk = jnp.take(pend // MOE_BLOCK, blk_e).astype(I32)
    return dest.astype(I32), blk_e, nxt_blk, n_used.astype(I32).reshape(1), fill, P


def _layer(h0_f, h0_bf, p_l, w_in, b_in, conv_w, conv_b, mh_norm_g, w_conv_out, w_mlstm_out,
           w_mix_out, ln1_g, ln1_b, w_router, router_bias, w_gu_e, w_down_e, w_gu_s, w_down_s,
           w_ple_gate, b_ple_gate, w_ple_proj, ln2_g, ln2_b, alpha, batch, seq):
    T, D = h0_f.shape
    d_conv = conv_w.shape[1]
    d_v = mh_norm_g.shape[0]
    d_qk = d_v // 2
    n_if = 2 * N_HEADS
    c_qk = 3 * d_conv
    c_if = c_qk + 2 * d_qk + 2 * d_v
    c_gate = c_if + n_if
    b2d = b_in.reshape(1, -1)

    w_t = jnp.swapaxes(w_in, 0, 1)
    ya_pre = _conv_branch(h0_bf, w_t, b2d, conv_w, conv_b.reshape(1, -1), seq, d_conv)
    qkvo = _proj(h0_bf, w_t, b2d, c_qk, 2 * d_qk + 2 * d_v)
    if_col, if_row = _if_gates(h0_bf, w_t[c_if:c_gate], b_in[c_if:c_gate])
    gates = _proj(h0_bf, w_t, b2d, c_gate, 2 * D, act="sigmoid")
    yb_pre = _mlstm(qkvo, if_col, if_row, mh_norm_g.reshape(1, -1), batch, seq, d_qk, d_v)
    u = _merge(ya_pre, yb_pre, w_conv_out, w_mlstm_out, gates)
    pre1 = _mix(u, w_mix_out, h0_f, alpha)

    rt = min(ROUTE_TILE, T)
    h1, h1_bf, h1_packed, idx, rank, w_col, counts = _ln1_router(pre1, ln1_g, ln1_b, w_router,
                                                                 router_bias, rt)
    dest, blk_e, nxt_blk, n_used, fill, n_rows = _dispatch_tables(idx, rank, counts[:, 0].astype(I32))
    dest_kt = dest.transpose(1, 0, 2).reshape(TOP_K, T)
    ct = min(COMBINE_TILE, T)
    dest_tiles = dest_kt.reshape(TOP_K, T // ct, ct).transpose(1, 0, 2).reshape(T // ct, TOP_K * ct)

    s_mid = _glu_up(h1_bf, w_gu_s)
    pre2, xs = _dense_tail(h1_bf, h1, h1_packed, dest_kt, fill, n_rows, s_mid, p_l, w_down_s,
                           w_ple_gate, b_ple_gate, w_ple_proj, alpha)

    hmid = _expert_matmul(xs, w_gu_e, blk_e, nxt_blk, n_used, _glu_block, w_gu_e.shape[2] // 2, BF16,
                          "expert_up", EXPERT_UP_SUB)
    y_packed = _expert_matmul(hmid, w_down_e, blk_e, nxt_blk, n_used, _down_block, D // 2, U32,
                              "expert_down", EXPERT_DOWN_SUB)
    return _combine_ln(y_packed, dest_tiles, w_col, pre2, ln2_g, ln2_b)


def kernel(x, p, ln_in_g, ln_in_b, w_in, b_in, conv_w, conv_b, mh_norm_g, w_conv_out, w_mlstm_out,
           w_mix_out, ln1_g, ln1_b, w_router, router_bias, w_gu_e, w_down_e, w_gu_s, w_down_s,
           w_ple_gate, b_ple_gate, w_ple_proj, ln2_g, ln2_b):
    B, S, D = x.shape
    depth = w_in.shape[0]
    alpha = (2 * depth) ** 0.25
    T = B * S
    h_f, h_bf = _ln_in(x.reshape(T, D), ln_in_g, ln_in_b)
    for l in range(depth):
        h_f = _layer(h_f, h_bf, p[l].reshape(T, -1), w_in[l], b_in[l], conv_w[l], conv_b[l],
                     mh_norm_g[l], w_conv_out[l], w_mlstm_out[l], w_mix_out[l], ln1_g[l], ln1_b[l],
                     w_router[l], router_bias[l], w_gu_e[l], w_down_e[l], w_gu_s[l], w_down_s[l],
                     w_ple_gate[l], b_ple_gate[l], w_ple_proj[l], ln2_g[l], ln2_b[l], alpha, B, S)
        if l + 1 < depth:
            h_bf = h_f.astype(BF16)
    return h_f.reshape(B, S, D)
```

```python
import functools

import jax
import jax.numpy as jnp
from jax import lax
from jax.experimental import pallas as pl
from jax.experimental.pallas import tpu as pltpu

F32 = jnp.float32
BF16 = jnp.bfloat16
U32 = jnp.uint32
I32 = jnp.int32

N_HEADS = 8
TOP_K = 8
N_GROUPS = 8
TOP_GROUPS = 4
ROUTE_SCALE = 2.5
MOE_BLOCK = 256
LN_EPS = 1e-5
CONV_W = 3
MLSTM_CHUNK = 128
MLSTM_STEP = 128
ROUTE_TILE = 256
COMBINE_TILE = 128
EXPERT_UP_SUB = 2
EXPERT_DOWN_SUB = 4
NEG_BIG = -1e30
HI_MASK = 0xFFFF0000
V7X_VMEM_BYTES = 64 * 1024 * 1024


def _cparams(n_axes, vmem_mib):
    assert vmem_mib * 1024 * 1024 <= V7X_VMEM_BYTES
    return pltpu.CompilerParams(dimension_semantics=("arbitrary",) * n_axes,
                                vmem_limit_bytes=vmem_mib * 1024 * 1024)


def _ln_rows(x, g, b):
    mu = jnp.mean(x, axis=-1, keepdims=True)
    xc = x - mu
    var = jnp.mean(xc * xc, axis=-1, keepdims=True)
    return xc * lax.rsqrt(var + LN_EPS) * g + b


def _pack_halves(lo, hi):
    lo = pltpu.bitcast(lo.astype(BF16).astype(F32), U32)
    hi = pltpu.bitcast(hi.astype(BF16).astype(F32), U32)
    return (hi & jnp.uint32(HI_MASK)) | (lo >> jnp.uint32(16))


def _sigmoid(x):
    return 0.5 * jnp.tanh(0.5 * x) + 0.5


def _unpack_halves(w):
    lo = pltpu.bitcast(w << jnp.uint32(16), F32)
    hi = pltpu.bitcast(w & jnp.uint32(HI_MASK), F32)
    return lo, hi


def _ln_in_kernel(x_ref, g_ref, b_ref, of_ref, ob_ref):
    y = _ln_rows(x_ref[...], g_ref[...], b_ref[...])
    of_ref[...] = y
    ob_ref[...] = y.astype(BF16)


def _ln_in(x2, g, b, tm=256):
    T, D = x2.shape
    return pl.pallas_call(
        _ln_in_kernel,
        out_shape=(jax.ShapeDtypeStruct((T, D), F32), jax.ShapeDtypeStruct((T, D), BF16)),
        grid=(T // tm,),
        in_specs=[pl.BlockSpec((tm, D), lambda i: (i, 0)),
                  pl.BlockSpec((1, D), lambda i: (0, 0)),
                  pl.BlockSpec((1, D), lambda i: (0, 0))],
        out_specs=(pl.BlockSpec((tm, D), lambda i: (i, 0)),
                   pl.BlockSpec((tm, D), lambda i: (i, 0))),
        compiler_params=_cparams(1, 40),
        name="ln_in",
    )(x2, g.reshape(1, D), b.reshape(1, D))


_NT = (((1,), (1,)), ((), ()))


def _proj_kernel(x_ref, w_ref, b_ref, o_ref, wbf_ref, *, act):
    @pl.when(pl.program_id(1) == 0)
    def _():
        wbf_ref[...] = w_ref[...].astype(BF16)

    acc = lax.dot_general(x_ref[...], wbf_ref[...], _NT, preferred_element_type=F32) + b_ref[...]
    if act == "sigmoid":
        acc = _sigmoid(acc)
    o_ref[...] = acc.astype(o_ref.dtype)


def _proj(x, w_t, bias2d, row0, n_cols, *, act=None, tm=1024, tn=None, out_dtype=BF16):
    T, K = x.shape
    tm = min(tm, T)
    if tn is None:
        tn = next(t for t in (768, 512, 256, 128) if n_cols % t == 0)
    assert n_cols % tn == 0 and T % tm == 0 and row0 % 8 == 0
    if row0 % tn == 0:
        jb = row0 // tn
        w_spec = pl.BlockSpec((tn, K), lambda j, i: (jb + j, 0))
        b_spec = pl.BlockSpec((1, tn), lambda j, i: (0, jb + j))
    else:
        w_spec = pl.BlockSpec((pl.Element(tn), pl.Element(K)),
                              lambda j, i: (pl.multiple_of(row0 + j * tn, 8), 0))
        bias2d = bias2d[:, row0:row0 + n_cols]
        b_spec = pl.BlockSpec((1, tn), lambda j, i: (0, j))
    return pl.pallas_call(
        functools.partial(_proj_kernel, act=act),
        out_shape=jax.ShapeDtypeStruct((T, n_cols), out_dtype),
        grid=(n_cols // tn, T // tm),
        in_specs=[pl.BlockSpec((tm, K), lambda j, i: (i, 0)), w_spec, b_spec],
        out_specs=pl.BlockSpec((tm, tn), lambda j, i: (i, j)),
        scratch_shapes=[pltpu.VMEM((tn, K), BF16)],
        compiler_params=_cparams(2, 60),
        name="proj_" + (act or "lin"),
    )(x, w_t, bias2d)


def _conv_kernel(x_ref, wh_ref, wc_ref, wb_ref, bh_ref, bc_ref, bb_ref, cw_ref, cb_ref,
                 o_ref, wbf_ref, zprev_ref, *, tiles_per_seq):
    i = pl.program_id(1)

    @pl.when(i == 0)
    def _():
        wbf_ref[0] = wh_ref[...].astype(BF16)
        wbf_ref[1] = wc_ref[...].astype(BF16)
        wbf_ref[2] = wb_ref[...].astype(BF16)

    @pl.when(i % tiles_per_seq == 0)
    def _():
        zprev_ref[...] = jnp.zeros_like(zprev_ref)

    x = x_ref[...]
    ha = lax.dot_general(x, wbf_ref[0], _NT, preferred_element_type=F32) + bh_ref[...]
    ca = lax.dot_general(x, wbf_ref[1], _NT, preferred_element_type=F32) + bc_ref[...]
    ba = lax.dot_general(x, wbf_ref[2], _NT, preferred_element_type=F32) + bb_ref[...]
    z = ca * ha
    tm = z.shape[0]
    prev = zprev_ref[...]
    row8 = lax.broadcasted_iota(I32, prev.shape, 0)
    z1 = pltpu.roll(z, 1, 0)
    z2 = pltpu.roll(z, 2, 0)
    p1 = pltpu.roll(prev, 1, 0)
    p2 = pltpu.roll(prev, 2, 0)
    z1 = jnp.concatenate([jnp.where(row8 < 1, p1, z1[:8]), z1[8:]], axis=0)
    z2 = jnp.concatenate([jnp.where(row8 < 2, p2, z2[:8]), z2[8:]], axis=0)
    cw = cw_ref[...]
    y = cw[0:1] * z2 + cw[1:2] * z1 + cw[2:3] * z + cb_ref[...]
    o_ref[...] = (ba * y).astype(o_ref.dtype)
    zprev_ref[...] = z[tm - 8:]


def _conv_branch(x, w_t, b2d, conv_w, conv_b2d, seq, d_conv, *, tm=1024, tn=256):
    T, K = x.shape
    tm = min(tm, seq)
    assert seq % tm == 0 and d_conv % tn == 0 and tm % 8 == 0
    nb = d_conv // tn
    wspec = lambda g: pl.BlockSpec((tn, K), lambda j, i: (g * nb + j, 0))
    bspec = lambda g: pl.BlockSpec((1, tn), lambda j, i: (0, g * nb + j))
    return pl.pallas_call(
        functools.partial(_conv_kernel, tiles_per_seq=seq // tm),
        out_shape=jax.ShapeDtypeStruct((T, d_conv), BF16),
        grid=(nb, T // tm),
        in_specs=[pl.BlockSpec((tm, K), lambda j, i: (i, 0)),
                  wspec(0), wspec(1), wspec(2), bspec(0), bspec(1), bspec(2),
                  pl.BlockSpec((CONV_W, tn), lambda j, i: (0, j)),
                  pl.BlockSpec((1, tn), lambda j, i: (0, j))],
        out_specs=pl.BlockSpec((tm, tn), lambda j, i: (i, j)),
        scratch_shapes=[pltpu.VMEM((3, tn, K), BF16), pltpu.VMEM((8, tn), F32)],
        compiler_params=_cparams(2, 56),
        name="conv_branch",
    )(x, w_t, w_t, w_t, b2d, b2d, b2d, conv_w, conv_b2d)


def _if_kernel(x_ref, wt_ref, bc_ref, br_ref, oc_ref, or_ref):
    x = x_ref[...]
    w = wt_ref[...].astype(BF16)
    oc_ref[...] = lax.dot_general(x, w, _NT, preferred_element_type=F32) + bc_ref[...]
    or_ref[...] = lax.dot_general(w, x, _NT, preferred_element_type=F32) + br_ref[...]


def _if_gates(x, w_if_t, b_if, tm=1024):
    T, K = x.shape
    tm = min(tm, T)
    n = w_if_t.shape[0]
    return pl.pallas_call(
        _if_kernel,
        out_shape=(jax.ShapeDtypeStruct((T, n), F32), jax.ShapeDtypeStruct((n, T), F32)),
        grid=(T // tm,),
        in_specs=[pl.BlockSpec((tm, K), lambda i: (i, 0)),
                  pl.BlockSpec((n, K), lambda i: (0, 0)),
                  pl.BlockSpec((1, n), lambda i: (0, 0)),
                  pl.BlockSpec((n, 1), lambda i: (0, 0))],
        out_specs=(pl.BlockSpec((tm, n), lambda i: (i, 0)),
                   pl.BlockSpec((n, tm), lambda i: (0, i))),
        compiler_params=_cparams(1, 32),
        name="if_gates",
    )(x, w_if_t, b_if.reshape(1, n), b_if.reshape(n, 1))


def _mlstm_kernel(q_ref, k_ref, v_ref, o_ref, ifc_ref, ifr_ref, g_ref, y_ref, c_ref, m_ref,
                  *, dk, dv, L):
    H = N_HEADS

    @pl.when(pl.program_id(1) == 0)
    def _():
        c_ref[...] = jnp.zeros_like(c_ref)
        m_ref[...] = jnp.zeros_like(m_ref)

    scale = dk ** -0.5
    r = lax.broadcasted_iota(I32, (L, L), 0)
    c = lax.broadcasted_iota(I32, (L, L), 1)
    causal = r >= c
    tri_lower = causal.astype(F32)
    tri_upper = (r <= c).astype(F32)
    ones_col = (lax.broadcasted_iota(I32, (L, 128), 1) == 0).astype(BF16)

    for cc in range(q_ref.shape[0] // L):
        rows = slice(cc * L, (cc + 1) * L)
        ifc = ifc_ref[rows, :]
        ifr = ifr_ref[:, rows]
        ig_c = ifc[:, 0:H]
        lf_c = jax.nn.log_sigmoid(ifc[:, H:2 * H])
        ig_r = ifr[0:H, :]
        lf_r = jax.nn.log_sigmoid(ifr[H:2 * H, :])
        bcum_c = jnp.dot(tri_lower, lf_c, preferred_element_type=F32,
                         precision=lax.Precision.HIGHEST)
        bcum_r = jnp.dot(lf_r, tri_upper, preferred_element_type=F32,
                         precision=lax.Precision.HIGHEST)
        d_c = ig_c - bcum_c
        d_r = ig_r - bcum_r

        for h in range(H):
            q = q_ref[rows, h * dk:(h + 1) * dk]
            k = k_ref[rows, h * dk:(h + 1) * dk]
            v = v_ref[rows, h * dv:(h + 1) * dv]
            v_aug = jnp.concatenate([v, ones_col], axis=1)
            bc = bcum_c[:, h:h + 1]
            m_prev = m_ref[h]
            c_prev = c_ref[h]

            qk = lax.dot_general(q, k, _NT, preferred_element_type=F32) * scale
            logd = jnp.where(causal, bc + d_r[h:h + 1, :], NEG_BIG)
            log_inter = bc + m_prev
            m_t = jnp.maximum(log_inter, jnp.max(logd, axis=1, keepdims=True))
            w_inter = jnp.exp(log_inter - m_t)
            s_mat = (qk * jnp.exp(logd - m_t)).astype(BF16)
            tot = (w_inter * jnp.dot(q, c_prev.astype(BF16), preferred_element_type=F32)
                   + jnp.dot(s_mat, v_aug, preferred_element_type=F32))
            den = jnp.maximum(jnp.abs(tot[:, dv:dv + 1]), jnp.exp(-m_t))
            hh = tot[:, :dv] / den
            mu = jnp.mean(hh, axis=-1, keepdims=True)
            hc = hh - mu
            var = jnp.mean(hc * hc, axis=-1, keepdims=True)
            hn = hc * lax.rsqrt(var + LN_EPS) * g_ref[:, h * dv:(h + 1) * dv]
            og = _sigmoid(o_ref[rows, h * dv:(h + 1) * dv].astype(F32))
            y_ref[rows, h * dv:(h + 1) * dv] = (og * hn).astype(y_ref.dtype)

            g_tot = bc[L - 1:L, :]
            a = g_tot + d_c[:, h:h + 1]
            m_new = jnp.maximum(g_tot + m_prev, jnp.max(a, axis=0, keepdims=True))
            kw = (k.astype(F32) * (jnp.exp(a - m_new) * scale)).astype(BF16)
            kv = lax.dot_general(kw, v_aug, (((0,), (0,)), ((), ())), preferred_element_type=F32)
            c_ref[h] = jnp.exp(g_tot + m_prev - m_new) * c_prev + kv
            m_ref[h] = m_new


def _mlstm(qkvo, if_col, if_row, mh_g2d, batch, seq, d_qk, d_v, L=MLSTM_CHUNK, step_rows=MLSTM_STEP):
    T = qkvo.shape[0]
    L = min(L, seq)
    R = min(max(step_rows, L), seq)
    assert seq % R == 0 and R % L == 0 and d_v == 2 * d_qk
    nc = seq // R
    dk, dv = d_qk // N_HEADS, d_v // N_HEADS
    row = lambda b, c: b * nc + c
    return pl.pallas_call(
        functools.partial(_mlstm_kernel, dk=dk, dv=dv, L=L),
        out_shape=jax.ShapeDtypeStruct((T, d_v), BF16),
        grid=(batch, nc),
        in_specs=[pl.BlockSpec((R, d_qk), lambda b, c: (row(b, c), 0)),
                  pl.BlockSpec((R, d_qk), lambda b, c: (row(b, c), 1)),
                  pl.BlockSpec((R, d_v), lambda b, c: (row(b, c), 1)),
                  pl.BlockSpec((R, d_v), lambda b, c: (row(b, c), 2)),
                  pl.BlockSpec((R, 2 * N_HEADS), lambda b, c: (row(b, c), 0)),
                  pl.BlockSpec((2 * N_HEADS, R), lambda b, c: (0, row(b, c))),
                  pl.BlockSpec((1, d_v), lambda b, c: (0, 0))],
        out_specs=pl.BlockSpec((R, d_v), lambda b, c: (row(b, c), 0)),
        scratch_shapes=[pltpu.VMEM((N_HEADS, dk, dv + 128), F32),
                        pltpu.VMEM((N_HEADS, 1, 1), F32)],
        compiler_params=_cparams(2, 40),
        name="mlstm",
    )(qkvo, qkvo, qkvo, qkvo, if_col, if_row, mh_g2d)


def _merge_kernel(a_ref, b_ref, wa_ref, wb_ref, ga_ref, gb_ref, o_ref, wabf_ref, wbbf_ref):
    @pl.when(pl.program_id(1) == 0)
    def _():
        wabf_ref[...] = wa_ref[...].astype(BF16)
        wbbf_ref[...] = wb_ref[...].astype(BF16)

    ya = jnp.dot(a_ref[...], wabf_ref[...], preferred_element_type=F32)
    yb = jnp.dot(b_ref[...], wbbf_ref[...], preferred_element_type=F32)
    u = ga_ref[...].astype(F32) * ya + gb_ref[...].astype(F32) * yb
    o_ref[...] = u.astype(o_ref.dtype)


def _merge(ya_pre, yb_pre, w_a, w_b, gates, *, tm=1024, tn=512):
    T, K = ya_pre.shape
    D = w_a.shape[1]
    tm = min(tm, T)
    nj = D // tn
    return pl.pallas_call(
        _merge_kernel,
        out_shape=jax.ShapeDtypeStruct((T, D), BF16),
        grid=(nj, T // tm),
        in_specs=[pl.BlockSpec((tm, K), lambda j, i: (i, 0)),
                  pl.BlockSpec((tm, K), lambda j, i: (i, 0)),
                  pl.BlockSpec((K, tn), lambda j, i: (0, j)),
                  pl.BlockSpec((K, tn), lambda j, i: (0, j)),
                  pl.BlockSpec((tm, tn), lambda j, i: (i, j)),
                  pl.BlockSpec((tm, tn), lambda j, i: (i, nj + j))],
        out_specs=pl.BlockSpec((tm, tn), lambda j, i: (i, j)),
        scratch_shapes=[pltpu.VMEM((K, tn), BF16), pltpu.VMEM((K, tn), BF16)],
        compiler_params=_cparams(2, 56),
        name="merge",
    )(ya_pre, yb_pre, w_a, w_b, gates, gates)


def _mix_kernel(u_ref, w_ref, h_ref, o_ref, wbf_ref, *, alpha):
    @pl.when(pl.program_id(1) == 0)
    def _():
        wbf_ref[...] = w_ref[...].astype(BF16)

    o_ref[...] = alpha * h_ref[...] + jnp.dot(u_ref[...], wbf_ref[...], preferred_element_type=F32)


def _mix(u, w, h0, alpha, *, tm=1024, tn=512):
    T, D = h0.shape
    tm = min(tm, T)
    return pl.pallas_call(
        functools.partial(_mix_kernel, alpha=alpha),
        out_shape=jax.ShapeDtypeStruct((T, D), F32),
        grid=(D // tn, T // tm),
        in_specs=[pl.BlockSpec((tm, D), lambda j, i: (i, 0)),
                  pl.BlockSpec((D, tn), lambda j, i: (0, j)),
                  pl.BlockSpec((tm, tn), lambda j, i: (i, j))],
        out_specs=pl.BlockSpec((tm, tn), lambda j, i: (i, j)),
        scratch_shapes=[pltpu.VMEM((D, tn), BF16)],
        compiler_params=_cparams(2, 56),
        name="mix_out",
    )(u, w, h0)


def _ln1_router_kernel(pre_ref, g_ref, b_ref, wr_ref, rb_ref, hf_ref, hb_ref, hp_ref,
                       idx_ref, rank_ref, wcol_ref, cnt_ref, carry_ref):
    i = pl.program_id(0)
    E = wr_ref.shape[0]
    tm, D = pre_ref.shape
    G, M = N_GROUPS, E // N_GROUPS

    @pl.when(i == 0)
    def _():
        carry_ref[...] = jnp.zeros_like(carry_ref)

    h = _ln_rows(pre_ref[...], g_ref[...], b_ref[...])
    hf_ref[...] = h
    hb_ref[...] = h.astype(BF16)
    hp_ref[...] = _pack_halves(h[:, :D // 2], h[:, D // 2:])

    logits = lax.dot_general(wr_ref[...], h, (((1,), (1,)), ((), ())),
                             preferred_element_type=F32, precision=lax.Precision.HIGHEST)
    scores = jax.nn.sigmoid(logits)
    scores3 = scores.reshape(G, M, tm)
    sel3 = (scores + rb_ref[...]).reshape(G, M, tm)
    midx = lax.broadcasted_iota(I32, (G, M, tm), 1)
    gidx3 = lax.broadcasted_iota(I32, (G, M, tm), 0)
    eidx = gidx3 * M + midx
    gidx = lax.broadcasted_iota(I32, (G, 1, tm), 0)
    neg_inf = -jnp.inf

    top1 = jnp.max(sel3, axis=1, keepdims=True)
    first1 = jnp.min(jnp.where(sel3 == top1, midx, M), axis=1, keepdims=True)
    top2 = jnp.max(jnp.where(midx == first1, neg_inf, sel3), axis=1, keepdims=True)
    gs = top1 + top2
    gkeep = jnp.zeros((G, 1, tm), F32)
    for _ in range(TOP_GROUPS):
        mx = jnp.max(gs, axis=0, keepdims=True)
        first = jnp.min(jnp.where(gs == mx, gidx, G), axis=0, keepdims=True)
        hit = gidx == first
        gkeep = jnp.where(hit, 1.0, gkeep)
        gs = jnp.where(hit, neg_inf, gs)
    selm = jnp.where(gkeep > 0.5, sel3, neg_inf)

    idx_rows, sc_rows = [], []
    chosen = jnp.zeros((G, M, tm), F32)
    for _ in range(TOP_K):
        mx = jnp.max(jnp.max(selm, axis=1, keepdims=True), axis=0, keepdims=True)
        first = jnp.min(jnp.min(jnp.where(selm == mx, eidx, E), axis=1, keepdims=True),
                        axis=0, keepdims=True)
        hit = eidx == first
        sc = jnp.sum(jnp.sum(jnp.where(hit, scores3, 0.0), axis=1, keepdims=True),
                     axis=0, keepdims=True)
        chosen = jnp.where(hit, 1.0, chosen)
        selm = jnp.where(hit, neg_inf, selm)
        idx_rows.append(first)
        sc_rows.append(sc)
    denom = sc_rows[0]
    for s in sc_rows[1:]:
        denom = denom + s

    tr = lax.broadcasted_iota(I32, (tm, tm), 0)
    tc = lax.broadcasted_iota(I32, (tm, tm), 1)
    before = (tr < tc).astype(BF16)
    chosen2 = chosen.reshape(E, tm)
    rank2 = jnp.dot(chosen2.astype(BF16), before, preferred_element_type=F32) + carry_ref[...]
    rank3 = rank2.reshape(G, M, tm)
    w_rows = []
    for k in range(TOP_K):
        hit = eidx == idx_rows[k]
        rk = jnp.sum(jnp.sum(jnp.where(hit, rank3, 0.0), axis=1, keepdims=True),
                     axis=0, keepdims=True)
        idx_ref[k:k + 1, :] = idx_rows[k].reshape(1, tm)
        rank_ref[k:k + 1, :] = rk.reshape(1, tm).astype(I32)
        w_rows.append((sc_rows[k] / denom * ROUTE_SCALE).reshape(1, tm))
    wcol_ref[...] = jnp.concatenate(w_rows, axis=0).T
    carry_ref[...] = carry_ref[...] + jnp.sum(chosen2, axis=1, keepdims=True)

    @pl.when(i == pl.num_programs(0) - 1)
    def _():
        cnt_ref[...] = carry_ref[...]


def _ln1_router(pre, g, b, w_router, router_bias, tm):
    T, D = pre.shape
    E = w_router.shape[1]
    nt = T // tm
    row = lambda i: (i, 0)
    fixed = lambda i: (0, 0)
    return pl.pallas_call(
        _ln1_router_kernel,
        out_shape=(jax.ShapeDtypeStruct((T, D), F32), jax.ShapeDtypeStruct((T, D), BF16),
                   jax.ShapeDtypeStruct((T, D // 2), U32),
                   jax.ShapeDtypeStruct((nt, TOP_K, tm), I32), jax.ShapeDtypeStruct((nt, TOP_K, tm), I32),
                   jax.ShapeDtypeStruct((T, TOP_K), F32), jax.ShapeDtypeStruct((E, 1), F32)),
        grid=(nt,),
        in_specs=[pl.BlockSpec((tm, D), row), pl.BlockSpec((1, D), fixed), pl.BlockSpec((1, D), fixed),
                  pl.BlockSpec((E, D), fixed), pl.BlockSpec((E, 1), fixed)],
        out_specs=(pl.BlockSpec((tm, D), row), pl.BlockSpec((tm, D), row), pl.BlockSpec((tm, D // 2), row),
                   pl.BlockSpec((None, TOP_K, tm), lambda i: (i, 0, 0)),
                   pl.BlockSpec((None, TOP_K, tm), lambda i: (i, 0, 0)),
                   pl.BlockSpec((tm, TOP_K), row),
                   pl.BlockSpec((E, 1), fixed)),
        scratch_shapes=[pltpu.VMEM((E, 1), F32)],
        compiler_params=_cparams(1, 40),
        name="ln1_router",
    )(pre, g.reshape(1, D), b.reshape(1, D), w_router.T, router_bias.reshape(E, 1))


def _glu_block(x, wbf_ref):
    half = x.shape[1]
    f = wbf_ref.shape[1] // 2
    lo, hi = _unpack_halves(x)
    gu = (jnp.dot(lo.astype(BF16), wbf_ref[:half], preferred_element_type=F32)
          + jnp.dot(hi.astype(BF16), wbf_ref[half:], preferred_element_type=F32))
    g = gu[:, :f]
    return (g * _sigmoid(g) * gu[:, f:]).astype(BF16)


def _down_block(h, wbf_ref):
    y = jnp.dot(h, wbf_ref[...], preferred_element_type=F32)
    half = y.shape[1] // 2
    return _pack_halves(y[:, :half], y[:, half:])


def _expert_kernel(be_ref, nxt_ref, nused_ref, x_ref, w_hbm, o_ref, wbf_ref, stage_ref, wsem, *, block_fn):
    step = pl.program_id(0)
    nb = be_ref.shape[0]
    n_used = nused_ref[0]
    sub = x_ref.shape[0] // MOE_BLOCK

    def fetch(e_):
        return pltpu.make_async_copy(w_hbm.at[e_], stage_ref, wsem)

    @pl.when(step == 0)
    def _():
        fetch(be_ref[0]).start()

    for s in range(sub):
        b = step * sub + s
        rows = slice(s * MOE_BLOCK, (s + 1) * MOE_BLOCK)
        e = be_ref[b]
        run_start = jnp.logical_or(b == 0, e != be_ref[jnp.maximum(b - 1, 0)])

        @pl.when(jnp.logical_and(run_start, b < n_used))
        def _():
            fetch(e).wait()
            wbf_ref[...] = stage_ref[...].astype(BF16)
            b_next = nxt_ref[b]

            @pl.when(b_next < n_used)
            def _():
                fetch(be_ref[jnp.minimum(b_next, nb - 1)]).start()

        @pl.when(b < n_used)
        def _():
            o_ref[rows, :] = block_fn(x_ref[rows, :], wbf_ref)

        @pl.when(b >= n_used)
        def _():
            o_ref[rows, :] = jnp.zeros((MOE_BLOCK, o_ref.shape[1]), o_ref.dtype)


def _expert_matmul(x, w_e, blk_e, nxt_blk, n_used, block_fn, n_out, out_dtype, name, sub):
    P, kx = x.shape
    E, kw, nw = w_e.shape
    nb = P // MOE_BLOCK
    assert nb % sub == 0
    rows = sub * MOE_BLOCK
    xmap = lambda s, be, nx, nu: (jnp.minimum(s, (nu[0] - 1) // sub), 0)
    return pl.pallas_call(
        functools.partial(_expert_kernel, block_fn=block_fn),
        out_shape=jax.ShapeDtypeStruct((P, n_out), out_dtype),
        grid_spec=pltpu.PrefetchScalarGridSpec(
            num_scalar_prefetch=3,
            grid=(nb // sub,),
            in_specs=[pl.BlockSpec((rows, kx), xmap), pl.BlockSpec(memory_space=pl.ANY)],
            out_specs=pl.BlockSpec((rows, n_out), lambda s, be, nx, nu: (s, 0)),
            scratch_shapes=[pltpu.VMEM((kw, nw), BF16), pltpu.VMEM((kw, nw), F32),
                            pltpu.SemaphoreType.DMA]),
        compiler_params=_cparams(1, 56),
        name=name,
    )(blk_e, nxt_blk, n_used, x, w_e)


def _combine_kernel(dest_hbm, y_hbm, w_ref, pre_ref, g_ref, b_ref, o_ref, idx_smem, buf_ref, isem, gsem):
    i = pl.program_id(0)
    n = pl.num_programs(0)
    tc = o_ref.shape[0]
    half = buf_ref.shape[3]
    n_idx = TOP_K * tc

    def idx_copy(tile, slot):
        return pltpu.make_async_copy(dest_hbm.at[tile], idx_smem.at[pl.ds(slot * n_idx, n_idx)],
                                     isem.at[slot])

    def issue_tile(slot):
        base = slot * n_idx

        def body(r8, carry):
            for s in range(8):
                r = r8 * 8 + s
                for k in range(TOP_K):
                    d = idx_smem[base + k * tc + r]
                    pltpu.make_async_copy(y_hbm.at[pl.ds(d, 1)], buf_ref.at[slot, k, pl.ds(r, 1)],
                                          gsem.at[slot]).start(priority=k % 2)
            return carry
        lax.fori_loop(0, tc // 8, body, 0)

    slot = i % 2

    @pl.when(i == 0)
    def _():
        idx_copy(0, 0).start()
        idx_copy(0, 0).wait()
        issue_tile(0)

        @pl.when(n > 1)
        def _():
            idx_copy(1, 1).start()

    @pl.when(i + 1 < n)
    def _():
        idx_copy(i + 1, 1 - slot).wait()
        for sl in range(2):
            @pl.when(slot == 1 - sl)
            def _():
                issue_tile(sl)

    @pl.when(i + 2 < n)
    def _():
        idx_copy(i + 2, slot).start()

    for k in range(TOP_K):
        pltpu.make_async_copy(buf_ref.at[slot, k], buf_ref.at[slot, k], gsem.at[slot]).wait()

    w = w_ref[...]
    acc_lo = jnp.zeros((tc, half), F32)
    acc_hi = jnp.zeros((tc, half), F32)
    for k in range(TOP_K):
        lo, hi = _unpack_halves(buf_ref[slot, k])
        wk = w[:, k:k + 1]
        acc_lo = acc_lo + wk * lo
        acc_hi = acc_hi + wk * hi
    y_lo = pre_ref[:, :half] + acc_lo
    y_hi = pre_ref[:, half:] + acc_hi
    inv_d = 1.0 / (2 * half)
    mu = (jnp.sum(y_lo, axis=-1, keepdims=True) + jnp.sum(y_hi, axis=-1, keepdims=True)) * inv_d
    c_lo = y_lo - mu
    c_hi = y_hi - mu
    var = (jnp.sum(c_lo * c_lo, axis=-1, keepdims=True)
           + jnp.sum(c_hi * c_hi, axis=-1, keepdims=True)) * inv_d
    rstd = lax.rsqrt(var + LN_EPS)
    o_ref[:, :half] = c_lo * rstd * g_ref[:, :half] + b_ref[:, :half]
    o_ref[:, half:] = c_hi * rstd * g_ref[:, half:] + b_ref[:, half:]


def _combine_ln(y_packed, dest_tiles, w_col, pre, g, b):
    T, D = pre.shape
    half = y_packed.shape[1]
    nt, n_idx = dest_tiles.shape
    tc = n_idx // TOP_K
    return pl.pallas_call(
        _combine_kernel,
        out_shape=jax.ShapeDtypeStruct((T, D), F32),
        grid=(nt,),
        in_specs=[pl.BlockSpec(memory_space=pl.ANY), pl.BlockSpec(memory_space=pl.ANY),
                  pl.BlockSpec((tc, TOP_K), lambda i: (i, 0)),
                  pl.BlockSpec((tc, D), lambda i: (i, 0)),
                  pl.BlockSpec((1, D), lambda i: (0, 0)),
                  pl.BlockSpec((1, D), lambda i: (0, 0))],
        out_specs=pl.BlockSpec((tc, D), lambda i: (i, 0)),
        scratch_shapes=[pltpu.SMEM((2 * n_idx,), I32),
                        pltpu.VMEM((2, TOP_K, tc, half), U32),
                        pltpu.SemaphoreType.DMA((2,)), pltpu.SemaphoreType.DMA((2,))],
        compiler_params=_cparams(1, 58),
        name="combine_ln2",
    )(dest_tiles, y_packed, w_col, pre, g.reshape(1, D), b.reshape(1, D))


def _glu_up_kernel(h_ref, wg_ref, wu_ref, o_ref, wbf_ref):
    tn = wg_ref.shape[1]

    @pl.when(pl.program_id(1) == 0)
    def _():
        wbf_ref[:, :tn] = wg_ref[...].astype(BF16)
        wbf_ref[:, tn:] = wu_ref[...].astype(BF16)

    gu = jnp.dot(h_ref[...], wbf_ref[...], preferred_element_type=F32)
    g = gu[:, :tn]
    o_ref[...] = (g * _sigmoid(g) * gu[:, tn:]).astype(o_ref.dtype)


def _glu_up(h_bf, w_gu, *, tm=512, tn=384):
    T, D = h_bf.shape
    f = w_gu.shape[1] // 2
    tm = min(tm, T)
    nc = f // tn
    return pl.pallas_call(
        _glu_up_kernel,
        out_shape=jax.ShapeDtypeStruct((T, f), BF16),
        grid=(nc, T // tm),
        in_specs=[pl.BlockSpec((tm, D), lambda c, i: (i, 0)),
                  pl.BlockSpec((D, tn), lambda c, i: (0, c)),
                  pl.BlockSpec((D, tn), lambda c, i: (0, nc + c))],
        out_specs=pl.BlockSpec((tm, tn), lambda c, i: (i, c)),
        scratch_shapes=[pltpu.VMEM((D, 2 * tn), BF16)],
        compiler_params=_cparams(2, 56),
        name="shared_up",
    )(h_bf, w_gu, w_gu)


def _dense_tail_kernel(fill_ref, dest_hbm, x_ref, hb_ref, hj_ref, s_ref, p_ref, wd_ref, wg_ref, bg_ref,
                       wp_ref, o_ref, xs_hbm, wdbf_ref, wgbf_ref, wpbf_ref, idx_smem, zero_ref,
                       isem, ssem, zsem, *, alpha, k_per_step):
    j = pl.program_id(0)
    i = pl.program_id(1)
    ni = pl.num_programs(1)
    step = j * ni + i
    tm = x_ref.shape[0]
    n_idx = k_per_step * tm
    R = zero_ref.shape[0]

    def idx_copies(jj, ii, slot):
        return [pltpu.make_async_copy(dest_hbm.at[jj * k_per_step + kk, pl.ds(pl.multiple_of(ii * tm, 128), tm)],
                                      idx_smem.at[pl.ds(slot * n_idx + kk * tm, tm)], isem.at[slot])
                for kk in range(k_per_step)]

    @pl.when(step == 0)
    def _():
        for cp in idx_copies(0, 0, 0):
            cp.start()
        zero_ref[...] = jnp.zeros_like(zero_ref)

        def fill(f):
            return pltpu.make_async_copy(zero_ref, xs_hbm.at[pl.ds(fill_ref[f] * R, R)], zsem)

        def start(f, carry):
            @pl.when(fill_ref[f] >= 0)
            def _():
                fill(f).start()
            return carry
        lax.fori_loop(0, fill_ref.shape[0], start, 0)

        def wait(f, carry):
            @pl.when(fill_ref[f] >= 0)
            def _():
                fill(f).wait()
            return carry
        lax.fori_loop(0, fill_ref.shape[0], wait, 0)

    @pl.when(i == 0)
    def _():
        wdbf_ref[...] = wd_ref[...].astype(BF16)
        wgbf_ref[...] = wg_ref[...].astype(BF16)
        wpbf_ref[...] = wp_ref[...].astype(BF16)

    slot = step % 2
    for cp in idx_copies(j, i, slot):
        cp.wait()

    @pl.when(step + 1 < pl.num_programs(0) * ni)
    def _():
        nxt = step + 1
        for cp in idx_copies(nxt // ni, nxt % ni, 1 - slot):
            cp.start()

    base = slot * n_idx
    for kk in range(k_per_step):
        for r in range(tm):
            d = idx_smem[base + kk * tm + r]
            pltpu.make_async_copy(x_ref.at[pl.ds(r, 1)], xs_hbm.at[pl.ds(d, 1)], ssem).start(priority=r % 2)

    shared = jnp.dot(s_ref[...], wdbf_ref[...], preferred_element_type=F32)
    gate = _sigmoid(jnp.dot(hb_ref[...], wgbf_ref[...], preferred_element_type=F32) + bg_ref[...])
    proj = jnp.dot(p_ref[...].astype(BF16), wpbf_ref[...], preferred_element_type=F32)
    o_ref[...] = alpha * hj_ref[...] + shared + gate * proj

    for kk in range(k_per_step):
        pltpu.make_async_copy(x_ref, x_ref, ssem).wait()


def _dense_tail(h1_bf, h1, h1_packed, dest_kt, fill_blocks, n_rows, s_mid, p, w_down_s, w_gate, b_gate,
                w_proj, alpha, *, tm=512, tn=512):
    T, D = h1.shape
    f = s_mid.shape[1]
    dp = p.shape[1]
    W = h1_packed.shape[1]
    tm = min(tm, T)
    nj = D // tn
    assert TOP_K % nj == 0 and tm % 128 == 0
    k_per_step = TOP_K // nj
    row = lambda j, i, fb: (i, 0)
    col = lambda j, i, fb: (0, j)
    tile = lambda j, i, fb: (i, j)
    return pl.pallas_call(
        functools.partial(_dense_tail_kernel, alpha=alpha, k_per_step=k_per_step),
        out_shape=(jax.ShapeDtypeStruct((T, D), F32), jax.ShapeDtypeStruct((n_rows, W), h1_packed.dtype)),
        grid_spec=pltpu.PrefetchScalarGridSpec(
            num_scalar_prefetch=1,
            grid=(nj, T // tm),
            in_specs=[pl.BlockSpec(memory_space=pl.ANY),
                      pl.BlockSpec((tm, W), row),
                      pl.BlockSpec((tm, D), row),
                      pl.BlockSpec((tm, tn), tile),
                      pl.BlockSpec((tm, f), row),
                      pl.BlockSpec((tm, dp), row),
                      pl.BlockSpec((f, tn), col),
                      pl.BlockSpec((D, tn), col),
                      pl.BlockSpec((1, tn), col),
                      pl.BlockSpec((dp, tn), col)],
            out_specs=(pl.BlockSpec((tm, tn), tile), pl.BlockSpec(memory_space=pl.ANY)),
            scratch_shapes=[pltpu.VMEM((f, tn), BF16), pltpu.VMEM((D, tn), BF16), pltpu.VMEM((dp, tn), BF16),
                            pltpu.SMEM((2 * k_per_step * tm,), I32), pltpu.VMEM((MOE_BLOCK, W), h1_packed.dtype),
                            pltpu.SemaphoreType.DMA((2,)), pltpu.SemaphoreType.DMA, pltpu.SemaphoreType.DMA]),
        compiler_params=pltpu.CompilerParams(dimension_semantics=("arbitrary", "arbitrary"),
                                             vmem_limit_bytes=58 * 1024 * 1024, has_side_effects=True),
        name="dense_tail_dispatch",
    )(fill_blocks, dest_kt, h1_packed, h1_bf, h1, s_mid, p, w_down_s, w_gate, b_gate.reshape(1, D), w_proj)


def _dispatch_tables(idx, rank, counts):
    E = counts.shape[0]
    P = idx.size + E * MOE_BLOCK
    nb = P // MOE_BLOCK
    padded = (counts + MOE_BLOCK - 1) // MOE_BLOCK * MOE_BLOCK
    pend = jnp.cumsum(padded)
    pstart = pend - padded
    sel = idx[None] == jnp.arange(E, dtype=I32).reshape(E, 1, 1, 1)
    dest = rank + jnp.sum(jnp.where(sel, pstart.reshape(E, 1, 1, 1), 0), axis=0)
    blk_row = jnp.arange(nb, dtype=I32) * MOE_BLOCK
    blk_e = jnp.minimum(jnp.sum(pend[None, :] <= blk_row[:, None], axis=1), E - 1).astype(I32)
    n_used = pend[-1] // MOE_BLOCK
    last_blk = jnp.where(counts > 0, pend // MOE_BLOCK - 1, -1)
    tail_blk = n_used + jnp.arange(E, dtype=I32)
    tail_blk = jnp.where(tail_blk < nb, tail_blk, -1)
    fill = jnp.concatenate([last_blk, tail_blk]).astype(I32)
    nxt_blk = jnp.take(pend // MOE_BLOCK, blk_e).astype(I32)
    return dest.astype(I32), blk_e, nxt_blk, n_used.astype(I32).reshape(1), fill, P


def _layer(h0_f, h0_bf, p_l, w_in, b_in, conv_w, conv_b, mh_norm_g, w_conv_out, w_mlstm_out,
           w_mix_out, ln1_g, ln1_b, w_router, router_bias, w_gu_e, w_down_e, w_gu_s, w_down_s,
           w_ple_gate, b_ple_gate, w_ple_proj, ln2_g, ln2_b, alpha, batch, seq):
    T, D = h0_f.shape
    d_conv = conv_w.shape[1]
    d_v = mh_norm_g.shape[0]
    d_qk = d_v // 2
    n_if = 2 * N_HEADS
    c_qk = 3 * d_conv
    c_if = c_qk + 2 * d_qk + 2 * d_v
    c_gate = c_if + n_if
    b2d = b_in.reshape(1, -1)

    w_t = jnp.swapaxes(w_in, 0, 1)
    ya_pre = _conv_branch(h0_bf, w_t, b2d, conv_w, conv_b.reshape(1, -1), seq, d_conv)
    qkvo = _proj(h0_bf, w_t, b2d, c_qk, 2 * d_qk + 2 * d_v)
    if_col, if_row = _if_gates(h0_bf, w_t[c_if:c_gate], b_in[c_if:c_gate])
    gates = _proj(h0_bf, w_t, b2d, c_gate, 2 * D, act="sigmoid")
    yb_pre = _mlstm(qkvo, if_col, if_row, mh_norm_g.reshape(1, -1), batch, seq, d_qk, d_v)
    u = _merge(ya_pre, yb_pre, w_conv_out, w_mlstm_out, gates)
    pre1 = _mix(u, w_mix_out, h0_f, alpha)

    rt = min(ROUTE_TILE, T)
    h1, h1_bf, h1_packed, idx, rank, w_col, counts = _ln1_router(pre1, ln1_g, ln1_b, w_router,
                                                                 router_bias, rt)
    dest, blk_e, nxt_blk, n_used, fill, n_rows = _dispatch_tables(idx, rank, counts[:, 0].astype(I32))
    dest_kt = dest.transpose(1, 0, 2).reshape(TOP_K, T)
    ct = min(COMBINE_TILE, T)
    dest_tiles = dest_kt.reshape(TOP_K, T // ct, ct).transpose(1, 0, 2).reshape(T // ct, TOP_K * ct)

    s_mid = _glu_up(h1_bf, w_gu_s)
    pre2, xs = _dense_tail(h1_bf, h1, h1_packed, dest_kt, fill, n_rows, s_mid, p_l, w_down_s,
                           w_ple_gate, b_ple_gate, w_ple_proj, alpha)

    hmid = _expert_matmul(xs, w_gu_e, blk_e, nxt_blk, n_used, _glu_block, w_gu_e.shape[2] // 2, BF16,
                          "expert_up", EXPERT_UP_SUB)
    y_packed = _expert_matmul(hmid, w_down_e, blk_e, nxt_blk, n_used, _down_block, D // 2, U32,
                              "expert_down", EXPERT_DOWN_SUB)
    return _combine_ln(y_packed, dest_tiles, w_col, pre2, ln2_g, ln2_b)


def kernel(x, p, ln_in_g, ln_in_b, w_in, b_in, conv_w, conv_b, mh_norm_g, w_conv_out, w_mlstm_out,
           w_mix_out, ln1_g, ln1_b, w_router, router_bias, w_gu_e, w_down_e, w_gu_s, w_down_s,
           w_ple_gate, b_ple_gate, w_ple_proj, ln2_g, ln2_b):
    B, S, D = x.shape
    depth = w_in.shape[0]
    alpha = (2 * depth) ** 0.25
    T = B * S
    h_f, h_bf = _ln_in(x.reshape(T, D), ln_in_g, ln_in_b)
    for l in range(depth):
        h_f = _layer(h_f, h_bf, p[l].reshape(T, -1), w_in[l], b_in[l], conv_w[l], conv_b[l],
                     mh_norm_g[l], w_conv_out[l], w_mlstm_out[l], w_mix_out[l], ln1_g[l], ln1_b[l],
                     w_router[l], router_bias[l], w_gu_e[l], w_down_e[l], w_gu_s[l], w_down_s[l],
                     w_ple_gate[l], b_ple_gate[l], w_ple_proj[l], ln2_g[l], ln2_b[l], alpha, B, S)
        if l + 1 < depth:
            h_bf = h_f.astype(BF16)
    return h_f.reshape(B, S, D)
```

```python
import functools

import jax
import jax.numpy as jnp
from jax import lax
from jax.experimental import pallas as pl
from jax.experimental.pallas import tpu as pltpu

F32 = jnp.float32
BF16 = jnp.bfloat16
U32 = jnp.uint32
I32 = jnp.int32

N_HEADS = 8
TOP_K = 8
N_GROUPS = 8
TOP_GROUPS = 4
ROUTE_SCALE = 2.5
MOE_BLOCK = 256
LN_EPS = 1e-5
CONV_W = 3
MLSTM_CHUNK = 128
MLSTM_STEP = 128
ROUTE_TILE = 256
COMBINE_TILE = 128
EXPERT_UP_SUB = 3
EXPERT_DOWN_SUB = 4
NEG_BIG = -1e30
HI_MASK = 0xFFFF0000
V7X_VMEM_BYTES = 64 * 1024 * 1024


def _cparams(n_axes, vmem_mib):
    assert vmem_mib * 1024 * 1024 <= V7X_VMEM_BYTES
    return pltpu.CompilerParams(dimension_semantics=("arbitrary",) * n_axes,
                                vmem_limit_bytes=vmem_mib * 1024 * 1024)


def _ln_rows(x, g, b):
    mu = jnp.mean(x, axis=-1, keepdims=True)
    xc = x - mu
    var = jnp.mean(xc * xc, axis=-1, keepdims=True)
    return xc * lax.rsqrt(var + LN_EPS) * g + b


def _pack_halves(lo, hi):
    lo = pltpu.bitcast(lo.astype(BF16).astype(F32), U32)
    hi = pltpu.bitcast(hi.astype(BF16).astype(F32), U32)
    return (hi & jnp.uint32(HI_MASK)) | (lo >> jnp.uint32(16))


def _sigmoid(x):
    return 0.5 * jnp.tanh(0.5 * x) + 0.5


def _unpack_halves(w):
    lo = pltpu.bitcast(w << jnp.uint32(16), F32)
    hi = pltpu.bitcast(w & jnp.uint32(HI_MASK), F32)
    return lo, hi


def _ln_in_kernel(x_ref, g_ref, b_ref, of_ref, ob_ref):
    y = _ln_rows(x_ref[...], g_ref[...], b_ref[...])
    of_ref[...] = y
    ob_ref[...] = y.astype(BF16)


def _ln_in(x2, g, b, tm=512):
    T, D = x2.shape
    tm = min(tm, T)
    return pl.pallas_call(
        _ln_in_kernel,
        out_shape=(jax.ShapeDtypeStruct((T, D), F32), jax.ShapeDtypeStruct((T, D), BF16)),
        grid=(T // tm,),
        in_specs=[pl.BlockSpec((tm, D), lambda i: (i, 0)),
                  pl.BlockSpec((1, D), lambda i: (0, 0)),
                  pl.BlockSpec((1, D), lambda i: (0, 0))],
        out_specs=(pl.BlockSpec((tm, D), lambda i: (i, 0)),
                   pl.BlockSpec((tm, D), lambda i: (i, 0))),
        compiler_params=_cparams(1, 56),
        name="ln_in",
    )(x2, g.reshape(1, D), b.reshape(1, D))


_NT = (((1,), (1,)), ((), ()))


def _proj_kernel(x_ref, w_ref, b_ref, o_ref, wbf_ref, *, act):
    @pl.when(pl.program_id(1) == 0)
    def _():
        wbf_ref[...] = w_ref[...].astype(BF16)

    acc = lax.dot_general(x_ref[...], wbf_ref[...], _NT, preferred_element_type=F32) + b_ref[...]
    if act == "sigmoid":
        acc = _sigmoid(acc)
    o_ref[...] = acc.astype(o_ref.dtype)


def _proj(x, w_t, bias2d, row0, n_cols, *, act=None, tm=1024, tn=None, out_dtype=BF16):
    T, K = x.shape
    tm = min(tm, T)
    if tn is None:
        tn = next(t for t in (768, 512, 256, 128) if n_cols % t == 0)
    assert n_cols % tn == 0 and T % tm == 0 and row0 % 8 == 0
    if row0 % tn == 0:
        jb = row0 // tn
        w_spec = pl.BlockSpec((tn, K), lambda j, i: (jb + j, 0))
        b_spec = pl.BlockSpec((1, tn), lambda j, i: (0, jb + j))
    else:
        w_spec = pl.BlockSpec((pl.Element(tn), pl.Element(K)),
                              lambda j, i: (pl.multiple_of(row0 + j * tn, 8), 0))
        bias2d = bias2d[:, row0:row0 + n_cols]
        b_spec = pl.BlockSpec((1, tn), lambda j, i: (0, j))
    return pl.pallas_call(
        functools.partial(_proj_kernel, act=act),
        out_shape=jax.ShapeDtypeStruct((T, n_cols), out_dtype),
        grid=(n_cols // tn, T // tm),
        in_specs=[pl.BlockSpec((tm, K), lambda j, i: (i, 0)), w_spec, b_spec],
        out_specs=pl.BlockSpec((tm, tn), lambda j, i: (i, j)),
        scratch_shapes=[pltpu.VMEM((tn, K), BF16)],
        compiler_params=_cparams(2, 60),
        name="proj_" + (act or "lin"),
    )(x, w_t, bias2d)


def _conv_kernel(x_ref, wh_ref, wc_ref, wb_ref, bh_ref, bc_ref, bb_ref, cw_ref, cb_ref,
                 o_ref, wbf_ref, zprev_ref, *, tiles_per_seq):
    i = pl.program_id(1)

    @pl.when(i == 0)
    def _():
        wbf_ref[0] = wh_ref[...].astype(BF16)
        wbf_ref[1] = wc_ref[...].astype(BF16)
        wbf_ref[2] = wb_ref[...].astype(BF16)

    @pl.when(i % tiles_per_seq == 0)
    def _():
        zprev_ref[...] = jnp.zeros_like(zprev_ref)

    x = x_ref[...]
    ha = lax.dot_general(x, wbf_ref[0], _NT, preferred_element_type=F32) + bh_ref[...]
    ca = lax.dot_general(x, wbf_ref[1], _NT, preferred_element_type=F32) + bc_ref[...]
    ba = lax.dot_general(x, wbf_ref[2], _NT, preferred_element_type=F32) + bb_ref[...]
    z = ca * ha
    tm = z.shape[0]
    prev = zprev_ref[...]
    row8 = lax.broadcasted_iota(I32, prev.shape, 0)
    z1 = pltpu.roll(z, 1, 0)
    z2 = pltpu.roll(z, 2, 0)
    p1 = pltpu.roll(prev, 1, 0)
    p2 = pltpu.roll(prev, 2, 0)
    z1 = jnp.concatenate([jnp.where(row8 < 1, p1, z1[:8]), z1[8:]], axis=0)
    z2 = jnp.concatenate([jnp.where(row8 < 2, p2, z2[:8]), z2[8:]], axis=0)
    cw = cw_ref[...]
    y = cw[0:1] * z2 + cw[1:2] * z1 + cw[2:3] * z + cb_ref[...]
    o_ref[...] = (ba * y).astype(o_ref.dtype)
    zprev_ref[...] = z[tm - 8:]


def _conv_branch(x, w_t, b2d, conv_w, conv_b2d, seq, d_conv, *, tm=1024, tn=256):
    T, K = x.shape
    tm = min(tm, seq)
    assert seq % tm == 0 and d_conv % tn == 0 and tm % 8 == 0
    nb = d_conv // tn
    wspec = lambda g: pl.BlockSpec((tn, K), lambda j, i: (g * nb + j, 0))
    bspec = lambda g: pl.BlockSpec((1, tn), lambda j, i: (0, g * nb + j))
    return pl.pallas_call(
        functools.partial(_conv_kernel, tiles_per_seq=seq // tm),
        out_shape=jax.ShapeDtypeStruct((T, d_conv), BF16),
        grid=(nb, T // tm),
        in_specs=[pl.BlockSpec((tm, K), lambda j, i: (i, 0)),
                  wspec(0), wspec(1), wspec(2), bspec(0), bspec(1), bspec(2),
                  pl.BlockSpec((CONV_W, tn), lambda j, i: (0, j)),
                  pl.BlockSpec((1, tn), lambda j, i: (0, j))],
        out_specs=pl.BlockSpec((tm, tn), lambda j, i: (i, j)),
        scratch_shapes=[pltpu.VMEM((3, tn, K), BF16), pltpu.VMEM((8, tn), F32)],
        compiler_params=_cparams(2, 56),
        name="conv_branch",
    )(x, w_t, w_t, w_t, b2d, b2d, b2d, conv_w, conv_b2d)


def _if_kernel(x_ref, wt_ref, bc_ref, br_ref, oc_ref, or_ref):
    x = x_ref[...]
    w = wt_ref[...].astype(BF16)
    oc_ref[...] = lax.dot_general(x, w, _NT, preferred_element_type=F32) + bc_ref[...]
    or_ref[...] = lax.dot_general(w, x, _NT, preferred_element_type=F32) + br_ref[...]


def _if_gates(x, w_if_t, b_if, tm=1024):
    T, K = x.shape
    tm = min(tm, T)
    n = w_if_t.shape[0]
    return pl.pallas_call(
        _if_kernel,
        out_shape=(jax.ShapeDtypeStruct((T, n), F32), jax.ShapeDtypeStruct((n, T), F32)),
        grid=(T // tm,),
        in_specs=[pl.BlockSpec((tm, K), lambda i: (i, 0)),
                  pl.BlockSpec((n, K), lambda i: (0, 0)),
                  pl.BlockSpec((1, n), lambda i: (0, 0)),
                  pl.BlockSpec((n, 1), lambda i: (0, 0))],
        out_specs=(pl.BlockSpec((tm, n), lambda i: (i, 0)),
                   pl.BlockSpec((n, tm), lambda i: (0, i))),
        compiler_params=_cparams(1, 32),
        name="if_gates",
    )(x, w_if_t, b_if.reshape(1, n), b_if.reshape(n, 1))


def _mlstm_kernel(q_ref, k_ref, v_ref, o_ref, ifc_ref, ifr_ref, g_ref, y_ref, c_ref, m_ref,
                  *, dk, dv, L):
    H = N_HEADS

    @pl.when(pl.program_id(1) == 0)
    def _():
        c_ref[...] = jnp.zeros_like(c_ref)
        m_ref[...] = jnp.zeros_like(m_ref)

    scale = dk ** -0.5
    r = lax.broadcasted_iota(I32, (L, L), 0)
    c = lax.broadcasted_iota(I32, (L, L), 1)
    causal = r >= c
    tri_lower = causal.astype(F32)
    tri_upper = (r <= c).astype(F32)
    ones_col = (lax.broadcasted_iota(I32, (L, 128), 1) == 0).astype(BF16)

    for cc in range(q_ref.shape[0] // L):
        rows = slice(cc * L, (cc + 1) * L)
        ifc = ifc_ref[rows, :]
        ifr = ifr_ref[:, rows]
        ig_c = ifc[:, 0:H]
        lf_c = jax.nn.log_sigmoid(ifc[:, H:2 * H])
        ig_r = ifr[0:H, :]
        lf_r = jax.nn.log_sigmoid(ifr[H:2 * H, :])
        bcum_c = jnp.dot(tri_lower, lf_c, preferred_element_type=F32,
                         precision=lax.Precision.HIGHEST)
        bcum_r = jnp.dot(lf_r, tri_upper, preferred_element_type=F32,
                         precision=lax.Precision.HIGHEST)
        d_c = ig_c - bcum_c
        d_r = ig_r - bcum_r

        for h in range(H):
            q = q_ref[rows, h * dk:(h + 1) * dk]
            k = k_ref[rows, h * dk:(h + 1) * dk]
            v = v_ref[rows, h * dv:(h + 1) * dv]
            v_aug = jnp.concatenate([v, ones_col], axis=1)
            bc = bcum_c[:, h:h + 1]
            m_prev = m_ref[h]
            c_prev = c_ref[h]

            qk = lax.dot_general(q, k, _NT, preferred_element_type=F32) * scale
            logd = jnp.where(causal, bc + d_r[h:h + 1, :], NEG_BIG)
            log_inter = bc + m_prev
            m_t = jnp.maximum(log_inter, jnp.max(logd, axis=1, keepdims=True))
            w_inter = jnp.exp(log_inter - m_t)
            s_mat = (qk * jnp.exp(logd - m_t)).astype(BF16)
            tot = (w_inter * jnp.dot(q, c_prev.astype(BF16), preferred_element_type=F32)
                   + jnp.dot(s_mat, v_aug, preferred_element_type=F32))
            den = jnp.maximum(jnp.abs(tot[:, dv:dv + 1]), jnp.exp(-m_t))
            hh = tot[:, :dv] / den
            mu = jnp.mean(hh, axis=-1, keepdims=True)
            hc = hh - mu
            var = jnp.mean(hc * hc, axis=-1, keepdims=True)
            hn = hc * lax.rsqrt(var + LN_EPS) * g_ref[:, h * dv:(h + 1) * dv]
            og = _sigmoid(o_ref[rows, h * dv:(h + 1) * dv].astype(F32))
            y_ref[rows, h * dv:(h + 1) * dv] = (og * hn).astype(y_ref.dtype)

            g_tot = bc[L - 1:L, :]
            a = g_tot + d_c[:, h:h + 1]
            m_new = jnp.maximum(g_tot + m_prev, jnp.max(a, axis=0, keepdims=True))
            kw = (k.astype(F32) * (jnp.exp(a - m_new) * scale)).astype(BF16)
            kv = lax.dot_general(kw, v_aug, (((0,), (0,)), ((), ())), preferred_element_type=F32)
            c_ref[h] = jnp.exp(g_tot + m_prev - m_new) * c_prev + kv
            m_ref[h] = m_new


def _mlstm(qkvo, if_col, if_row, mh_g2d, batch, seq, d_qk, d_v, L=MLSTM_CHUNK, step_rows=MLSTM_STEP):
    T = qkvo.shape[0]
    L = min(L, seq)
    R = min(max(step_rows, L), seq)
    assert seq % R == 0 and R % L == 0 and d_v == 2 * d_qk
    nc = seq // R
    dk, dv = d_qk // N_HEADS, d_v // N_HEADS
    row = lambda b, c: b * nc + c
    return pl.pallas_call(
        functools.partial(_mlstm_kernel, dk=dk, dv=dv, L=L),
        out_shape=jax.ShapeDtypeStruct((T, d_v), BF16),
        grid=(batch, nc),
        in_specs=[pl.BlockSpec((R, d_qk), lambda b, c: (row(b, c), 0)),
                  pl.BlockSpec((R, d_qk), lambda b, c: (row(b, c), 1)),
                  pl.BlockSpec((R, d_v), lambda b, c: (row(b, c), 1)),
                  pl.BlockSpec((R, d_v), lambda b, c: (row(b, c), 2)),
                  pl.BlockSpec((R, 2 * N_HEADS), lambda b, c: (row(b, c), 0)),
                  pl.BlockSpec((2 * N_HEADS, R), lambda b, c: (0, row(b, c))),
                  pl.BlockSpec((1, d_v), lambda b, c: (0, 0))],
        out_specs=pl.BlockSpec((R, d_v), lambda b, c: (row(b, c), 0)),
        scratch_shapes=[pltpu.VMEM((N_HEADS, dk, dv + 128), F32),
                        pltpu.VMEM((N_HEADS, 1, 1), F32)],
        compiler_params=_cparams(2, 40),
        name="mlstm",
    )(qkvo, qkvo, qkvo, qkvo, if_col, if_row, mh_g2d)


def _merge_kernel(a_ref, b_ref, wa_ref, wb_ref, ga_ref, gb_ref, o_ref, wabf_ref, wbbf_ref):
    @pl.when(pl.program_id(1) == 0)
    def _():
        wabf_ref[...] = wa_ref[...].astype(BF16)
        wbbf_ref[...] = wb_ref[...].astype(BF16)

    ya = jnp.dot(a_ref[...], wabf_ref[...], preferred_element_type=F32)
    yb = jnp.dot(b_ref[...], wbbf_ref[...], preferred_element_type=F32)
    u = ga_ref[...].astype(F32) * ya + gb_ref[...].astype(F32) * yb
    o_ref[...] = u.astype(o_ref.dtype)


def _merge(ya_pre, yb_pre, w_a, w_b, gates, *, tm=1024, tn=512):
    T, K = ya_pre.shape
    D = w_a.shape[1]
    tm = min(tm, T)
    nj = D // tn
    return pl.pallas_call(
        _merge_kernel,
        out_shape=jax.ShapeDtypeStruct((T, D), BF16),
        grid=(nj, T // tm),
        in_specs=[pl.BlockSpec((tm, K), lambda j, i: (i, 0)),
                  pl.BlockSpec((tm, K), lambda j, i: (i, 0)),
                  pl.BlockSpec((K, tn), lambda j, i: (0, j)),
                  pl.BlockSpec((K, tn), lambda j, i: (0, j)),
                  pl.BlockSpec((tm, tn), lambda j, i: (i, j)),
                  pl.BlockSpec((tm, tn), lambda j, i: (i, nj + j))],
        out_specs=pl.BlockSpec((tm, tn), lambda j, i: (i, j)),
        scratch_shapes=[pltpu.VMEM((K, tn), BF16), pltpu.VMEM((K, tn), BF16)],
        compiler_params=_cparams(2, 56),
        name="merge",
    )(ya_pre, yb_pre, w_a, w_b, gates, gates)


def _mix_kernel(u_ref, w_ref, h_ref, o_ref, wbf_ref, *, alpha):
    @pl.when(pl.program_id(1) == 0)
    def _():
        wbf_ref[...] = w_ref[...].astype(BF16)

    o_ref[...] = alpha * h_ref[...] + jnp.dot(u_ref[...], wbf_ref[...], preferred_element_type=F32)


def _mix(u, w, h0, alpha, *, tm=1024, tn=512):
    T, D = h0.shape
    tm = min(tm, T)
    return pl.pallas_call(
        functools.partial(_mix_kernel, alpha=alpha),
        out_shape=jax.ShapeDtypeStruct((T, D), F32),
        grid=(D // tn, T // tm),
        in_specs=[pl.BlockSpec((tm, D), lambda j, i: (i, 0)),
                  pl.BlockSpec((D, tn), lambda j, i: (0, j)),
                  pl.BlockSpec((tm, tn), lambda j, i: (i, j))],
        out_specs=pl.BlockSpec((tm, tn), lambda j, i: (i, j)),
        scratch_shapes=[pltpu.VMEM((D, tn), BF16)],
        compiler_params=_cparams(2, 56),
        name="mix_out",
    )(u, w, h0)


def _ln1_router_kernel(pre_ref, g_ref, b_ref, wr_ref, rb_ref, hf_ref, hb_ref, hp_ref,
                       idx_ref, rank_ref, wcol_ref, cnt_ref, carry_ref):
    i = pl.program_id(0)
    E = wr_ref.shape[0]
    tm, D = pre_ref.shape
    G, M = N_GROUPS, E // N_GROUPS

    @pl.when(i == 0)
    def _():
        carry_ref[...] = jnp.zeros_like(carry_ref)

    h = _ln_rows(pre_ref[...], g_ref[...], b_ref[...])
    hf_ref[...] = h
    hb_ref[...] = h.astype(BF16)
    hp_ref[...] = _pack_halves(h[:, :D // 2], h[:, D // 2:])

    logits = lax.dot_general(wr_ref[...], h, (((1,), (1,)), ((), ())),
                             preferred_element_type=F32, precision=lax.Precision.HIGHEST)
    scores = jax.nn.sigmoid(logits)
    scores3 = scores.reshape(G, M, tm)
    sel3 = (scores + rb_ref[...]).reshape(G, M, tm)
    midx = lax.broadcasted_iota(I32, (G, M, tm), 1)
    gidx3 = lax.broadcasted_iota(I32, (G, M, tm), 0)
    eidx = gidx3 * M + midx
    gidx = lax.broadcasted_iota(I32, (G, 1, tm), 0)
    neg_inf = -jnp.inf

    top1 = jnp.max(sel3, axis=1, keepdims=True)
    first1 = jnp.min(jnp.where(sel3 == top1, midx, M), axis=1, keepdims=True)
    top2 = jnp.max(jnp.where(midx == first1, neg_inf, sel3), axis=1, keepdims=True)
    gs = top1 + top2
    gkeep = jnp.zeros((G, 1, tm), F32)
    for _ in range(TOP_GROUPS):
        mx = jnp.max(gs, axis=0, keepdims=True)
        first = jnp.min(jnp.where(gs == mx, gidx, G), axis=0, keepdims=True)
        hit = gidx == first
        gkeep = jnp.where(hit, 1.0, gkeep)
        gs = jnp.where(hit, neg_inf, gs)
    selm = jnp.where(gkeep > 0.5, sel3, neg_inf)

    idx_rows, sc_rows = [], []
    chosen = jnp.zeros((G, M, tm), F32)
    for _ in range(TOP_K):
        mx = jnp.max(jnp.max(selm, axis=1, keepdims=True), axis=0, keepdims=True)
        first = jnp.min(jnp.min(jnp.where(selm == mx, eidx, E), axis=1, keepdims=True),
                        axis=0, keepdims=True)
        hit = eidx == first
        sc = jnp.sum(jnp.sum(jnp.where(hit, scores3, 0.0), axis=1, keepdims=True),
                     axis=0, keepdims=True)
        chosen = jnp.where(hit, 1.0, chosen)
        selm = jnp.where(hit, neg_inf, selm)
        idx_rows.append(first)
        sc_rows.append(sc)
    denom = sc_rows[0]
    for s in sc_rows[1:]:
        denom = denom + s

    tr = lax.broadcasted_iota(I32, (tm, tm), 0)
    tc = lax.broadcasted_iota(I32, (tm, tm), 1)
    before = (tr < tc).astype(BF16)
    chosen2 = chosen.reshape(E, tm)
    rank2 = jnp.dot(chosen2.astype(BF16), before, preferred_element_type=F32) + carry_ref[...]
    rank3 = rank2.reshape(G, M, tm)
    w_rows = []
    for k in range(TOP_K):
        hit = eidx == idx_rows[k]
        rk = jnp.sum(jnp.sum(jnp.where(hit, rank3, 0.0), axis=1, keepdims=True),
                     axis=0, keepdims=True)
        idx_ref[k:k + 1, :] = idx_rows[k].reshape(1, tm)
        rank_ref[k:k + 1, :] = rk.reshape(1, tm).astype(I32)
        w_rows.append((sc_rows[k] / denom * ROUTE_SCALE).reshape(1, tm))
    wcol_ref[...] = jnp.concatenate(w_rows, axis=0).T
    carry_ref[...] = carry_ref[...] + jnp.sum(chosen2, axis=1, keepdims=True)

    @pl.when(i == pl.num_programs(0) - 1)
    def _():
        cnt_ref[...] = carry_ref[...]


def _ln1_router(pre, g, b, w_router, router_bias, tm):
    T, D = pre.shape
    E = w_router.shape[1]
    nt = T // tm
    row = lambda i: (i, 0)
    fixed = lambda i: (0, 0)
    return pl.pallas_call(
        _ln1_router_kernel,
        out_shape=(jax.ShapeDtypeStruct((T, D), F32), jax.ShapeDtypeStruct((T, D), BF16),
                   jax.ShapeDtypeStruct((T, D // 2), U32),
                   jax.ShapeDtypeStruct((nt, TOP_K, tm), I32), jax.ShapeDtypeStruct((nt, TOP_K, tm), I32),
                   jax.ShapeDtypeStruct((T, TOP_K), F32), jax.ShapeDtypeStruct((E, 1), F32)),
        grid=(nt,),
        in_specs=[pl.BlockSpec((tm, D), row), pl.BlockSpec((1, D), fixed), pl.BlockSpec((1, D), fixed),
                  pl.BlockSpec((E, D), fixed), pl.BlockSpec((E, 1), fixed)],
        out_specs=(pl.BlockSpec((tm, D), row), pl.BlockSpec((tm, D), row), pl.BlockSpec((tm, D // 2), row),
                   pl.BlockSpec((None, TOP_K, tm), lambda i: (i, 0, 0)),
                   pl.BlockSpec((None, TOP_K, tm), lambda i: (i, 0, 0)),
                   pl.BlockSpec((tm, TOP_K), row),
                   pl.BlockSpec((E, 1), fixed)),
        scratch_shapes=[pltpu.VMEM((E, 1), F32)],
        compiler_params=_cparams(1, 40),
        name="ln1_router",
    )(pre, g.reshape(1, D), b.reshape(1, D), w_router.T, router_bias.reshape(E, 1))


def _glu_block(x, wbf_ref):
    half = x.shape[1]
    f = wbf_ref.shape[1] // 2
    lo, hi = _unpack_halves(x)
    gu = (jnp.dot(lo.astype(BF16), wbf_ref[:half], preferred_element_type=F32)
          + jnp.dot(hi.astype(BF16), wbf_ref[half:], preferred_element_type=F32))
    g = gu[:, :f]
    return (g * _sigmoid(g) * gu[:, f:]).astype(BF16)


def _down_block(h, wbf_ref):
    y = jnp.dot(h, wbf_ref[...], preferred_element_type=F32)
    half = y.shape[1] // 2
    return _pack_halves(y[:, :half], y[:, half:])


def _expert_kernel(be_ref, nxt_ref, nused_ref, x_ref, w_hbm, o_ref, wbf_ref, stage_ref, wsem, *, block_fn):
    step = pl.program_id(0)
    nb = be_ref.shape[0]
    n_used = nused_ref[0]
    sub = x_ref.shape[0] // MOE_BLOCK

    def fetch(e_):
        return pltpu.make_async_copy(w_hbm.at[e_], stage_ref, wsem)

    @pl.when(step == 0)
    def _():
        fetch(be_ref[0]).start()

    for s in range(sub):
        b = step * sub + s
        rows = slice(s * MOE_BLOCK, (s + 1) * MOE_BLOCK)
        e = be_ref[b]
        run_start = jnp.logical_or(b == 0, e != be_ref[jnp.maximum(b - 1, 0)])

        @pl.when(jnp.logical_and(run_start, b < n_used))
        def _():
            fetch(e).wait()
            wbf_ref[...] = stage_ref[...].astype(BF16)
            b_next = nxt_ref[b]

            @pl.when(b_next < n_used)
            def _():
                fetch(be_ref[jnp.minimum(b_next, nb - 1)]).start()

        @pl.when(b < n_used)
        def _():
            o_ref[rows, :] = block_fn(x_ref[rows, :], wbf_ref)

        @pl.when(b >= n_used)
        def _():
            o_ref[rows, :] = jnp.zeros((MOE_BLOCK, o_ref.shape[1]), o_ref.dtype)


def _expert_matmul(x, w_e, blk_e, nxt_blk, n_used, block_fn, n_out, out_dtype, name, sub):
    P, kx = x.shape
    E, kw, nw = w_e.shape
    nb = P // MOE_BLOCK
    assert nb % sub == 0
    rows = sub * MOE_BLOCK
    xmap = lambda s, be, nx, nu: (jnp.minimum(s, (nu[0] - 1) // sub), 0)
    return pl.pallas_call(
        functools.partial(_expert_kernel, block_fn=block_fn),
        out_shape=jax.ShapeDtypeStruct((P, n_out), out_dtype),
        grid_spec=pltpu.PrefetchScalarGridSpec(
            num_scalar_prefetch=3,
            grid=(nb // sub,),
            in_specs=[pl.BlockSpec((rows, kx), xmap), pl.BlockSpec(memory_space=pl.ANY)],
            out_specs=pl.BlockSpec((rows, n_out), lambda s, be, nx, nu: (s, 0)),
            scratch_shapes=[pltpu.VMEM((kw, nw), BF16), pltpu.VMEM((kw, nw), F32),
                            pltpu.SemaphoreType.DMA]),
        compiler_params=_cparams(1, 56),
        name=name,
    )(blk_e, nxt_blk, n_used, x, w_e)


def _combine_kernel(dest_hbm, y_hbm, w_ref, pre_ref, g_ref, b_ref, o_ref, idx_smem, buf_ref, isem, gsem):
    i = pl.program_id(0)
    n = pl.num_programs(0)
    tc = o_ref.shape[0]
    half = buf_ref.shape[3]
    n_idx = TOP_K * tc

    def idx_copy(tile, slot):
        return pltpu.make_async_copy(dest_hbm.at[tile], idx_smem.at[pl.ds(slot * n_idx, n_idx)],
                                     isem.at[slot])

    def issue_tile(slot):
        base = slot * n_idx

        def body(r8, carry):
            for s in range(8):
                r = r8 * 8 + s
                for k in range(TOP_K):
                    d = idx_smem[base + k * tc + r]
                    pltpu.make_async_copy(y_hbm.at[pl.ds(d, 1)], buf_ref.at[slot, k, pl.ds(r, 1)],
                                          gsem.at[slot]).start(priority=k % 2)
            return carry
        lax.fori_loop(0, tc // 8, body, 0)

    slot = i % 2

    @pl.when(i == 0)
    def _():
        idx_copy(0, 0).start()
        idx_copy(0, 0).wait()
        issue_tile(0)

        @pl.when(n > 1)
        def _():
            idx_copy(1, 1).start()

    @pl.when(i + 1 < n)
    def _():
        idx_copy(i + 1, 1 - slot).wait()
        for sl in range(2):
            @pl.when(slot == 1 - sl)
            def _():
                issue_tile(sl)

    @pl.when(i + 2 < n)
    def _():
        idx_copy(i + 2, slot).start()

    for k in range(TOP_K):
        pltpu.make_async_copy(buf_ref.at[slot, k], buf_ref.at[slot, k], gsem.at[slot]).wait()

    w = w_ref[...]
    acc_lo = jnp.zeros((tc, half), F32)
    acc_hi = jnp.zeros((tc, half), F32)
    for k in range(TOP_K):
        lo, hi = _unpack_halves(buf_ref[slot, k])
        wk = w[:, k:k + 1]
        acc_lo = acc_lo + wk * lo
        acc_hi = acc_hi + wk * hi
    y_lo = pre_ref[:, :half] + acc_lo
    y_hi = pre_ref[:, half:] + acc_hi
    inv_d = 1.0 / (2 * half)
    mu = (jnp.sum(y_lo, axis=-1, keepdims=True) + jnp.sum(y_hi, axis=-1, keepdims=True)) * inv_d
    c_lo = y_lo - mu
    c_hi = y_hi - mu
    var = (jnp.sum(c_lo * c_lo, axis=-1, keepdims=True)
           + jnp.sum(c_hi * c_hi, axis=-1, keepdims=True)) * inv_d
    rstd = lax.rsqrt(var + LN_EPS)
    o_ref[:, :half] = c_lo * rstd * g_ref[:, :half] + b_ref[:, :half]
    o_ref[:, half:] = c_hi * rstd * g_ref[:, half:] + b_ref[:, half:]


def _combine_ln(y_packed, dest_tiles, w_col, pre, g, b):
    T, D = pre.shape
    half = y_packed.shape[1]
    nt, n_idx = dest_tiles.shape
    tc = n_idx // TOP_K
    return pl.pallas_call(
        _combine_kernel,
        out_shape=jax.ShapeDtypeStruct((T, D), F32),
        grid=(nt,),
        in_specs=[pl.BlockSpec(memory_space=pl.ANY), pl.BlockSpec(memory_space=pl.ANY),
                  pl.BlockSpec((tc, TOP_K), lambda i: (i, 0)),
                  pl.BlockSpec((tc, D), lambda i: (i, 0)),
                  pl.BlockSpec((1, D), lambda i: (0, 0)),
                  pl.BlockSpec((1, D), lambda i: (0, 0))],
        out_specs=pl.BlockSpec((tc, D), lambda i: (i, 0)),
        scratch_shapes=[pltpu.SMEM((2 * n_idx,), I32),
                        pltpu.VMEM((2, TOP_K, tc, half), U32),
                        pltpu.SemaphoreType.DMA((2,)), pltpu.SemaphoreType.DMA((2,))],
        compiler_params=_cparams(1, 58),
        name="combine_ln2",
    )(dest_tiles, y_packed, w_col, pre, g.reshape(1, D), b.reshape(1, D))


def _glu_up_kernel(h_ref, wg_ref, wu_ref, o_ref, wbf_ref):
    tn = wg_ref.shape[1]

    @pl.when(pl.program_id(1) == 0)
    def _():
        wbf_ref[:, :tn] = wg_ref[...].astype(BF16)
        wbf_ref[:, tn:] = wu_ref[...].astype(BF16)

    gu = jnp.dot(h_ref[...], wbf_ref[...], preferred_element_type=F32)
    g = gu[:, :tn]
    o_ref[...] = (g * _sigmoid(g) * gu[:, tn:]).astype(o_ref.dtype)


def _glu_up(h_bf, w_gu, *, tm=512, tn=384):
    T, D = h_bf.shape
    f = w_gu.shape[1] // 2
    tm = min(tm, T)
    nc = f // tn
    return pl.pallas_call(
        _glu_up_kernel,
        out_shape=jax.ShapeDtypeStruct((T, f), BF16),
        grid=(nc, T // tm),
        in_specs=[pl.BlockSpec((tm, D), lambda c, i: (i, 0)),
                  pl.BlockSpec((D, tn), lambda c, i: (0, c)),
                  pl.BlockSpec((D, tn), lambda c, i: (0, nc + c))],
        out_specs=pl.BlockSpec((tm, tn), lambda c, i: (i, c)),
        scratch_shapes=[pltpu.VMEM((D, 2 * tn), BF16)],
        compiler_params=_cparams(2, 56),
        name="shared_up",
    )(h_bf, w_gu, w_gu)


def _dense_tail_kernel(fill_ref, dest_hbm, x_ref, hb_ref, hj_ref, s_ref, p_ref, wd_ref, wg_ref, bg_ref,
                       wp_ref, o_ref, xs_hbm, wdbf_ref, wgbf_ref, wpbf_ref, idx_smem, zero_ref,
                       isem, ssem, zsem, *, alpha, k_per_step):
    j = pl.program_id(0)
    i = pl.program_id(1)
    ni = pl.num_programs(1)
    step = j * ni + i
    tm = x_ref.shape[0]
    n_idx = k_per_step * tm
    R = zero_ref.shape[0]

    def idx_copies(jj, ii, slot):
        return [pltpu.make_async_copy(dest_hbm.at[jj * k_per_step + kk, pl.ds(pl.multiple_of(ii * tm, 128), tm)],
                                      idx_smem.at[pl.ds(slot * n_idx + kk * tm, tm)], isem.at[slot])
                for kk in range(k_per_step)]

    @pl.when(step == 0)
    def _():
        for cp in idx_copies(0, 0, 0):
            cp.start()
        zero_ref[...] = jnp.zeros_like(zero_ref)

        def fill(f):
            return pltpu.make_async_copy(zero_ref, xs_hbm.at[pl.ds(fill_ref[f] * R, R)], zsem)

        def start(f, carry):
            @pl.when(fill_ref[f] >= 0)
            def _():
                fill(f).start()
            return carry
        lax.fori_loop(0, fill_ref.shape[0], start, 0)

        def wait(f, carry):
            @pl.when(fill_ref[f] >= 0)
            def _():
                fill(f).wait()
            return carry
        lax.fori_loop(0, fill_ref.shape[0], wait, 0)

    @pl.when(i == 0)
    def _():
        wdbf_ref[...] = wd_ref[...].astype(BF16)
        wgbf_ref[...] = wg_ref[...].astype(BF16)
        wpbf_ref[...] = wp_ref[...].astype(BF16)

    slot = step % 2
    for cp in idx_copies(j, i, slot):
        cp.wait()

    @pl.when(step + 1 < pl.num_programs(0) * ni)
    def _():
        nxt = step + 1
        for cp in idx_copies(nxt // ni, nxt % ni, 1 - slot):
            cp.start()

    base = slot * n_idx
    for kk in range(k_per_step):
        for r in range(tm):
            d = idx_smem[base + kk * tm + r]
            pltpu.make_async_copy(x_ref.at[pl.ds(r, 1)], xs_hbm.at[pl.ds(d, 1)], ssem).start(priority=r % 2)

    shared = jnp.dot(s_ref[...], wdbf_ref[...], preferred_element_type=F32)
    gate = _sigmoid(jnp.dot(hb_ref[...], wgbf_ref[...], preferred_element_type=F32) + bg_ref[...])
    proj = jnp.dot(p_ref[...].astype(BF16), wpbf_ref[...], preferred_element_type=F32)
    o_ref[...] = alpha * hj_ref[...] + shared + gate * proj

    for kk in range(k_per_step):
        pltpu.make_async_copy(x_ref, x_ref, ssem).wait()


def _dense_tail(h1_bf, h1, h1_packed, dest_kt, fill_blocks, n_rows, s_mid, p, w_down_s, w_gate, b_gate,
                w_proj, alpha, *, tm=512, tn=512):
    T, D = h1.shape
    f = s_mid.shape[1]
    dp = p.shape[1]
    W = h1_packed.shape[1]
    tm = min(tm, T)
    nj = D // tn
    assert TOP_K % nj == 0 and tm % 128 == 0
    k_per_step = TOP_K // nj
    row = lambda j, i, fb: (i, 0)
    col = lambda j, i, fb: (0, j)
    tile = lambda j, i, fb: (i, j)
    return pl.pallas_call(
        functools.partial(_dense_tail_kernel, alpha=alpha, k_per_step=k_per_step),
        out_shape=(jax.ShapeDtypeStruct((T, D), F32), jax.ShapeDtypeStruct((n_rows, W), h1_packed.dtype)),
        grid_spec=pltpu.PrefetchScalarGridSpec(
            num_scalar_prefetch=1,
            grid=(nj, T // tm),
            in_specs=[pl.BlockSpec(memory_space=pl.ANY),
                      pl.BlockSpec((tm, W), row),
                      pl.BlockSpec((tm, D), row),
                      pl.BlockSpec((tm, tn), tile),
                      pl.BlockSpec((tm, f), row),
                      pl.BlockSpec((tm, dp), row),
                      pl.BlockSpec((f, tn), col),
                      pl.BlockSpec((D, tn), col),
                      pl.BlockSpec((1, tn), col),
                      pl.BlockSpec((dp, tn), col)],
            out_specs=(pl.BlockSpec((tm, tn), tile), pl.BlockSpec(memory_space=pl.ANY)),
            scratch_shapes=[pltpu.VMEM((f, tn), BF16), pltpu.VMEM((D, tn), BF16), pltpu.VMEM((dp, tn), BF16),
                            pltpu.SMEM((2 * k_per_step * tm,), I32), pltpu.VMEM((MOE_BLOCK, W), h1_packed.dtype),
                            pltpu.SemaphoreType.DMA((2,)), pltpu.SemaphoreType.DMA, pltpu.SemaphoreType.DMA]),
        compiler_params=pltpu.CompilerParams(dimension_semantics=("arbitrary", "arbitrary"),
                                             vmem_limit_bytes=58 * 1024 * 1024, has_side_effects=True),
        name="dense_tail_dispatch",
    )(fill_blocks, dest_kt, h1_packed, h1_bf, h1, s_mid, p, w_down_s, w_gate, b_gate.reshape(1, D), w_proj)


def _dest_kernel(pstart_ref, idx_ref, rank_ref, o_ref):
    idx = idx_ref[...]
    acc = rank_ref[...]
    for e in range(pstart_ref.shape[0]):
        acc = jnp.where(idx == e, acc + pstart_ref[e], acc)
    o_ref[...] = acc


def _dest_rows(idx, rank, pstart):
    shape2 = (idx.shape[0] * idx.shape[1], idx.shape[2])
    whole = pl.BlockSpec(shape2, lambda i, ps: (0, 0))
    return pl.pallas_call(
        _dest_kernel,
        out_shape=jax.ShapeDtypeStruct(shape2, I32),
        grid_spec=pltpu.PrefetchScalarGridSpec(num_scalar_prefetch=1, grid=(1,), in_specs=[whole, whole],
                                               out_specs=whole),
        name="dest_rows",
    )(pstart, idx.reshape(shape2), rank.reshape(shape2)).reshape(idx.shape)


def _dispatch_tables(idx, rank, counts):
    E = counts.shape[0]
    P = idx.size + E * MOE_BLOCK
    nb = P // MOE_BLOCK
    padded = (counts + MOE_BLOCK - 1) // MOE_BLOCK * MOE_BLOCK
    pend = jnp.cumsum(padded)
    pstart = pend - padded
    dest = _dest_rows(idx, rank, pstart.astype(I32))
    blk_row = jnp.arange(nb, dtype=I32) * MOE_BLOCK
    blk_e = jnp.minimum(jnp.sum(pend[None, :] <= blk_row[:, None], axis=1), E - 1).astype(I32)
    n_used = pend[-1] // MOE_BLOCK
    last_blk = jnp.where(counts > 0, pend // MOE_BLOCK - 1, -1)
    tail_blk = n_used + jnp.arange(E, dtype=I32)
    tail_blk = jnp.where(tail_blk < nb, tail_blk, -1)
    fill = jnp.concatenate([last_blk, tail_blk]).astype(I32)
    nxt_blk = jnp.take(pend // MOE_BLOCK, blk_e).astype(I32)
    return dest.astype(I32), blk_e, nxt_blk, n_used.astype(I32).reshape(1), fill, P


def _layer(h0_f, h0_bf, p_l, w_in, b_in, conv_w, conv_b, mh_norm_g, w_conv_out, w_mlstm_out,
           w_mix_out, ln1_g, ln1_b, w_router, router_bias, w_gu_e, w_down_e, w_gu_s, w_down_s,
           w_ple_gate, b_ple_gate, w_ple_proj, ln2_g, ln2_b, alpha, batch, seq):
    T, D = h0_f.shape
    d_conv = conv_w.shape[1]
    d_v = mh_norm_g.shape[0]
    d_qk = d_v // 2
    n_if = 2 * N_HEADS
    c_qk = 3 * d_conv
    c_if = c_qk + 2 * d_qk + 2 * d_v
    c_gate = c_if + n_if
    b2d = b_in.reshape(1, -1)

    w_t = jnp.swapaxes(w_in, 0, 1)
    ya_pre = _conv_branch(h0_bf, w_t, b2d, conv_w, conv_b.reshape(1, -1), seq, d_conv)
    qkvo = _proj(h0_bf, w_t, b2d, c_qk, 2 * d_qk + 2 * d_v)
    if_col, if_row = _if_gates(h0_bf, w_t[c_if:c_gate], b_in[c_if:c_gate])
    gates = _proj(h0_bf, w_t, b2d, c_gate, 2 * D, act="sigmoid")
    yb_pre = _mlstm(qkvo, if_col, if_row, mh_norm_g.reshape(1, -1), batch, seq, d_qk, d_v)
    u = _merge(ya_pre, yb_pre, w_conv_out, w_mlstm_out, gates)
    pre1 = _mix(u, w_mix_out, h0_f, alpha)

    rt = min(ROUTE_TILE, T)
    h1, h1_bf, h1_packed, idx, rank, w_col, counts = _ln1_router(pre1, ln1_g, ln1_b, w_router,
                                                                 router_bias, rt)
    dest, blk_e, nxt_blk, n_used, fill, n_rows = _dispatch_tables(idx, rank, counts[:, 0].astype(I32))
    dest_kt = dest.transpose(1, 0, 2).reshape(TOP_K, T)
    ct = min(COMBINE_TILE, T)
    dest_tiles = dest_kt.reshape(TOP_K, T // ct, ct).transpose(1, 0, 2).reshape(T // ct, TOP_K * ct)

    s_mid = _glu_up(h1_bf, w_gu_s)
    pre2, xs = _dense_tail(h1_bf, h1, h1_packed, dest_kt, fill, n_rows, s_mid, p_l, w_down_s,
                           w_ple_gate, b_ple_gate, w_ple_proj, alpha)

    hmid = _expert_matmul(xs, w_gu_e, blk_e, nxt_blk, n_used, _glu_block, w_gu_e.shape[2] // 2, BF16,
                          "expert_up", EXPERT_UP_SUB)
    y_packed = _expert_matmul(hmid, w_down_e, blk_e, nxt_blk, n_used, _down_block, D // 2, U32,
                              "expert_down", EXPERT_DOWN_SUB)
    return _combine_ln(y_packed, dest_tiles, w_col, pre2, ln2_g, ln2_b)


def kernel(x, p, ln_in_g, ln_in_b, w_in, b_in, conv_w, conv_b, mh_norm_g, w_conv_out, w_mlstm_out,
           w_mix_out, ln1_g, ln1_b, w_router, router_bias, w_gu_e, w_down_e, w_gu_s, w_down_s,
           w_ple_gate, b_ple_gate, w_ple_proj, ln2_g, ln2_b):
    B, S, D = x.shape
    depth = w_in.shape[0]
    alpha = (2 * depth) ** 0.25
    T = B * S
    h_f, h_bf = _ln_in(x.reshape(T, D), ln_in_g, ln_in_b)
    for l in range(depth):
        h_f = _layer(h_f, h_bf, p[l].reshape(T, -1), w_in[l], b_in[l], conv_w[l], conv_b[l],
                     mh_norm_g[l], w_conv_out[l], w_mlstm_out[l], w_mix_out[l], ln1_g[l], ln1_b[l],
                     w_router[l], router_bias[l], w_gu_e[l], w_down_e[l], w_gu_s[l], w_down_s[l],
                     w_ple_gate[l], b_ple_gate[l], w_ple_proj[l], ln2_g[l], ln2_b[l], alpha, B, S)
        if l + 1 < depth:
            h_bf = h_f.astype(BF16)
    return h_f.reshape(B, S, D)
```
